```python
import functools
import jax, jax.numpy as jnp
from jax import lax
import numpy as np

D_MODEL = 1024
BATCH = 1
SEQ = 16384
DEPTH = 1
DEC_BATCH = 128
DEC_SEQ = 8
PAST_LEN = 16384
PAGE_SIZE = 128

N_HEADS = 16
N_KV_HEADS = 4
HEAD_DIM = D_MODEL // N_HEADS
GROUP = N_HEADS // N_KV_HEADS
ATTN_WIDTH = N_HEADS * HEAD_DIM
KV_WIDTH = N_KV_HEADS * HEAD_DIM
WINDOW = 128
BLOCK = 128
ROPE_THETA = 10000.0
CONV_WIDTH = D_MODEL
CONV_K = 3
N_META = 16
EPS = 1e-6
NEG_INF = -1e30
IN_SIZES = (ATTN_WIDTH, KV_WIDTH, KV_WIDTH, ATTN_WIDTH,
            CONV_WIDTH, CONV_WIDTH, CONV_WIDTH, CONV_WIDTH, D_MODEL, D_MODEL)
IN_WIDTH = sum(IN_SIZES)

kernel_name = 'hybrid_swa_sink_shortconv_gated_merge_step'


def _split_points():
    return [int(s) for s in np.cumsum(IN_SIZES)[:-1]]


def _rms_norm(x, w):
    xf = x.astype(jnp.float32)
    r = lax.rsqrt(jnp.mean(xf * xf, axis=-1, keepdims=True) + EPS)
    return (xf * r).astype(x.dtype) * w.astype(x.dtype)


def _rope(x, pos):
    half = HEAD_DIM // 2
    inv = jnp.power(ROPE_THETA, -jnp.arange(half, dtype=jnp.float32) * (2.0 / HEAD_DIM))
    ang = pos.astype(jnp.float32)[:, None] * inv[None, :]
    cos = jnp.cos(ang)[:, None, :]
    sin = jnp.sin(ang)[:, None, :]
    xf = x.astype(jnp.float32)
    x1, x2 = xf[..., :half], xf[..., half:]
    return jnp.concatenate([x1 * cos - x2 * sin, x2 * cos + x1 * sin], axis=-1).astype(x.dtype)


def _sink_softmax(s, sinks):
    sk = sinks.astype(jnp.float32).reshape(N_KV_HEADS, GROUP)[:, :, None, None]
    m = jnp.maximum(jnp.max(s, axis=-1, keepdims=True), sk)
    p = jnp.exp(s - m)
    return p / (jnp.sum(p, axis=-1, keepdims=True) + jnp.exp(sk - m))


def _window_attn_prompt(q, k, v, sinks, pos):
    n, L = q.shape[:2]
    nb = L // BLOCK
    qb = q.reshape(n, nb, BLOCK, N_KV_HEADS, GROUP, HEAD_DIM)
    kb = k.reshape(n, nb, BLOCK, N_KV_HEADS, HEAD_DIM)
    vb = v.reshape(n, nb, BLOCK, N_KV_HEADS, HEAD_DIM)
    shift = ((0, 0), (1, 0), (0, 0), (0, 0), (0, 0))
    kk = jnp.concatenate([jnp.pad(kb[:, :-1], shift), kb], axis=2)
    vv = jnp.concatenate([jnp.pad(vb[:, :-1], shift), vb], axis=2)
    pb = pos.reshape(nb, BLOCK)
    kpos = jnp.concatenate([pb - BLOCK, pb], axis=1)
    diff = pb[:, :, None] - kpos[:, None, :]
    mask = (diff >= 0) & (diff <= WINDOW) & (kpos[:, None, :] >= 0)
    s = jnp.einsum('bnqkgd,bnskd->bnkgqs', qb, kk).astype(jnp.float32) * (HEAD_DIM ** -0.5)
    s = jnp.where(mask[None, :, None, None], s, NEG_INF)
    p = _sink_softmax(s, sinks).astype(v.dtype)
    o = jnp.einsum('bnkgqs,bnskd->bnqkgd', p, vv)
    return o.reshape(n, L, ATTN_WIDTH)


def _window_attn_sample(q, k, v, sinks, cache_k, cache_v):
    n, T = q.shape[:2]
    kk = jnp.concatenate([cache_k.astype(k.dtype), k], axis=1)
    vv = jnp.concatenate([cache_v.astype(v.dtype), v], axis=1)
    qpos = PAST_LEN + jnp.arange(T)
    kpos = PAST_LEN - WINDOW + jnp.arange(WINDOW + T)
    diff = qpos[:, None] - kpos[None, :]
    mask = (diff >= 0) & (diff <= WINDOW)
    qg = q.reshape(n, T, N_KV_HEADS, GROUP, HEAD_DIM)
    s = jnp.einsum('btkgd,bskd->bkgts', qg, kk).astype(jnp.float32) * (HEAD_DIM ** -0.5)
    s = jnp.where(mask[None, None, None], s, NEG_INF)
    p = _sink_softmax(s, sinks).astype(v.dtype)
    o = jnp.einsum('bkgts,bskd->btkgd', p, vv)
    return o.reshape(n, T, ATTN_WIDTH)


def _causal_conv(u_ext, w):
    T = u_ext.shape[1] - (CONV_K - 1)
    w = w.astype(u_ext.dtype)
    out = w[0] * u_ext[:, 0:T]
    for i in range(1, CONV_K):
        out = out + w[i] * u_ext[:, i:i + T]
    return out


def _layer(h, pos, valid, conv_prefix, attend, norm_w, w_in, q_norm_w, k_norm_w, sinks,
           conv_w, w_proj_a, w_proj_b, w_out):
    n, t, _ = h.shape
    xn = _rms_norm(h, norm_w)
    p = jnp.einsum('ntd,de->nte', xn, w_in)
    q, k, v, z_a, b, c, hc, z_b, g_a, g_b = jnp.split(p, _split_points(), axis=-1)
    q = _rope(_rms_norm(q.reshape(n, t, N_HEADS, HEAD_DIM), q_norm_w), pos)
    k = _rope(_rms_norm(k.reshape(n, t, N_KV_HEADS, HEAD_DIM), k_norm_w), pos)
    v = v.reshape(n, t, N_KV_HEADS, HEAD_DIM)
    o_a = attend(q, k, v, sinks)
    u = jnp.where(valid[None, :, None], c * hc, 0)
    u_ext = jnp.concatenate([conv_prefix.astype(u.dtype), u], axis=1)
    o_b = b * _causal_conv(u_ext, conv_w)
    br_a = jnp.einsum('nte,ed->ntd', o_a * jax.nn.silu(z_a), w_proj_a)
    br_b = jnp.einsum('nte,ed->ntd', o_b * jax.nn.silu(z_b), w_proj_b)
    mixed = jax.nn.sigmoid(g_a) * br_a + jax.nn.sigmoid(g_b) * br_b
    return h + jnp.einsum('ntd,de->nte', mixed, w_out), k, v, u_ext


def setup_inputs(seed: int = 0) -> dict:
    key = jax.random.key(seed)
    ks = jax.random.split(key, 16)
    f32 = jnp.float32
    nrm = lambda k, shape, scale=1.0: jax.random.normal(k, shape, f32) * scale
    return {
        'x_prompt': nrm(ks[0], (BATCH, SEQ, D_MODEL)),
        'x_sample': nrm(ks[1], (DEC_BATCH, DEC_SEQ, D_MODEL)),
        'cache_k': nrm(ks[2], (DEPTH, DEC_BATCH, WINDOW, N_KV_HEADS, HEAD_DIM)),
        'cache_v': nrm(ks[3], (DEPTH, DEC_BATCH, WINDOW, N_KV_HEADS, HEAD_DIM)),
        'state_conv': nrm(ks[4], (DEPTH, DEC_BATCH, CONV_K - 1, CONV_WIDTH)),
        'meta_tokens': nrm(ks[5], (N_META, D_MODEL)),
        'norm_w': 1.0 + nrm(ks[6], (DEPTH, D_MODEL), 0.02),
        'w_in': nrm(ks[7], (DEPTH, D_MODEL, IN_WIDTH), D_MODEL ** -0.5),
        'q_norm_w': 1.0 + nrm(ks[8], (DEPTH, HEAD_DIM), 0.02),
        'k_norm_w': 1.0 + nrm(ks[9], (DEPTH, HEAD_DIM), 0.02),
        'sinks': nrm(ks[10], (DEPTH, N_HEADS)),
        'conv_w': nrm(ks[11], (DEPTH, CONV_K, CONV_WIDTH), CONV_K ** -0.5),
        'w_proj_a': nrm(ks[12], (DEPTH, ATTN_WIDTH, D_MODEL), ATTN_WIDTH ** -0.5),
        'w_proj_b': nrm(ks[13], (DEPTH, CONV_WIDTH, D_MODEL), CONV_WIDTH ** -0.5),
        'w_out': nrm(ks[14], (DEPTH, D_MODEL, D_MODEL), D_MODEL ** -0.5),
    }


def reference(x_prompt, x_sample, cache_k, cache_v, state_conv, meta_tokens, norm_w, w_in,
              q_norm_w, k_norm_w, sinks, conv_w, w_proj_a, w_proj_b, w_out):
    nb_, _, d = x_prompt.shape
    lead = BLOCK - N_META
    meta = jnp.broadcast_to(meta_tokens.astype(x_prompt.dtype)[None], (nb_, N_META, d))
    h = jnp.concatenate([jnp.zeros((nb_, lead, d), x_prompt.dtype), meta, x_prompt], axis=1)
    pos_p = jnp.arange(h.shape[1]) - lead
    valid_p = pos_p >= 0
    attend_p = functools.partial(_window_attn_prompt, pos=pos_p)
    conv0 = jnp.zeros((nb_, CONV_K - 1, CONV_WIDTH), x_prompt.dtype)
    nk_p, nv_p, nc_p = [], [], []
    for l in range(DEPTH):
        h, k, v, u_ext = _layer(h, pos_p, valid_p, conv0, attend_p, norm_w[l], w_in[l],
                                q_norm_w[l], k_norm_w[l], sinks[l], conv_w[l],
                                w_proj_a[l], w_proj_b[l], w_out[l])
        nk_p.append(k[:, -WINDOW:])
        nv_p.append(v[:, -WINDOW:])
        nc_p.append(u_ext[:, -(CONV_K - 1):])
    y_prompt = h[:, BLOCK:]

    hs = x_sample
    t = x_sample.shape[1]
    pos_s = PAST_LEN + jnp.arange(t)
    valid_s = jnp.ones((t,), dtype=bool)
    nk_s, nv_s, nc_s = [], [], []
    for l in range(DEPTH):
        attend_s = functools.partial(_window_attn_sample, cache_k=cache_k[l], cache_v=cache_v[l])
        hs, k, v, u_ext = _layer(hs, pos_s, valid_s, state_conv[l], attend_s, norm_w[l], w_in[l],
                                 q_norm_w[l], k_norm_w[l], sinks[l], conv_w[l],
                                 w_proj_a[l], w_proj_b[l], w_out[l])
        nk_s.append(jnp.concatenate([cache_k[l].astype(k.dtype), k], axis=1)[:, -WINDOW:])
        nv_s.append(jnp.concatenate([cache_v[l].astype(v.dtype), v], axis=1)[:, -WINDOW:])
        nc_s.append(u_ext[:, -(CONV_K - 1):])
    y_sample = hs

    new_k_prompt = jnp.stack(nk_p, 0)
    new_v_prompt = jnp.stack(nv_p, 0)
    new_conv_prompt = jnp.stack(nc_p, 0)
    new_k_sample = jnp.stack(nk_s, 0)
    new_v_sample = jnp.stack(nv_s, 0)
    new_conv_sample = jnp.stack(nc_s, 0)
    return (y_prompt, y_sample, new_k_prompt, new_v_prompt, new_conv_prompt,
            new_k_sample, new_v_sample, new_conv_sample)
```

```python
import functools

import numpy as np
import jax
import jax.numpy as jnp
from jax import lax
from jax.experimental import pallas as pl
from jax.experimental.pallas import tpu as pltpu

D_MODEL = 1024
SEQ = 16384
DEC_BATCH = 128
DEC_SEQ = 8
PAST_LEN = 16384
N_HEADS = 16
N_KV_HEADS = 4
GROUP = N_HEADS // N_KV_HEADS
HEAD_DIM = 64
HALF = HEAD_DIM // 2
KV_WIDTH = N_KV_HEADS * HEAD_DIM
WINDOW = 128
BLOCK = 128
ROPE_THETA = 10000.0
CONV_K = 3
N_META = 16
LEAD = BLOCK - N_META
EPS = 1e-6
NEG_INF = -1e30
Q_SCALE = HEAD_DIM ** -0.5

R_V = 0
R_ZA = R_V + KV_WIDTH
R_B = R_ZA + D_MODEL
R_C = R_B + D_MODEL
R_H = R_C + D_MODEL
R_ZB = R_H + D_MODEL
R_GA = R_ZB + D_MODEL
R_GB = R_GA + D_MODEL
R_END = R_GB + D_MODEL

LANES = 128
SUBLANES = 8
VMEM_BYTES_V7X = 64 * 1024 * 1024

TM_PROMPT = 256
BC_SAMPLE = 16
VMEM_LIMIT = VMEM_BYTES_V7X - 6 * 1024 * 1024

BF16 = jnp.bfloat16
F32 = jnp.float32


def _dot(a, b):
    return jnp.dot(a, b, preferred_element_type=F32)


def _dot_nt(a, b):
    return lax.dot_general(a, b, (((1,), (1,)), ((), ())), preferred_element_type=F32)


def _rms_rows(x, w):
    ms = jnp.mean(x * x, axis=-1, keepdims=True)
    return (x * lax.rsqrt(ms + EPS)) * w


def _head_rms(t, bd, w):
    n = t.shape[1] // 256
    r = t.shape[0]
    sq = (t * t).astype(BF16)
    stacked = jnp.concatenate([sq[:, 256 * c:256 * (c + 1)] for c in range(n)], axis=0)
    ms = _dot(stacked, bd)
    ms = jnp.concatenate([ms[r * c:r * (c + 1)] for c in range(n)], axis=1)
    wt = jnp.concatenate([w] * n, axis=1)
    return (t * lax.rsqrt(ms + EPS)) * wt


def _rope_split(t, cos, sin):
    out = []
    for c in range(t.shape[1] // 256):
        a = t[:, 256 * c:256 * c + LANES]
        b = t[:, 256 * c + LANES:256 * (c + 1)]
        out.append(a * cos - b * sin)
        out.append(b * cos + a * sin)
    return jnp.concatenate(out, axis=1)


def _rope_natural(t, cos, sin_signed):
    lane = lax.broadcasted_iota(jnp.int32, (t.shape[0], LANES), 1)
    first_half = (lane & (HEAD_DIM - 1)) < HALF
    out = []
    for c in range(t.shape[1] // LANES):
        tc = t[:, LANES * c:LANES * (c + 1)]
        from_below = pltpu.roll(tc, HALF, 1)
        from_above = pltpu.roll(tc, LANES - HALF, 1)
        out.append(tc * cos + jnp.where(first_half, from_above, from_below) * sin_signed)
    return jnp.concatenate(out, axis=1)


def _lane_group_masks(group_of_lane, dtype):
    return [(group_of_lane == g).astype(dtype) for g in range(N_KV_HEADS)]


def _softmax_parts(s, sk):
    m = jnp.maximum(jnp.max(s, axis=-1, keepdims=True), sk)
    p = jnp.exp(s - m)
    den = jnp.sum(p, axis=-1, keepdims=True) + jnp.exp(sk - m)
    return p, 1.0 / den


def _gated_tail(x, xn, o_a, conv, wr_ref, wpa_ref, wpb_ref, wo_ref):
    z_a = _dot(xn, wr_ref[:, R_ZA:R_B])
    a_in = (o_a * (z_a * jax.nn.sigmoid(z_a))).astype(BF16)
    br_a = _dot(a_in, wpa_ref[...])
    b = _dot(xn, wr_ref[:, R_B:R_C])
    o_b = b * conv
    z_b = _dot(xn, wr_ref[:, R_ZB:R_GA])
    b_in = (o_b * (z_b * jax.nn.sigmoid(z_b))).astype(BF16)
    br_b = _dot(b_in, wpb_ref[...])
    g_a = _dot(xn, wr_ref[:, R_GA:R_GB])
    g_b = _dot(xn, wr_ref[:, R_GB:R_END])
    mixed = (jax.nn.sigmoid(g_a) * br_a + jax.nn.sigmoid(g_b) * br_b).astype(BF16)
    return x + _dot(mixed, wo_ref[...])


def _prompt_kernel(sinks_ref, x_ref, cos_ref, sin_ref, x0_ref, cos0_ref, sin0_ref, nw_ref,
                   qnw_ref, knw_ref, cw_ref, wqk_ref, wr_ref, wpa_ref, wpb_ref, wo_ref, bd_ref,
                   y_ref, nk_ref, nv_ref, nc_ref, kprev, vprev, utail, *, tm):
    i = pl.program_id(0)
    nw = nw_ref[...]
    bd = bd_ref[...]
    knw = knw_ref[...]

    @pl.when(i == 0)
    def _init():
        x0 = x0_ref[...]
        xn0 = _rms_rows(x0, nw).astype(BF16)
        k0 = _dot(xn0, wqk_ref[:, D_MODEL:D_MODEL + KV_WIDTH])
        k0 = _rope_split(_head_rms(k0, bd, knw), cos0_ref[...], sin0_ref[...])
        v0 = _dot(xn0, wr_ref[:, R_V:R_ZA])
        c0 = _dot(xn0, wr_ref[:, R_C:R_H])
        h0 = _dot(xn0, wr_ref[:, R_H:R_ZB])
        row = lax.broadcasted_iota(jnp.int32, (BLOCK, D_MODEL), 0)
        u0 = jnp.where(row >= LEAD, c0 * h0, 0.0)
        kprev[...] = k0
        vprev[...] = v0
        utail[...] = u0[BLOCK - SUBLANES:BLOCK]

    x = x_ref[...]
    xn = _rms_rows(x, nw).astype(BF16)
    cos = cos_ref[...]
    sin = sin_ref[...]

    qk = _dot(xn, wqk_ref[...])
    q = _head_rms(qk[:, :D_MODEL], bd, qnw_ref[...])
    q = _rope_split(q, cos * Q_SCALE, sin * Q_SCALE).astype(BF16)
    k = _rope_split(_head_rms(qk[:, D_MODEL:], bd, knw), cos, sin)
    v = _dot(xn, wr_ref[:, R_V:R_ZA])
    kb = k.astype(BF16)
    vb = v.astype(BF16)

    lane = lax.broadcasted_iota(jnp.int32, (1, 2 * BLOCK), 1)
    kmasks = _lane_group_masks((lane & (LANES - 1)) >> 5, BF16)
    vmasks = _lane_group_masks(lane >> 6, BF16)
    vgroup = lane >> 6
    r_io = lax.broadcasted_iota(jnp.int32, (BLOCK, 2 * BLOCK), 0)
    c_io = lax.broadcasted_iota(jnp.int32, (BLOCK, 2 * BLOCK), 1)
    band = (c_io >= r_io) & (c_io <= r_io + WINDOW)
    first_band = band & (c_io >= jnp.where(i == 0, LEAD, 0))

    o_blocks = []
    for b in range(tm // BLOCK):
        lo = b * BLOCK
        if b == 0:
            kpair = jnp.concatenate([kprev[...].astype(BF16), kb[:BLOCK]], axis=0)
            vpair = jnp.concatenate([vprev[...].astype(BF16), vb[:BLOCK]], axis=0)
            mask = first_band
        else:
            kpair = kb[lo - BLOCK:lo + BLOCK]
            vpair = vb[lo - BLOCK:lo + BLOCK]
            mask = band
        kbd = jnp.concatenate([kpair * km for km in kmasks], axis=0)
        vbd = jnp.concatenate([vpair * vm for vm in vmasks], axis=0)
        qs = jnp.concatenate([q[lo:lo + BLOCK, 256 * j:256 * (j + 1)] for j in range(GROUP)],
                             axis=0)
        s = _dot_nt(qs, kbd)
        p_rows = []
        mults = []
        for j in range(GROUP):
            p_cols = []
            invs = []
            for g in range(N_KV_HEADS):
                sl = s[BLOCK * j:BLOCK * (j + 1), 256 * g:256 * (g + 1)]
                sl = jnp.where(mask, sl, NEG_INF)
                p, inv = _softmax_parts(sl, sinks_ref[GROUP * g + j])
                p_cols.append(p.astype(BF16))
                invs.append(inv)
            p_rows.append(jnp.concatenate(p_cols, axis=1))
            mults.append(jnp.where(vgroup == 0, invs[0],
                                   jnp.where(vgroup == 1, invs[1],
                                             jnp.where(vgroup == 2, invs[2], invs[3]))))
        o = _dot(jnp.concatenate(p_rows, axis=0), vbd)
        o_blocks.append(jnp.concatenate(
            [o[BLOCK * j:BLOCK * (j + 1)] * mults[j] for j in range(GROUP)], axis=1))
    o_a = o_blocks[0] if len(o_blocks) == 1 else jnp.concatenate(o_blocks, axis=0)

    kprev[...] = k[tm - BLOCK:]
    vprev[...] = v[tm - BLOCK:]
    nk_ref[...] = k[tm - BLOCK:]
    nv_ref[...] = v[tm - BLOCK:]

    c = _dot(xn, wr_ref[:, R_C:R_H])
    hc = _dot(xn, wr_ref[:, R_H:R_ZB])
    u = c * hc
    tail = utail[...]
    prev1 = tail[SUBLANES - 1:SUBLANES]
    prev2 = tail[SUBLANES - 2:SUBLANES - 1]
    row8 = lax.broadcasted_iota(jnp.int32, (SUBLANES, D_MODEL), 0)
    r1 = pltpu.roll(u, 1, 0)
    r2 = pltpu.roll(u, 2, 0)
    um1 = jnp.concatenate(
        [jnp.where(row8 == 0, prev1, r1[:SUBLANES]), r1[SUBLANES:]], axis=0)
    um2 = jnp.concatenate(
        [jnp.where(row8 == 0, prev2, jnp.where(row8 == 1, prev1, r2[:SUBLANES])),
         r2[SUBLANES:]], axis=0)
    cw = cw_ref[...]
    conv = cw[0:1] * um2 + cw[1:2] * um1 + cw[2:3] * u
    utail[...] = u[tm - SUBLANES:]
    nc_ref[...] = u[tm - (CONV_K - 1):]

    y_ref[...] = _gated_tail(x, xn, o_a, conv, wr_ref, wpa_ref, wpb_ref, wo_ref)


def _const_spec(shape):
    zeros = (0,) * len(shape)
    return pl.BlockSpec(shape, lambda i: zeros, pipeline_mode=pl.Buffered(1))


def _prompt_call(x, tabs, x0, tabs0, nw, qnw, knw, cw, sinks, wqk, wr, wpa, wpb, wo, bd):
    tm = TM_PROMPT
    n_tiles = SEQ // tm
    row_spec = lambda w: pl.BlockSpec((tm, w), lambda i: (i, 0))
    smem = pl.BlockSpec(memory_space=pltpu.SMEM)
    in_specs = [
        smem,
        row_spec(D_MODEL), row_spec(LANES), row_spec(LANES),
        _const_spec((BLOCK, D_MODEL)), _const_spec((BLOCK, LANES)), _const_spec((BLOCK, LANES)),
        _const_spec((1, D_MODEL)), _const_spec((1, 256)), _const_spec((1, 256)),
        _const_spec((CONV_K, D_MODEL)),
        _const_spec(wqk.shape), _const_spec(wr.shape), _const_spec(wpa.shape),
        _const_spec(wpb.shape), _const_spec(wo.shape), _const_spec(bd.shape),
    ]
    out_specs = [
        row_spec(D_MODEL),
        pl.BlockSpec((BLOCK, KV_WIDTH), lambda i: (0, 0)),
        pl.BlockSpec((BLOCK, KV_WIDTH), lambda i: (0, 0)),
        pl.BlockSpec((CONV_K - 1, D_MODEL), lambda i: (0, 0)),
    ]
    out_shape = [
        jax.ShapeDtypeStruct((SEQ, D_MODEL), F32),
        jax.ShapeDtypeStruct((BLOCK, KV_WIDTH), F32),
        jax.ShapeDtypeStruct((BLOCK, KV_WIDTH), F32),
        jax.ShapeDtypeStruct((CONV_K - 1, D_MODEL), F32),
    ]
    return pl.pallas_call(
        functools.partial(_prompt_kernel, tm=tm),
        grid=(n_tiles,),
        in_specs=in_specs,
        out_specs=out_specs,
        out_shape=out_shape,
        scratch_shapes=[
            pltpu.VMEM((BLOCK, KV_WIDTH), F32),
            pltpu.VMEM((BLOCK, KV_WIDTH), F32),
            pltpu.VMEM((SUBLANES, D_MODEL), F32),
        ],
        compiler_params=pltpu.CompilerParams(
            dimension_semantics=("arbitrary",), vmem_limit_bytes=VMEM_LIMIT),
        name="prompt_layer",
    )(sinks, x, tabs[0], tabs[1], x0, tabs0[0], tabs0[1], nw, qnw, knw, cw,
      wqk, wr, wpa, wpb, wo, bd)


def _sample_kernel(x_ref, cos_ref, sin_ref, skcol_ref, ck_ref, cv_ref, st_ref, nw_ref, qnw_ref,
                   knw_ref, cw_ref, wqk_ref, wr_ref, wpa_ref, wpb_ref, wo_ref, bd_ref,
                   y_ref, nk_ref, nv_ref, nc_ref, q_s, k_s, v_s, oa_s, *, bc):
    t_len = DEC_SEQ
    rows = bc * t_len
    nw = nw_ref[...]
    bd = bd_ref[...]
    x = x_ref[...]
    xn = _rms_rows(x, nw).astype(BF16)
    cos = cos_ref[...]
    sin = sin_ref[...]
    qk = _dot(xn, wqk_ref[...])
    q = _head_rms(qk[:, :D_MODEL], bd, qnw_ref[...])
    q_s[...] = _rope_natural(q, cos * Q_SCALE, sin * Q_SCALE)
    k_s[...] = _rope_natural(_head_rms(qk[:, D_MODEL:], bd, knw_ref[...]), cos, sin)
    v_s[...] = _dot(xn, wr_ref[:, R_V:R_ZA])

    n_exp = GROUP * N_KV_HEADS * t_len
    lane = lax.broadcasted_iota(jnp.int32, (1, KV_WIDTH), 1)
    vgroup = lane >> 6
    r_io = lax.broadcasted_iota(jnp.int32, (n_exp, 2 * WINDOW), 0) & (t_len - 1)
    c_io = lax.broadcasted_iota(jnp.int32, (n_exp, 2 * WINDOW), 1)
    band = (c_io >= r_io) & (c_io <= r_io + WINDOW)
    skcol = skcol_ref[...]
    key_pad = jnp.zeros((WINDOW - t_len, KV_WIDTH), F32)

    def per_batch(n, carry):
        r0 = pl.multiple_of(n * t_len, t_len)
        qn = q_s[pl.ds(r0, t_len), :]
        k_new = k_s[pl.ds(r0, t_len), :]
        v_new = v_s[pl.ds(r0, t_len), :]
        k_old = ck_ref[n]
        v_old = cv_ref[n]
        nk_ref[n, 0:WINDOW - t_len, :] = k_old[t_len:]
        nk_ref[n, WINDOW - t_len:WINDOW, :] = k_new
        nv_ref[n, 0:WINDOW - t_len, :] = v_old[t_len:]
        nv_ref[n, WINDOW - t_len:WINDOW, :] = v_new
        keys = jnp.concatenate([k_old, k_new, key_pad], axis=0).astype(BF16)
        vals = jnp.concatenate([v_old, v_new, key_pad], axis=0).astype(BF16)
        q_exp = jnp.concatenate(
            [jnp.where(vgroup == g, qn[:, 256 * j:256 * (j + 1)], 0.0)
             for j in range(GROUP) for g in range(N_KV_HEADS)], axis=0).astype(BF16)
        s = jnp.where(band, _dot_nt(q_exp, keys), NEG_INF)
        p, inv = _softmax_parts(s, skcol)
        o = _dot(p.astype(BF16), vals) * inv
        outs = []
        for j in range(GROUP):
            acc = None
            for g in range(N_KV_HEADS):
                r = (j * N_KV_HEADS + g) * t_len
                piece = jnp.where(vgroup == g, o[r:r + t_len], 0.0)
                acc = piece if acc is None else acc + piece
            outs.append(acc)
        oa_s[pl.ds(r0, t_len), :] = jnp.concatenate(outs, axis=1)
        return carry

    lax.fori_loop(0, bc, per_batch, 0)

    c = _dot(xn, wr_ref[:, R_C:R_H])
    hc = _dot(xn, wr_ref[:, R_H:R_ZB])
    u3 = (c * hc).reshape(bc, t_len, D_MODEL)
    st = st_ref[...]
    s_m2 = st[:, 0:1, :]
    s_m1 = st[:, 1:2, :]
    t_io = lax.broadcasted_iota(jnp.int32, (bc, t_len, D_MODEL), 1)
    um1 = jnp.where(t_io == 0, s_m1, pltpu.roll(u3, 1, 1))
    um2 = jnp.where(t_io == 0, s_m2, jnp.where(t_io == 1, s_m1, pltpu.roll(u3, 2, 1)))
    cw = cw_ref[...]
    conv = (cw[0:1] * um2 + cw[1:2] * um1 + cw[2:3] * u3).reshape(rows, D_MODEL)
    nc_ref[...] = u3[:, t_len - (CONV_K - 1):, :]

    y_ref[...] = _gated_tail(x, xn, oa_s[...], conv, wr_ref, wpa_ref, wpb_ref, wo_ref)


def _sample_call(x, tabs, skcol, ck, cv, st, nw, qnw, knw, cw, wqk, wr, wpa, wpb, wo, bd):
    bc = BC_SAMPLE
    rows = bc * DEC_SEQ
    n_steps = DEC_BATCH // bc
    n_exp = GROUP * N_KV_HEADS * DEC_SEQ
    in_specs = [
        pl.BlockSpec((rows, D_MODEL), lambda i: (i, 0)),
        _const_spec((rows, LANES)), _const_spec((rows, LANES)),
        _const_spec((n_exp, 1)),
        pl.BlockSpec((bc, WINDOW, KV_WIDTH), lambda i: (i, 0, 0)),
        pl.BlockSpec((bc, WINDOW, KV_WIDTH), lambda i: (i, 0, 0)),
        pl.BlockSpec((bc, CONV_K - 1, D_MODEL), lambda i: (i, 0, 0)),
        _const_spec((1, D_MODEL)), _const_spec((1, 256)), _const_spec((1, 256)),
        _const_spec((CONV_K, D_MODEL)),
        _const_spec(wqk.shape), _const_spec(wr.shape), _const_spec(wpa.shape),
        _const_spec(wpb.shape), _const_spec(wo.shape), _const_spec(bd.shape),
    ]
    out_specs = [
        pl.BlockSpec((rows, D_MODEL), lambda i: (i, 0)),
        pl.BlockSpec((bc, WINDOW, KV_WIDTH), lambda i: (i, 0, 0)),
        pl.BlockSpec((bc, WINDOW, KV_WIDTH), lambda i: (i, 0, 0)),
        pl.BlockSpec((bc, CONV_K - 1, D_MODEL), lambda i: (i, 0, 0)),
    ]
    out_shape = [
        jax.ShapeDtypeStruct((DEC_BATCH * DEC_SEQ, D_MODEL), F32),
        jax.ShapeDtypeStruct((DEC_BATCH, WINDOW, KV_WIDTH), F32),
        jax.ShapeDtypeStruct((DEC_BATCH, WINDOW, KV_WIDTH), F32),
        jax.ShapeDtypeStruct((DEC_BATCH, CONV_K - 1, D_MODEL), F32),
    ]
    return pl.pallas_call(
        functools.partial(_sample_kernel, bc=bc),
        grid=(n_steps,),
        in_specs=in_specs,
        out_specs=out_specs,
        out_shape=out_shape,
        scratch_shapes=[
            pltpu.VMEM((rows, D_MODEL), F32),
            pltpu.VMEM((rows, KV_WIDTH), F32),
            pltpu.VMEM((rows, KV_WIDTH), F32),
            pltpu.VMEM((rows, D_MODEL), F32),
        ],
        compiler_params=pltpu.CompilerParams(
            dimension_semantics=("arbitrary",), vmem_limit_bytes=VMEM_LIMIT),
        name="sample_layer",
    )(x, tabs[0], tabs[1], skcol, ck, cv, st, nw, qnw, knw, cw, wqk, wr, wpa, wpb, wo, bd)


def _rope_tables(positions, signed):
    inv = np.power(ROPE_THETA, -np.arange(HALF, dtype=np.float64) * (2.0 / HEAD_DIM))
    ang = np.asarray(positions, dtype=np.float64)[:, None] * inv[None, :]
    cos = np.tile(np.cos(ang), (1, LANES // HALF))
    sin = np.tile(np.sin(ang), (1, LANES // HALF))
    if signed:
        sign = np.where((np.arange(LANES) % HEAD_DIM) < HALF, -1.0, 1.0)
        sin = sin * sign[None, :]
    return jnp.asarray(cos, F32), jnp.asarray(sin, F32)


def _head_mean_matrix(split_half):
    lane = np.arange(256)
    grp = (lane % LANES) // HALF if split_half else lane // HEAD_DIM
    return jnp.asarray((grp[:, None] == grp[None, :]) / float(HEAD_DIM), BF16)


def kernel(x_prompt, x_sample, cache_k, cache_v, state_conv, meta_tokens, norm_w, w_in,
           q_norm_w, k_norm_w, sinks, conv_w, w_proj_a, w_proj_b, w_out):
    assert x_prompt.shape == (1, SEQ, D_MODEL) and x_sample.shape == (DEC_BATCH, DEC_SEQ, D_MODEL)
    assert w_in.shape[0] == 1, "single layer"
    w = w_in[0]
    wq = w[:, :D_MODEL].reshape(D_MODEL, N_KV_HEADS, GROUP, 2, HALF)
    wk = w[:, D_MODEL:D_MODEL + KV_WIDTH].reshape(D_MODEL, N_KV_HEADS, 2, HALF)
    o1 = D_MODEL + 2 * KV_WIDTH
    wv = w[:, D_MODEL + KV_WIDTH:o1]
    wza = w[:, o1:o1 + D_MODEL].reshape(D_MODEL, N_KV_HEADS, GROUP, HEAD_DIM)
    wq_split = wq.transpose(0, 2, 3, 1, 4).reshape(D_MODEL, D_MODEL)
    wk_split = wk.transpose(0, 2, 1, 3).reshape(D_MODEL, KV_WIDTH)
    wq_nat = wq.transpose(0, 2, 1, 3, 4).reshape(D_MODEL, D_MODEL)
    wza_p = wza.transpose(0, 2, 1, 3).reshape(D_MODEL, D_MODEL)
    wqk_p = jnp.concatenate([wq_split, wk_split], axis=1).astype(BF16)
    wqk_s = jnp.concatenate([wq_nat, w[:, D_MODEL:D_MODEL + KV_WIDTH]], axis=1).astype(BF16)
    wr = jnp.concatenate([wv, wza_p, w[:, o1 + D_MODEL:]], axis=1).astype(BF16)
    wpa = (w_proj_a[0].reshape(N_KV_HEADS, GROUP, HEAD_DIM, D_MODEL)
           .transpose(1, 0, 2, 3).reshape(D_MODEL, D_MODEL).astype(BF16))
    wpb = w_proj_b[0].astype(BF16)
    wo = w_out[0].astype(BF16)

    nw = norm_w[0].reshape(1, D_MODEL)
    cw = conv_w[0]
    qn = q_norm_w[0]
    kn = k_norm_w[0]
    split = lambda v: jnp.concatenate([jnp.tile(v[:HALF], 4), jnp.tile(v[HALF:], 4)]).reshape(1, 256)
    nat = lambda v: jnp.tile(v, 4).reshape(1, 256)
    sk = sinks[0].astype(F32)

    tabs_p = _rope_tables(np.arange(SEQ) + N_META, signed=False)
    tabs_0 = _rope_tables(np.arange(BLOCK) - LEAD, signed=False)
    x0 = jnp.concatenate([jnp.zeros((LEAD, D_MODEL), x_prompt.dtype),
                          meta_tokens.astype(x_prompt.dtype)], axis=0)
    y_p, nk_p, nv_p, nc_p = _prompt_call(
        x_prompt[0], tabs_p, x0, tabs_0, nw, split(qn), split(kn), cw, sk, wqk_p, wr, wpa, wpb, wo,
        _head_mean_matrix(True))
    nk_p = nk_p.reshape(BLOCK, 2, N_KV_HEADS, HALF).transpose(0, 2, 1, 3)

    rows = BC_SAMPLE * DEC_SEQ
    tabs_s = _rope_tables(PAST_LEN + (np.arange(rows) % DEC_SEQ), signed=True)
    skcol = jnp.repeat(sk.reshape(N_KV_HEADS, GROUP).T.reshape(-1), DEC_SEQ).reshape(-1, 1)
    y_s, nk_s, nv_s, nc_s = _sample_call(
        x_sample.reshape(DEC_BATCH * DEC_SEQ, D_MODEL), tabs_s, skcol,
        cache_k[0].reshape(DEC_BATCH, WINDOW, KV_WIDTH),
        cache_v[0].reshape(DEC_BATCH, WINDOW, KV_WIDTH),
        state_conv[0], nw, nat(qn), nat(kn), cw, wqk_s, wr, wpa, wpb, wo,
        _head_mean_matrix(False))

    kv5 = (1, -1, WINDOW, N_KV_HEADS, HEAD_DIM)
    return (y_p.reshape(1, SEQ, D_MODEL),
            y_s.reshape(DEC_BATCH, DEC_SEQ, D_MODEL),
            nk_p.reshape(1, 1, WINDOW, N_KV_HEADS, HEAD_DIM),
            nv_p.reshape(1, 1, WINDOW, N_KV_HEADS, HEAD_DIM),
            nc_p.reshape(1, 1, CONV_K - 1, D_MODEL),
            nk_s.reshape(kv5),
            nv_s.reshape(kv5),
            nc_s.reshape(1, DEC_BATCH, CONV_K - 1, D_MODEL))
```

```python
import functools

import numpy as np
import jax
import jax.numpy as jnp
from jax import lax
from jax.experimental import pallas as pl
from jax.experimental.pallas import tpu as pltpu

D_MODEL = 1024
SEQ = 16384
DEC_BATCH = 128
DEC_SEQ = 8
PAST_LEN = 16384
N_HEADS = 16
N_KV_HEADS = 4
GROUP = N_HEADS // N_KV_HEADS
HEAD_DIM = 64
HALF = HEAD_DIM // 2
KV_WIDTH = N_KV_HEADS * HEAD_DIM
WINDOW = 128
BLOCK = 128
ROPE_THETA = 10000.0
CONV_K = 3
N_META = 16
LEAD = BLOCK - N_META
EPS = 1e-6
NEG_INF = -1e30
Q_SCALE = HEAD_DIM ** -0.5

G_B = 0
G_C = G_B + D_MODEL
G_H = G_C + D_MODEL
G_ZB = G_H + D_MODEL
G_GA = G_ZB + D_MODEL
G_GB = G_GA + D_MODEL
G_END = G_GB + D_MODEL

LANES = 128
SUBLANES = 8
VMEM_BYTES_V7X = 64 * 1024 * 1024

TM_PROMPT = 256
TM_SAMPLE = 256
BC_SAMPLE = LANES // DEC_SEQ
VMEM_LIMIT = VMEM_BYTES_V7X - 6 * 1024 * 1024

BF16 = jnp.bfloat16
F32 = jnp.float32


def _dot(a, b):
    return jnp.dot(a, b, preferred_element_type=F32)


def _dot_nt(a, b):
    return lax.dot_general(a, b, (((1,), (1,)), ((), ())), preferred_element_type=F32)


def _rms_rows(x, w):
    ms = jnp.mean(x * x, axis=-1, keepdims=True)
    return (x * lax.rsqrt(ms + EPS)) * w


def _head_rms(t, bd, w):
    n = t.shape[1] // 256
    r = t.shape[0]
    sq = (t * t).astype(BF16)
    stacked = jnp.concatenate([sq[:, 256 * c:256 * (c + 1)] for c in range(n)], axis=0)
    ms = _dot(stacked, bd)
    ms = jnp.concatenate([ms[r * c:r * (c + 1)] for c in range(n)], axis=1)
    wt = jnp.concatenate([w] * n, axis=1)
    return (t * lax.rsqrt(ms + EPS)) * wt


def _rope_split(t, cos, sin):
    out = []
    for c in range(t.shape[1] // 256):
        a = t[:, 256 * c:256 * c + LANES]
        b = t[:, 256 * c + LANES:256 * (c + 1)]
        out.append(a * cos - b * sin)
        out.append(b * cos + a * sin)
    return jnp.concatenate(out, axis=1)


def _rope_natural(t, cos, sin_signed):
    lane = lax.broadcasted_iota(jnp.int32, (t.shape[0], LANES), 1)
    first_half = (lane & (HEAD_DIM - 1)) < HALF
    out = []
    for c in range(t.shape[1] // LANES):
        tc = t[:, LANES * c:LANES * (c + 1)]
        from_below = pltpu.roll(tc, HALF, 1)
        from_above = pltpu.roll(tc, LANES - HALF, 1)
        out.append(tc * cos + jnp.where(first_half, from_above, from_below) * sin_signed)
    return jnp.concatenate(out, axis=1)


def _rope_rows(t, cos, sin):
    out = []
    for g in range(t.shape[0] // HEAD_DIM):
        a = t[HEAD_DIM * g:HEAD_DIM * g + HALF]
        b = t[HEAD_DIM * g + HALF:HEAD_DIM * (g + 1)]
        out.append(a * cos - b * sin)
        out.append(b * cos + a * sin)
    return jnp.concatenate(out, axis=0)


def _lane_group_masks(group_of_lane, dtype):
    return [(group_of_lane == g).astype(dtype) for g in range(N_KV_HEADS)]


def _softmax_parts(s, sk):
    m = jnp.maximum(jnp.max(s, axis=-1, keepdims=True), sk)
    p = jnp.exp(s - m)
    den = jnp.sum(p, axis=-1, keepdims=True) + jnp.exp(sk - m)
    return p, 1.0 / den


def _silu(z):
    return z * jax.nn.sigmoid(z)


def _branch_b(xn, conv, wg_ref, wpb_ref):
    b = _dot(xn, wg_ref[:, G_B:G_C])
    z_b = _dot(xn, wg_ref[:, G_ZB:G_GA])
    b_in = ((b * conv) * _silu(z_b)).astype(BF16)
    return _dot(b_in, wpb_ref[...])


def _prompt_kernel(sinks_ref, x_ref, cos_ref, sin_ref, x0_ref, cos0_ref, sin0_ref, nw_ref,
                   qnw_ref, knw_ref, cw_ref, wq_ref, wk_ref, wv_ref, wza_ref, wg_ref, wpa_ref,
                   wpb_ref, wo_ref, bd_ref,
                   y_ref, nk_ref, nv_ref, nc_ref, kprev, vprev, utail, *, tm):
    i = pl.program_id(0)
    nw = nw_ref[...]
    bd = bd_ref[...]
    knw = knw_ref[...]

    @pl.when(i == 0)
    def _init():
        x0 = x0_ref[...]
        xn0 = _rms_rows(x0, nw).astype(BF16)
        k0 = _dot(xn0, wk_ref[...])
        k0 = _rope_split(_head_rms(k0, bd, knw), cos0_ref[...], sin0_ref[...])
        v0 = _dot(xn0, wv_ref[...])
        c0 = _dot(xn0, wg_ref[:, G_C:G_H])
        h0 = _dot(xn0, wg_ref[:, G_H:G_ZB])
        row = lax.broadcasted_iota(jnp.int32, (BLOCK, D_MODEL), 0)
        u0 = jnp.where(row >= LEAD, c0 * h0, 0.0)
        kprev[...] = k0
        vprev[...] = v0
        utail[...] = u0[BLOCK - SUBLANES:BLOCK]

    x = x_ref[...]
    xn = _rms_rows(x, nw).astype(BF16)
    cos = cos_ref[...]
    sin = sin_ref[...]

    q = _head_rms(_dot(xn, wq_ref[...]), bd, qnw_ref[...])
    q = _rope_split(q, cos * Q_SCALE, sin * Q_SCALE).astype(BF16)
    k = _rope_split(_head_rms(_dot(xn, wk_ref[...]), bd, knw), cos, sin)
    v = _dot(xn, wv_ref[...])
    kb = k.astype(BF16)
    vb = v.astype(BF16)

    lane = lax.broadcasted_iota(jnp.int32, (1, 2 * BLOCK), 1)
    kmasks = _lane_group_masks((lane & (LANES - 1)) >> 5, BF16)
    vmasks = _lane_group_masks(lane >> 6, BF16)
    vgroup = lane >> 6
    r_io = lax.broadcasted_iota(jnp.int32, (BLOCK, 2 * BLOCK), 0)
    c_io = lax.broadcasted_iota(jnp.int32, (BLOCK, 2 * BLOCK), 1)
    band = (c_io >= r_io) & (c_io <= r_io + WINDOW)
    first_band = band & (c_io >= jnp.where(i == 0, LEAD, 0))

    o_blocks = []
    for b in range(tm // BLOCK):
        lo = b * BLOCK
        if b == 0:
            kpair = jnp.concatenate([kprev[...].astype(BF16), kb[:BLOCK]], axis=0)
            vpair = jnp.concatenate([vprev[...].astype(BF16), vb[:BLOCK]], axis=0)
            mask = first_band
        else:
            kpair = kb[lo - BLOCK:lo + BLOCK]
            vpair = vb[lo - BLOCK:lo + BLOCK]
            mask = band
        kbd = jnp.concatenate([kpair * km for km in kmasks], axis=0)
        vbd = jnp.concatenate([vpair * vm for vm in vmasks], axis=0)
        qs = jnp.concatenate([q[lo:lo + BLOCK, 256 * j:256 * (j + 1)] for j in range(GROUP)],
                             axis=0)
        s = _dot_nt(qs, kbd)
        p_rows = []
        mults = []
        for j in range(GROUP):
            p_cols = []
            invs = []
            for g in range(N_KV_HEADS):
                sl = s[BLOCK * j:BLOCK * (j + 1), 256 * g:256 * (g + 1)]
                sl = jnp.where(mask, sl, NEG_INF)
                p, inv = _softmax_parts(sl, sinks_ref[GROUP * g + j])
                p_cols.append(p.astype(BF16))
                invs.append(inv)
            p_rows.append(jnp.concatenate(p_cols, axis=1))
            mults.append(jnp.where(vgroup == 0, invs[0],
                                   jnp.where(vgroup == 1, invs[1],
                                             jnp.where(vgroup == 2, invs[2], invs[3]))))
        o = _dot(jnp.concatenate(p_rows, axis=0), vbd)
        o_blocks.append(jnp.concatenate(
            [o[BLOCK * j:BLOCK * (j + 1)] * mults[j] for j in range(GROUP)], axis=1))
    o_a = o_blocks[0] if len(o_blocks) == 1 else jnp.concatenate(o_blocks, axis=0)

    kprev[...] = k[tm - BLOCK:]
    vprev[...] = v[tm - BLOCK:]
    nk_ref[...] = k[tm - BLOCK:]
    nv_ref[...] = v[tm - BLOCK:]

    c = _dot(xn, wg_ref[:, G_C:G_H])
    hc = _dot(xn, wg_ref[:, G_H:G_ZB])
    u = c * hc
    tail = utail[...]
    prev1 = tail[SUBLANES - 1:SUBLANES]
    prev2 = tail[SUBLANES - 2:SUBLANES - 1]
    row8 = lax.broadcasted_iota(jnp.int32, (SUBLANES, D_MODEL), 0)
    r1 = pltpu.roll(u, 1, 0)
    r2 = pltpu.roll(u, 2, 0)
    um1 = jnp.concatenate(
        [jnp.where(row8 == 0, prev1, r1[:SUBLANES]), r1[SUBLANES:]], axis=0)
    um2 = jnp.concatenate(
        [jnp.where(row8 == 0, prev2, jnp.where(row8 == 1, prev1, r2[:SUBLANES])),
         r2[SUBLANES:]], axis=0)
    cw = cw_ref[...]
    conv = cw[0:1] * um2 + cw[1:2] * um1 + cw[2:3] * u
    utail[...] = u[tm - SUBLANES:]
    nc_ref[...] = u[tm - (CONV_K - 1):]

    z_a = _dot(xn, wza_ref[...])
    br_a = _dot((o_a * _silu(z_a)).astype(BF16), wpa_ref[...])
    br_b = _branch_b(xn, conv, wg_ref, wpb_ref)
    g_a = _dot(xn, wg_ref[:, G_GA:G_GB])
    g_b = _dot(xn, wg_ref[:, G_GB:G_END])
    mixed = (jax.nn.sigmoid(g_a) * br_a + jax.nn.sigmoid(g_b) * br_b).astype(BF16)
    y_ref[...] = x + _dot(mixed, wo_ref[...])


def _const_spec(shape):
    zeros = (0,) * len(shape)
    return pl.BlockSpec(shape, lambda i: zeros, pipeline_mode=pl.Buffered(1))


def _params():
    return pltpu.CompilerParams(dimension_semantics=("arbitrary",), vmem_limit_bytes=VMEM_LIMIT)


def _prompt_call(x, tabs, x0, tabs0, nw, qnw, knw, cw, sinks, wq, wk, wv, wza, wg, wpa, wpb, wo, bd):
    tm = TM_PROMPT
    row_spec = lambda w: pl.BlockSpec((tm, w), lambda i: (i, 0))
    consts = (x0, tabs0[0], tabs0[1], nw, qnw, knw, cw, wq, wk, wv, wza, wg, wpa, wpb, wo, bd)
    in_specs = [pl.BlockSpec(memory_space=pltpu.SMEM),
                row_spec(D_MODEL), row_spec(LANES), row_spec(LANES)]
    in_specs += [_const_spec(a.shape) for a in consts]
    resident = lambda shape: pl.BlockSpec(shape, lambda i: (0, 0))
    out_specs = [row_spec(D_MODEL), resident((BLOCK, KV_WIDTH)), resident((BLOCK, KV_WIDTH)),
                 resident((CONV_K - 1, D_MODEL))]
    out_shape = [jax.ShapeDtypeStruct((SEQ, D_MODEL), F32),
                 jax.ShapeDtypeStruct((BLOCK, KV_WIDTH), F32),
                 jax.ShapeDtypeStruct((BLOCK, KV_WIDTH), F32),
                 jax.ShapeDtypeStruct((CONV_K - 1, D_MODEL), F32)]
    return pl.pallas_call(
        functools.partial(_prompt_kernel, tm=tm),
        grid=(SEQ // tm,),
        in_specs=in_specs,
        out_specs=out_specs,
        out_shape=out_shape,
        scratch_shapes=[pltpu.VMEM((BLOCK, KV_WIDTH), F32),
                        pltpu.VMEM((BLOCK, KV_WIDTH), F32),
                        pltpu.VMEM((SUBLANES, D_MODEL), F32)],
        compiler_params=_params(),
        name="prompt_layer",
    )(sinks, x, tabs[0], tabs[1], *consts)


def _sample_pre_kernel(x_ref, cos_ref, sin_ref, cost_ref, sint_ref, st_ref, nw_ref, qnw_ref,
                       knwc_ref, cw_ref, wq_ref, wkt_ref, wvt_ref, wza_ref, wg_ref, wpb_ref, bd_ref,
                       q_ref, kt_ref, vt_ref, sza_ref, sga_ref, tb_ref, nc_ref, *, rows):
    t_len = DEC_SEQ
    bc = rows // t_len
    bd = bd_ref[...]
    x = x_ref[...]
    xn = _rms_rows(x, nw_ref[...]).astype(BF16)

    q = _head_rms(_dot(xn, wq_ref[...]), bd, qnw_ref[...])
    q_ref[...] = _rope_natural(q, cos_ref[...] * Q_SCALE, sin_ref[...] * Q_SCALE)

    kt = _dot_nt(wkt_ref[...], xn)
    ms = _dot(bd, (kt * kt).astype(BF16))
    kt = (kt * lax.rsqrt(ms + EPS)) * knwc_ref[...]
    kt_ref[...] = _rope_rows(kt, cost_ref[...], sint_ref[...])
    vt_ref[...] = _dot_nt(wvt_ref[...], xn)

    c = _dot(xn, wg_ref[:, G_C:G_H])
    hc = _dot(xn, wg_ref[:, G_H:G_ZB])
    u3 = (c * hc).reshape(bc, t_len, D_MODEL)
    st = st_ref[...]
    s_m2 = st[:, 0:1, :]
    s_m1 = st[:, 1:2, :]
    t_io = lax.broadcasted_iota(jnp.int32, (bc, t_len, D_MODEL), 1)
    um1 = jnp.where(t_io == 0, s_m1, pltpu.roll(u3, 1, 1))
    um2 = jnp.where(t_io == 0, s_m2, jnp.where(t_io == 1, s_m1, pltpu.roll(u3, 2, 1)))
    cw = cw_ref[...]
    conv = (cw[0:1] * um2 + cw[1:2] * um1 + cw[2:3] * u3).reshape(rows, D_MODEL)
    nc_ref[...] = u3[:, t_len - (CONV_K - 1):, :]

    sza_ref[...] = _silu(_dot(xn, wza_ref[...]))
    sga_ref[...] = jax.nn.sigmoid(_dot(xn, wg_ref[:, G_GA:G_GB]))
    br_b = _branch_b(xn, conv, wg_ref, wpb_ref)
    tb_ref[...] = jax.nn.sigmoid(_dot(xn, wg_ref[:, G_GB:G_END])) * br_b


def _sample_pre_call(x, tabs, tabs_t, st, nw, qnw, knwc, cw, wq, wkt, wvt, wza, wg, wpb, bd):
    rows = TM_SAMPLE
    bc = rows // DEC_SEQ
    n_rows = DEC_BATCH * DEC_SEQ
    row_spec = pl.BlockSpec((rows, D_MODEL), lambda i: (i, 0))
    col_spec = pl.BlockSpec((KV_WIDTH, rows), lambda i: (0, i))
    st_spec = pl.BlockSpec((bc, CONV_K - 1, D_MODEL), lambda i: (i, 0, 0))
    consts_a = (tabs[0], tabs[1], tabs_t[0], tabs_t[1])
    consts_b = (nw, qnw, knwc, cw, wq, wkt, wvt, wza, wg, wpb, bd)
    in_specs = [row_spec] + [_const_spec(a.shape) for a in consts_a] + [st_spec]
    in_specs += [_const_spec(a.shape) for a in consts_b]
    out_specs = [row_spec, col_spec, col_spec, row_spec, row_spec, row_spec, st_spec]
    rows_sds = jax.ShapeDtypeStruct((n_rows, D_MODEL), F32)
    cols_sds = jax.ShapeDtypeStruct((KV_WIDTH, n_rows), F32)
    out_shape = [rows_sds, cols_sds, cols_sds, rows_sds, rows_sds, rows_sds,
                 jax.ShapeDtypeStruct((DEC_BATCH, CONV_K - 1, D_MODEL), F32)]
    return pl.pallas_call(
        functools.partial(_sample_pre_kernel, rows=rows),
        grid=(n_rows // rows,),
        in_specs=in_specs,
        out_specs=out_specs,
        out_shape=out_shape,
        compiler_params=_params(),
        name="sample_pre",
    )(x, *consts_a, st, *consts_b)


def _sample_attn_kernel(x_ref, q_ref, kt_ref, vt_ref, ck_ref, cv_ref, sza_ref, sga_ref, tb_ref,
                        skcol_ref, wpa_ref, wo_ref,
                        y_ref, nk_ref, nv_ref, oa_s, *, bc):
    t_len = DEC_SEQ
    n_exp = GROUP * N_KV_HEADS * t_len
    ktc = kt_ref[...]
    vtc = vt_ref[...]
    ktc_b = ktc.astype(BF16)
    vtc_b = vtc.astype(BF16)
    lane = lax.broadcasted_iota(jnp.int32, (1, KV_WIDTH), 1)
    vgroup = lane >> 6
    t_io = lax.broadcasted_iota(jnp.int32, (n_exp, WINDOW), 0) & (t_len - 1)
    c_io = lax.broadcasted_iota(jnp.int32, (n_exp, WINDOW), 1)
    old_ok = c_io >= t_io
    c_minus_t = c_io - t_io
    new_lane = lax.broadcasted_iota(jnp.int32, (KV_WIDTH, WINDOW), 1) >= WINDOW - t_len
    skcol = skcol_ref[...]

    def per_batch(n, carry):
        r0 = pl.multiple_of(n * t_len, t_len)
        qn = q_ref[pl.ds(r0, t_len), :]
        q_exp = jnp.concatenate(
            [jnp.where(vgroup == g, qn[:, 256 * j:256 * (j + 1)], 0.0)
             for j in range(GROUP) for g in range(N_KV_HEADS)], axis=0).astype(BF16)
        kt_old = ck_ref[n]
        vt_old = cv_ref[n]
        keys = jnp.concatenate([kt_old.astype(BF16), ktc_b], axis=1)
        vals = jnp.concatenate([vt_old.astype(BF16), vtc_b], axis=1)
        s = _dot(q_exp, keys)
        new_ok = (c_io >= r0) & (c_minus_t <= r0)
        s = jnp.where(jnp.concatenate([old_ok, new_ok], axis=1), s, NEG_INF)
        p, inv = _softmax_parts(s, skcol)
        o = _dot_nt(p.astype(BF16), vals) * inv
        outs = []
        for j in range(GROUP):
            acc = None
            for g in range(N_KV_HEADS):
                r = (j * N_KV_HEADS + g) * t_len
                piece = jnp.where(vgroup == g, o[r:r + t_len], 0.0)
                acc = piece if acc is None else acc + piece
            outs.append(acc)
        oa_s[pl.ds(r0, t_len), :] = jnp.concatenate(outs, axis=1)
        shift_new = WINDOW - t_len - r0
        nk_ref[n] = jnp.where(new_lane, pltpu.roll(ktc, shift_new, 1),
                              pltpu.roll(kt_old, WINDOW - t_len, 1))
        nv_ref[n] = jnp.where(new_lane, pltpu.roll(vtc, shift_new, 1),
                              pltpu.roll(vt_old, WINDOW - t_len, 1))
        return carry

    lax.fori_loop(0, bc, per_batch, 0)

    a_in = (oa_s[...] * sza_ref[...]).astype(BF16)
    br_a = _dot(a_in, wpa_ref[...])
    mixed = (sga_ref[...] * br_a + tb_ref[...]).astype(BF16)
    y_ref[...] = x_ref[...] + _dot(mixed, wo_ref[...])


def _sample_attn_call(x, q, kt, vt, ck, cv, sza, sga, tb, skcol, wpa, wo):
    bc = BC_SAMPLE
    rows = bc * DEC_SEQ
    row_spec = pl.BlockSpec((rows, D_MODEL), lambda i: (i, 0))
    col_spec = pl.BlockSpec((KV_WIDTH, rows), lambda i: (0, i))
    cache_spec = pl.BlockSpec((bc, KV_WIDTH, WINDOW), lambda i: (i, 0, 0))
    in_specs = [row_spec, row_spec, col_spec, col_spec, cache_spec, cache_spec,
                row_spec, row_spec, row_spec,
                _const_spec(skcol.shape), _const_spec(wpa.shape), _const_spec(wo.shape)]
    out_specs = [row_spec, cache_spec, cache_spec]
    cache_sds = jax.ShapeDtypeStruct((DEC_BATCH, KV_WIDTH, WINDOW), F32)
    out_shape = [jax.ShapeDtypeStruct((DEC_BATCH * DEC_SEQ, D_MODEL), F32), cache_sds, cache_sds]
    return pl.pallas_call(
        functools.partial(_sample_attn_kernel, bc=bc),
        grid=(DEC_BATCH // bc,),
        in_specs=in_specs,
        out_specs=out_specs,
        out_shape=out_shape,
        scratch_shapes=[pltpu.VMEM((rows, D_MODEL), F32)],
        compiler_params=_params(),
        name="sample_attn",
    )(x, q, kt, vt, ck, cv, sza, sga, tb, skcol, wpa, wo)


def _rope_angles(positions):
    inv = np.power(ROPE_THETA, -np.arange(HALF, dtype=np.float64) * (2.0 / HEAD_DIM))
    return np.asarray(positions, dtype=np.float64)[:, None] * inv[None, :]


def _rope_tables(positions, signed):
    ang = _rope_angles(positions)
    cos = np.tile(np.cos(ang), (1, LANES // HALF))
    sin = np.tile(np.sin(ang), (1, LANES // HALF))
    if signed:
        sign = np.where((np.arange(LANES) % HEAD_DIM) < HALF, -1.0, 1.0)
        sin = sin * sign[None, :]
    return jnp.asarray(cos, F32), jnp.asarray(sin, F32)


def _rope_tables_rows(positions):
    ang = _rope_angles(positions).T
    return jnp.asarray(np.cos(ang), F32), jnp.asarray(np.sin(ang), F32)


def _head_mean_matrix(split_half):
    lane = np.arange(256)
    grp = (lane % LANES) // HALF if split_half else lane // HEAD_DIM
    return jnp.asarray((grp[:, None] == grp[None, :]) / float(HEAD_DIM), BF16)


def kernel(x_prompt, x_sample, cache_k, cache_v, state_conv, meta_tokens, norm_w, w_in,
           q_norm_w, k_norm_w, sinks, conv_w, w_proj_a, w_proj_b, w_out):
    assert x_prompt.shape == (1, SEQ, D_MODEL) and x_sample.shape == (DEC_BATCH, DEC_SEQ, D_MODEL)
    assert w_in.shape[0] == 1, "single layer"
    w = w_in[0].astype(BF16)
    o_k = D_MODEL
    o_v = o_k + KV_WIDTH
    o_za = o_v + KV_WIDTH
    o_g = o_za + D_MODEL
    wq5 = w[:, :o_k].reshape(D_MODEL, N_KV_HEADS, GROUP, 2, HALF)
    wk_nat = w[:, o_k:o_v]
    wv = w[:, o_v:o_za]
    wq_split = wq5.transpose(0, 2, 3, 1, 4).reshape(D_MODEL, D_MODEL)
    wq_nat = wq5.transpose(0, 2, 1, 3, 4).reshape(D_MODEL, D_MODEL)
    wk_split = (wk_nat.reshape(D_MODEL, N_KV_HEADS, 2, HALF)
                .transpose(0, 2, 1, 3).reshape(D_MODEL, KV_WIDTH))
    wza = (w[:, o_za:o_g].reshape(D_MODEL, N_KV_HEADS, GROUP, HEAD_DIM)
           .transpose(0, 2, 1, 3).reshape(D_MODEL, D_MODEL))
    wg = w[:, o_g:]
    wpa = (w_proj_a[0].astype(BF16).reshape(N_KV_HEADS, GROUP, HEAD_DIM, D_MODEL)
           .transpose(1, 0, 2, 3).reshape(D_MODEL, D_MODEL))
    wpb = w_proj_b[0].astype(BF16)
    wo = w_out[0].astype(BF16)

    nw = norm_w[0].reshape(1, D_MODEL)
    cw = conv_w[0]
    qn = q_norm_w[0]
    kn = k_norm_w[0]
    split = lambda v: jnp.concatenate([jnp.tile(v[:HALF], 4), jnp.tile(v[HALF:], 4)]).reshape(1, 256)
    nat = lambda v: jnp.tile(v, 4).reshape(1, 256)
    sk = sinks[0].astype(F32)

    tabs_p = _rope_tables(np.arange(SEQ) + N_META, signed=False)
    tabs_0 = _rope_tables(np.arange(BLOCK) - LEAD, signed=False)
    x0 = jnp.concatenate([jnp.zeros((LEAD, D_MODEL), x_prompt.dtype),
                          meta_tokens.astype(x_prompt.dtype)], axis=0)
    y_p, nk_p, nv_p, nc_p = _prompt_call(
        x_prompt[0], tabs_p, x0, tabs_0, nw, split(qn), split(kn), cw, sk,
        wq_split, wk_split, wv, wza, wg, wpa, wpb, wo, _head_mean_matrix(True))
    nk_p = nk_p.reshape(BLOCK, 2, N_KV_HEADS, HALF).transpose(0, 2, 1, 3)

    pos_s = PAST_LEN + (np.arange(TM_SAMPLE) % DEC_SEQ)
    xs = x_sample.reshape(DEC_BATCH * DEC_SEQ, D_MODEL)
    q_s, kt_s, vt_s, sza, sga, tb, nc_s = _sample_pre_call(
        xs, _rope_tables(pos_s, signed=True), _rope_tables_rows(pos_s), state_conv[0],
        nw, nat(qn), jnp.tile(kn, N_KV_HEADS).reshape(KV_WIDTH, 1), cw,
        wq_nat, wk_nat.T, wv.T, wza, wg, wpb, _head_mean_matrix(False))
    skcol = jnp.repeat(sk.reshape(N_KV_HEADS, GROUP).T.reshape(-1), DEC_SEQ).reshape(-1, 1)
    to_tiles = lambda c: c[0].transpose(0, 2, 3, 1).reshape(DEC_BATCH, KV_WIDTH, WINDOW)
    y_s, nk_s, nv_s = _sample_attn_call(
        xs, q_s, kt_s, vt_s, to_tiles(cache_k), to_tiles(cache_v), sza, sga, tb, skcol, wpa, wo)
    from_tiles = lambda c: (c.reshape(DEC_BATCH, N_KV_HEADS, HEAD_DIM, WINDOW)
                            .transpose(0, 3, 1, 2)[None])

    return (y_p.reshape(1, SEQ, D_MODEL),
            y_s.reshape(DEC_BATCH, DEC_SEQ, D_MODEL),
            nk_p.reshape(1, 1, WINDOW, N_KV_HEADS, HEAD_DIM),
            nv_p.reshape(1, 1, WINDOW, N_KV_HEADS, HEAD_DIM),
            nc_p.reshape(1, 1, CONV_K - 1, D_MODEL),
            from_tiles(nk_s),
            from_tiles(nv_s),
            nc_s.reshape(1, DEC_BATCH, CONV_K - 1, D_MODEL))
```

```python
import functools

import numpy as np
import jax
import jax.numpy as jnp
from jax import lax
from jax.experimental import pallas as pl
from jax.experimental.pallas import tpu as pltpu

D_MODEL = 1024
SEQ = 16384
DEC_BATCH = 128
DEC_SEQ = 8
PAST_LEN = 16384
N_HEADS = 16
N_KV_HEADS = 4
GROUP = N_HEADS // N_KV_HEADS
HEAD_DIM = 64
HALF = HEAD_DIM // 2
KV_WIDTH = N_KV_HEADS * HEAD_DIM
WINDOW = 128
BLOCK = 128
ROPE_THETA = 10000.0
CONV_K = 3
N_META = 16
LEAD = BLOCK - N_META
EPS = 1e-6
NEG_INF = -1e30
Q_SCALE = HEAD_DIM ** -0.5

G_B = 0
G_C = G_B + D_MODEL
G_H = G_C + D_MODEL
G_ZB = G_H + D_MODEL
G_GA = G_ZB + D_MODEL
G_GB = G_GA + D_MODEL
G_END = G_GB + D_MODEL

LANES = 128
SUBLANES = 8
VMEM_BYTES_V7X = 64 * 1024 * 1024

TM_PROMPT = 512
TM_SAMPLE = 256
BC_SAMPLE = LANES // DEC_SEQ
VMEM_LIMIT = VMEM_BYTES_V7X - 6 * 1024 * 1024

BF16 = jnp.bfloat16
F32 = jnp.float32


def _dot(a, b):
    return jnp.dot(a, b, preferred_element_type=F32)


def _dot_nt(a, b):
    return lax.dot_general(a, b, (((1,), (1,)), ((), ())), preferred_element_type=F32)


def _rms_rows(x, w):
    ms = jnp.mean(x * x, axis=-1, keepdims=True)
    return (x * lax.rsqrt(ms + EPS)) * w


def _head_rms(t, bd, w):
    n = t.shape[1] // 256
    r = t.shape[0]
    sq = (t * t).astype(BF16)
    stacked = jnp.concatenate([sq[:, 256 * c:256 * (c + 1)] for c in range(n)], axis=0)
    ms = _dot(stacked, bd)
    ms = jnp.concatenate([ms[r * c:r * (c + 1)] for c in range(n)], axis=1)
    wt = jnp.concatenate([w] * n, axis=1)
    return (t * lax.rsqrt(ms + EPS)) * wt


def _rope_split(t, cos, sin):
    out = []
    for c in range(t.shape[1] // 256):
        a = t[:, 256 * c:256 * c + LANES]
        b = t[:, 256 * c + LANES:256 * (c + 1)]
        out.append(a * cos - b * sin)
        out.append(b * cos + a * sin)
    return jnp.concatenate(out, axis=1)


def _rope_natural(t, cos, sin_signed):
    lane = lax.broadcasted_iota(jnp.int32, (t.shape[0], LANES), 1)
    first_half = (lane & (HEAD_DIM - 1)) < HALF
    out = []
    for c in range(t.shape[1] // LANES):
        tc = t[:, LANES * c:LANES * (c + 1)]
        from_below = pltpu.roll(tc, HALF, 1)
        from_above = pltpu.roll(tc, LANES - HALF, 1)
        out.append(tc * cos + jnp.where(first_half, from_above, from_below) * sin_signed)
    return jnp.concatenate(out, axis=1)


def _rope_rows(t, cos, sin):
    out = []
    for g in range(t.shape[0] // HEAD_DIM):
        a = t[HEAD_DIM * g:HEAD_DIM * g + HALF]
        b = t[HEAD_DIM * g + HALF:HEAD_DIM * (g + 1)]
        out.append(a * cos - b * sin)
        out.append(b * cos + a * sin)
    return jnp.concatenate(out, axis=0)


def _lane_group_masks(group_of_lane, dtype):
    return [(group_of_lane == g).astype(dtype) for g in range(N_KV_HEADS)]


def _softmax_parts(s, sk):
    m = jnp.maximum(jnp.max(s, axis=-1, keepdims=True), sk)
    p = jnp.exp(s - m)
    den = jnp.sum(p, axis=-1, keepdims=True) + jnp.exp(sk - m)
    return p, 1.0 / den


def _silu(z):
    return z * jax.nn.sigmoid(z)


def _branch_b(xn, conv, wg_ref, wpb_ref):
    b = _dot(xn, wg_ref[:, G_B:G_C])
    z_b = _dot(xn, wg_ref[:, G_ZB:G_GA])
    b_in = ((b * conv) * _silu(z_b)).astype(BF16)
    return _dot(b_in, wpb_ref[...])


def _prompt_kernel(sinks_ref, x_ref, cos_ref, sin_ref, x0_ref, cos0_ref, sin0_ref, nw_ref,
                   qnw_ref, knw_ref, cw_ref, wq_ref, wk_ref, wv_ref, wza_ref, wg_ref, wpa_ref,
                   wpb_ref, wo_ref, bd_ref,
                   y_ref, nk_ref, nv_ref, nc_ref, kprev, vprev, utail, *, tm):
    i = pl.program_id(0)
    nw = nw_ref[...]
    bd = bd_ref[...]
    knw = knw_ref[...]

    @pl.when(i == 0)
    def _init():
        x0 = x0_ref[...]
        xn0 = _rms_rows(x0, nw).astype(BF16)
        k0 = _dot(xn0, wk_ref[...])
        k0 = _rope_split(_head_rms(k0, bd, knw), cos0_ref[...], sin0_ref[...])
        v0 = _dot(xn0, wv_ref[...])
        c0 = _dot(xn0, wg_ref[:, G_C:G_H])
        h0 = _dot(xn0, wg_ref[:, G_H:G_ZB])
        row = lax.broadcasted_iota(jnp.int32, (BLOCK, D_MODEL), 0)
        u0 = jnp.where(row >= LEAD, c0 * h0, 0.0)
        kprev[...] = k0
        vprev[...] = v0
        utail[...] = u0[BLOCK - SUBLANES:BLOCK]

    x = x_ref[...]
    xn = _rms_rows(x, nw).astype(BF16)
    cos = cos_ref[...]
    sin = sin_ref[...]

    q = _head_rms(_dot(xn, wq_ref[...]), bd, qnw_ref[...])
    q = _rope_split(q, cos * Q_SCALE, sin * Q_SCALE).astype(BF16)
    k = _rope_split(_head_rms(_dot(xn, wk_ref[...]), bd, knw), cos, sin)
    v = _dot(xn, wv_ref[...])
    kb = k.astype(BF16)
    vb = v.astype(BF16)

    lane = lax.broadcasted_iota(jnp.int32, (1, 2 * BLOCK), 1)
    kmasks = _lane_group_masks((lane & (LANES - 1)) >> 5, BF16)
    vmasks = _lane_group_masks(lane >> 6, BF16)
    vgroup = lane >> 6
    r_io = lax.broadcasted_iota(jnp.int32, (BLOCK, 2 * BLOCK), 0)
    c_io = lax.broadcasted_iota(jnp.int32, (BLOCK, 2 * BLOCK), 1)
    band = (c_io >= r_io) & (c_io <= r_io + WINDOW)
    first_band = band & (c_io >= jnp.where(i == 0, LEAD, 0))

    o_blocks = []
    for b in range(tm // BLOCK):
        lo = b * BLOCK
        if b == 0:
            kpair = jnp.concatenate([kprev[...].astype(BF16), kb[:BLOCK]], axis=0)
            vpair = jnp.concatenate([vprev[...].astype(BF16), vb[:BLOCK]], axis=0)
            mask = first_band
        else:
            kpair = kb[lo - BLOCK:lo + BLOCK]
            vpair = vb[lo - BLOCK:lo + BLOCK]
            mask = band
        kbd = jnp.concatenate([kpair * km for km in kmasks], axis=0)
        vbd = jnp.concatenate([vpair * vm for vm in vmasks], axis=0)
        qs = jnp.concatenate([q[lo:lo + BLOCK, 256 * j:256 * (j + 1)] for j in range(GROUP)],
                             axis=0)
        s = _dot_nt(qs, kbd)
        p_rows = []
        mults = []
        for j in range(GROUP):
            p_cols = []
            invs = []
            for g in range(N_KV_HEADS):
                sl = s[BLOCK * j:BLOCK * (j + 1), 256 * g:256 * (g + 1)]
                sl = jnp.where(mask, sl, NEG_INF)
                p, inv = _softmax_parts(sl, sinks_ref[GROUP * g + j])
                p_cols.append(p.astype(BF16))
                invs.append(inv)
            p_rows.append(jnp.concatenate(p_cols, axis=1))
            mults.append(jnp.where(vgroup == 0, invs[0],
                                   jnp.where(vgroup == 1, invs[1],
                                             jnp.where(vgroup == 2, invs[2], invs[3]))))
        o = _dot(jnp.concatenate(p_rows, axis=0), vbd)
        o_blocks.append(jnp.concatenate(
            [o[BLOCK * j:BLOCK * (j + 1)] * mults[j] for j in range(GROUP)], axis=1))
    o_a = o_blocks[0] if len(o_blocks) == 1 else jnp.concatenate(o_blocks, axis=0)

    kprev[...] = k[tm - BLOCK:]
    vprev[...] = v[tm - BLOCK:]
    nk_ref[...] = k[tm - BLOCK:]
    nv_ref[...] = v[tm - BLOCK:]

    c = _dot(xn, wg_ref[:, G_C:G_H])
    hc = _dot(xn, wg_ref[:, G_H:G_ZB])
    u = c * hc
    tail = utail[...]
    prev1 = tail[SUBLANES - 1:SUBLANES]
    prev2 = tail[SUBLANES - 2:SUBLANES - 1]
    row8 = lax.broadcasted_iota(jnp.int32, (SUBLANES, D_MODEL), 0)
    r1 = pltpu.roll(u, 1, 0)
    r2 = pltpu.roll(u, 2, 0)
    um1 = jnp.concatenate(
        [jnp.where(row8 == 0, prev1, r1[:SUBLANES]), r1[SUBLANES:]], axis=0)
    um2 = jnp.concatenate(
        [jnp.where(row8 == 0, prev2, jnp.where(row8 == 1, prev1, r2[:SUBLANES])),
         r2[SUBLANES:]], axis=0)
    cw = cw_ref[...]
    conv = cw[0:1] * um2 + cw[1:2] * um1 + cw[2:3] * u
    utail[...] = u[tm - SUBLANES:]
    nc_ref[...] = u[tm - (CONV_K - 1):]

    z_a = _dot(xn, wza_ref[...])
    br_a = _dot((o_a * _silu(z_a)).astype(BF16), wpa_ref[...])
    br_b = _branch_b(xn, conv, wg_ref, wpb_ref)
    g_a = _dot(xn, wg_ref[:, G_GA:G_GB])
    g_b = _dot(xn, wg_ref[:, G_GB:G_END])
    mixed = (jax.nn.sigmoid(g_a) * br_a + jax.nn.sigmoid(g_b) * br_b).astype(BF16)
    y_ref[...] = x + _dot(mixed, wo_ref[...])


def _const_spec(shape):
    zeros = (0,) * len(shape)
    return pl.BlockSpec(shape, lambda i: zeros, pipeline_mode=pl.Buffered(1))


def _params():
    return pltpu.CompilerParams(dimension_semantics=("arbitrary",), vmem_limit_bytes=VMEM_LIMIT)


def _prompt_call(x, tabs, x0, tabs0, nw, qnw, knw, cw, sinks, wq, wk, wv, wza, wg, wpa, wpb, wo, bd):
    tm = TM_PROMPT
    row_spec = lambda w: pl.BlockSpec((tm, w), lambda i: (i, 0))
    consts = (x0, tabs0[0], tabs0[1], nw, qnw, knw, cw, wq, wk, wv, wza, wg, wpa, wpb, wo, bd)
    in_specs = [pl.BlockSpec(memory_space=pltpu.SMEM),
                row_spec(D_MODEL), row_spec(LANES), row_spec(LANES)]
    in_specs += [_const_spec(a.shape) for a in consts]
    resident = lambda shape: pl.BlockSpec(shape, lambda i: (0, 0))
    out_specs = [row_spec(D_MODEL), resident((BLOCK, KV_WIDTH)), resident((BLOCK, KV_WIDTH)),
                 resident((CONV_K - 1, D_MODEL))]
    out_shape = [jax.ShapeDtypeStruct((SEQ, D_MODEL), F32),
                 jax.ShapeDtypeStruct((BLOCK, KV_WIDTH), F32),
                 jax.ShapeDtypeStruct((BLOCK, KV_WIDTH), F32),
                 jax.ShapeDtypeStruct((CONV_K - 1, D_MODEL), F32)]
    return pl.pallas_call(
        functools.partial(_prompt_kernel, tm=tm),
        grid=(SEQ // tm,),
        in_specs=in_specs,
        out_specs=out_specs,
        out_shape=out_shape,
        scratch_shapes=[pltpu.VMEM((BLOCK, KV_WIDTH), F32),
                        pltpu.VMEM((BLOCK, KV_WIDTH), F32),
                        pltpu.VMEM((SUBLANES, D_MODEL), F32)],
        compiler_params=_params(),
        name="prompt_layer",
    )(sinks, x, tabs[0], tabs[1], *consts)


def _sample_pre_kernel(x_ref, cos_ref, sin_ref, cost_ref, sint_ref, st_ref, nw_ref, qnw_ref,
                       knwc_ref, cw_ref, wq_ref, wkt_ref, wvt_ref, wza_ref, wg_ref, wpb_ref, bd_ref,
                       q_ref, kt_ref, vt_ref, sza_ref, sga_ref, tb_ref, nc_ref, *, rows):
    t_len = DEC_SEQ
    bc = rows // t_len
    bd = bd_ref[...]
    x = x_ref[...]
    xn = _rms_rows(x, nw_ref[...]).astype(BF16)

    q = _head_rms(_dot(xn, wq_ref[...]), bd, qnw_ref[...])
    q_ref[...] = _rope_natural(q, cos_ref[...] * Q_SCALE, sin_ref[...] * Q_SCALE)

    kt = _dot_nt(wkt_ref[...], xn)
    ms = _dot(bd, (kt * kt).astype(BF16))
    kt = (kt * lax.rsqrt(ms + EPS)) * knwc_ref[...]
    kt_ref[...] = _rope_rows(kt, cost_ref[...], sint_ref[...])
    vt_ref[...] = _dot_nt(wvt_ref[...], xn)

    c = _dot(xn, wg_ref[:, G_C:G_H])
    hc = _dot(xn, wg_ref[:, G_H:G_ZB])
    u3 = (c * hc).reshape(bc, t_len, D_MODEL)
    st = st_ref[...]
    s_m2 = st[:, 0:1, :]
    s_m1 = st[:, 1:2, :]
    t_io = lax.broadcasted_iota(jnp.int32, (bc, t_len, D_MODEL), 1)
    um1 = jnp.where(t_io == 0, s_m1, pltpu.roll(u3, 1, 1))
    um2 = jnp.where(t_io == 0, s_m2, jnp.where(t_io == 1, s_m1, pltpu.roll(u3, 2, 1)))
    cw = cw_ref[...]
    conv = (cw[0:1] * um2 + cw[1:2] * um1 + cw[2:3] * u3).reshape(rows, D_MODEL)
    nc_ref[...] = u3[:, t_len - (CONV_K - 1):, :]

    sza_ref[...] = _silu(_dot(xn, wza_ref[...]))
    sga_ref[...] = jax.nn.sigmoid(_dot(xn, wg_ref[:, G_GA:G_GB]))
    br_b = _branch_b(xn, conv, wg_ref, wpb_ref)
    tb_ref[...] = jax.nn.sigmoid(_dot(xn, wg_ref[:, G_GB:G_END])) * br_b


def _sample_pre_call(x, tabs, tabs_t, st, nw, qnw, knwc, cw, wq, wkt, wvt, wza, wg, wpb, bd):
    rows = TM_SAMPLE
    bc = rows // DEC_SEQ
    n_rows = DEC_BATCH * DEC_SEQ
    row_spec = pl.BlockSpec((rows, D_MODEL), lambda i: (i, 0))
    col_spec = pl.BlockSpec((KV_WIDTH, rows), lambda i: (0, i))
    st_spec = pl.BlockSpec((bc, CONV_K - 1, D_MODEL), lambda i: (i, 0, 0))
    consts_a = (tabs[0], tabs[1], tabs_t[0], tabs_t[1])
    consts_b = (nw, qnw, knwc, cw, wq, wkt, wvt, wza, wg, wpb, bd)
    in_specs = [row_spec] + [_const_spec(a.shape) for a in consts_a] + [st_spec]
    in_specs += [_const_spec(a.shape) for a in consts_b]
    out_specs = [row_spec, col_spec, col_spec, row_spec, row_spec, row_spec, st_spec]
    rows_sds = jax.ShapeDtypeStruct((n_rows, D_MODEL), F32)
    cols_sds = jax.ShapeDtypeStruct((KV_WIDTH, n_rows), F32)
    out_shape = [rows_sds, cols_sds, cols_sds, rows_sds, rows_sds, rows_sds,
                 jax.ShapeDtypeStruct((DEC_BATCH, CONV_K - 1, D_MODEL), F32)]
    return pl.pallas_call(
        functools.partial(_sample_pre_kernel, rows=rows),
        grid=(n_rows // rows,),
        in_specs=in_specs,
        out_specs=out_specs,
        out_shape=out_shape,
        compiler_params=_params(),
        name="sample_pre",
    )(x, *consts_a, st, *consts_b)


def _sample_attn_kernel(x_ref, q_ref, kt_ref, vt_ref, ck_ref, cv_ref, sza_ref, sga_ref, tb_ref,
                        skcol_ref, wpa_ref, wo_ref,
                        y_ref, nk_ref, nv_ref, oa_s, *, bc):
    t_len = DEC_SEQ
    n_exp = GROUP * N_KV_HEADS * t_len
    ktc = kt_ref[...]
    vtc = vt_ref[...]
    ktc_b = ktc.astype(BF16)
    vtc_b = vtc.astype(BF16)
    lane = lax.broadcasted_iota(jnp.int32, (1, KV_WIDTH), 1)
    vgroup = lane >> 6
    t_io = lax.broadcasted_iota(jnp.int32, (n_exp, WINDOW), 0) & (t_len - 1)
    c_io = lax.broadcasted_iota(jnp.int32, (n_exp, WINDOW), 1)
    old_ok = c_io >= t_io
    c_minus_t = c_io - t_io
    new_lane = lax.broadcasted_iota(jnp.int32, (KV_WIDTH, WINDOW), 1) >= WINDOW - t_len
    skcol = skcol_ref[...]

    def per_batch(n):
        r0 = n * t_len
        qn = q_ref[r0:r0 + t_len, :]
        q_exp = jnp.concatenate(
            [jnp.where(vgroup == g, qn[:, 256 * j:256 * (j + 1)], 0.0)
             for j in range(GROUP) for g in range(N_KV_HEADS)], axis=0).astype(BF16)
        kt_old = ck_ref[n]
        vt_old = cv_ref[n]
        keys = jnp.concatenate([kt_old.astype(BF16), ktc_b], axis=1)
        vals = jnp.concatenate([vt_old.astype(BF16), vtc_b], axis=1)
        s = _dot(q_exp, keys)
        new_ok = (c_io >= r0) & (c_minus_t <= r0)
        s = jnp.where(jnp.concatenate([old_ok, new_ok], axis=1), s, NEG_INF)
        p, inv = _softmax_parts(s, skcol)
        o = _dot_nt(p.astype(BF16), vals) * inv
        outs = []
        for j in range(GROUP):
            acc = None
            for g in range(N_KV_HEADS):
                r = (j * N_KV_HEADS + g) * t_len
                piece = jnp.where(vgroup == g, o[r:r + t_len], 0.0)
                acc = piece if acc is None else acc + piece
            outs.append(acc)
        oa_s[r0:r0 + t_len, :] = jnp.concatenate(outs, axis=1)
        shift_new = WINDOW - t_len - r0
        nk_ref[n] = jnp.where(new_lane, pltpu.roll(ktc, shift_new, 1),
                              pltpu.roll(kt_old, WINDOW - t_len, 1))
        nv_ref[n] = jnp.where(new_lane, pltpu.roll(vtc, shift_new, 1),
                              pltpu.roll(vt_old, WINDOW - t_len, 1))

    for n in range(bc):
        per_batch(n)

    a_in = (oa_s[...] * sza_ref[...]).astype(BF16)
    br_a = _dot(a_in, wpa_ref[...])
    mixed = (sga_ref[...] * br_a + tb_ref[...]).astype(BF16)
    y_ref[...] = x_ref[...] + _dot(mixed, wo_ref[...])


def _sample_attn_call(x, q, kt, vt, ck, cv, sza, sga, tb, skcol, wpa, wo):
    bc = BC_SAMPLE
    rows = bc * DEC_SEQ
    row_spec = pl.BlockSpec((rows, D_MODEL), lambda i: (i, 0))
    col_spec = pl.BlockSpec((KV_WIDTH, rows), lambda i: (0, i))
    cache_spec = pl.BlockSpec((bc, KV_WIDTH, WINDOW), lambda i: (i, 0, 0))
    in_specs = [row_spec, row_spec, col_spec, col_spec, cache_spec, cache_spec,
                row_spec, row_spec, row_spec,
                _const_spec(skcol.shape), _const_spec(wpa.shape), _const_spec(wo.shape)]
    out_specs = [row_spec, cache_spec, cache_spec]
    cache_sds = jax.ShapeDtypeStruct((DEC_BATCH, KV_WIDTH, WINDOW), F32)
    out_shape = [jax.ShapeDtypeStruct((DEC_BATCH * DEC_SEQ, D_MODEL), F32), cache_sds, cache_sds]
    return pl.pallas_call(
        functools.partial(_sample_attn_kernel, bc=bc),
        grid=(DEC_BATCH // bc,),
        in_specs=in_specs,
        out_specs=out_specs,
        out_shape=out_shape,
        scratch_shapes=[pltpu.VMEM((rows, D_MODEL), F32)],
        compiler_params=_params(),
        name="sample_attn",
    )(x, q, kt, vt, ck, cv, sza, sga, tb, skcol, wpa, wo)


def _rope_angles(positions):
    inv = np.power(ROPE_THETA, -np.arange(HALF, dtype=np.float64) * (2.0 / HEAD_DIM))
    return np.asarray(positions, dtype=np.float64)[:, None] * inv[None, :]


def _rope_tables(positions, signed):
    ang = _rope_angles(positions)
    cos = np.tile(np.cos(ang), (1, LANES // HALF))
    sin = np.tile(np.sin(ang), (1, LANES // HALF))
    if signed:
        sign = np.where((np.arange(LANES) % HEAD_DIM) < HALF, -1.0, 1.0)
        sin = sin * sign[None, :]
    return jnp.asarray(cos, F32), jnp.asarray(sin, F32)


def _rope_tables_rows(positions):
    ang = _rope_angles(positions).T
    return jnp.asarray(np.cos(ang), F32), jnp.asarray(np.sin(ang), F32)


def _head_mean_matrix(split_half):
    lane = np.arange(256)
    grp = (lane % LANES) // HALF if split_half else lane // HEAD_DIM
    return jnp.asarray((grp[:, None] == grp[None, :]) / float(HEAD_DIM), BF16)


def kernel(x_prompt, x_sample, cache_k, cache_v, state_conv, meta_tokens, norm_w, w_in,
           q_norm_w, k_norm_w, sinks, conv_w, w_proj_a, w_proj_b, w_out):
    assert x_prompt.shape == (1, SEQ, D_MODEL) and x_sample.shape == (DEC_BATCH, DEC_SEQ, D_MODEL)
    assert w_in.shape[0] == 1, "single layer"
    o_k = D_MODEL
    o_v = o_k + KV_WIDTH
    o_za = o_v + KV_WIDTH
    o_g = o_za + D_MODEL
    w = w_in[0, :, :o_g].astype(BF16)
    wg = w_in[0, :, o_g:].astype(BF16)
    wq5 = w[:, :o_k].reshape(D_MODEL, N_KV_HEADS, GROUP, 2, HALF)
    wk_nat = w[:, o_k:o_v]
    wv = w[:, o_v:o_za]
    wq_split = wq5.transpose(0, 2, 3, 1, 4).reshape(D_MODEL, D_MODEL)
    wq_nat = wq5.transpose(0, 2, 1, 3, 4).reshape(D_MODEL, D_MODEL)
    wk_split = (wk_nat.reshape(D_MODEL, N_KV_HEADS, 2, HALF)
                .transpose(0, 2, 1, 3).reshape(D_MODEL, KV_WIDTH))
    wza = (w[:, o_za:o_g].reshape(D_MODEL, N_KV_HEADS, GROUP, HEAD_DIM)
           .transpose(0, 2, 1, 3).reshape(D_MODEL, D_MODEL))
    wpa = (w_proj_a[0].astype(BF16).reshape(N_KV_HEADS, GROUP, HEAD_DIM, D_MODEL)
           .transpose(1, 0, 2, 3).reshape(D_MODEL, D_MODEL))
    wpb = w_proj_b[0].astype(BF16)
    wo = w_out[0].astype(BF16)

    nw = norm_w[0].reshape(1, D_MODEL)
    cw = conv_w[0]
    qn = q_norm_w[0]
    kn = k_norm_w[0]
    split = lambda v: jnp.concatenate([jnp.tile(v[:HALF], 4), jnp.tile(v[HALF:], 4)]).reshape(1, 256)
    nat = lambda v: jnp.tile(v, 4).reshape(1, 256)
    sk = sinks[0].astype(F32)

    tabs_p = _rope_tables(np.arange(SEQ) + N_META, signed=False)
    tabs_0 = _rope_tables(np.arange(BLOCK) - LEAD, signed=False)
    x0 = jnp.concatenate([jnp.zeros((LEAD, D_MODEL), x_prompt.dtype),
                          meta_tokens.astype(x_prompt.dtype)], axis=0)
    y_p, nk_p, nv_p, nc_p = _prompt_call(
        x_prompt[0], tabs_p, x0, tabs_0, nw, split(qn), split(kn), cw, sk,
        wq_split, wk_split, wv, wza, wg, wpa, wpb, wo, _head_mean_matrix(True))
    nk_p = nk_p.reshape(BLOCK, 2, N_KV_HEADS, HALF).transpose(0, 2, 1, 3)

    pos_s = PAST_LEN + (np.arange(TM_SAMPLE) % DEC_SEQ)
    xs = x_sample.reshape(DEC_BATCH * DEC_SEQ, D_MODEL)
    q_s, kt_s, vt_s, sza, sga, tb, nc_s = _sample_pre_call(
        xs, _rope_tables(pos_s, signed=True), _rope_tables_rows(pos_s), state_conv[0],
        nw, nat(qn), jnp.tile(kn, N_KV_HEADS).reshape(KV_WIDTH, 1), cw,
        wq_nat, wk_nat.T, wv.T, wza, wg, wpb, _head_mean_matrix(False))
    skcol = jnp.repeat(sk.reshape(N_KV_HEADS, GROUP).T.reshape(-1), DEC_SEQ).reshape(-1, 1)
    to_tiles = lambda c: c[0].transpose(0, 2, 3, 1).reshape(DEC_BATCH, KV_WIDTH, WINDOW)
    y_s, nk_s, nv_s = _sample_attn_call(
        xs, q_s, kt_s, vt_s, to_tiles(cache_k), to_tiles(cache_v), sza, sga, tb, skcol, wpa, wo)
    from_tiles = lambda c: (c.reshape(DEC_BATCH, N_KV_HEADS, HEAD_DIM, WINDOW)
                            .transpose(0, 3, 1, 2)[None])

    return (y_p.reshape(1, SEQ, D_MODEL),
            y_s.reshape(DEC_BATCH, DEC_SEQ, D_MODEL),
            nk_p.reshape(1, 1, WINDOW, N_KV_HEADS, HEAD_DIM),
            nv_p.reshape(1, 1, WINDOW, N_KV_HEADS, HEAD_DIM),
            nc_p.reshape(1, 1, CONV_K - 1, D_MODEL),
            from_tiles(nk_s),
            from_tiles(nv_s),
            nc_s.reshape(1, DEC_BATCH, CONV_K - 1, D_MODEL))
```

```python
import functools

import numpy as np
import jax
import jax.numpy as jnp
from jax import lax
from jax.experimental import pallas as pl
from jax.experimental.pallas import tpu as pltpu

D_MODEL = 1024
SEQ = 16384
DEC_BATCH = 128
DEC_SEQ = 8
PAST_LEN = 16384
N_HEADS = 16
N_KV_HEADS = 4
GROUP = N_HEADS // N_KV_HEADS
HEAD_DIM = 64
HALF = HEAD_DIM // 2
KV_WIDTH = N_KV_HEADS * HEAD_DIM
WINDOW = 128
BLOCK = 128
ROPE_THETA = 10000.0
CONV_K = 3
N_META = 16
LEAD = BLOCK - N_META
EPS = 1e-6
NEG_INF = -1e30
Q_SCALE = HEAD_DIM ** -0.5

W_K = D_MODEL
W_V = W_K + KV_WIDTH
W_ZA = W_V + KV_WIDTH
G_B = W_ZA + D_MODEL
G_C = G_B + D_MODEL
G_H = G_C + D_MODEL
G_ZB = G_H + D_MODEL
G_GA = G_ZB + D_MODEL
G_GB = G_GA + D_MODEL
G_END = G_GB + D_MODEL

LANES = 128
SUBLANES = 8
VMEM_BYTES_V7X = 64 * 1024 * 1024

TM_PROMPT = 512
TM_SAMPLE = 512
BC_SAMPLE = LANES // DEC_SEQ
VMEM_LIMIT = VMEM_BYTES_V7X - 6 * 1024 * 1024

BF16 = jnp.bfloat16
F32 = jnp.float32


def _dot(a, b):
    return jnp.dot(a, b, preferred_element_type=F32)


def _dot_nt(a, b):
    return lax.dot_general(a, b, (((1,), (1,)), ((), ())), preferred_element_type=F32)


def _rms_rows(x, w):
    ms = jnp.mean(x * x, axis=-1, keepdims=True)
    return (x * lax.rsqrt(ms + EPS)) * w


def _head_rms(t, bd, w):
    n = t.shape[1] // 256
    r = t.shape[0]
    sq = (t * t).astype(BF16)
    stacked = jnp.concatenate([sq[:, 256 * c:256 * (c + 1)] for c in range(n)], axis=0)
    ms = _dot(stacked, bd)
    ms = jnp.concatenate([ms[r * c:r * (c + 1)] for c in range(n)], axis=1)
    wt = jnp.concatenate([w] * n, axis=1)
    return (t * lax.rsqrt(ms + EPS)) * wt


def _head_rms_split(t, bd, w):
    n = t.shape[1] // 256
    r = t.shape[0]
    halves = [(t[:, 256 * c:256 * c + LANES], t[:, 256 * c + LANES:256 * (c + 1)]) for c in range(n)]
    pieces = [a * a + b * b for a, b in halves]
    if n == 1:
        pieces = [pieces[0][:r // 2], pieces[0][r // 2:]]
    rp = pieces[0].shape[0]
    stacked = jnp.concatenate(
        [jnp.concatenate(pieces[2 * i:2 * i + 2], axis=1) for i in range(len(pieces) // 2)],
        axis=0).astype(BF16)
    ms = _dot(stacked, bd)
    ms = [ms[rp * (i // 2):rp * (i // 2 + 1), LANES * (i % 2):LANES * (i % 2 + 1)]
          for i in range(len(pieces))]
    if n == 1:
        ms = [jnp.concatenate(ms, axis=0)]
    out = []
    for (a, b), m in zip(halves, ms):
        scale = lax.rsqrt(m + EPS)
        out.append((a * scale) * w[:, :LANES])
        out.append((b * scale) * w[:, LANES:])
    return jnp.concatenate(out, axis=1)


def _rope_split(t, cos, sin):
    out = []
    for c in range(t.shape[1] // 256):
        a = t[:, 256 * c:256 * c + LANES]
        b = t[:, 256 * c + LANES:256 * (c + 1)]
        out.append(a * cos - b * sin)
        out.append(b * cos + a * sin)
    return jnp.concatenate(out, axis=1)


def _rope_natural(t, cos, sin_signed):
    lane = lax.broadcasted_iota(jnp.int32, (t.shape[0], LANES), 1)
    first_half = (lane & (HEAD_DIM - 1)) < HALF
    out = []
    for c in range(t.shape[1] // LANES):
        tc = t[:, LANES * c:LANES * (c + 1)]
        from_below = pltpu.roll(tc, HALF, 1)
        from_above = pltpu.roll(tc, LANES - HALF, 1)
        out.append(tc * cos + jnp.where(first_half, from_above, from_below) * sin_signed)
    return jnp.concatenate(out, axis=1)


def _rope_rows(t, cos, sin):
    out = []
    for g in range(t.shape[0] // HEAD_DIM):
        a = t[HEAD_DIM * g:HEAD_DIM * g + HALF]
        b = t[HEAD_DIM * g + HALF:HEAD_DIM * (g + 1)]
        out.append(a * cos - b * sin)
        out.append(b * cos + a * sin)
    return jnp.concatenate(out, axis=0)


def _lane_group_masks(group_of_lane, dtype):
    return [(group_of_lane == g).astype(dtype) for g in range(N_KV_HEADS)]


def _softmax_parts(s, sk):
    m = jnp.maximum(jnp.max(s, axis=-1, keepdims=True), sk)
    p = jnp.exp(s - m)
    den = jnp.sum(p, axis=-1, keepdims=True) + jnp.exp(sk - m)
    return p, 1.0 / den


def _silu(z):
    return z * jax.nn.sigmoid(z)


def _branch_b(xn, conv, wg_ref, wpb_ref):
    b = _dot(xn, wg_ref[:, G_B:G_C])
    z_b = _dot(xn, wg_ref[:, G_ZB:G_GA])
    b_in = ((b * conv) * _silu(z_b)).astype(BF16)
    return _dot(b_in, wpb_ref[...])


def _prompt_kernel(sinks_ref, x_ref, cos_ref, sin_ref, x0_ref, cos0_ref, sin0_ref, nw_ref,
                   qnw_ref, knw_ref, cw_ref, wq_ref, wk_ref, wza_ref, wg_ref, wpa_ref,
                   wpb_ref, wo_ref, bd_ref,
                   y_ref, nk_ref, nv_ref, nc_ref, kprev, vprev, utail, *, tm):
    i = pl.program_id(0)
    nw = nw_ref[...]
    bd = bd_ref[...]
    knw = knw_ref[...]

    @pl.when(i == 0)
    def _init():
        x0 = x0_ref[...]
        xn0 = _rms_rows(x0, nw).astype(BF16)
        k0 = _dot(xn0, wk_ref[...])
        k0 = _rope_split(_head_rms_split(k0, bd, knw), cos0_ref[...], sin0_ref[...])
        v0 = _dot(xn0, wg_ref[:, W_V:W_ZA])
        c0 = _dot(xn0, wg_ref[:, G_C:G_H])
        h0 = _dot(xn0, wg_ref[:, G_H:G_ZB])
        row = lax.broadcasted_iota(jnp.int32, (BLOCK, D_MODEL), 0)
        u0 = jnp.where(row >= LEAD, c0 * h0, 0.0)
        kprev[...] = k0
        vprev[...] = v0
        utail[...] = u0[BLOCK - SUBLANES:BLOCK]

    x = x_ref[...]
    xn = _rms_rows(x, nw).astype(BF16)
    cos = cos_ref[...]
    sin = sin_ref[...]

    q = _head_rms_split(_dot(xn, wq_ref[...]), bd, qnw_ref[...])
    q = _rope_split(q, cos * Q_SCALE, sin * Q_SCALE).astype(BF16)
    k = _rope_split(_head_rms_split(_dot(xn, wk_ref[...]), bd, knw), cos, sin)
    v = _dot(xn, wg_ref[:, W_V:W_ZA])
    kb = k.astype(BF16)
    vb = v.astype(BF16)

    lane = lax.broadcasted_iota(jnp.int32, (1, 2 * BLOCK), 1)
    kmasks = _lane_group_masks((lane & (LANES - 1)) >> 5, BF16)
    vmasks = _lane_group_masks(lane >> 6, BF16)
    vgroup = lane >> 6
    r_io = lax.broadcasted_iota(jnp.int32, (BLOCK, 2 * BLOCK), 0)
    c_io = lax.broadcasted_iota(jnp.int32, (BLOCK, 2 * BLOCK), 1)
    band = (c_io >= r_io) & (c_io <= r_io + WINDOW)
    first_band = band & (c_io >= jnp.where(i == 0, LEAD, 0))

    o_blocks = []
    for b in range(tm // BLOCK):
        lo = b * BLOCK
        if b == 0:
            kpair = jnp.concatenate([kprev[...].astype(BF16), kb[:BLOCK]], axis=0)
            vpair = jnp.concatenate([vprev[...].astype(BF16), vb[:BLOCK]], axis=0)
            mask = first_band
        else:
            kpair = kb[lo - BLOCK:lo + BLOCK]
            vpair = vb[lo - BLOCK:lo + BLOCK]
            mask = band
        kbd = jnp.concatenate([kpair * km for km in kmasks], axis=0)
        vbd = jnp.concatenate([vpair * vm for vm in vmasks], axis=0)
        qs = jnp.concatenate([q[lo:lo + BLOCK, 256 * j:256 * (j + 1)] for j in range(GROUP)],
                             axis=0)
        s = _dot_nt(qs, kbd)
        p_rows = []
        mults = []
        for j in range(GROUP):
            p_cols = []
            invs = []
            for g in range(N_KV_HEADS):
                sl = s[BLOCK * j:BLOCK * (j + 1), 256 * g:256 * (g + 1)]
                sl = jnp.where(mask, sl, NEG_INF)
                p, inv = _softmax_parts(sl, sinks_ref[GROUP * g + j])
                p_cols.append(p.astype(BF16))
                invs.append(inv)
            p_rows.append(jnp.concatenate(p_cols, axis=1))
            mults.append(jnp.where(vgroup == 0, invs[0],
                                   jnp.where(vgroup == 1, invs[1],
                                             jnp.where(vgroup == 2, invs[2], invs[3]))))
        o = _dot(jnp.concatenate(p_rows, axis=0), vbd)
        o_blocks.append(jnp.concatenate(
            [o[BLOCK * j:BLOCK * (j + 1)] * mults[j] for j in range(GROUP)], axis=1))
    o_a = o_blocks[0] if len(o_blocks) == 1 else jnp.concatenate(o_blocks, axis=0)

    kprev[...] = k[tm - BLOCK:]
    vprev[...] = v[tm - BLOCK:]
    nk_ref[...] = k[tm - BLOCK:]
    nv_ref[...] = v[tm - BLOCK:]

    c = _dot(xn, wg_ref[:, G_C:G_H])
    hc = _dot(xn, wg_ref[:, G_H:G_ZB])
    u = c * hc
    tail = utail[...]
    prev1 = tail[SUBLANES - 1:SUBLANES]
    prev2 = tail[SUBLANES - 2:SUBLANES - 1]
    row8 = lax.broadcasted_iota(jnp.int32, (SUBLANES, D_MODEL), 0)
    r1 = pltpu.roll(u, 1, 0)
    r2 = pltpu.roll(u, 2, 0)
    um1 = jnp.concatenate(
        [jnp.where(row8 == 0, prev1, r1[:SUBLANES]), r1[SUBLANES:]], axis=0)
    um2 = jnp.concatenate(
        [jnp.where(row8 == 0, prev2, jnp.where(row8 == 1, prev1, r2[:SUBLANES])),
         r2[SUBLANES:]], axis=0)
    cw = cw_ref[...]
    conv = cw[0:1] * um2 + cw[1:2] * um1 + cw[2:3] * u
    utail[...] = u[tm - SUBLANES:]
    nc_ref[...] = u[tm - (CONV_K - 1):]

    z_a = _dot(xn, wza_ref[...])
    br_a = _dot((o_a * _silu(z_a)).astype(BF16), wpa_ref[...])
    br_b = _branch_b(xn, conv, wg_ref, wpb_ref)
    g_a = _dot(xn, wg_ref[:, G_GA:G_GB])
    g_b = _dot(xn, wg_ref[:, G_GB:G_END])
    mixed = (jax.nn.sigmoid(g_a) * br_a + jax.nn.sigmoid(g_b) * br_b).astype(BF16)
    y_ref[...] = x + _dot(mixed, wo_ref[...])


def _const_spec(shape):
    zeros = (0,) * len(shape)
    return pl.BlockSpec(shape, lambda i: zeros, pipeline_mode=pl.Buffered(1))


def _params():
    return pltpu.CompilerParams(dimension_semantics=("arbitrary",), vmem_limit_bytes=VMEM_LIMIT)


def _prompt_call(x, tabs, x0, tabs0, nw, qnw, knw, cw, sinks, wq, wk, wza, wg, wpa, wpb, wo, bd):
    tm = TM_PROMPT
    row_spec = lambda w: pl.BlockSpec((tm, w), lambda i: (i, 0))
    consts = (x0, tabs0[0], tabs0[1], nw, qnw, knw, cw, wq, wk, wza, wg, wpa, wpb, wo, bd)
    in_specs = [pl.BlockSpec(memory_space=pltpu.SMEM),
                row_spec(D_MODEL), row_spec(LANES), row_spec(LANES)]
    in_specs += [_const_spec(a.shape) for a in consts]
    resident = lambda shape: pl.BlockSpec(shape, lambda i: (0, 0))
    out_specs = [row_spec(D_MODEL), resident((BLOCK, KV_WIDTH)), resident((BLOCK, KV_WIDTH)),
                 resident((CONV_K - 1, D_MODEL))]
    out_shape = [jax.ShapeDtypeStruct((SEQ, D_MODEL), F32),
                 jax.ShapeDtypeStruct((BLOCK, KV_WIDTH), F32),
                 jax.ShapeDtypeStruct((BLOCK, KV_WIDTH), F32),
                 jax.ShapeDtypeStruct((CONV_K - 1, D_MODEL), F32)]
    return pl.pallas_call(
        functools.partial(_prompt_kernel, tm=tm),
        grid=(SEQ // tm,),
        in_specs=in_specs,
        out_specs=out_specs,
        out_shape=out_shape,
        scratch_shapes=[pltpu.VMEM((BLOCK, KV_WIDTH), F32),
                        pltpu.VMEM((BLOCK, KV_WIDTH), F32),
                        pltpu.VMEM((SUBLANES, D_MODEL), F32)],
        compiler_params=_params(),
        name="prompt_layer",
    )(sinks, x, tabs[0], tabs[1], *consts)


def _sample_pre_kernel(x_ref, cos_ref, sin_ref, cost_ref, sint_ref, st_ref, nw_ref, qnw_ref,
                       knwc_ref, cw_ref, wq_ref, wkt_ref, wvt_ref, wza_ref, wg_ref, wpb_ref, bd_ref,
                       q_ref, kt_ref, vt_ref, sza_ref, sga_ref, tb_ref, nc_ref, *, rows):
    t_len = DEC_SEQ
    bc = rows // t_len
    bd = bd_ref[...]
    x = x_ref[...]
    xn = _rms_rows(x, nw_ref[...]).astype(BF16)

    q = _head_rms(_dot(xn, wq_ref[...]), bd, qnw_ref[...])
    q_ref[...] = _rope_natural(q, cos_ref[...] * Q_SCALE, sin_ref[...] * Q_SCALE)

    kt = _dot_nt(wkt_ref[...], xn)
    ms = _dot(bd, (kt * kt).astype(BF16))
    kt = (kt * lax.rsqrt(ms + EPS)) * knwc_ref[...]
    kt_ref[...] = _rope_rows(kt, cost_ref[...], sint_ref[...])
    vt_ref[...] = _dot_nt(wvt_ref[...], xn)

    c = _dot(xn, wg_ref[:, G_C:G_H])
    hc = _dot(xn, wg_ref[:, G_H:G_ZB])
    u3 = (c * hc).reshape(bc, t_len, D_MODEL)
    st = st_ref[...]
    s_m2 = st[:, 0:1, :]
    s_m1 = st[:, 1:2, :]
    t_io = lax.broadcasted_iota(jnp.int32, (bc, t_len, D_MODEL), 1)
    um1 = jnp.where(t_io == 0, s_m1, pltpu.roll(u3, 1, 1))
    um2 = jnp.where(t_io == 0, s_m2, jnp.where(t_io == 1, s_m1, pltpu.roll(u3, 2, 1)))
    cw = cw_ref[...]
    conv = (cw[0:1] * um2 + cw[1:2] * um1 + cw[2:3] * u3).reshape(rows, D_MODEL)
    nc_ref[...] = u3[:, t_len - (CONV_K - 1):, :]

    sza_ref[...] = _silu(_dot(xn, wza_ref[...]))
    sga_ref[...] = jax.nn.sigmoid(_dot(xn, wg_ref[:, G_GA:G_GB]))
    br_b = _branch_b(xn, conv, wg_ref, wpb_ref)
    tb_ref[...] = jax.nn.sigmoid(_dot(xn, wg_ref[:, G_GB:G_END])) * br_b


def _sample_pre_call(x, tabs, tabs_t, st, nw, qnw, knwc, cw, wq, wkt, wvt, wza, wg, wpb, bd):
    rows = TM_SAMPLE
    bc = rows // DEC_SEQ
    n_rows = DEC_BATCH * DEC_SEQ
    row_spec = pl.BlockSpec((rows, D_MODEL), lambda i: (i, 0))
    col_spec = pl.BlockSpec((KV_WIDTH, rows), lambda i: (0, i))
    st_spec = pl.BlockSpec((bc, CONV_K - 1, D_MODEL), lambda i: (i, 0, 0))
    consts_a = (tabs[0], tabs[1], tabs_t[0], tabs_t[1])
    consts_b = (nw, qnw, knwc, cw, wq, wkt, wvt, wza, wg, wpb, bd)
    in_specs = [row_spec] + [_const_spec(a.shape) for a in consts_a] + [st_spec]
    in_specs += [_const_spec(a.shape) for a in consts_b]
    out_specs = [row_spec, col_spec, col_spec, row_spec, row_spec, row_spec, st_spec]
    rows_sds = jax.ShapeDtypeStruct((n_rows, D_MODEL), F32)
    cols_sds = jax.ShapeDtypeStruct((KV_WIDTH, n_rows), F32)
    out_shape = [rows_sds, cols_sds, cols_sds, rows_sds, rows_sds, rows_sds,
                 jax.ShapeDtypeStruct((DEC_BATCH, CONV_K - 1, D_MODEL), F32)]
    return pl.pallas_call(
        functools.partial(_sample_pre_kernel, rows=rows),
        grid=(n_rows // rows,),
        in_specs=in_specs,
        out_specs=out_specs,
        out_shape=out_shape,
        compiler_params=_params(),
        name="sample_pre",
    )(x, *consts_a, st, *consts_b)


def _sample_attn_kernel(x_ref, q_ref, kt_ref, vt_ref, ck_ref, cv_ref, sza_ref, sga_ref, tb_ref,
                        skcol_ref, wpa_ref, wo_ref,
                        y_ref, nk_ref, nv_ref, oa_s, *, bc):
    t_len = DEC_SEQ
    n_exp = GROUP * N_KV_HEADS * t_len
    ktc = kt_ref[...]
    vtc = vt_ref[...]
    ktc_b = ktc.astype(BF16)
    vtc_b = vtc.astype(BF16)
    lane = lax.broadcasted_iota(jnp.int32, (1, KV_WIDTH), 1)
    vgroup = lane >> 6
    t_io = lax.broadcasted_iota(jnp.int32, (n_exp, WINDOW), 0) & (t_len - 1)
    c_io = lax.broadcasted_iota(jnp.int32, (n_exp, WINDOW), 1)
    old_ok = c_io >= t_io
    c_minus_t = c_io - t_io
    new_lane = lax.broadcasted_iota(jnp.int32, (KV_WIDTH, WINDOW), 1) >= WINDOW - t_len
    skcol = skcol_ref[...]

    def per_batch(n):
        r0 = n * t_len
        qn = q_ref[r0:r0 + t_len, :]
        q_exp = jnp.concatenate(
            [jnp.where(vgroup == g, qn[:, 256 * j:256 * (j + 1)], 0.0)
             for j in range(GROUP) for g in range(N_KV_HEADS)], axis=0).astype(BF16)
        kt_old = ck_ref[n]
        vt_old = cv_ref[n]
        keys = jnp.concatenate([kt_old.astype(BF16), ktc_b], axis=1)
        vals = jnp.concatenate([vt_old.astype(BF16), vtc_b], axis=1)
        s = _dot(q_exp, keys)
        new_ok = (c_io >= r0) & (c_minus_t <= r0)
        s = jnp.where(jnp.concatenate([old_ok, new_ok], axis=1), s, NEG_INF)
        p, inv = _softmax_parts(s, skcol)
        o = _dot_nt(p.astype(BF16), vals) * inv
        outs = []
        for j in range(GROUP):
            acc = None
            for g in range(N_KV_HEADS):
                r = (j * N_KV_HEADS + g) * t_len
                piece = jnp.where(vgroup == g, o[r:r + t_len], 0.0)
                acc = piece if acc is None else acc + piece
            outs.append(acc)
        oa_s[r0:r0 + t_len, :] = jnp.concatenate(outs, axis=1)
        shift_new = WINDOW - t_len - r0
        nk_ref[n] = jnp.where(new_lane, pltpu.roll(ktc, shift_new, 1),
                              pltpu.roll(kt_old, WINDOW - t_len, 1))
        nv_ref[n] = jnp.where(new_lane, pltpu.roll(vtc, shift_new, 1),
                              pltpu.roll(vt_old, WINDOW - t_len, 1))

    for n in range(bc):
        per_batch(n)

    a_in = (oa_s[...] * sza_ref[...]).astype(BF16)
    br_a = _dot(a_in, wpa_ref[...])
    mixed = (sga_ref[...] * br_a + tb_ref[...]).astype(BF16)
    y_ref[...] = x_ref[...] + _dot(mixed, wo_ref[...])


def _sample_attn_call(x, q, kt, vt, ck, cv, sza, sga, tb, skcol, wpa, wo):
    bc = BC_SAMPLE
    rows = bc * DEC_SEQ
    row_spec = pl.BlockSpec((rows, D_MODEL), lambda i: (i, 0))
    col_spec = pl.BlockSpec((KV_WIDTH, rows), lambda i: (0, i))
    cache_spec = pl.BlockSpec((bc, KV_WIDTH, WINDOW), lambda i: (i, 0, 0))
    in_specs = [row_spec, row_spec, col_spec, col_spec, cache_spec, cache_spec,
                row_spec, row_spec, row_spec,
                _const_spec(skcol.shape), _const_spec(wpa.shape), _const_spec(wo.shape)]
    out_specs = [row_spec, cache_spec, cache_spec]
    cache_sds = jax.ShapeDtypeStruct((DEC_BATCH, KV_WIDTH, WINDOW), F32)
    out_shape = [jax.ShapeDtypeStruct((DEC_BATCH * DEC_SEQ, D_MODEL), F32), cache_sds, cache_sds]
    return pl.pallas_call(
        functools.partial(_sample_attn_kernel, bc=bc),
        grid=(DEC_BATCH // bc,),
        in_specs=in_specs,
        out_specs=out_specs,
        out_shape=out_shape,
        scratch_shapes=[pltpu.VMEM((rows, D_MODEL), F32)],
        compiler_params=_params(),
        name="sample_attn",
    )(x, q, kt, vt, ck, cv, sza, sga, tb, skcol, wpa, wo)


def _rope_angles(positions):
    inv = np.power(ROPE_THETA, -np.arange(HALF, dtype=np.float64) * (2.0 / HEAD_DIM))
    return np.asarray(positions, dtype=np.float64)[:, None] * inv[None, :]


def _rope_tables(positions, signed):
    ang = _rope_angles(positions)
    cos = np.tile(np.cos(ang), (1, LANES // HALF))
    sin = np.tile(np.sin(ang), (1, LANES // HALF))
    if signed:
        sign = np.where((np.arange(LANES) % HEAD_DIM) < HALF, -1.0, 1.0)
        sin = sin * sign[None, :]
    return jnp.asarray(cos, F32), jnp.asarray(sin, F32)


def _rope_tables_rows(positions):
    ang = _rope_angles(positions).T
    return jnp.asarray(np.cos(ang), F32), jnp.asarray(np.sin(ang), F32)


def _head_mean_matrix(group_lanes):
    grp = np.arange(256) // group_lanes
    return jnp.asarray((grp[:, None] == grp[None, :]) / float(HEAD_DIM), BF16)


def kernel(x_prompt, x_sample, cache_k, cache_v, state_conv, meta_tokens, norm_w, w_in,
           q_norm_w, k_norm_w, sinks, conv_w, w_proj_a, w_proj_b, w_out):
    assert x_prompt.shape == (1, SEQ, D_MODEL) and x_sample.shape == (DEC_BATCH, DEC_SEQ, D_MODEL)
    assert w_in.shape[0] == 1, "single layer"
    w = w_in[0].astype(BF16)
    wq5 = w[:, :W_K].reshape(D_MODEL, N_KV_HEADS, GROUP, 2, HALF)
    wk_nat = w[:, W_K:W_V]
    wv = w[:, W_V:W_ZA]
    wq_split = wq5.transpose(0, 2, 3, 1, 4).reshape(D_MODEL, D_MODEL)
    wq_nat = wq5.transpose(0, 2, 1, 3, 4).reshape(D_MODEL, D_MODEL)
    wk_split = (wk_nat.reshape(D_MODEL, N_KV_HEADS, 2, HALF)
                .transpose(0, 2, 1, 3).reshape(D_MODEL, KV_WIDTH))
    wg = w
    wza = (w[:, W_ZA:G_B].reshape(D_MODEL, N_KV_HEADS, GROUP, HEAD_DIM)
           .transpose(0, 2, 1, 3).reshape(D_MODEL, D_MODEL))
    wpa = (w_proj_a[0].astype(BF16).reshape(N_KV_HEADS, GROUP, HEAD_DIM, D_MODEL)
           .transpose(1, 0, 2, 3).reshape(D_MODEL, D_MODEL))
    wpb = w_proj_b[0].astype(BF16)
    wo = w_out[0].astype(BF16)

    nw = norm_w[0].reshape(1, D_MODEL)
    cw = conv_w[0]
    qn = q_norm_w[0]
    kn = k_norm_w[0]
    split = lambda v: jnp.concatenate([jnp.tile(v[:HALF], 4), jnp.tile(v[HALF:], 4)]).reshape(1, 256)
    nat = lambda v: jnp.tile(v, 4).reshape(1, 256)
    sk = sinks[0].astype(F32)

    tabs_p = _rope_tables(np.arange(SEQ) + N_META, signed=False)
    tabs_0 = _rope_tables(np.arange(BLOCK) - LEAD, signed=False)
    x0 = jnp.concatenate([jnp.zeros((LEAD, D_MODEL), x_prompt.dtype),
                          meta_tokens.astype(x_prompt.dtype)], axis=0)
    y_p, nk_p, nv_p, nc_p = _prompt_call(
        x_prompt[0], tabs_p, x0, tabs_0, nw, split(qn), split(kn), cw, sk,
        wq_split, wk_split, wza, wg, wpa, wpb, wo, _head_mean_matrix(HALF))
    nk_p = nk_p.reshape(BLOCK, 2, N_KV_HEADS, HALF).transpose(0, 2, 1, 3)

    pos_s = PAST_LEN + (np.arange(TM_SAMPLE) % DEC_SEQ)
    xs = x_sample.reshape(DEC_BATCH * DEC_SEQ, D_MODEL)
    q_s, kt_s, vt_s, sza, sga, tb, nc_s = _sample_pre_call(
        xs, _rope_tables(pos_s, signed=True), _rope_tables_rows(pos_s), state_conv[0],
        nw, nat(qn), jnp.tile(kn, N_KV_HEADS).reshape(KV_WIDTH, 1), cw,
        wq_nat, wk_nat.T, wv.T, wza, wg, wpb, _head_mean_matrix(HEAD_DIM))
    skcol = jnp.repeat(sk.reshape(N_KV_HEADS, GROUP).T.reshape(-1), DEC_SEQ).reshape(-1, 1)
    to_tiles = lambda c: c[0].transpose(0, 2, 3, 1).reshape(DEC_BATCH, KV_WIDTH, WINDOW)
    y_s, nk_s, nv_s = _sample_attn_call(
        xs, q_s, kt_s, vt_s, to_tiles(cache_k), to_tiles(cache_v), sza, sga, tb, skcol, wpa, wo)
    from_tiles = lambda c: (c.reshape(DEC_BATCH, N_KV_HEADS, HEAD_DIM, WINDOW)
                            .transpose(0, 3, 1, 2)[None])

    return (y_p.reshape(1, SEQ, D_MODEL),
            y_s.reshape(DEC_BATCH, DEC_SEQ, D_MODEL),
            nk_p.reshape(1, 1, WINDOW, N_KV_HEADS, HEAD_DIM),
            nv_p.reshape(1, 1, WINDOW, N_KV_HEADS, HEAD_DIM),
            nc_p.reshape(1, 1, CONV_K - 1, D_MODEL),
            from_tiles(nk_s),
            from_tiles(nv_s),
            nc_s.reshape(1, DEC_BATCH, CONV_K - 1, D_MODEL))
```

```python
import functools

import numpy as np
import jax
import jax.numpy as jnp
from jax import lax
from jax.experimental import pallas as pl
from jax.experimental.pallas import tpu as pltpu

D_MODEL = 1024
SEQ = 16384
DEC_BATCH = 128
DEC_SEQ = 8
PAST_LEN = 16384
N_HEADS = 16
N_KV_HEADS = 4
GROUP = N_HEADS // N_KV_HEADS
HEAD_DIM = 64
HALF = HEAD_DIM // 2
KV_WIDTH = N_KV_HEADS * HEAD_DIM
WINDOW = 128
BLOCK = 128
ROPE_THETA = 10000.0
CONV_K = 3
N_META = 16
LEAD = BLOCK - N_META
EPS = 1e-6
NEG_INF = -1e30
Q_SCALE = HEAD_DIM ** -0.5

W_K = D_MODEL
W_V = W_K + KV_WIDTH
W_ZA = W_V + KV_WIDTH
G_B = W_ZA + D_MODEL
G_C = G_B + D_MODEL
G_H = G_C + D_MODEL
G_ZB = G_H + D_MODEL
G_GA = G_ZB + D_MODEL
G_GB = G_GA + D_MODEL
G_END = G_GB + D_MODEL

LANES = 128
SUBLANES = 8
VMEM_BYTES_V7X = 64 * 1024 * 1024

TM_PROMPT = 512
TM_SAMPLE = 512
BC_SAMPLE = LANES // DEC_SEQ
VMEM_LIMIT = VMEM_BYTES_V7X - 6 * 1024 * 1024

BF16 = jnp.bfloat16
F32 = jnp.float32


def _dot(a, b):
    return jnp.dot(a, b, preferred_element_type=F32)


def _dot_nt(a, b):
    return lax.dot_general(a, b, (((1,), (1,)), ((), ())), preferred_element_type=F32)


def _rms_rows(x, w):
    ms = jnp.mean(x * x, axis=-1, keepdims=True)
    return (x * lax.rsqrt(ms + EPS)) * w


def _head_rms(t, bd, w):
    n = t.shape[1] // 256
    r = t.shape[0]
    sq = (t * t).astype(BF16)
    stacked = jnp.concatenate([sq[:, 256 * c:256 * (c + 1)] for c in range(n)], axis=0)
    ms = _dot(stacked, bd)
    ms = jnp.concatenate([ms[r * c:r * (c + 1)] for c in range(n)], axis=1)
    wt = jnp.concatenate([w] * n, axis=1)
    return (t * lax.rsqrt(ms + EPS)) * wt


def _head_rms_split(t, bd, w):
    n = t.shape[1] // 256
    r = t.shape[0]
    halves = [(t[:, 256 * c:256 * c + LANES], t[:, 256 * c + LANES:256 * (c + 1)]) for c in range(n)]
    pieces = [a * a + b * b for a, b in halves]
    if n == 1:
        pieces = [pieces[0][:r // 2], pieces[0][r // 2:]]
    rp = pieces[0].shape[0]
    stacked = jnp.concatenate(
        [jnp.concatenate(pieces[2 * i:2 * i + 2], axis=1) for i in range(len(pieces) // 2)],
        axis=0).astype(BF16)
    ms = _dot(stacked, bd)
    ms = [ms[rp * (i // 2):rp * (i // 2 + 1), LANES * (i % 2):LANES * (i % 2 + 1)]
          for i in range(len(pieces))]
    if n == 1:
        ms = [jnp.concatenate(ms, axis=0)]
    out = []
    for (a, b), m in zip(halves, ms):
        scale = lax.rsqrt(m + EPS)
        out.append((a * scale) * w[:, :LANES])
        out.append((b * scale) * w[:, LANES:])
    return jnp.concatenate(out, axis=1)


def _rope_split(t, cos, sin):
    out = []
    for c in range(t.shape[1] // 256):
        a = t[:, 256 * c:256 * c + LANES]
        b = t[:, 256 * c + LANES:256 * (c + 1)]
        out.append(a * cos - b * sin)
        out.append(b * cos + a * sin)
    return jnp.concatenate(out, axis=1)


def _rope_natural(t, cos, sin_signed):
    lane = lax.broadcasted_iota(jnp.int32, (t.shape[0], LANES), 1)
    first_half = (lane & (HEAD_DIM - 1)) < HALF
    out = []
    for c in range(t.shape[1] // LANES):
        tc = t[:, LANES * c:LANES * (c + 1)]
        from_below = pltpu.roll(tc, HALF, 1)
        from_above = pltpu.roll(tc, LANES - HALF, 1)
        out.append(tc * cos + jnp.where(first_half, from_above, from_below) * sin_signed)
    return jnp.concatenate(out, axis=1)


def _rope_rows(t, cos, sin):
    out = []
    for g in range(t.shape[0] // HEAD_DIM):
        a = t[HEAD_DIM * g:HEAD_DIM * g + HALF]
        b = t[HEAD_DIM * g + HALF:HEAD_DIM * (g + 1)]
        out.append(a * cos - b * sin)
        out.append(b * cos + a * sin)
    return jnp.concatenate(out, axis=0)


def _lane_group_masks(group_of_lane, dtype):
    return [(group_of_lane == g).astype(dtype) for g in range(N_KV_HEADS)]


def _softmax_parts(s, sk):
    m = jnp.maximum(jnp.max(s, axis=-1, keepdims=True), sk)
    p = jnp.exp(s - m)
    den = jnp.sum(p, axis=-1, keepdims=True) + jnp.exp(sk - m)
    return p, 1.0 / den


def _silu(z):
    return z * jax.nn.sigmoid(z)


def _branch_b(xn, conv, wg_ref, wpb_ref):
    b = _dot(xn, wg_ref[:, G_B:G_C])
    z_b = _dot(xn, wg_ref[:, G_ZB:G_GA])
    b_in = ((b * conv) * _silu(z_b)).astype(BF16)
    return _dot(b_in, wpb_ref[...])


def _shift_cache_windows(i, kt_ref, vt_ref, ck_ref, cv_ref, nck_ref, ncv_ref):
    per_step = ck_ref.shape[0]
    new_lane = lax.broadcasted_iota(jnp.int32, (KV_WIDTH, WINDOW), 1) >= WINDOW - DEC_SEQ
    ktc = kt_ref[...]
    vtc = vt_ref[...]
    first = (i * per_step) % BC_SAMPLE
    for b in range(per_step):
        shift_new = WINDOW - DEC_SEQ - (first + b) * DEC_SEQ
        nck_ref[b] = jnp.where(new_lane, pltpu.roll(ktc, shift_new, 1),
                               pltpu.roll(ck_ref[b], WINDOW - DEC_SEQ, 1))
        ncv_ref[b] = jnp.where(new_lane, pltpu.roll(vtc, shift_new, 1),
                               pltpu.roll(cv_ref[b], WINDOW - DEC_SEQ, 1))


def _prompt_kernel(sinks_ref, x_ref, cos_ref, sin_ref, kt_ref, vt_ref, ck_ref, cv_ref,
                   x0_ref, cos0_ref, sin0_ref, nw_ref,
                   qnw_ref, knw_ref, cw_ref, wq_ref, wk_ref, wza_ref, wg_ref, wpa_ref,
                   wpb_ref, wo_ref, bd_ref,
                   y_ref, nk_ref, nv_ref, nc_ref, nck_ref, ncv_ref, kprev, vprev, utail, *, tm):
    i = pl.program_id(0)
    nw = nw_ref[...]
    bd = bd_ref[...]
    knw = knw_ref[...]

    @pl.when(i == 0)
    def _init():
        x0 = x0_ref[...]
        xn0 = _rms_rows(x0, nw).astype(BF16)
        k0 = _dot(xn0, wk_ref[...])
        k0 = _rope_split(_head_rms_split(k0, bd, knw), cos0_ref[...], sin0_ref[...])
        v0 = _dot(xn0, wg_ref[:, W_V:W_ZA])
        c0 = _dot(xn0, wg_ref[:, G_C:G_H])
        h0 = _dot(xn0, wg_ref[:, G_H:G_ZB])
        row = lax.broadcasted_iota(jnp.int32, (BLOCK, D_MODEL), 0)
        u0 = jnp.where(row >= LEAD, c0 * h0, 0.0)
        kprev[...] = k0
        vprev[...] = v0
        utail[...] = u0[BLOCK - SUBLANES:BLOCK]

    _shift_cache_windows(i, kt_ref, vt_ref, ck_ref, cv_ref, nck_ref, ncv_ref)

    x = x_ref[...]
    xn = _rms_rows(x, nw).astype(BF16)
    cos = cos_ref[...]
    sin = sin_ref[...]

    q = _head_rms_split(_dot(xn, wq_ref[...]), bd, qnw_ref[...])
    q = _rope_split(q, cos * Q_SCALE, sin * Q_SCALE).astype(BF16)
    k = _rope_split(_head_rms_split(_dot(xn, wk_ref[...]), bd, knw), cos, sin)
    v = _dot(xn, wg_ref[:, W_V:W_ZA])
    kb = k.astype(BF16)
    vb = v.astype(BF16)

    lane = lax.broadcasted_iota(jnp.int32, (1, 2 * BLOCK), 1)
    kmasks = _lane_group_masks((lane & (LANES - 1)) >> 5, BF16)
    vmasks = _lane_group_masks(lane >> 6, BF16)
    vgroup = lane >> 6
    r_io = lax.broadcasted_iota(jnp.int32, (BLOCK, 2 * BLOCK), 0)
    c_io = lax.broadcasted_iota(jnp.int32, (BLOCK, 2 * BLOCK), 1)
    band = (c_io >= r_io) & (c_io <= r_io + WINDOW)
    first_band = band & (c_io >= jnp.where(i == 0, LEAD, 0))

    o_blocks = []
    for b in range(tm // BLOCK):
        lo = b * BLOCK
        if b == 0:
            kpair = jnp.concatenate([kprev[...].astype(BF16), kb[:BLOCK]], axis=0)
            vpair = jnp.concatenate([vprev[...].astype(BF16), vb[:BLOCK]], axis=0)
            mask = first_band
        else:
            kpair = kb[lo - BLOCK:lo + BLOCK]
            vpair = vb[lo - BLOCK:lo + BLOCK]
            mask = band
        kbd = jnp.concatenate([kpair * km for km in kmasks], axis=0)
        vbd = jnp.concatenate([vpair * vm for vm in vmasks], axis=0)
        qs = jnp.concatenate([q[lo:lo + BLOCK, 256 * j:256 * (j + 1)] for j in range(GROUP)],
                             axis=0)
        s = _dot_nt(qs, kbd)
        p_rows = []
        mults = []
        for j in range(GROUP):
            p_cols = []
            invs = []
            for g in range(N_KV_HEADS):
                sl = s[BLOCK * j:BLOCK * (j + 1), 256 * g:256 * (g + 1)]
                sl = jnp.where(mask, sl, NEG_INF)
                p, inv = _softmax_parts(sl, sinks_ref[GROUP * g + j])
                p_cols.append(p.astype(BF16))
                invs.append(inv)
            p_rows.append(jnp.concatenate(p_cols, axis=1))
            mults.append(jnp.where(vgroup == 0, invs[0],
                                   jnp.where(vgroup == 1, invs[1],
                                             jnp.where(vgroup == 2, invs[2], invs[3]))))
        o = _dot(jnp.concatenate(p_rows, axis=0), vbd)
        o_blocks.append(jnp.concatenate(
            [o[BLOCK * j:BLOCK * (j + 1)] * mults[j] for j in range(GROUP)], axis=1))
    o_a = o_blocks[0] if len(o_blocks) == 1 else jnp.concatenate(o_blocks, axis=0)

    kprev[...] = k[tm - BLOCK:]
    vprev[...] = v[tm - BLOCK:]
    nk_ref[...] = k[tm - BLOCK:]
    nv_ref[...] = v[tm - BLOCK:]

    c = _dot(xn, wg_ref[:, G_C:G_H])
    hc = _dot(xn, wg_ref[:, G_H:G_ZB])
    u = c * hc
    tail = utail[...]
    prev1 = tail[SUBLANES - 1:SUBLANES]
    prev2 = tail[SUBLANES - 2:SUBLANES - 1]
    row8 = lax.broadcasted_iota(jnp.int32, (SUBLANES, D_MODEL), 0)
    r1 = pltpu.roll(u, 1, 0)
    r2 = pltpu.roll(u, 2, 0)
    um1 = jnp.concatenate(
        [jnp.where(row8 == 0, prev1, r1[:SUBLANES]), r1[SUBLANES:]], axis=0)
    um2 = jnp.concatenate(
        [jnp.where(row8 == 0, prev2, jnp.where(row8 == 1, prev1, r2[:SUBLANES])),
         r2[SUBLANES:]], axis=0)
    cw = cw_ref[...]
    conv = cw[0:1] * um2 + cw[1:2] * um1 + cw[2:3] * u
    utail[...] = u[tm - SUBLANES:]
    nc_ref[...] = u[tm - (CONV_K - 1):]

    z_a = _dot(xn, wza_ref[...])
    br_a = _dot((o_a * _silu(z_a)).astype(BF16), wpa_ref[...])
    br_b = _branch_b(xn, conv, wg_ref, wpb_ref)
    g_a = _dot(xn, wg_ref[:, G_GA:G_GB])
    g_b = _dot(xn, wg_ref[:, G_GB:G_END])
    mixed = (jax.nn.sigmoid(g_a) * br_a + jax.nn.sigmoid(g_b) * br_b).astype(BF16)
    y_ref[...] = x + _dot(mixed, wo_ref[...])


def _const_spec(shape):
    zeros = (0,) * len(shape)
    return pl.BlockSpec(shape, lambda i: zeros, pipeline_mode=pl.Buffered(1))


def _params():
    return pltpu.CompilerParams(dimension_semantics=("arbitrary",), vmem_limit_bytes=VMEM_LIMIT)


def _prompt_call(x, tabs, kt, vt, ck, cv, x0, tabs0, nw, qnw, knw, cw, sinks, wq, wk, wza, wg, wpa,
                 wpb, wo, bd):
    tm = TM_PROMPT
    n_steps = SEQ // tm
    per_step = DEC_BATCH // n_steps
    assert per_step * n_steps == DEC_BATCH and BC_SAMPLE % per_step == 0
    row_spec = lambda w: pl.BlockSpec((tm, w), lambda i: (i, 0))
    new_spec = pl.BlockSpec((KV_WIDTH, LANES), lambda i: (0, (i * per_step) // BC_SAMPLE))
    cache_spec = pl.BlockSpec((per_step, KV_WIDTH, WINDOW), lambda i: (i, 0, 0))
    consts = (x0, tabs0[0], tabs0[1], nw, qnw, knw, cw, wq, wk, wza, wg, wpa, wpb, wo, bd)
    in_specs = [pl.BlockSpec(memory_space=pltpu.SMEM),
                row_spec(D_MODEL), row_spec(LANES), row_spec(LANES),
                new_spec, new_spec, cache_spec, cache_spec]
    in_specs += [_const_spec(a.shape) for a in consts]
    resident = lambda shape: pl.BlockSpec(shape, lambda i: (0, 0))
    out_specs = [row_spec(D_MODEL), resident((BLOCK, KV_WIDTH)), resident((BLOCK, KV_WIDTH)),
                 resident((CONV_K - 1, D_MODEL)), cache_spec, cache_spec]
    cache_sds = jax.ShapeDtypeStruct((DEC_BATCH, KV_WIDTH, WINDOW), F32)
    out_shape = [jax.ShapeDtypeStruct((SEQ, D_MODEL), F32),
                 jax.ShapeDtypeStruct((BLOCK, KV_WIDTH), F32),
                 jax.ShapeDtypeStruct((BLOCK, KV_WIDTH), F32),
                 jax.ShapeDtypeStruct((CONV_K - 1, D_MODEL), F32), cache_sds, cache_sds]
    return pl.pallas_call(
        functools.partial(_prompt_kernel, tm=tm),
        grid=(n_steps,),
        in_specs=in_specs,
        out_specs=out_specs,
        out_shape=out_shape,
        scratch_shapes=[pltpu.VMEM((BLOCK, KV_WIDTH), F32),
                        pltpu.VMEM((BLOCK, KV_WIDTH), F32),
                        pltpu.VMEM((SUBLANES, D_MODEL), F32)],
        compiler_params=_params(),
        name="prompt_layer",
    )(sinks, x, tabs[0], tabs[1], kt, vt, ck, cv, *consts)


def _sample_pre_kernel(x_ref, cos_ref, sin_ref, cost_ref, sint_ref, st_ref, nw_ref, qnw_ref,
                       knwc_ref, cw_ref, wq_ref, wkt_ref, wvt_ref, wza_ref, wg_ref, wpb_ref, bd_ref,
                       q_ref, kt_ref, vt_ref, sza_ref, sga_ref, tb_ref, nc_ref, *, rows):
    t_len = DEC_SEQ
    bc = rows // t_len
    bd = bd_ref[...]
    x = x_ref[...]
    xn = _rms_rows(x, nw_ref[...]).astype(BF16)

    q = _head_rms(_dot(xn, wq_ref[...]), bd, qnw_ref[...])
    q_ref[...] = _rope_natural(q, cos_ref[...] * Q_SCALE, sin_ref[...] * Q_SCALE)

    kt = _dot_nt(wkt_ref[...], xn)
    ms = _dot(bd, (kt * kt).astype(BF16))
    kt = (kt * lax.rsqrt(ms + EPS)) * knwc_ref[...]
    kt_ref[...] = _rope_rows(kt, cost_ref[...], sint_ref[...])
    vt_ref[...] = _dot_nt(wvt_ref[...], xn)

    c = _dot(xn, wg_ref[:, G_C:G_H])
    hc = _dot(xn, wg_ref[:, G_H:G_ZB])
    u3 = (c * hc).reshape(bc, t_len, D_MODEL)
    st = st_ref[...]
    s_m2 = st[:, 0:1, :]
    s_m1 = st[:, 1:2, :]
    t_io = lax.broadcasted_iota(jnp.int32, (bc, t_len, D_MODEL), 1)
    um1 = jnp.where(t_io == 0, s_m1, pltpu.roll(u3, 1, 1))
    um2 = jnp.where(t_io == 0, s_m2, jnp.where(t_io == 1, s_m1, pltpu.roll(u3, 2, 1)))
    cw = cw_ref[...]
    conv = (cw[0:1] * um2 + cw[1:2] * um1 + cw[2:3] * u3).reshape(rows, D_MODEL)
    nc_ref[...] = u3[:, t_len - (CONV_K - 1):, :]

    sza_ref[...] = _silu(_dot(xn, wza_ref[...]))
    sga_ref[...] = jax.nn.sigmoid(_dot(xn, wg_ref[:, G_GA:G_GB]))
    br_b = _branch_b(xn, conv, wg_ref, wpb_ref)
    tb_ref[...] = jax.nn.sigmoid(_dot(xn, wg_ref[:, G_GB:G_END])) * br_b


def _sample_pre_call(x, tabs, tabs_t, st, nw, qnw, knwc, cw, wq, wkt, wvt, wza, wg, wpb, bd):
    rows = TM_SAMPLE
    bc = rows // DEC_SEQ
    n_rows = DEC_BATCH * DEC_SEQ
    row_spec = pl.BlockSpec((rows, D_MODEL), lambda i: (i, 0))
    col_spec = pl.BlockSpec((KV_WIDTH, rows), lambda i: (0, i))
    st_spec = pl.BlockSpec((bc, CONV_K - 1, D_MODEL), lambda i: (i, 0, 0))
    consts_a = (tabs[0], tabs[1], tabs_t[0], tabs_t[1])
    consts_b = (nw, qnw, knwc, cw, wq, wkt, wvt, wza, wg, wpb, bd)
    in_specs = [row_spec] + [_const_spec(a.shape) for a in consts_a] + [st_spec]
    in_specs += [_const_spec(a.shape) for a in consts_b]
    out_specs = [row_spec, col_spec, col_spec, row_spec, row_spec, row_spec, st_spec]
    rows_sds = jax.ShapeDtypeStruct((n_rows, D_MODEL), F32)
    cols_sds = jax.ShapeDtypeStruct((KV_WIDTH, n_rows), F32)
    out_shape = [rows_sds, cols_sds, cols_sds, rows_sds, rows_sds, rows_sds,
                 jax.ShapeDtypeStruct((DEC_BATCH, CONV_K - 1, D_MODEL), F32)]
    return pl.pallas_call(
        functools.partial(_sample_pre_kernel, rows=rows),
        grid=(n_rows // rows,),
        in_specs=in_specs,
        out_specs=out_specs,
        out_shape=out_shape,
        compiler_params=_params(),
        name="sample_pre",
    )(x, *consts_a, st, *consts_b)


def _sample_attn_kernel(x_ref, q_ref, kt_ref, vt_ref, ck_ref, cv_ref, sza_ref, sga_ref, tb_ref,
                        skcol_ref, wpa_ref, wo_ref,
                        y_ref, oa_s, *, bc):
    t_len = DEC_SEQ
    n_exp = GROUP * N_KV_HEADS * t_len
    ktc_b = kt_ref[...].astype(BF16)
    vtc_b = vt_ref[...].astype(BF16)
    lane = lax.broadcasted_iota(jnp.int32, (1, KV_WIDTH), 1)
    vgroup = lane >> 6
    t_io = lax.broadcasted_iota(jnp.int32, (n_exp, WINDOW), 0) & (t_len - 1)
    c_io = lax.broadcasted_iota(jnp.int32, (n_exp, WINDOW), 1)
    old_ok = c_io >= t_io
    c_minus_t = c_io - t_io
    skcol = skcol_ref[...]

    def per_batch(n):
        r0 = n * t_len
        qn = q_ref[r0:r0 + t_len, :]
        q_exp = jnp.concatenate(
            [jnp.where(vgroup == g, qn[:, 256 * j:256 * (j + 1)], 0.0)
             for j in range(GROUP) for g in range(N_KV_HEADS)], axis=0).astype(BF16)
        kt_old = ck_ref[n]
        vt_old = cv_ref[n]
        keys = jnp.concatenate([kt_old.astype(BF16), ktc_b], axis=1)
        vals = jnp.concatenate([vt_old.astype(BF16), vtc_b], axis=1)
        s = _dot(q_exp, keys)
        new_ok = (c_io >= r0) & (c_minus_t <= r0)
        s = jnp.where(jnp.concatenate([old_ok, new_ok], axis=1), s, NEG_INF)
        p, inv = _softmax_parts(s, skcol)
        o = _dot_nt(p.astype(BF16), vals) * inv
        outs = []
        for j in range(GROUP):
            acc = None
            for g in range(N_KV_HEADS):
                r = (j * N_KV_HEADS + g) * t_len
                piece = jnp.where(vgroup == g, o[r:r + t_len], 0.0)
                acc = piece if acc is None else acc + piece
            outs.append(acc)
        oa_s[r0:r0 + t_len, :] = jnp.concatenate(outs, axis=1)

    for n in range(bc):
        per_batch(n)

    a_in = (oa_s[...] * sza_ref[...]).astype(BF16)
    br_a = _dot(a_in, wpa_ref[...])
    mixed = (sga_ref[...] * br_a + tb_ref[...]).astype(BF16)
    y_ref[...] = x_ref[...] + _dot(mixed, wo_ref[...])


def _sample_attn_call(x, q, kt, vt, ck, cv, sza, sga, tb, skcol, wpa, wo):
    bc = BC_SAMPLE
    rows = bc * DEC_SEQ
    row_spec = pl.BlockSpec((rows, D_MODEL), lambda i: (i, 0))
    col_spec = pl.BlockSpec((KV_WIDTH, rows), lambda i: (0, i))
    cache_spec = pl.BlockSpec((bc, KV_WIDTH, WINDOW), lambda i: (i, 0, 0))
    in_specs = [row_spec, row_spec, col_spec, col_spec, cache_spec, cache_spec,
                row_spec, row_spec, row_spec,
                _const_spec(skcol.shape), _const_spec(wpa.shape), _const_spec(wo.shape)]
    out_specs = row_spec
    out_shape = jax.ShapeDtypeStruct((DEC_BATCH * DEC_SEQ, D_MODEL), F32)
    return pl.pallas_call(
        functools.partial(_sample_attn_kernel, bc=bc),
        grid=(DEC_BATCH // bc,),
        in_specs=in_specs,
        out_specs=out_specs,
        out_shape=out_shape,
        scratch_shapes=[pltpu.VMEM((rows, D_MODEL), F32)],
        compiler_params=_params(),
        name="sample_attn",
    )(x, q, kt, vt, ck, cv, sza, sga, tb, skcol, wpa, wo)


def _rope_angles(positions):
    inv = np.power(ROPE_THETA, -np.arange(HALF, dtype=np.float64) * (2.0 / HEAD_DIM))
    return np.asarray(positions, dtype=np.float64)[:, None] * inv[None, :]


def _rope_tables(positions, signed):
    ang = _rope_angles(positions)
    cos = np.tile(np.cos(ang), (1, LANES // HALF))
    sin = np.tile(np.sin(ang), (1, LANES // HALF))
    if signed:
        sign = np.where((np.arange(LANES) % HEAD_DIM) < HALF, -1.0, 1.0)
        sin = sin * sign[None, :]
    return jnp.asarray(cos, F32), jnp.asarray(sin, F32)


def _rope_tables_rows(positions):
    ang = _rope_angles(positions).T
    return jnp.asarray(np.cos(ang), F32), jnp.asarray(np.sin(ang), F32)


def _head_mean_matrix(group_lanes):
    grp = np.arange(256) // group_lanes
    return jnp.asarray((grp[:, None] == grp[None, :]) / float(HEAD_DIM), BF16)


def kernel(x_prompt, x_sample, cache_k, cache_v, state_conv, meta_tokens, norm_w, w_in,
           q_norm_w, k_norm_w, sinks, conv_w, w_proj_a, w_proj_b, w_out):
    assert x_prompt.shape == (1, SEQ, D_MODEL) and x_sample.shape == (DEC_BATCH, DEC_SEQ, D_MODEL)
    assert w_in.shape[0] == 1, "single layer"
    w = w_in[0].astype(BF16)
    wq5 = w[:, :W_K].reshape(D_MODEL, N_KV_HEADS, GROUP, 2, HALF)
    wk_nat = w[:, W_K:W_V]
    wv = w[:, W_V:W_ZA]
    wq_split = wq5.transpose(0, 2, 3, 1, 4).reshape(D_MODEL, D_MODEL)
    wq_nat = wq5.transpose(0, 2, 1, 3, 4).reshape(D_MODEL, D_MODEL)
    wk_split = (wk_nat.reshape(D_MODEL, N_KV_HEADS, 2, HALF)
                .transpose(0, 2, 1, 3).reshape(D_MODEL, KV_WIDTH))
    wg = w
    wza = (w[:, W_ZA:G_B].reshape(D_MODEL, N_KV_HEADS, GROUP, HEAD_DIM)
           .transpose(0, 2, 1, 3).reshape(D_MODEL, D_MODEL))
    wpa = (w_proj_a[0].astype(BF16).reshape(N_KV_HEADS, GROUP, HEAD_DIM, D_MODEL)
           .transpose(1, 0, 2, 3).reshape(D_MODEL, D_MODEL))
    wpb = w_proj_b[0].astype(BF16)
    wo = w_out[0].astype(BF16)

    nw = norm_w[0].reshape(1, D_MODEL)
    cw = conv_w[0]
    qn = q_norm_w[0]
    kn = k_norm_w[0]
    split = lambda v: jnp.concatenate([jnp.tile(v[:HALF], 4), jnp.tile(v[HALF:], 4)]).reshape(1, 256)
    nat = lambda v: jnp.tile(v, 4).reshape(1, 256)
    sk = sinks[0].astype(F32)

    pos_s = PAST_LEN + (np.arange(TM_SAMPLE) % DEC_SEQ)
    xs = x_sample.reshape(DEC_BATCH * DEC_SEQ, D_MODEL)
    q_s, kt_s, vt_s, sza, sga, tb, nc_s = _sample_pre_call(
        xs, _rope_tables(pos_s, signed=True), _rope_tables_rows(pos_s), state_conv[0],
        nw, nat(qn), jnp.tile(kn, N_KV_HEADS).reshape(KV_WIDTH, 1), cw,
        wq_nat, wk_nat.T, wv.T, wza, wg, wpb, _head_mean_matrix(HEAD_DIM))
    to_tiles = lambda c: c[0].transpose(0, 2, 3, 1).reshape(DEC_BATCH, KV_WIDTH, WINDOW)
    from_tiles = lambda c: (c.reshape(DEC_BATCH, N_KV_HEADS, HEAD_DIM, WINDOW)
                            .transpose(0, 3, 1, 2)[None])
    ck_t = to_tiles(cache_k)
    cv_t = to_tiles(cache_v)

    tabs_p = _rope_tables(np.arange(SEQ) + N_META, signed=False)
    tabs_0 = _rope_tables(np.arange(BLOCK) - LEAD, signed=False)
    x0 = jnp.concatenate([jnp.zeros((LEAD, D_MODEL), x_prompt.dtype),
                          meta_tokens.astype(x_prompt.dtype)], axis=0)
    y_p, nk_p, nv_p, nc_p, nk_s, nv_s = _prompt_call(
        x_prompt[0], tabs_p, kt_s, vt_s, ck_t, cv_t, x0, tabs_0, nw, split(qn), split(kn), cw, sk,
        wq_split, wk_split, wza, wg, wpa, wpb, wo, _head_mean_matrix(HALF))
    nk_p = nk_p.reshape(BLOCK, 2, N_KV_HEADS, HALF).transpose(0, 2, 1, 3)

    skcol = jnp.repeat(sk.reshape(N_KV_HEADS, GROUP).T.reshape(-1), DEC_SEQ).reshape(-1, 1)
    y_s = _sample_attn_call(xs, q_s, kt_s, vt_s, ck_t, cv_t, sza, sga, tb, skcol, wpa, wo)

    return (y_p.reshape(1, SEQ, D_MODEL),
            y_s.reshape(DEC_BATCH, DEC_SEQ, D_MODEL),
            nk_p.reshape(1, 1, WINDOW, N_KV_HEADS, HEAD_DIM),
            nv_p.reshape(1, 1, WINDOW, N_KV_HEADS, HEAD_DIM),
            nc_p.reshape(1, 1, CONV_K - 1, D_MODEL),
            from_tiles(nk_s),
            from_tiles(nv_s),
            nc_s.reshape(1, DEC_BATCH, CONV_K - 1, D_MODEL))
```

```python
import functools

import numpy as np
import jax
import jax.numpy as jnp
from jax import lax
from jax.experimental import pallas as pl
from jax.experimental.pallas import tpu as pltpu

D_MODEL = 1024
SEQ = 16384
DEC_BATCH = 128
DEC_SEQ = 8
PAST_LEN = 16384
N_HEADS = 16
N_KV_HEADS = 4
GROUP = N_HEADS // N_KV_HEADS
HEAD_DIM = 64
HALF = HEAD_DIM // 2
KV_WIDTH = N_KV_HEADS * HEAD_DIM
WINDOW = 128
BLOCK = 128
ROPE_THETA = 10000.0
CONV_K = 3
N_META = 16
LEAD = BLOCK - N_META
EPS = 1e-6
NEG_INF = -1e30
Q_SCALE = HEAD_DIM ** -0.5

W_K = D_MODEL
W_V = W_K + KV_WIDTH
W_ZA = W_V + KV_WIDTH
G_B = W_ZA + D_MODEL
G_C = G_B + D_MODEL
G_H = G_C + D_MODEL
G_ZB = G_H + D_MODEL
G_GA = G_ZB + D_MODEL
G_GB = G_GA + D_MODEL
G_END = G_GB + D_MODEL

LANES = 128
SUBLANES = 8
VMEM_BYTES_V7X = 64 * 1024 * 1024

TM_PROMPT = 512
TM_SAMPLE = 512
BC_SAMPLE = LANES // DEC_SEQ
VMEM_LIMIT = VMEM_BYTES_V7X - 6 * 1024 * 1024

BF16 = jnp.bfloat16
F32 = jnp.float32


def _dot(a, b):
    return jnp.dot(a, b, preferred_element_type=F32)


def _dot_nt(a, b):
    return lax.dot_general(a, b, (((1,), (1,)), ((), ())), preferred_element_type=F32)


def _rms_rows(x, w):
    ms = jnp.mean(x * x, axis=-1, keepdims=True)
    return (x * lax.rsqrt(ms + EPS)) * w


def _head_rms(t, bd, w):
    n = t.shape[1] // 256
    r = t.shape[0]
    sq = (t * t).astype(BF16)
    stacked = jnp.concatenate([sq[:, 256 * c:256 * (c + 1)] for c in range(n)], axis=0)
    ms = _dot(stacked, bd)
    ms = jnp.concatenate([ms[r * c:r * (c + 1)] for c in range(n)], axis=1)
    wt = jnp.concatenate([w] * n, axis=1)
    return (t * lax.rsqrt(ms + EPS)) * wt


def _head_rms_split(t, bd, w):
    n = t.shape[1] // 256
    r = t.shape[0]
    halves = [(t[:, 256 * c:256 * c + LANES], t[:, 256 * c + LANES:256 * (c + 1)]) for c in range(n)]
    pieces = [a * a + b * b for a, b in halves]
    if n == 1:
        pieces = [pieces[0][:r // 2], pieces[0][r // 2:]]
    rp = pieces[0].shape[0]
    stacked = jnp.concatenate(
        [jnp.concatenate(pieces[2 * i:2 * i + 2], axis=1) for i in range(len(pieces) // 2)],
        axis=0).astype(BF16)
    ms = _dot(stacked, bd)
    ms = [ms[rp * (i // 2):rp * (i // 2 + 1), LANES * (i % 2):LANES * (i % 2 + 1)]
          for i in range(len(pieces))]
    if n == 1:
        ms = [jnp.concatenate(ms, axis=0)]
    out = []
    for (a, b), m in zip(halves, ms):
        scale = lax.rsqrt(m + EPS)
        out.append((a * scale) * w[:, :LANES])
        out.append((b * scale) * w[:, LANES:])
    return jnp.concatenate(out, axis=1)


def _rope_split(t, cos, sin):
    out = []
    for c in range(t.shape[1] // 256):
        a = t[:, 256 * c:256 * c + LANES]
        b = t[:, 256 * c + LANES:256 * (c + 1)]
        out.append(a * cos - b * sin)
        out.append(b * cos + a * sin)
    return jnp.concatenate(out, axis=1)


def _rope_natural(t, cos, sin_signed):
    lane = lax.broadcasted_iota(jnp.int32, (t.shape[0], LANES), 1)
    first_half = (lane & (HEAD_DIM - 1)) < HALF
    out = []
    for c in range(t.shape[1] // LANES):
        tc = t[:, LANES * c:LANES * (c + 1)]
        from_below = pltpu.roll(tc, HALF, 1)
        from_above = pltpu.roll(tc, LANES - HALF, 1)
        out.append(tc * cos + jnp.where(first_half, from_above, from_below) * sin_signed)
    return jnp.concatenate(out, axis=1)


def _rope_rows(t, cos, sin):
    out = []
    for g in range(t.shape[0] // HEAD_DIM):
        a = t[HEAD_DIM * g:HEAD_DIM * g + HALF]
        b = t[HEAD_DIM * g + HALF:HEAD_DIM * (g + 1)]
        out.append(a * cos - b * sin)
        out.append(b * cos + a * sin)
    return jnp.concatenate(out, axis=0)


def _lane_group_masks(group_of_lane, dtype):
    return [(group_of_lane == g).astype(dtype) for g in range(N_KV_HEADS)]


def _softmax_parts(s, sk):
    m = jnp.maximum(jnp.max(s, axis=-1, keepdims=True), sk)
    p = jnp.exp(s - m)
    den = jnp.sum(p, axis=-1, keepdims=True) + jnp.exp(sk - m)
    return p, 1.0 / den


def _silu(z):
    return z * jax.nn.sigmoid(z)


def _branch_b(xn, conv, wg_ref, wpb_ref):
    b = _dot(xn, wg_ref[:, G_B:G_C])
    z_b = _dot(xn, wg_ref[:, G_ZB:G_GA])
    b_in = ((b * conv) * _silu(z_b)).astype(BF16)
    return _dot(b_in, wpb_ref[...])


def _shift_cache_windows(i, kt_ref, vt_ref, ck_ref, cv_ref, nck_ref, ncv_ref):
    per_step = ck_ref.shape[0]
    new_lane = lax.broadcasted_iota(jnp.int32, (KV_WIDTH, WINDOW), 1) >= WINDOW - DEC_SEQ
    ktc = kt_ref[...]
    vtc = vt_ref[...]
    first = (i * per_step) % BC_SAMPLE
    for b in range(per_step):
        shift_new = WINDOW - DEC_SEQ - (first + b) * DEC_SEQ
        nck_ref[b] = jnp.where(new_lane, pltpu.roll(ktc, shift_new, 1),
                               pltpu.roll(ck_ref[b], WINDOW - DEC_SEQ, 1))
        ncv_ref[b] = jnp.where(new_lane, pltpu.roll(vtc, shift_new, 1),
                               pltpu.roll(cv_ref[b], WINDOW - DEC_SEQ, 1))


def _prompt_kernel(sinks_ref, x_ref, cos_ref, sin_ref, kt_ref, vt_ref, ck_ref, cv_ref,
                   meta_ref, cos0_ref, sin0_ref, nw_ref,
                   qnw_ref, knw_ref, cw_ref, wq_ref, wk_ref, wza_ref, wg_ref, wpa_ref,
                   wpb_ref, wo_ref, bd_ref,
                   y_ref, nkt_ref, nvt_ref, nc_ref, nck_ref, ncv_ref, kprev, vprev, utail, *, tm):
    i = pl.program_id(0)
    nw = nw_ref[...]
    bd = bd_ref[...]
    knw = knw_ref[...]

    @pl.when(i == 0)
    def _init():
        x0 = jnp.concatenate([jnp.zeros((LEAD, D_MODEL), F32), meta_ref[...]], axis=0)
        xn0 = _rms_rows(x0, nw).astype(BF16)
        k0 = _dot(xn0, wk_ref[...])
        k0 = _rope_split(_head_rms_split(k0, bd, knw), cos0_ref[...], sin0_ref[...])
        v0 = _dot(xn0, wg_ref[:, W_V:W_ZA])
        c0 = _dot(xn0, wg_ref[:, G_C:G_H])
        h0 = _dot(xn0, wg_ref[:, G_H:G_ZB])
        row = lax.broadcasted_iota(jnp.int32, (BLOCK, D_MODEL), 0)
        u0 = jnp.where(row >= LEAD, c0 * h0, 0.0)
        kprev[...] = k0
        vprev[...] = v0
        utail[...] = u0[BLOCK - SUBLANES:BLOCK]

    _shift_cache_windows(i, kt_ref, vt_ref, ck_ref, cv_ref, nck_ref, ncv_ref)

    x = x_ref[...]
    xn = _rms_rows(x, nw).astype(BF16)
    cos = cos_ref[...]
    sin = sin_ref[...]

    q = _head_rms_split(_dot(xn, wq_ref[...]), bd, qnw_ref[...])
    q = _rope_split(q, cos * Q_SCALE, sin * Q_SCALE).astype(BF16)
    k = _rope_split(_head_rms_split(_dot(xn, wk_ref[...]), bd, knw), cos, sin)
    v = _dot(xn, wg_ref[:, W_V:W_ZA])
    kb = k.astype(BF16)
    vb = v.astype(BF16)

    lane = lax.broadcasted_iota(jnp.int32, (1, 2 * BLOCK), 1)
    kmasks = _lane_group_masks((lane & (LANES - 1)) >> 5, BF16)
    vmasks = _lane_group_masks(lane >> 6, BF16)
    vgroup = lane >> 6
    r_io = lax.broadcasted_iota(jnp.int32, (BLOCK, 2 * BLOCK), 0)
    c_io = lax.broadcasted_iota(jnp.int32, (BLOCK, 2 * BLOCK), 1)
    band = (c_io >= r_io) & (c_io <= r_io + WINDOW)
    first_band = band & (c_io >= jnp.where(i == 0, LEAD, 0))

    o_blocks = []
    for b in range(tm // BLOCK):
        lo = b * BLOCK
        if b == 0:
            kpair = jnp.concatenate([kprev[...].astype(BF16), kb[:BLOCK]], axis=0)
            vpair = jnp.concatenate([vprev[...].astype(BF16), vb[:BLOCK]], axis=0)
            mask = first_band
        else:
            kpair = kb[lo - BLOCK:lo + BLOCK]
            vpair = vb[lo - BLOCK:lo + BLOCK]
            mask = band
        kbd = jnp.concatenate([kpair * km for km in kmasks], axis=0)
        vbd = jnp.concatenate([vpair * vm for vm in vmasks], axis=0)
        qs = jnp.concatenate([q[lo:lo + BLOCK, 256 * j:256 * (j + 1)] for j in range(GROUP)],
                             axis=0)
        s = _dot_nt(qs, kbd)
        p_rows = []
        mults = []
        for j in range(GROUP):
            p_cols = []
            invs = []
            for g in range(N_KV_HEADS):
                sl = s[BLOCK * j:BLOCK * (j + 1), 256 * g:256 * (g + 1)]
                sl = jnp.where(mask, sl, NEG_INF)
                p, inv = _softmax_parts(sl, sinks_ref[0, GROUP * g + j])
                p_cols.append(p.astype(BF16))
                invs.append(inv)
            p_rows.append(jnp.concatenate(p_cols, axis=1))
            mults.append(jnp.where(vgroup == 0, invs[0],
                                   jnp.where(vgroup == 1, invs[1],
                                             jnp.where(vgroup == 2, invs[2], invs[3]))))
        o = _dot(jnp.concatenate(p_rows, axis=0), vbd)
        o_blocks.append(jnp.concatenate(
            [o[BLOCK * j:BLOCK * (j + 1)] * mults[j] for j in range(GROUP)], axis=1))
    o_a = o_blocks[0] if len(o_blocks) == 1 else jnp.concatenate(o_blocks, axis=0)

    kprev[...] = k[tm - BLOCK:]
    vprev[...] = v[tm - BLOCK:]
    kt = k[tm - BLOCK:].T
    nkt_ref[...] = jnp.concatenate(
        [kt[LANES * half + HALF * g:LANES * half + HALF * (g + 1)]
         for g in range(N_KV_HEADS) for half in range(2)], axis=0)
    nvt_ref[...] = v[tm - BLOCK:].T

    c = _dot(xn, wg_ref[:, G_C:G_H])
    hc = _dot(xn, wg_ref[:, G_H:G_ZB])
    u = c * hc
    tail = utail[...]
    prev1 = tail[SUBLANES - 1:SUBLANES]
    prev2 = tail[SUBLANES - 2:SUBLANES - 1]
    row8 = lax.broadcasted_iota(jnp.int32, (SUBLANES, D_MODEL), 0)
    r1 = pltpu.roll(u, 1, 0)
    r2 = pltpu.roll(u, 2, 0)
    um1 = jnp.concatenate(
        [jnp.where(row8 == 0, prev1, r1[:SUBLANES]), r1[SUBLANES:]], axis=0)
    um2 = jnp.concatenate(
        [jnp.where(row8 == 0, prev2, jnp.where(row8 == 1, prev1, r2[:SUBLANES])),
         r2[SUBLANES:]], axis=0)
    cw = cw_ref[0]
    conv = cw[0:1] * um2 + cw[1:2] * um1 + cw[2:3] * u
    utail[...] = u[tm - SUBLANES:]
    nc_ref[...] = u[tm - (CONV_K - 1):]

    z_a = _dot(xn, wza_ref[...])
    br_a = _dot((o_a * _silu(z_a)).astype(BF16), wpa_ref[...])
    br_b = _branch_b(xn, conv, wg_ref, wpb_ref)
    g_a = _dot(xn, wg_ref[:, G_GA:G_GB])
    g_b = _dot(xn, wg_ref[:, G_GB:G_END])
    mixed = (jax.nn.sigmoid(g_a) * br_a + jax.nn.sigmoid(g_b) * br_b).astype(BF16)
    y_ref[...] = x + _dot(mixed, wo_ref[...])


def _const_spec(shape):
    zeros = (0,) * len(shape)
    return pl.BlockSpec(shape, lambda i: zeros, pipeline_mode=pl.Buffered(1))


def _params():
    return pltpu.CompilerParams(dimension_semantics=("arbitrary",), vmem_limit_bytes=VMEM_LIMIT)


def _prompt_call(x, tabs, kt, vt, ck, cv, x0, tabs0, nw, qnw, knw, cw, sinks, wq, wk, wza, wg, wpa,
                 wpb, wo, bd):
    tm = TM_PROMPT
    n_steps = SEQ // tm
    per_step = DEC_BATCH // n_steps
    assert per_step * n_steps == DEC_BATCH and BC_SAMPLE % per_step == 0
    row_spec = lambda w: pl.BlockSpec((tm, w), lambda i: (i, 0))
    new_spec = pl.BlockSpec((KV_WIDTH, LANES), lambda i: (0, (i * per_step) // BC_SAMPLE))
    cache_spec = pl.BlockSpec((per_step, KV_WIDTH, WINDOW), lambda i: (i, 0, 0))
    consts = (x0, tabs0[0], tabs0[1], nw, qnw, knw, cw, wq, wk, wza, wg, wpa, wpb, wo, bd)
    in_specs = [pl.BlockSpec(memory_space=pltpu.SMEM),
                row_spec(D_MODEL), row_spec(LANES), row_spec(LANES),
                new_spec, new_spec, cache_spec, cache_spec]
    in_specs += [_const_spec(a.shape) for a in consts]
    resident = lambda shape: pl.BlockSpec(shape, lambda i: (0, 0))
    out_specs = [row_spec(D_MODEL), resident((KV_WIDTH, BLOCK)), resident((KV_WIDTH, BLOCK)),
                 resident((CONV_K - 1, D_MODEL)), cache_spec, cache_spec]
    cache_sds = jax.ShapeDtypeStruct((DEC_BATCH, KV_WIDTH, WINDOW), F32)
    out_shape = [jax.ShapeDtypeStruct((SEQ, D_MODEL), F32),
                 jax.ShapeDtypeStruct((KV_WIDTH, BLOCK), F32),
                 jax.ShapeDtypeStruct((KV_WIDTH, BLOCK), F32),
                 jax.ShapeDtypeStruct((CONV_K - 1, D_MODEL), F32), cache_sds, cache_sds]
    return pl.pallas_call(
        functools.partial(_prompt_kernel, tm=tm),
        grid=(n_steps,),
        in_specs=in_specs,
        out_specs=out_specs,
        out_shape=out_shape,
        scratch_shapes=[pltpu.VMEM((BLOCK, KV_WIDTH), F32),
                        pltpu.VMEM((BLOCK, KV_WIDTH), F32),
                        pltpu.VMEM((SUBLANES, D_MODEL), F32)],
        compiler_params=_params(),
        name="prompt_layer",
    )(sinks, x, tabs[0], tabs[1], kt, vt, ck, cv, *consts)


def _sample_pre_kernel(x_ref, cos_ref, sin_ref, cost_ref, sint_ref, st_ref, nw_ref, qnw_ref,
                       knwc_ref, cw_ref, wq_ref, wkt_ref, wvt_ref, wza_ref, wg_ref, wpb_ref, bd_ref,
                       q_ref, kt_ref, vt_ref, sza_ref, sga_ref, tb_ref, nc_ref, *, rows):
    t_len = DEC_SEQ
    bc = rows // t_len
    bd = bd_ref[...]
    x = x_ref[...]
    xn = _rms_rows(x, nw_ref[...]).astype(BF16)

    q = _head_rms(_dot(xn, wq_ref[...]), bd, qnw_ref[...])
    q_ref[...] = _rope_natural(q, cos_ref[...] * Q_SCALE, sin_ref[...] * Q_SCALE)

    kt = _dot_nt(wkt_ref[...], xn)
    ms = _dot(bd, (kt * kt).astype(BF16))
    kt = (kt * lax.rsqrt(ms + EPS)) * knwc_ref[...]
    kt_ref[...] = _rope_rows(kt, cost_ref[...], sint_ref[...])
    vt_ref[...] = _dot_nt(wvt_ref[...], xn)

    c = _dot(xn, wg_ref[:, G_C:G_H])
    hc = _dot(xn, wg_ref[:, G_H:G_ZB])
    u3 = (c * hc).reshape(bc, t_len, D_MODEL)
    st = st_ref[...]
    s_m2 = st[:, 0:1, :]
    s_m1 = st[:, 1:2, :]
    t_io = lax.broadcasted_iota(jnp.int32, (bc, t_len, D_MODEL), 1)
    um1 = jnp.where(t_io == 0, s_m1, pltpu.roll(u3, 1, 1))
    um2 = jnp.where(t_io == 0, s_m2, jnp.where(t_io == 1, s_m1, pltpu.roll(u3, 2, 1)))
    cw = cw_ref[0]
    conv = (cw[0:1] * um2 + cw[1:2] * um1 + cw[2:3] * u3).reshape(rows, D_MODEL)
    nc_ref[...] = u3[:, t_len - (CONV_K - 1):, :]

    sza_ref[...] = _silu(_dot(xn, wza_ref[...]))
    sga_ref[...] = jax.nn.sigmoid(_dot(xn, wg_ref[:, G_GA:G_GB]))
    br_b = _branch_b(xn, conv, wg_ref, wpb_ref)
    tb_ref[...] = jax.nn.sigmoid(_dot(xn, wg_ref[:, G_GB:G_END])) * br_b


def _sample_pre_call(x, tabs, tabs_t, st, nw, qnw, knwc, cw, wq, wkt, wvt, wza, wg, wpb, bd):
    rows = TM_SAMPLE
    bc = rows // DEC_SEQ
    n_rows = DEC_BATCH * DEC_SEQ
    row_spec = pl.BlockSpec((rows, D_MODEL), lambda i: (i, 0))
    col_spec = pl.BlockSpec((KV_WIDTH, rows), lambda i: (0, i))
    st_spec = pl.BlockSpec((bc, CONV_K - 1, D_MODEL), lambda i: (i, 0, 0))
    consts_a = (tabs[0], tabs[1], tabs_t[0], tabs_t[1])
    consts_b = (nw, qnw, knwc, cw, wq, wkt, wvt, wza, wg, wpb, bd)
    in_specs = [row_spec] + [_const_spec(a.shape) for a in consts_a] + [st_spec]
    in_specs += [_const_spec(a.shape) for a in consts_b]
    out_specs = [row_spec, col_spec, col_spec, row_spec, row_spec, row_spec, st_spec]
    rows_sds = jax.ShapeDtypeStruct((n_rows, D_MODEL), F32)
    cols_sds = jax.ShapeDtypeStruct((KV_WIDTH, n_rows), F32)
    out_shape = [rows_sds, cols_sds, cols_sds, rows_sds, rows_sds, rows_sds,
                 jax.ShapeDtypeStruct((DEC_BATCH, CONV_K - 1, D_MODEL), F32)]
    return pl.pallas_call(
        functools.partial(_sample_pre_kernel, rows=rows),
        grid=(n_rows // rows,),
        in_specs=in_specs,
        out_specs=out_specs,
        out_shape=out_shape,
        compiler_params=_params(),
        name="sample_pre",
    )(x, *consts_a, st, *consts_b)


def _sample_attn_kernel(x_ref, q_ref, kt_ref, vt_ref, ck_ref, cv_ref, sza_ref, sga_ref, tb_ref,
                        sinks_ref, wpa_ref, wo_ref,
                        y_ref, oa_s, *, bc):
    t_len = DEC_SEQ
    n_exp = GROUP * N_KV_HEADS * t_len
    ktc_b = kt_ref[...].astype(BF16)
    vtc_b = vt_ref[...].astype(BF16)
    lane = lax.broadcasted_iota(jnp.int32, (1, KV_WIDTH), 1)
    vgroup = lane >> 6
    t_io = lax.broadcasted_iota(jnp.int32, (n_exp, WINDOW), 0) & (t_len - 1)
    c_io = lax.broadcasted_iota(jnp.int32, (n_exp, WINDOW), 1)
    old_ok = c_io >= t_io
    c_minus_t = c_io - t_io
    skcol = jnp.concatenate(
        [jnp.full((t_len, 1), sinks_ref[0, GROUP * g + j], F32)
         for j in range(GROUP) for g in range(N_KV_HEADS)], axis=0)

    def per_batch(n):
        r0 = n * t_len
        qn = q_ref[r0:r0 + t_len, :]
        q_exp = jnp.concatenate(
            [jnp.where(vgroup == g, qn[:, 256 * j:256 * (j + 1)], 0.0)
             for j in range(GROUP) for g in range(N_KV_HEADS)], axis=0).astype(BF16)
        kt_old = ck_ref[n]
        vt_old = cv_ref[n]
        keys = jnp.concatenate([kt_old.astype(BF16), ktc_b], axis=1)
        vals = jnp.concatenate([vt_old.astype(BF16), vtc_b], axis=1)
        s = _dot(q_exp, keys)
        new_ok = (c_io >= r0) & (c_minus_t <= r0)
        s = jnp.where(jnp.concatenate([old_ok, new_ok], axis=1), s, NEG_INF)
        p, inv = _softmax_parts(s, skcol)
        o = _dot_nt(p.astype(BF16), vals) * inv
        outs = []
        for j in range(GROUP):
            acc = None
            for g in range(N_KV_HEADS):
                r = (j * N_KV_HEADS + g) * t_len
                piece = jnp.where(vgroup == g, o[r:r + t_len], 0.0)
                acc = piece if acc is None else acc + piece
            outs.append(acc)
        oa_s[r0:r0 + t_len, :] = jnp.concatenate(outs, axis=1)

    for n in range(bc):
        per_batch(n)

    a_in = (oa_s[...] * sza_ref[...]).astype(BF16)
    br_a = _dot(a_in, wpa_ref[...])
    mixed = (sga_ref[...] * br_a + tb_ref[...]).astype(BF16)
    y_ref[...] = x_ref[...] + _dot(mixed, wo_ref[...])


def _sample_attn_call(x, q, kt, vt, ck, cv, sza, sga, tb, sinks, wpa, wo):
    bc = BC_SAMPLE
    rows = bc * DEC_SEQ
    row_spec = pl.BlockSpec((rows, D_MODEL), lambda i: (i, 0))
    col_spec = pl.BlockSpec((KV_WIDTH, rows), lambda i: (0, i))
    cache_spec = pl.BlockSpec((bc, KV_WIDTH, WINDOW), lambda i: (i, 0, 0))
    in_specs = [row_spec, row_spec, col_spec, col_spec, cache_spec, cache_spec,
                row_spec, row_spec, row_spec,
                pl.BlockSpec(memory_space=pltpu.SMEM), _const_spec(wpa.shape), _const_spec(wo.shape)]
    out_specs = row_spec
    out_shape = jax.ShapeDtypeStruct((DEC_BATCH * DEC_SEQ, D_MODEL), F32)
    return pl.pallas_call(
        functools.partial(_sample_attn_kernel, bc=bc),
        grid=(DEC_BATCH // bc,),
        in_specs=in_specs,
        out_specs=out_specs,
        out_shape=out_shape,
        scratch_shapes=[pltpu.VMEM((rows, D_MODEL), F32)],
        compiler_params=_params(),
        name="sample_attn",
    )(x, q, kt, vt, ck, cv, sza, sga, tb, sinks, wpa, wo)


def _rope_angles(positions):
    inv = np.power(ROPE_THETA, -np.arange(HALF, dtype=np.float64) * (2.0 / HEAD_DIM))
    return np.asarray(positions, dtype=np.float64)[:, None] * inv[None, :]


def _rope_tables(positions, signed):
    ang = _rope_angles(positions)
    cos = np.tile(np.cos(ang), (1, LANES // HALF))
    sin = np.tile(np.sin(ang), (1, LANES // HALF))
    if signed:
        sign = np.where((np.arange(LANES) % HEAD_DIM) < HALF, -1.0, 1.0)
        sin = sin * sign[None, :]
    return jnp.asarray(cos, F32), jnp.asarray(sin, F32)


def _rope_tables_rows(positions):
    ang = _rope_angles(positions).T
    return jnp.asarray(np.cos(ang), F32), jnp.asarray(np.sin(ang), F32)


def _head_mean_matrix(group_lanes):
    grp = np.arange(256) // group_lanes
    return jnp.asarray((grp[:, None] == grp[None, :]) / float(HEAD_DIM), BF16)


def kernel(x_prompt, x_sample, cache_k, cache_v, state_conv, meta_tokens, norm_w, w_in,
           q_norm_w, k_norm_w, sinks, conv_w, w_proj_a, w_proj_b, w_out):
    assert x_prompt.shape == (1, SEQ, D_MODEL) and x_sample.shape == (DEC_BATCH, DEC_SEQ, D_MODEL)
    assert w_in.shape[0] == 1, "single layer"
    w = w_in[0].astype(BF16)
    wq5 = w[:, :W_K].reshape(D_MODEL, N_KV_HEADS, GROUP, 2, HALF)
    wk_nat = w[:, W_K:W_V]
    wv = w[:, W_V:W_ZA]
    wq_split = wq5.transpose(0, 2, 3, 1, 4).reshape(D_MODEL, D_MODEL)
    wq_nat = wq5.transpose(0, 2, 1, 3, 4).reshape(D_MODEL, D_MODEL)
    wk_split = (wk_nat.reshape(D_MODEL, N_KV_HEADS, 2, HALF)
                .transpose(0, 2, 1, 3).reshape(D_MODEL, KV_WIDTH))
    wg = w
    wza = (w[:, W_ZA:G_B].reshape(D_MODEL, N_KV_HEADS, GROUP, HEAD_DIM)
           .transpose(0, 2, 1, 3).reshape(D_MODEL, D_MODEL))
    wpa = (w_proj_a[0].astype(BF16).reshape(N_KV_HEADS, GROUP, HEAD_DIM, D_MODEL)
           .transpose(1, 0, 2, 3).reshape(D_MODEL, D_MODEL))
    wpb = w_proj_b[0].astype(BF16)
    wo = w_out[0].astype(BF16)

    nw = norm_w[0].reshape(1, D_MODEL)
    lane = np.arange(256)
    idx_split = HALF * (lane // LANES) + lane % HALF
    idx_nat = lane % HEAD_DIM
    qn = q_norm_w[0]
    kn = k_norm_w[0]
    qn_split = qn[idx_split].reshape(1, 256)
    kn_split = kn[idx_split].reshape(1, 256)
    qn_nat = qn[idx_nat].reshape(1, 256)
    kn_col = kn[idx_nat].reshape(KV_WIDTH, 1)

    pos_s = PAST_LEN + (np.arange(TM_SAMPLE) % DEC_SEQ)
    xs = x_sample.reshape(DEC_BATCH * DEC_SEQ, D_MODEL)
    q_s, kt_s, vt_s, sza, sga, tb, nc_s = _sample_pre_call(
        xs, _rope_tables(pos_s, signed=True), _rope_tables_rows(pos_s), state_conv[0],
        nw, qn_nat, kn_col, conv_w,
        wq_nat, wk_nat.T, wv.T, wza, wg, wpb, _head_mean_matrix(HEAD_DIM))
    to_tiles = lambda c: c[0].transpose(0, 2, 3, 1).reshape(DEC_BATCH, KV_WIDTH, WINDOW)
    from_tiles = lambda c: (c.reshape(DEC_BATCH, N_KV_HEADS, HEAD_DIM, WINDOW)
                            .transpose(0, 3, 1, 2)[None])
    ck_t = to_tiles(cache_k)
    cv_t = to_tiles(cache_v)

    tabs_p = _rope_tables(np.arange(SEQ) + N_META, signed=False)
    tabs_0 = _rope_tables(np.arange(BLOCK) - LEAD, signed=False)
    y_p, nkt_p, nvt_p, nc_p, nk_s, nv_s = _prompt_call(
        x_prompt[0], tabs_p, kt_s, vt_s, ck_t, cv_t, meta_tokens.astype(x_prompt.dtype), tabs_0,
        nw, qn_split, kn_split, conv_w, sinks,
        wq_split, wk_split, wza, wg, wpa, wpb, wo, _head_mean_matrix(HALF))
    from_tile = lambda t: t.reshape(N_KV_HEADS, HEAD_DIM, WINDOW).transpose(2, 0, 1)

    y_s = _sample_attn_call(xs, q_s, kt_s, vt_s, ck_t, cv_t, sza, sga, tb, sinks, wpa, wo)

    return (y_p.reshape(1, SEQ, D_MODEL),
            y_s.reshape(DEC_BATCH, DEC_SEQ, D_MODEL),
            from_tile(nkt_p).reshape(1, 1, WINDOW, N_KV_HEADS, HEAD_DIM),
            from_tile(nvt_p).reshape(1, 1, WINDOW, N_KV_HEADS, HEAD_DIM),
            nc_p.reshape(1, 1, CONV_K - 1, D_MODEL),
            from_tiles(nk_s),
            from_tiles(nv_s),
            nc_s.reshape(1, DEC_BATCH, CONV_K - 1, D_MODEL))
```

```python
import functools

import numpy as np
import jax
import jax.numpy as jnp
from jax import lax
from jax.experimental import pallas as pl
from jax.experimental.pallas import tpu as pltpu

D_MODEL = 1024
SEQ = 16384
DEC_BATCH = 128
DEC_SEQ = 8
PAST_LEN = 16384
N_HEADS = 16
N_KV_HEADS = 4
GROUP = N_HEADS // N_KV_HEADS
HEAD_DIM = 64
HALF = HEAD_DIM // 2
KV_WIDTH = N_KV_HEADS * HEAD_DIM
WINDOW = 128
BLOCK = 128
ROPE_THETA = 10000.0
CONV_K = 3
N_META = 16
LEAD = BLOCK - N_META
EPS = 1e-6
NEG_INF = -1e30
Q_SCALE = HEAD_DIM ** -0.5

W_K = D_MODEL
W_V = W_K + KV_WIDTH
W_ZA = W_V + KV_WIDTH
G_B = W_ZA + D_MODEL
G_C = G_B + D_MODEL
G_H = G_C + D_MODEL
G_ZB = G_H + D_MODEL
G_GA = G_ZB + D_MODEL
G_GB = G_GA + D_MODEL
G_END = G_GB + D_MODEL

LANES = 128
SUBLANES = 8
VMEM_BYTES_V7X = 64 * 1024 * 1024

TM_PROMPT = 512
N_STEPS = SEQ // TM_PROMPT
DEC_PER_STEP = DEC_BATCH // N_STEPS
TS = DEC_PER_STEP * DEC_SEQ
VMEM_LIMIT = VMEM_BYTES_V7X - 4 * 1024 * 1024

BF16 = jnp.bfloat16
F32 = jnp.float32


def _dot(a, b):
    return jnp.dot(a, b, preferred_element_type=F32)


def _dot_nt(a, b):
    return lax.dot_general(a, b, (((1,), (1,)), ((), ())), preferred_element_type=F32)


def _rms_rows(x, w):
    ms = jnp.mean(x * x, axis=-1, keepdims=True)
    return (x * lax.rsqrt(ms + EPS)) * w


def _head_rms_split(t, bd, w):
    n = t.shape[1] // 256
    r = t.shape[0]
    halves = [(t[:, 256 * c:256 * c + LANES], t[:, 256 * c + LANES:256 * (c + 1)]) for c in range(n)]
    pieces = [a * a + b * b for a, b in halves]
    if n == 1:
        pieces = [pieces[0][:r // 2], pieces[0][r // 2:]]
    rp = pieces[0].shape[0]
    stacked = jnp.concatenate(
        [jnp.concatenate(pieces[2 * i:2 * i + 2], axis=1) for i in range(len(pieces) // 2)],
        axis=0).astype(BF16)
    ms = _dot(stacked, bd)
    ms = [ms[rp * (i // 2):rp * (i // 2 + 1), LANES * (i % 2):LANES * (i % 2 + 1)]
          for i in range(len(pieces))]
    if n == 1:
        ms = [jnp.concatenate(ms, axis=0)]
    out = []
    for (a, b), m in zip(halves, ms):
        scale = lax.rsqrt(m + EPS)
        out.append((a * scale) * w[:, :LANES])
        out.append((b * scale) * w[:, LANES:])
    return jnp.concatenate(out, axis=1)


def _rope_split(t, cos, sin):
    out = []
    for c in range(t.shape[1] // 256):
        a = t[:, 256 * c:256 * c + LANES]
        b = t[:, 256 * c + LANES:256 * (c + 1)]
        out.append(a * cos - b * sin)
        out.append(b * cos + a * sin)
    return jnp.concatenate(out, axis=1)


def _lane_group_masks(group_of_lane, dtype):
    return [(group_of_lane == g).astype(dtype) for g in range(N_KV_HEADS)]


def _softmax_parts(s, sk):
    m = jnp.maximum(jnp.max(s, axis=-1, keepdims=True), sk)
    p = jnp.exp(s - m)
    den = jnp.sum(p, axis=-1, keepdims=True) + jnp.exp(sk - m)
    return p, 1.0 / den


def _silu(z):
    return z * jax.nn.sigmoid(z)


def _rows_split_to_natural(t):
    return jnp.concatenate(
        [t[LANES * half + HALF * g:LANES * half + HALF * (g + 1)]
         for g in range(N_KV_HEADS) for half in range(2)], axis=0)


def _rows_natural_to_split(t):
    return jnp.concatenate(
        [t[HEAD_DIM * g + HALF * half:HEAD_DIM * g + HALF * (half + 1)]
         for half in range(2) for g in range(N_KV_HEADS)], axis=0)


def _prompt_attention(q, kb, vb, kprev, vprev, sinks_ref, first_step):
    tm = q.shape[0]
    lane = lax.broadcasted_iota(jnp.int32, (1, 2 * BLOCK), 1)
    kmasks = _lane_group_masks((lane & (LANES - 1)) >> 5, BF16)
    vmasks = _lane_group_masks(lane >> 6, BF16)
    vgroup = lane >> 6
    r_io = lax.broadcasted_iota(jnp.int32, (BLOCK, 2 * BLOCK), 0)
    c_io = lax.broadcasted_iota(jnp.int32, (BLOCK, 2 * BLOCK), 1)
    band = (c_io >= r_io) & (c_io <= r_io + WINDOW)
    first_band = band & (c_io >= jnp.where(first_step, LEAD, 0))

    o_blocks = []
    for b in range(tm // BLOCK):
        lo = b * BLOCK
        if b == 0:
            kpair = jnp.concatenate([kprev[...].astype(BF16), kb[:BLOCK]], axis=0)
            vpair = jnp.concatenate([vprev[...].astype(BF16), vb[:BLOCK]], axis=0)
            mask = first_band
        else:
            kpair = kb[lo - BLOCK:lo + BLOCK]
            vpair = vb[lo - BLOCK:lo + BLOCK]
            mask = band
        kbd = jnp.concatenate([kpair * km for km in kmasks], axis=0)
        vbd = jnp.concatenate([vpair * vm for vm in vmasks], axis=0)
        qs = jnp.concatenate([q[lo:lo + BLOCK, 256 * j:256 * (j + 1)] for j in range(GROUP)],
                             axis=0)
        s = _dot_nt(qs, kbd)
        p_rows = []
        mults = []
        for j in range(GROUP):
            p_cols = []
            invs = []
            for g in range(N_KV_HEADS):
                sl = s[BLOCK * j:BLOCK * (j + 1), 256 * g:256 * (g + 1)]
                sl = jnp.where(mask, sl, NEG_INF)
                p, inv = _softmax_parts(sl, sinks_ref[0, GROUP * g + j])
                p_cols.append(p.astype(BF16))
                invs.append(inv)
            p_rows.append(jnp.concatenate(p_cols, axis=1))
            mults.append(jnp.where(vgroup == 0, invs[0],
                                   jnp.where(vgroup == 1, invs[1],
                                             jnp.where(vgroup == 2, invs[2], invs[3]))))
        o = _dot(jnp.concatenate(p_rows, axis=0), vbd)
        o_blocks.append(jnp.concatenate(
            [o[BLOCK * j:BLOCK * (j + 1)] * mults[j] for j in range(GROUP)], axis=1))
    return jnp.concatenate(o_blocks, axis=0)


def _decode_attention(qs, ks, vs, ck_ref, cv_ref, nck_ref, ncv_ref, sinks_ref):
    t_len = DEC_SEQ
    n_exp = GROUP * N_KV_HEADS * t_len
    pad = jnp.zeros((LANES - TS, KV_WIDTH), F32)
    kst = jnp.concatenate([ks, pad], axis=0).T
    vst = jnp.concatenate([vs, pad], axis=0).T
    kst_nat = _rows_split_to_natural(kst)
    kst_b = kst.astype(BF16)
    vst_b = vst.astype(BF16)

    lane = lax.broadcasted_iota(jnp.int32, (1, KV_WIDTH), 1)
    kgroup = (lane & (LANES - 1)) >> 5
    vgroup = lane >> 6
    t_io = lax.broadcasted_iota(jnp.int32, (n_exp, WINDOW), 0) & (t_len - 1)
    c_io = lax.broadcasted_iota(jnp.int32, (n_exp, WINDOW), 1)
    old_ok = c_io >= t_io
    new_lane = lax.broadcasted_iota(jnp.int32, (KV_WIDTH, WINDOW), 1) >= WINDOW - t_len
    skcol = jnp.concatenate(
        [jnp.full((t_len, 1), sinks_ref[0, GROUP * g + j], F32)
         for j in range(GROUP) for g in range(N_KV_HEADS)], axis=0)

    outs = []
    for b in range(DEC_PER_STEP):
        r0 = b * t_len
        qn = qs[r0:r0 + t_len]
        q_exp = jnp.concatenate(
            [jnp.where(kgroup == g, qn[:, 256 * j:256 * (j + 1)], 0.0)
             for j in range(GROUP) for g in range(N_KV_HEADS)], axis=0).astype(BF16)
        kt_old = ck_ref[b]
        vt_old = cv_ref[b]
        keys = jnp.concatenate([_rows_natural_to_split(kt_old).astype(BF16), kst_b], axis=1)
        vals = jnp.concatenate([vt_old.astype(BF16), vst_b], axis=1)
        s = _dot(q_exp, keys)
        new_ok = (c_io >= r0) & (c_io - t_io <= r0)
        s = jnp.where(jnp.concatenate([old_ok, new_ok], axis=1), s, NEG_INF)
        p, inv = _softmax_parts(s, skcol)
        o = _dot_nt(p.astype(BF16), vals) * inv
        o_j = []
        for j in range(GROUP):
            acc = None
            for g in range(N_KV_HEADS):
                r = (j * N_KV_HEADS + g) * t_len
                piece = jnp.where(vgroup == g, o[r:r + t_len], 0.0)
                acc = piece if acc is None else acc + piece
            o_j.append(acc)
        outs.append(jnp.concatenate(o_j, axis=1))
        shift_new = WINDOW - t_len - r0
        nck_ref[b] = jnp.where(new_lane, pltpu.roll(kst_nat, shift_new, 1),
                               pltpu.roll(kt_old, WINDOW - t_len, 1))
        ncv_ref[b] = jnp.where(new_lane, pltpu.roll(vst, shift_new, 1),
                               pltpu.roll(vt_old, WINDOW - t_len, 1))
    return jnp.concatenate(outs, axis=0)


def _layer_kernel(sinks_ref, x_ref, xs_ref, cos_ref, sin_ref, ck_ref, cv_ref, st_ref,
                  meta_ref, cos0_ref, sin0_ref, coss_ref, sins_ref, nw_ref, qnw_ref, knw_ref, cw_ref,
                  wq_ref, wk_ref, wza_ref, wg_ref, wpa_ref, wpb_ref, wo_ref, bd_ref,
                  y_ref, ys_ref, nkt_ref, nvt_ref, nc_ref, nck_ref, ncv_ref, ncs_ref,
                  kprev, vprev, utail):
    tm = TM_PROMPT
    i = pl.program_id(0)
    nw = nw_ref[...]
    bd = bd_ref[...]
    knw = knw_ref[...]

    @pl.when(i == 0)
    def _init():
        x0 = jnp.concatenate([jnp.zeros((LEAD, D_MODEL), F32), meta_ref[...]], axis=0)
        xn0 = _rms_rows(x0, nw).astype(BF16)
        k0 = _dot(xn0, wk_ref[...])
        k0 = _rope_split(_head_rms_split(k0, bd, knw), cos0_ref[...], sin0_ref[...])
        v0 = _dot(xn0, wg_ref[:, W_V:W_ZA])
        c0 = _dot(xn0, wg_ref[:, G_C:G_H])
        h0 = _dot(xn0, wg_ref[:, G_H:G_ZB])
        row = lax.broadcasted_iota(jnp.int32, (BLOCK, D_MODEL), 0)
        u0 = jnp.where(row >= LEAD, c0 * h0, 0.0)
        kprev[...] = k0
        vprev[...] = v0
        utail[...] = u0[BLOCK - SUBLANES:BLOCK]

    x = jnp.concatenate([x_ref[...], xs_ref[...]], axis=0)
    xn = _rms_rows(x, nw).astype(BF16)
    cos = jnp.concatenate([cos_ref[...], coss_ref[...]], axis=0)
    sin = jnp.concatenate([sin_ref[...], sins_ref[...]], axis=0)

    q = _head_rms_split(_dot(xn, wq_ref[...]), bd, qnw_ref[...])
    q = _rope_split(q, cos * Q_SCALE, sin * Q_SCALE)
    k = _rope_split(_head_rms_split(_dot(xn, wk_ref[...]), bd, knw), cos, sin)
    v = _dot(xn, wg_ref[:, W_V:W_ZA])

    o_p = _prompt_attention(q[:tm].astype(BF16), k[:tm].astype(BF16), v[:tm].astype(BF16),
                            kprev, vprev, sinks_ref, i == 0)
    o_s = _decode_attention(q[tm:], k[tm:], v[tm:], ck_ref, cv_ref, nck_ref, ncv_ref, sinks_ref)
    o_a = jnp.concatenate([o_p, o_s], axis=0)

    kprev[...] = k[tm - BLOCK:tm]
    vprev[...] = v[tm - BLOCK:tm]
    nkt_ref[...] = _rows_split_to_natural(k[tm - BLOCK:tm].T)
    nvt_ref[...] = v[tm - BLOCK:tm].T

    c = _dot(xn, wg_ref[:, G_C:G_H])
    hc = _dot(xn, wg_ref[:, G_H:G_ZB])
    u_all = c * hc
    cw = cw_ref[0]
    u = u_all[:tm]
    tail = utail[...]
    prev1 = tail[SUBLANES - 1:SUBLANES]
    prev2 = tail[SUBLANES - 2:SUBLANES - 1]
    row8 = lax.broadcasted_iota(jnp.int32, (SUBLANES, D_MODEL), 0)
    r1 = pltpu.roll(u, 1, 0)
    r2 = pltpu.roll(u, 2, 0)
    um1 = jnp.concatenate(
        [jnp.where(row8 == 0, prev1, r1[:SUBLANES]), r1[SUBLANES:]], axis=0)
    um2 = jnp.concatenate(
        [jnp.where(row8 == 0, prev2, jnp.where(row8 == 1, prev1, r2[:SUBLANES])),
         r2[SUBLANES:]], axis=0)
    conv_p = cw[0:1] * um2 + cw[1:2] * um1 + cw[2:3] * u
    utail[...] = u[tm - SUBLANES:]
    nc_ref[...] = u[tm - (CONV_K - 1):]
    u3 = u_all[tm:].reshape(DEC_PER_STEP, DEC_SEQ, D_MODEL)
    st = st_ref[...]
    s_m2 = st[:, 0:1, :]
    s_m1 = st[:, 1:2, :]
    t_io = lax.broadcasted_iota(jnp.int32, (DEC_PER_STEP, DEC_SEQ, D_MODEL), 1)
    us1 = jnp.where(t_io == 0, s_m1, pltpu.roll(u3, 1, 1))
    us2 = jnp.where(t_io == 0, s_m2, jnp.where(t_io == 1, s_m1, pltpu.roll(u3, 2, 1)))
    conv_s = (cw[0:1] * us2 + cw[1:2] * us1 + cw[2:3] * u3).reshape(TS, D_MODEL)
    ncs_ref[...] = u3[:, DEC_SEQ - (CONV_K - 1):, :]
    conv = jnp.concatenate([conv_p, conv_s], axis=0)

    z_a = _dot(xn, wza_ref[...])
    br_a = _dot((o_a * _silu(z_a)).astype(BF16), wpa_ref[...])
    b = _dot(xn, wg_ref[:, G_B:G_C])
    z_b = _dot(xn, wg_ref[:, G_ZB:G_GA])
    br_b = _dot(((b * conv) * _silu(z_b)).astype(BF16), wpb_ref[...])
    g_a = _dot(xn, wg_ref[:, G_GA:G_GB])
    g_b = _dot(xn, wg_ref[:, G_GB:G_END])
    mixed = (jax.nn.sigmoid(g_a) * br_a + jax.nn.sigmoid(g_b) * br_b).astype(BF16)
    y = x + _dot(mixed, wo_ref[...])
    y_ref[...] = y[:tm]
    ys_ref[...] = y[tm:]


def _const_spec(shape):
    zeros = (0,) * len(shape)
    return pl.BlockSpec(shape, lambda i: zeros, pipeline_mode=pl.Buffered(1))


def _layer_call(x, xs, tabs, ck, cv, st, meta, tabs0, tabs_s, nw, qnw, knw, cw, sinks,
                wq, wk, wza, wg, wpa, wpb, wo, bd):
    tm = TM_PROMPT
    assert N_STEPS * DEC_PER_STEP == DEC_BATCH and TS % (2 * SUBLANES) == 0 and TS <= LANES
    row_spec = lambda rows, w: pl.BlockSpec((rows, w), lambda i: (i, 0))
    cache_spec = pl.BlockSpec((DEC_PER_STEP, KV_WIDTH, WINDOW), lambda i: (i, 0, 0))
    st_spec = pl.BlockSpec((DEC_PER_STEP, CONV_K - 1, D_MODEL), lambda i: (i, 0, 0))
    consts = (meta, tabs0[0], tabs0[1], tabs_s[0], tabs_s[1], nw, qnw, knw, cw,
              wq, wk, wza, wg, wpa, wpb, wo, bd)
    in_specs = [pl.BlockSpec(memory_space=pltpu.SMEM),
                row_spec(tm, D_MODEL), row_spec(TS, D_MODEL), row_spec(tm, LANES), row_spec(tm, LANES),
                cache_spec, cache_spec, st_spec]
    in_specs += [_const_spec(a.shape) for a in consts]
    resident = lambda shape: pl.BlockSpec(shape, lambda i: (0, 0))
    out_specs = [row_spec(tm, D_MODEL), row_spec(TS, D_MODEL),
                 resident((KV_WIDTH, BLOCK)), resident((KV_WIDTH, BLOCK)),
                 resident((CONV_K - 1, D_MODEL)), cache_spec, cache_spec, st_spec]
    cache_sds = jax.ShapeDtypeStruct((DEC_BATCH, KV_WIDTH, WINDOW), F32)
    out_shape = [jax.ShapeDtypeStruct((SEQ, D_MODEL), F32),
                 jax.ShapeDtypeStruct((DEC_BATCH * DEC_SEQ, D_MODEL), F32),
                 jax.ShapeDtypeStruct((KV_WIDTH, BLOCK), F32),
                 jax.ShapeDtypeStruct((KV_WIDTH, BLOCK), F32),
                 jax.ShapeDtypeStruct((CONV_K - 1, D_MODEL), F32), cache_sds, cache_sds,
                 jax.ShapeDtypeStruct((DEC_BATCH, CONV_K - 1, D_MODEL), F32)]
    return pl.pallas_call(
        _layer_kernel,
        grid=(N_STEPS,),
        in_specs=in_specs,
        out_specs=out_specs,
        out_shape=out_shape,
        scratch_shapes=[pltpu.VMEM((BLOCK, KV_WIDTH), F32),
                        pltpu.VMEM((BLOCK, KV_WIDTH), F32),
                        pltpu.VMEM((SUBLANES, D_MODEL), F32)],
        compiler_params=pltpu.CompilerParams(
            dimension_semantics=("arbitrary",), vmem_limit_bytes=VMEM_LIMIT),
        name="hybrid_layer",
    )(sinks, x, xs, tabs[0], tabs[1], ck, cv, st, *consts)


def _rope_tables(positions):
    inv = np.power(ROPE_THETA, -np.arange(HALF, dtype=np.float64) * (2.0 / HEAD_DIM))
    ang = np.asarray(positions, dtype=np.float64)[:, None] * inv[None, :]
    cos = np.tile(np.cos(ang), (1, LANES // HALF))
    sin = np.tile(np.sin(ang), (1, LANES // HALF))
    return jnp.asarray(cos, F32), jnp.asarray(sin, F32)


def _head_mean_matrix():
    grp = np.arange(256) // HALF
    return jnp.asarray((grp[:, None] == grp[None, :]) / float(HEAD_DIM), BF16)


def kernel(x_prompt, x_sample, cache_k, cache_v, state_conv, meta_tokens, norm_w, w_in,
           q_norm_w, k_norm_w, sinks, conv_w, w_proj_a, w_proj_b, w_out):
    assert x_prompt.shape == (1, SEQ, D_MODEL) and x_sample.shape == (DEC_BATCH, DEC_SEQ, D_MODEL)
    assert w_in.shape[0] == 1, "single layer"
    wg = w_in[0].astype(BF16)
    wq = (wg[:, :W_K].reshape(D_MODEL, N_KV_HEADS, GROUP, 2, HALF)
          .transpose(0, 2, 3, 1, 4).reshape(D_MODEL, D_MODEL))
    wk = (wg[:, W_K:W_V].reshape(D_MODEL, N_KV_HEADS, 2, HALF)
          .transpose(0, 2, 1, 3).reshape(D_MODEL, KV_WIDTH))
    wza = (wg[:, W_ZA:G_B].reshape(D_MODEL, N_KV_HEADS, GROUP, HEAD_DIM)
           .transpose(0, 2, 1, 3).reshape(D_MODEL, D_MODEL))
    wpa = (w_proj_a[0].astype(BF16).reshape(N_KV_HEADS, GROUP, HEAD_DIM, D_MODEL)
           .transpose(1, 0, 2, 3).reshape(D_MODEL, D_MODEL))
    wpb = w_proj_b[0].astype(BF16)
    wo = w_out[0].astype(BF16)

    nw = norm_w[0].reshape(1, D_MODEL)
    split = lambda v: jnp.broadcast_to(v.reshape(2, 1, HALF), (2, N_KV_HEADS, HALF)).reshape(1, 256)

    to_tiles = lambda c: c[0].transpose(0, 2, 3, 1).reshape(DEC_BATCH, KV_WIDTH, WINDOW)
    from_tiles = lambda c: (c.reshape(DEC_BATCH, N_KV_HEADS, HEAD_DIM, WINDOW)
                            .transpose(0, 3, 1, 2)[None])
    from_tile = lambda t: t.reshape(N_KV_HEADS, HEAD_DIM, WINDOW).transpose(2, 0, 1)

    tabs_p = _rope_tables(np.arange(SEQ) + N_META)
    tabs_0 = _rope_tables(np.arange(BLOCK) - LEAD)
    tabs_s = _rope_tables(PAST_LEN + (np.arange(TS) % DEC_SEQ))
    y_p, y_s, nkt_p, nvt_p, nc_p, nk_s, nv_s, nc_s = _layer_call(
        x_prompt[0], x_sample.reshape(DEC_BATCH * DEC_SEQ, D_MODEL), tabs_p,
        to_tiles(cache_k), to_tiles(cache_v), state_conv[0],
        meta_tokens.astype(x_prompt.dtype), tabs_0, tabs_s,
        nw, split(q_norm_w), split(k_norm_w), conv_w, sinks,
        wq, wk, wza, wg, wpa, wpb, wo, _head_mean_matrix())

    return (y_p.reshape(1, SEQ, D_MODEL),
            y_s.reshape(DEC_BATCH, DEC_SEQ, D_MODEL),
            from_tile(nkt_p).reshape(1, 1, WINDOW, N_KV_HEADS, HEAD_DIM),
            from_tile(nvt_p).reshape(1, 1, WINDOW, N_KV_HEADS, HEAD_DIM),
            nc_p.reshape(1, 1, CONV_K - 1, D_MODEL),
            from_tiles(nk_s),
            from_tiles(nv_s),
            nc_s.reshape(1, DEC_BATCH, CONV_K - 1, D_MODEL))
```

```python
import functools

import numpy as np
import jax
import jax.numpy as jnp
from jax import lax
from jax.experimental import pallas as pl
from jax.experimental.pallas import tpu as pltpu

D_MODEL = 1024
SEQ = 16384
DEC_BATCH = 128
DEC_SEQ = 8
PAST_LEN = 16384
N_HEADS = 16
N_KV_HEADS = 4
GROUP = N_HEADS // N_KV_HEADS
HEAD_DIM = 64
HALF = HEAD_DIM // 2
KV_WIDTH = N_KV_HEADS * HEAD_DIM
WINDOW = 128
BLOCK = 128
ROPE_THETA = 10000.0
CONV_K = 3
N_META = 16
LEAD = BLOCK - N_META
EPS = 1e-6
NEG_INF = -1e30
Q_SCALE = HEAD_DIM ** -0.5

W_K = D_MODEL
W_V = W_K + KV_WIDTH
W_ZA = W_V + KV_WIDTH
G_B = W_ZA + D_MODEL
G_C = G_B + D_MODEL
G_H = G_C + D_MODEL
G_ZB = G_H + D_MODEL
G_GA = G_ZB + D_MODEL
G_GB = G_GA + D_MODEL
G_END = G_GB + D_MODEL

LANES = 128
SUBLANES = 8
VMEM_BYTES_V7X = 64 * 1024 * 1024

TM_PROMPT = 512
N_STEPS = SEQ // TM_PROMPT
DEC_PER_STEP = DEC_BATCH // N_STEPS
TS = DEC_PER_STEP * DEC_SEQ
GW_CHUNK = 512
SUB_TILES = 2
TM_SUB = TM_PROMPT // SUB_TILES
VMEM_LIMIT = VMEM_BYTES_V7X - 4 * 1024 * 1024

BF16 = jnp.bfloat16
F32 = jnp.float32


def _dot(a, b):
    return jnp.dot(a, b, preferred_element_type=F32)


def _dot_nt(a, b):
    return lax.dot_general(a, b, (((1,), (1,)), ((), ())), preferred_element_type=F32)


def _rms_rows(x, w):
    ms = jnp.mean(x * x, axis=-1, keepdims=True)
    return (x * lax.rsqrt(ms + EPS)) * w


def _head_rms_split(t, bd, w):
    n = t.shape[1] // 256
    r = t.shape[0]
    halves = [(t[:, 256 * c:256 * c + LANES], t[:, 256 * c + LANES:256 * (c + 1)]) for c in range(n)]
    pieces = [a * a + b * b for a, b in halves]
    if n == 1:
        pieces = [pieces[0], pieces[0]]
    pair_ms = [_dot(jnp.concatenate(pieces[2 * i:2 * i + 2], axis=1).astype(BF16), bd)
               for i in range(len(pieces) // 2)]
    ms = [pair_ms[i // 2][:, LANES * (i % 2):LANES * (i % 2 + 1)] for i in range(n)]
    out = []
    for (a, b), m in zip(halves, ms):
        scale = lax.rsqrt(m + EPS)
        out.append((a * scale) * w[:, :LANES])
        out.append((b * scale) * w[:, LANES:])
    return jnp.concatenate(out, axis=1)


def _rope_split(t, cos, sin):
    out = []
    for c in range(t.shape[1] // 256):
        a = t[:, 256 * c:256 * c + LANES]
        b = t[:, 256 * c + LANES:256 * (c + 1)]
        out.append(a * cos - b * sin)
        out.append(b * cos + a * sin)
    return jnp.concatenate(out, axis=1)


def _lane_group_masks(group_of_lane, dtype):
    return [(group_of_lane == g).astype(dtype) for g in range(N_KV_HEADS)]


def _softmax_parts(s, sk):
    m = jnp.maximum(jnp.max(s, axis=-1, keepdims=True), sk)
    p = jnp.exp(s - m)
    den = jnp.sum(p, axis=-1, keepdims=True) + jnp.exp(sk - m)
    return p, 1.0 / den


def _silu(z):
    return z * jax.nn.sigmoid(z)


def _rows_split_to_natural(t):
    return jnp.concatenate(
        [t[LANES * half + HALF * g:LANES * half + HALF * (g + 1)]
         for g in range(N_KV_HEADS) for half in range(2)], axis=0)


def _rows_natural_to_split(t):
    return jnp.concatenate(
        [t[HEAD_DIM * g + HALF * half:HEAD_DIM * g + HALF * (half + 1)]
         for half in range(2) for g in range(N_KV_HEADS)], axis=0)


def _prompt_attention(q, kb, vb, kprev, vprev, sinks_ref, first_step):
    tm = q.shape[0]
    lane = lax.broadcasted_iota(jnp.int32, (1, 2 * BLOCK), 1)
    kmasks = _lane_group_masks((lane & (LANES - 1)) >> 5, BF16)
    vmasks = _lane_group_masks(lane >> 6, BF16)
    vgroup = lane >> 6
    r_io = lax.broadcasted_iota(jnp.int32, (BLOCK, 2 * BLOCK), 0)
    c_io = lax.broadcasted_iota(jnp.int32, (BLOCK, 2 * BLOCK), 1)
    band = (c_io >= r_io) & (c_io <= r_io + WINDOW)
    first_band = band & (c_io >= jnp.where(first_step, LEAD, 0))

    o_blocks = []
    for b in range(tm // BLOCK):
        lo = b * BLOCK
        if b == 0:
            kpair = jnp.concatenate([kprev, kb[:BLOCK]], axis=0)
            vpair = jnp.concatenate([vprev, vb[:BLOCK]], axis=0)
            mask = first_band
        else:
            kpair = kb[lo - BLOCK:lo + BLOCK]
            vpair = vb[lo - BLOCK:lo + BLOCK]
            mask = band
        kbd = jnp.concatenate([kpair * km for km in kmasks], axis=0)
        vbd = jnp.concatenate([vpair * vm for vm in vmasks], axis=0)
        qs = jnp.concatenate([q[lo:lo + BLOCK, 256 * j:256 * (j + 1)] for j in range(GROUP)],
                             axis=0)
        s = _dot_nt(qs, kbd)
        p_rows = []
        mults = []
        for j in range(GROUP):
            p_cols = []
            invs = []
            for g in range(N_KV_HEADS):
                sl = s[BLOCK * j:BLOCK * (j + 1), 256 * g:256 * (g + 1)]
                sl = jnp.where(mask, sl, NEG_INF)
                p, inv = _softmax_parts(sl, sinks_ref[0, GROUP * g + j])
                p_cols.append(p.astype(BF16))
                invs.append(inv)
            p_rows.append(jnp.concatenate(p_cols, axis=1))
            mults.append(jnp.where(vgroup == 0, invs[0],
                                   jnp.where(vgroup == 1, invs[1],
                                             jnp.where(vgroup == 2, invs[2], invs[3]))))
        p_all = jnp.concatenate(p_rows, axis=0)
        half_k = 2 * 2 * BLOCK
        o = _dot(p_all[:, :half_k], vbd[:half_k]) + _dot(p_all[:, half_k:], vbd[half_k:])
        o_blocks.append(jnp.concatenate(
            [o[BLOCK * j:BLOCK * (j + 1)] * mults[j] for j in range(GROUP)], axis=1))
    return jnp.concatenate(o_blocks, axis=0)


def _decode_attention(qs, ks, vs, ck_ref, cv_ref, nck_ref, ncv_ref, sinks_ref, first, fillers):
    t_len = DEC_SEQ
    n_exp = GROUP * N_KV_HEADS * t_len
    pad = jnp.zeros((LANES - qs.shape[0], KV_WIDTH), F32)
    kst = jnp.concatenate([ks, pad], axis=0).T
    vst = jnp.concatenate([vs, pad], axis=0).T
    kst_nat = _rows_split_to_natural(kst)
    kst_b = kst.astype(BF16)
    vst_b = vst.astype(BF16)

    lane = lax.broadcasted_iota(jnp.int32, (1, KV_WIDTH), 1)
    kgroup = (lane & (LANES - 1)) >> 5
    vgroup = lane >> 6
    t_io = lax.broadcasted_iota(jnp.int32, (n_exp, WINDOW), 0) & (t_len - 1)
    c_io = lax.broadcasted_iota(jnp.int32, (n_exp, WINDOW), 1)
    old_ok = c_io >= t_io
    new_lane = lax.broadcasted_iota(jnp.int32, (KV_WIDTH, WINDOW), 1) >= WINDOW - t_len
    skcol = jnp.concatenate(
        [jnp.full((t_len, 1), sinks_ref[0, GROUP * g + j], F32)
         for j in range(GROUP) for g in range(N_KV_HEADS)], axis=0)

    outs = []
    filled = []
    for b in range(qs.shape[0] // t_len):
        if b < len(fillers):
            filled.append(fillers[b]())
        r0 = b * t_len
        qn = qs[r0:r0 + t_len]
        q_exp = jnp.concatenate(
            [jnp.where(kgroup == g, qn[:, 256 * j:256 * (j + 1)], 0.0)
             for j in range(GROUP) for g in range(N_KV_HEADS)], axis=0).astype(BF16)
        kt_old = ck_ref[first + b]
        vt_old = cv_ref[first + b]
        keys = jnp.concatenate([_rows_natural_to_split(kt_old).astype(BF16), kst_b], axis=1)
        vals = jnp.concatenate([vt_old.astype(BF16), vst_b], axis=1)
        s = _dot(q_exp, keys)
        new_ok = (c_io >= r0) & (c_io - t_io <= r0)
        s = jnp.where(jnp.concatenate([old_ok, new_ok], axis=1), s, NEG_INF)
        p, inv = _softmax_parts(s, skcol)
        o = _dot_nt(p.astype(BF16), vals) * inv
        o_j = []
        for j in range(GROUP):
            acc = None
            for g in range(N_KV_HEADS):
                r = (j * N_KV_HEADS + g) * t_len
                piece = jnp.where(vgroup == g, o[r:r + t_len], 0.0)
                acc = piece if acc is None else acc + piece
            o_j.append(acc)
        outs.append(jnp.concatenate(o_j, axis=1))
        shift_new = WINDOW - t_len - r0
        nck_ref[first + b] = jnp.where(new_lane, pltpu.roll(kst_nat, shift_new, 1),
                                       pltpu.roll(kt_old, WINDOW - t_len, 1))
        ncv_ref[first + b] = jnp.where(new_lane, pltpu.roll(vst, shift_new, 1),
                                       pltpu.roll(vt_old, WINDOW - t_len, 1))
    filled += [f() for f in fillers[len(filled):]]
    return jnp.concatenate(outs, axis=0), filled


def _gate_weight_copy(w_hbm, wstage, wsem, chunk):
    slot = chunk % 2
    return pltpu.make_async_copy(
        w_hbm.at[0, :, pl.ds(G_B + GW_CHUNK * chunk, GW_CHUNK)], wstage.at[slot], wsem.at[slot])


def _load_gate_weights(w_hbm, wg_s, wstage, wsem):
    n_chunks = (G_END - G_B) // GW_CHUNK
    _gate_weight_copy(w_hbm, wstage, wsem, 0).start()
    for c in range(n_chunks):
        if c + 1 < n_chunks:
            _gate_weight_copy(w_hbm, wstage, wsem, c + 1).start()
        _gate_weight_copy(w_hbm, wstage, wsem, c).wait()
        wg_s[:, GW_CHUNK * c:GW_CHUNK * (c + 1)] = wstage[c % 2].astype(BF16)


def _layer_kernel(sinks_ref, x_ref, xs_ref, cos_ref, sin_ref, ck_ref, cv_ref, st_ref, w_hbm,
                  meta_ref, cos0_ref, sin0_ref, coss_ref, sins_ref, nw_ref, qnw_ref, knw_ref, cw_ref,
                  wq_ref, wkv_ref, wza_ref, wpa_ref, wpb_ref, wo_ref, bd_ref,
                  y_ref, ys_ref, nkt_ref, nvt_ref, nc_ref, nck_ref, ncv_ref, ncs_ref,
                  kprev, vprev, utail, wg_s, wstage, wsem):
    i = pl.program_id(0)
    nw = nw_ref[...]
    bd = bd_ref[...]
    knw = knw_ref[...]

    @pl.when(i == 0)
    def _init():
        _load_gate_weights(w_hbm, wg_s, wstage, wsem)
        x0 = jnp.concatenate([jnp.zeros((LEAD, D_MODEL), F32), meta_ref[...]], axis=0)
        xn0 = _rms_rows(x0, nw).astype(BF16)
        kv0 = _dot(xn0, wkv_ref[...])
        k0 = _rope_split(_head_rms_split(kv0[:, :KV_WIDTH], bd, knw), cos0_ref[...], sin0_ref[...])
        v0 = kv0[:, KV_WIDTH:]
        c0 = _dot(xn0, wg_s[:, G_C - G_B:G_H - G_B])
        h0 = _dot(xn0, wg_s[:, G_H - G_B:G_ZB - G_B])
        row = lax.broadcasted_iota(jnp.int32, (BLOCK, D_MODEL), 0)
        u0 = jnp.where(row >= LEAD, c0 * h0, 0.0)
        kprev[...] = k0
        vprev[...] = v0
        utail[...] = u0[BLOCK - SUBLANES:BLOCK]

    cw = cw_ref[0]
    row8 = lax.broadcasted_iota(jnp.int32, (SUBLANES, D_MODEL), 0)
    t_io = lax.broadcasted_iota(jnp.int32, (DEC_PER_STEP, DEC_SEQ, D_MODEL), 1)
    k_prev = kprev[...].astype(BF16)
    v_prev = vprev[...].astype(BF16)
    tail = utail[...]
    for h in range(SUB_TILES):
        ps = slice(TM_SUB * h, TM_SUB * (h + 1))
        with_dec = h == SUB_TILES - 1
        if with_dec:
            x = jnp.concatenate([x_ref[ps, :], xs_ref[...]], axis=0)
            cos = jnp.concatenate([cos_ref[ps, :], coss_ref[...]], axis=0)
            sin = jnp.concatenate([sin_ref[ps, :], sins_ref[...]], axis=0)
        else:
            x = x_ref[ps, :]
            cos = cos_ref[ps, :]
            sin = sin_ref[ps, :]
        xn = _rms_rows(x, nw).astype(BF16)

        q = _head_rms_split(_dot(xn, wq_ref[...]), bd, qnw_ref[...])
        q = _rope_split(q, cos * Q_SCALE, sin * Q_SCALE)
        kv = _dot(xn, wkv_ref[...])
        k = _rope_split(_head_rms_split(kv[:, :KV_WIDTH], bd, knw), cos, sin)
        v = kv[:, KV_WIDTH:]
        kb = k[:TM_SUB].astype(BF16)
        vb = v[:TM_SUB].astype(BF16)

        first = (i == 0) if h == 0 else False
        o_a = _prompt_attention(q[:TM_SUB].astype(BF16), kb, vb, k_prev, v_prev, sinks_ref, first)
        wide = [lambda lo=lo: _dot(xn, wg_s[:, lo - G_B:lo - G_B + D_MODEL])
                for lo in (G_C, G_H, G_B, G_ZB)]
        if with_dec:
            o_s, (c, hc, b, z_b) = _decode_attention(
                q[TM_SUB:], k[TM_SUB:], v[TM_SUB:], ck_ref, cv_ref, nck_ref, ncv_ref, sinks_ref, 0,
                wide)
            o_a = jnp.concatenate([o_a, o_s], axis=0)
        else:
            c, hc, b, z_b = [f() for f in wide]
        k_prev = kb[TM_SUB - BLOCK:]
        v_prev = vb[TM_SUB - BLOCK:]
        k_last = k[TM_SUB - BLOCK:TM_SUB]
        v_last = v[TM_SUB - BLOCK:TM_SUB]

        u_all = c * hc
        u = u_all[:TM_SUB]
        prev1 = tail[SUBLANES - 1:SUBLANES]
        prev2 = tail[SUBLANES - 2:SUBLANES - 1]
        r1 = pltpu.roll(u, 1, 0)
        r2 = pltpu.roll(u, 2, 0)
        um1 = jnp.concatenate(
            [jnp.where(row8 == 0, prev1, r1[:SUBLANES]), r1[SUBLANES:]], axis=0)
        um2 = jnp.concatenate(
            [jnp.where(row8 == 0, prev2, jnp.where(row8 == 1, prev1, r2[:SUBLANES])),
             r2[SUBLANES:]], axis=0)
        conv = cw[0:1] * um2 + cw[1:2] * um1 + cw[2:3] * u
        tail = u[TM_SUB - SUBLANES:]
        if with_dec:
            u3 = u_all[TM_SUB:].reshape(DEC_PER_STEP, DEC_SEQ, D_MODEL)
            st = st_ref[...]
            s_m2 = st[:, 0:1, :]
            s_m1 = st[:, 1:2, :]
            us1 = jnp.where(t_io == 0, s_m1, pltpu.roll(u3, 1, 1))
            us2 = jnp.where(t_io == 0, s_m2, jnp.where(t_io == 1, s_m1, pltpu.roll(u3, 2, 1)))
            conv_s = (cw[0:1] * us2 + cw[1:2] * us1 + cw[2:3] * u3).reshape(TS, D_MODEL)
            ncs_ref[...] = u3[:, DEC_SEQ - (CONV_K - 1):, :]
            conv = jnp.concatenate([conv, conv_s], axis=0)

        z_a = _dot(xn, wza_ref[...])
        br_a = _dot((o_a * _silu(z_a)).astype(BF16), wpa_ref[...])
        br_b = _dot(((b * conv) * _silu(z_b)).astype(BF16), wpb_ref[...])
        g_a = _dot(xn, wg_s[:, G_GA - G_B:G_GB - G_B])
        g_b = _dot(xn, wg_s[:, G_GB - G_B:G_END - G_B])
        mixed = (jax.nn.sigmoid(g_a) * br_a + jax.nn.sigmoid(g_b) * br_b).astype(BF16)
        y = x + _dot(mixed, wo_ref[...])
        y_ref[ps, :] = y[:TM_SUB]
        if with_dec:
            ys_ref[...] = y[TM_SUB:]

    kprev[...] = k_last
    vprev[...] = v_last
    utail[...] = tail
    nkt_ref[...] = _rows_split_to_natural(k_last.T)
    nvt_ref[...] = v_last.T
    nc_ref[...] = tail[SUBLANES - (CONV_K - 1):]


def _const_spec(shape):
    zeros = (0,) * len(shape)
    return pl.BlockSpec(shape, lambda i: zeros, pipeline_mode=pl.Buffered(1))


def _layer_call(x, xs, tabs, ck, cv, st, w_in, meta, tabs0, tabs_s, nw, qnw, knw, cw, sinks,
                wq, wk, wza, wpa, wpb, wo, bd):
    tm = TM_PROMPT
    assert N_STEPS * DEC_PER_STEP == DEC_BATCH and TS % (2 * SUBLANES) == 0 and TS <= LANES
    assert TM_SUB % BLOCK == 0
    row_spec = lambda rows, w: pl.BlockSpec((rows, w), lambda i: (i, 0))
    cache_spec = pl.BlockSpec((DEC_PER_STEP, KV_WIDTH, WINDOW), lambda i: (i, 0, 0))
    st_spec = pl.BlockSpec((DEC_PER_STEP, CONV_K - 1, D_MODEL), lambda i: (i, 0, 0))
    consts = (meta, tabs0[0], tabs0[1], tabs_s[0], tabs_s[1], nw, qnw, knw, cw,
              wq, wk, wza, wpa, wpb, wo, bd)
    in_specs = [pl.BlockSpec(memory_space=pltpu.SMEM),
                row_spec(tm, D_MODEL), row_spec(TS, D_MODEL), row_spec(tm, LANES), row_spec(tm, LANES),
                cache_spec, cache_spec, st_spec, pl.BlockSpec(memory_space=pl.ANY)]
    in_specs += [_const_spec(a.shape) for a in consts]
    resident = lambda shape: pl.BlockSpec(shape, lambda i: (0, 0))
    out_specs = [row_spec(tm, D_MODEL), row_spec(TS, D_MODEL),
                 resident((KV_WIDTH, BLOCK)), resident((KV_WIDTH, BLOCK)),
                 resident((CONV_K - 1, D_MODEL)), cache_spec, cache_spec, st_spec]
    cache_sds = jax.ShapeDtypeStruct((DEC_BATCH, KV_WIDTH, WINDOW), F32)
    out_shape = [jax.ShapeDtypeStruct((SEQ, D_MODEL), F32),
                 jax.ShapeDtypeStruct((DEC_BATCH * DEC_SEQ, D_MODEL), F32),
                 jax.ShapeDtypeStruct((KV_WIDTH, BLOCK), F32),
                 jax.ShapeDtypeStruct((KV_WIDTH, BLOCK), F32),
                 jax.ShapeDtypeStruct((CONV_K - 1, D_MODEL), F32), cache_sds, cache_sds,
                 jax.ShapeDtypeStruct((DEC_BATCH, CONV_K - 1, D_MODEL), F32)]
    return pl.pallas_call(
        _layer_kernel,
        grid=(N_STEPS,),
        in_specs=in_specs,
        out_specs=out_specs,
        out_shape=out_shape,
        scratch_shapes=[pltpu.VMEM((BLOCK, KV_WIDTH), F32),
                        pltpu.VMEM((BLOCK, KV_WIDTH), F32),
                        pltpu.VMEM((SUBLANES, D_MODEL), F32),
                        pltpu.VMEM((D_MODEL, G_END - G_B), BF16),
                        pltpu.VMEM((2, D_MODEL, GW_CHUNK), F32),
                        pltpu.SemaphoreType.DMA((2,))],
        compiler_params=pltpu.CompilerParams(
            dimension_semantics=("arbitrary",), vmem_limit_bytes=VMEM_LIMIT),
        name="hybrid_layer",
    )(sinks, x, xs, tabs[0], tabs[1], ck, cv, st, w_in, *consts)


def _rope_tables(positions):
    inv = np.power(ROPE_THETA, -np.arange(HALF, dtype=np.float64) * (2.0 / HEAD_DIM))
    ang = np.asarray(positions, dtype=np.float64)[:, None] * inv[None, :]
    cos = np.tile(np.cos(ang), (1, LANES // HALF))
    sin = np.tile(np.sin(ang), (1, LANES // HALF))
    return jnp.asarray(cos, F32), jnp.asarray(sin, F32)


def _head_mean_matrix():
    grp = np.arange(256) // HALF
    return jnp.asarray((grp[:, None] == grp[None, :]) / float(HEAD_DIM), BF16)


def kernel(x_prompt, x_sample, cache_k, cache_v, state_conv, meta_tokens, norm_w, w_in,
           q_norm_w, k_norm_w, sinks, conv_w, w_proj_a, w_proj_b, w_out):
    assert x_prompt.shape == (1, SEQ, D_MODEL) and x_sample.shape == (DEC_BATCH, DEC_SEQ, D_MODEL)
    assert w_in.shape[0] == 1, "single layer"
    wg = w_in[0, :, :G_B].astype(BF16)
    wq = (wg[:, :W_K].reshape(D_MODEL, N_KV_HEADS, GROUP, 2, HALF)
          .transpose(0, 2, 3, 1, 4).reshape(D_MODEL, D_MODEL))
    wk = (wg[:, W_K:W_V].reshape(D_MODEL, N_KV_HEADS, 2, HALF)
          .transpose(0, 2, 1, 3).reshape(D_MODEL, KV_WIDTH))
    wk = jnp.concatenate([wk, wg[:, W_V:W_ZA]], axis=1)
    wza = (wg[:, W_ZA:G_B].reshape(D_MODEL, N_KV_HEADS, GROUP, HEAD_DIM)
           .transpose(0, 2, 1, 3).reshape(D_MODEL, D_MODEL))
    wpa = (w_proj_a[0].astype(BF16).reshape(N_KV_HEADS, GROUP, HEAD_DIM, D_MODEL)
           .transpose(1, 0, 2, 3).reshape(D_MODEL, D_MODEL))
    wpb = w_proj_b[0].astype(BF16)
    wo = w_out[0].astype(BF16)

    nw = norm_w[0].reshape(1, D_MODEL)
    split = lambda v: jnp.broadcast_to(v.reshape(2, 1, HALF), (2, N_KV_HEADS, HALF)).reshape(1, 256)

    to_tiles = lambda c: c[0].transpose(0, 2, 3, 1).reshape(DEC_BATCH, KV_WIDTH, WINDOW)
    from_tiles = lambda c: (c.reshape(DEC_BATCH, N_KV_HEADS, HEAD_DIM, WINDOW)
                            .transpose(0, 3, 1, 2)[None])
    from_tile = lambda t: t.reshape(N_KV_HEADS, HEAD_DIM, WINDOW).transpose(2, 0, 1)

    tabs_p = _rope_tables(np.arange(SEQ) + N_META)
    tabs_0 = _rope_tables(np.arange(BLOCK) - LEAD)
    tabs_s = _rope_tables(PAST_LEN + (np.arange(TS) % DEC_SEQ))
    y_p, y_s, nkt_p, nvt_p, nc_p, nk_s, nv_s, nc_s = _layer_call(
        x_prompt[0], x_sample.reshape(DEC_BATCH * DEC_SEQ, D_MODEL), tabs_p,
        to_tiles(cache_k), to_tiles(cache_v), state_conv[0], w_in,
        meta_tokens.astype(x_prompt.dtype), tabs_0, tabs_s,
        nw, split(q_norm_w), split(k_norm_w), conv_w, sinks,
        wq, wk, wza, wpa, wpb, wo, _head_mean_matrix())

    return (y_p.reshape(1, SEQ, D_MODEL),
            y_s.reshape(DEC_BATCH, DEC_SEQ, D_MODEL),
            from_tile(nkt_p).reshape(1, 1, WINDOW, N_KV_HEADS, HEAD_DIM),
            from_tile(nvt_p).reshape(1, 1, WINDOW, N_KV_HEADS, HEAD_DIM),
            nc_p.reshape(1, 1, CONV_K - 1, D_MODEL),
            from_tiles(nk_s),
            from_tiles(nv_s),
            nc_s.reshape(1, DEC_BATCH, CONV_K - 1, D_MODEL))
```

```python
import functools

import numpy as np
import jax
import jax.numpy as jnp
from jax import lax
from jax.experimental import pallas as pl
from jax.experimental.pallas import tpu as pltpu

D_MODEL = 1024
SEQ = 16384
DEC_BATCH = 128
DEC_SEQ = 8
PAST_LEN = 16384
N_HEADS = 16
N_KV_HEADS = 4
GROUP = N_HEADS // N_KV_HEADS
HEAD_DIM = 64
HALF = HEAD_DIM // 2
KV_WIDTH = N_KV_HEADS * HEAD_DIM
WINDOW = 128
BLOCK = 128
ROPE_THETA = 10000.0
CONV_K = 3
N_META = 16
LEAD = BLOCK - N_META
EPS = 1e-6
NEG_INF = -1e30
Q_SCALE = HEAD_DIM ** -0.5

W_K = D_MODEL
W_V = W_K + KV_WIDTH
W_ZA = W_V + KV_WIDTH
G_B = W_ZA + D_MODEL
G_C = G_B + D_MODEL
G_H = G_C + D_MODEL
G_ZB = G_H + D_MODEL
G_GA = G_ZB + D_MODEL
G_GB = G_GA + D_MODEL
G_END = G_GB + D_MODEL

LANES = 128
SUBLANES = 8
VMEM_BYTES_V7X = 64 * 1024 * 1024

TM_PROMPT = 512
N_STEPS = SEQ // TM_PROMPT
DEC_PER_STEP = DEC_BATCH // N_STEPS
TS = DEC_PER_STEP * DEC_SEQ
GW_CHUNK = 128
SUB_TILES = 2
TM_SUB = TM_PROMPT // SUB_TILES
VMEM_LIMIT = VMEM_BYTES_V7X - 4 * 1024 * 1024

BF16 = jnp.bfloat16
F32 = jnp.float32


def _dot(a, b):
    return jnp.dot(a, b, preferred_element_type=F32)


def _dot_nt(a, b):
    return lax.dot_general(a, b, (((1,), (1,)), ((), ())), preferred_element_type=F32)


def _rms_rows(x, w):
    ms = jnp.mean(x * x, axis=-1, keepdims=True)
    return (x * lax.rsqrt(ms + EPS)) * w


def _head_rms_split(t, bd, w):
    n = t.shape[1] // 256
    r = t.shape[0]
    halves = [(t[:, 256 * c:256 * c + LANES], t[:, 256 * c + LANES:256 * (c + 1)]) for c in range(n)]
    pieces = [a * a + b * b for a, b in halves]
    if n == 1:
        pieces = [pieces[0], pieces[0]]
    pair_ms = [_dot(jnp.concatenate(pieces[2 * i:2 * i + 2], axis=1).astype(BF16), bd)
               for i in range(len(pieces) // 2)]
    ms = [pair_ms[i // 2][:, LANES * (i % 2):LANES * (i % 2 + 1)] for i in range(n)]
    out = []
    for (a, b), m in zip(halves, ms):
        scale = lax.rsqrt(m + EPS)
        out.append((a * scale) * w[:, :LANES])
        out.append((b * scale) * w[:, LANES:])
    return jnp.concatenate(out, axis=1)


def _rope_split(t, cos, sin):
    out = []
    for c in range(t.shape[1] // 256):
        a = t[:, 256 * c:256 * c + LANES]
        b = t[:, 256 * c + LANES:256 * (c + 1)]
        out.append(a * cos - b * sin)
        out.append(b * cos + a * sin)
    return jnp.concatenate(out, axis=1)


def _lane_group_masks(group_of_lane, dtype):
    return [(group_of_lane == g).astype(dtype) for g in range(N_KV_HEADS)]


def _softmax_parts(s, sk):
    m = jnp.maximum(jnp.max(s, axis=-1, keepdims=True), sk)
    p = jnp.exp(s - m)
    den = jnp.sum(p, axis=-1, keepdims=True) + jnp.exp(sk - m)
    return p, 1.0 / den


def _silu(z):
    return z * jax.nn.sigmoid(z)


def _rows_split_to_natural(t):
    return jnp.concatenate(
        [t[LANES * half + HALF * g:LANES * half + HALF * (g + 1)]
         for g in range(N_KV_HEADS) for half in range(2)], axis=0)


def _rows_natural_to_split(t):
    return jnp.concatenate(
        [t[HEAD_DIM * g + HALF * half:HEAD_DIM * g + HALF * (half + 1)]
         for half in range(2) for g in range(N_KV_HEADS)], axis=0)


def _prompt_attention(q, kb, vb, kprev, vprev, sinks_ref, first_step):
    tm = q.shape[0]
    lane = lax.broadcasted_iota(jnp.int32, (1, 2 * BLOCK), 1)
    kmasks = _lane_group_masks((lane & (LANES - 1)) >> 5, BF16)
    vmasks = _lane_group_masks(lane >> 6, BF16)
    vgroup = lane >> 6
    r_io = lax.broadcasted_iota(jnp.int32, (BLOCK, 2 * BLOCK), 0)
    c_io = lax.broadcasted_iota(jnp.int32, (BLOCK, 2 * BLOCK), 1)
    band = (c_io >= r_io) & (c_io <= r_io + WINDOW)
    first_band = band & (c_io >= jnp.where(first_step, LEAD, 0))

    o_blocks = []
    for b in range(tm // BLOCK):
        lo = b * BLOCK
        if b == 0:
            kpair = jnp.concatenate([kprev, kb[:BLOCK]], axis=0)
            vpair = jnp.concatenate([vprev, vb[:BLOCK]], axis=0)
            mask = first_band
        else:
            kpair = kb[lo - BLOCK:lo + BLOCK]
            vpair = vb[lo - BLOCK:lo + BLOCK]
            mask = band
        kbd = jnp.concatenate([kpair * km for km in kmasks], axis=0)
        vbd = jnp.concatenate([vpair * vm for vm in vmasks], axis=0)
        qs = jnp.concatenate([q[lo:lo + BLOCK, 256 * j:256 * (j + 1)] for j in range(GROUP)],
                             axis=0)
        s = _dot_nt(qs, kbd)
        p_rows = []
        mults = []
        for j in range(GROUP):
            p_cols = []
            invs = []
            for g in range(N_KV_HEADS):
                sl = s[BLOCK * j:BLOCK * (j + 1), 256 * g:256 * (g + 1)]
                sl = jnp.where(mask, sl, NEG_INF)
                p, inv = _softmax_parts(sl, sinks_ref[0, GROUP * g + j])
                p_cols.append(p.astype(BF16))
                invs.append(inv)
            p_rows.append(jnp.concatenate(p_cols, axis=1))
            mults.append(jnp.where(vgroup == 0, invs[0],
                                   jnp.where(vgroup == 1, invs[1],
                                             jnp.where(vgroup == 2, invs[2], invs[3]))))
        p_all = jnp.concatenate(p_rows, axis=0)
        half_k = 2 * 2 * BLOCK
        o = _dot(p_all[:, :half_k], vbd[:half_k]) + _dot(p_all[:, half_k:], vbd[half_k:])
        o_blocks.append(jnp.concatenate(
            [o[BLOCK * j:BLOCK * (j + 1)] * mults[j] for j in range(GROUP)], axis=1))
    return jnp.concatenate(o_blocks, axis=0)


def _decode_attention(qs, ks, vs, ck_ref, cv_ref, nck_ref, ncv_ref, sinks_ref, first, fillers):
    t_len = DEC_SEQ
    n_exp = GROUP * N_KV_HEADS * t_len
    pad = jnp.zeros((LANES - qs.shape[0], KV_WIDTH), F32)
    kst = jnp.concatenate([ks, pad], axis=0).T
    vst = jnp.concatenate([vs, pad], axis=0).T
    kst_nat = _rows_split_to_natural(kst)
    kst_b = kst.astype(BF16)
    vst_b = vst.astype(BF16)

    lane = lax.broadcasted_iota(jnp.int32, (1, KV_WIDTH), 1)
    kgroup = (lane & (LANES - 1)) >> 5
    vgroup = lane >> 6
    t_io = lax.broadcasted_iota(jnp.int32, (n_exp, WINDOW), 0) & (t_len - 1)
    c_io = lax.broadcasted_iota(jnp.int32, (n_exp, WINDOW), 1)
    old_ok = c_io >= t_io
    new_lane = lax.broadcasted_iota(jnp.int32, (KV_WIDTH, WINDOW), 1) >= WINDOW - t_len
    skcol = jnp.concatenate(
        [jnp.full((t_len, 1), sinks_ref[0, GROUP * g + j], F32)
         for j in range(GROUP) for g in range(N_KV_HEADS)], axis=0)

    outs = []
    filled = []
    for b in range(qs.shape[0] // t_len):
        if b < len(fillers):
            filled.append(fillers[b]())
        r0 = b * t_len
        qn = qs[r0:r0 + t_len]
        q_exp = jnp.concatenate(
            [jnp.where(kgroup == g, qn[:, 256 * j:256 * (j + 1)], 0.0)
             for j in range(GROUP) for g in range(N_KV_HEADS)], axis=0).astype(BF16)
        kt_old = ck_ref[first + b]
        vt_old = cv_ref[first + b]
        keys = jnp.concatenate([_rows_natural_to_split(kt_old).astype(BF16), kst_b], axis=1)
        vals = jnp.concatenate([vt_old.astype(BF16), vst_b], axis=1)
        s = _dot(q_exp, keys)
        new_ok = (c_io >= r0) & (c_io - t_io <= r0)
        s = jnp.where(jnp.concatenate([old_ok, new_ok], axis=1), s, NEG_INF)
        p, inv = _softmax_parts(s, skcol)
        o = _dot_nt(p.astype(BF16), vals) * inv
        o_j = []
        for j in range(GROUP):
            acc = None
            for g in range(N_KV_HEADS):
                r = (j * N_KV_HEADS + g) * t_len
                piece = jnp.where(vgroup == g, o[r:r + t_len], 0.0)
                acc = piece if acc is None else acc + piece
            o_j.append(acc)
        outs.append(jnp.concatenate(o_j, axis=1))
        shift_new = WINDOW - t_len - r0
        nck_ref[first + b] = jnp.where(new_lane, pltpu.roll(kst_nat, shift_new, 1),
                                       pltpu.roll(kt_old, WINDOW - t_len, 1))
        ncv_ref[first + b] = jnp.where(new_lane, pltpu.roll(vst, shift_new, 1),
                                       pltpu.roll(vt_old, WINDOW - t_len, 1))
    filled += [f() for f in fillers[len(filled):]]
    return jnp.concatenate(outs, axis=0), filled


def _gate_weight_copy(w_hbm, wstage, wsem, chunk):
    slot = chunk % 2
    return pltpu.make_async_copy(
        w_hbm.at[0, pl.ds(GW_CHUNK * chunk, GW_CHUNK), pl.ds(G_B, G_END - G_B)],
        wstage.at[slot], wsem.at[slot])


def _load_gate_weights(w_hbm, wg_s, wstage, wsem):
    n_chunks = D_MODEL // GW_CHUNK
    _gate_weight_copy(w_hbm, wstage, wsem, 0).start()
    for c in range(n_chunks):
        if c + 1 < n_chunks:
            _gate_weight_copy(w_hbm, wstage, wsem, c + 1).start()
        _gate_weight_copy(w_hbm, wstage, wsem, c).wait()
        wg_s[GW_CHUNK * c:GW_CHUNK * (c + 1), :] = wstage[c % 2].astype(BF16)


def _layer_kernel(sinks_ref, x_ref, xs_ref, cos_ref, sin_ref, ck_ref, cv_ref, st_ref, w_hbm,
                  meta_ref, cos0_ref, sin0_ref, coss_ref, sins_ref, nw_ref, qnw_ref, knw_ref, cw_ref,
                  wq_ref, wkv_ref, wza_ref, wpa_ref, wpb_ref, wo_ref, bd_ref,
                  y_ref, ys_ref, nkt_ref, nvt_ref, nc_ref, nck_ref, ncv_ref, ncs_ref,
                  kprev, vprev, utail, wg_s, wstage, wsem):
    i = pl.program_id(0)
    nw = nw_ref[...]
    bd = bd_ref[...]
    knw = knw_ref[...]

    @pl.when(i == 0)
    def _init():
        _load_gate_weights(w_hbm, wg_s, wstage, wsem)
        x0 = jnp.concatenate([jnp.zeros((LEAD, D_MODEL), F32), meta_ref[...]], axis=0)
        xn0 = _rms_rows(x0, nw).astype(BF16)
        kv0 = _dot(xn0, wkv_ref[...])
        k0 = _rope_split(_head_rms_split(kv0[:, :KV_WIDTH], bd, knw), cos0_ref[...], sin0_ref[...])
        v0 = kv0[:, KV_WIDTH:]
        c0 = _dot(xn0, wg_s[:, G_C - G_B:G_H - G_B])
        h0 = _dot(xn0, wg_s[:, G_H - G_B:G_ZB - G_B])
        row = lax.broadcasted_iota(jnp.int32, (BLOCK, D_MODEL), 0)
        u0 = jnp.where(row >= LEAD, c0 * h0, 0.0)
        kprev[...] = k0
        vprev[...] = v0
        utail[...] = u0[BLOCK - SUBLANES:BLOCK]

    cw = cw_ref[0]
    row8 = lax.broadcasted_iota(jnp.int32, (SUBLANES, D_MODEL), 0)
    t_io = lax.broadcasted_iota(jnp.int32, (DEC_PER_STEP, DEC_SEQ, D_MODEL), 1)
    k_prev = kprev[...].astype(BF16)
    v_prev = vprev[...].astype(BF16)
    tail = utail[...]
    for h in range(SUB_TILES):
        ps = slice(TM_SUB * h, TM_SUB * (h + 1))
        with_dec = h == SUB_TILES - 1
        if with_dec:
            x = jnp.concatenate([x_ref[ps, :], xs_ref[...]], axis=0)
            cos = jnp.concatenate([cos_ref[ps, :], coss_ref[...]], axis=0)
            sin = jnp.concatenate([sin_ref[ps, :], sins_ref[...]], axis=0)
        else:
            x = x_ref[ps, :]
            cos = cos_ref[ps, :]
            sin = sin_ref[ps, :]
        xn = _rms_rows(x, nw).astype(BF16)

        q = _head_rms_split(_dot(xn, wq_ref[...]), bd, qnw_ref[...])
        q = _rope_split(q, cos * Q_SCALE, sin * Q_SCALE)
        kv = _dot(xn, wkv_ref[...])
        k = _rope_split(_head_rms_split(kv[:, :KV_WIDTH], bd, knw), cos, sin)
        v = kv[:, KV_WIDTH:]
        kb = k[:TM_SUB].astype(BF16)
        vb = v[:TM_SUB].astype(BF16)

        first = (i == 0) if h == 0 else False
        o_a = _prompt_attention(q[:TM_SUB].astype(BF16), kb, vb, k_prev, v_prev, sinks_ref, first)
        wide = [lambda lo=lo: _dot(xn, wg_s[:, lo - G_B:lo - G_B + D_MODEL])
                for lo in (G_C, G_H, G_B, G_ZB)]
        if with_dec:
            o_s, (c, hc, b, z_b) = _decode_attention(
                q[TM_SUB:], k[TM_SUB:], v[TM_SUB:], ck_ref, cv_ref, nck_ref, ncv_ref, sinks_ref, 0,
                wide)
            o_a = jnp.concatenate([o_a, o_s], axis=0)
        else:
            c, hc, b, z_b = [f() for f in wide]
        k_prev = kb[TM_SUB - BLOCK:]
        v_prev = vb[TM_SUB - BLOCK:]
        k_last = k[TM_SUB - BLOCK:TM_SUB]
        v_last = v[TM_SUB - BLOCK:TM_SUB]

        u_all = c * hc
        u = u_all[:TM_SUB]
        prev1 = tail[SUBLANES - 1:SUBLANES]
        prev2 = tail[SUBLANES - 2:SUBLANES - 1]
        r1 = pltpu.roll(u, 1, 0)
        r2 = pltpu.roll(u, 2, 0)
        um1 = jnp.concatenate(
            [jnp.where(row8 == 0, prev1, r1[:SUBLANES]), r1[SUBLANES:]], axis=0)
        um2 = jnp.concatenate(
            [jnp.where(row8 == 0, prev2, jnp.where(row8 == 1, prev1, r2[:SUBLANES])),
             r2[SUBLANES:]], axis=0)
        conv = cw[0:1] * um2 + cw[1:2] * um1 + cw[2:3] * u
        tail = u[TM_SUB - SUBLANES:]
        if with_dec:
            u3 = u_all[TM_SUB:].reshape(DEC_PER_STEP, DEC_SEQ, D_MODEL)
            st = st_ref[...]
            s_m2 = st[:, 0:1, :]
            s_m1 = st[:, 1:2, :]
            us1 = jnp.where(t_io == 0, s_m1, pltpu.roll(u3, 1, 1))
            us2 = jnp.where(t_io == 0, s_m2, jnp.where(t_io == 1, s_m1, pltpu.roll(u3, 2, 1)))
            conv_s = (cw[0:1] * us2 + cw[1:2] * us1 + cw[2:3] * u3).reshape(TS, D_MODEL)
            ncs_ref[...] = u3[:, DEC_SEQ - (CONV_K - 1):, :]
            conv = jnp.concatenate([conv, conv_s], axis=0)

        z_a = _dot(xn, wza_ref[...])
        br_a = _dot((o_a * _silu(z_a)).astype(BF16), wpa_ref[...])
        br_b = _dot(((b * conv) * _silu(z_b)).astype(BF16), wpb_ref[...])
        g_a = _dot(xn, wg_s[:, G_GA - G_B:G_GB - G_B])
        g_b = _dot(xn, wg_s[:, G_GB - G_B:G_END - G_B])
        mixed = (jax.nn.sigmoid(g_a) * br_a + jax.nn.sigmoid(g_b) * br_b).astype(BF16)
        y = x + _dot(mixed, wo_ref[...])
        y_ref[ps, :] = y[:TM_SUB]
        if with_dec:
            ys_ref[...] = y[TM_SUB:]

    kprev[...] = k_last
    vprev[...] = v_last
    utail[...] = tail
    nkt_ref[...] = _rows_split_to_natural(k_last.T)
    nvt_ref[...] = v_last.T
    nc_ref[...] = tail[SUBLANES - (CONV_K - 1):]


def _const_spec(shape):
    zeros = (0,) * len(shape)
    return pl.BlockSpec(shape, lambda i: zeros, pipeline_mode=pl.Buffered(1))


def _layer_call(x, xs, tabs, ck, cv, st, w_in, meta, tabs0, tabs_s, nw, qnw, knw, cw, sinks,
                wq, wk, wza, wpa, wpb, wo, bd):
    tm = TM_PROMPT
    assert N_STEPS * DEC_PER_STEP == DEC_BATCH and TS % (2 * SUBLANES) == 0 and TS <= LANES
    assert TM_SUB % BLOCK == 0
    row_spec = lambda rows, w: pl.BlockSpec((rows, w), lambda i: (i, 0))
    cache_spec = pl.BlockSpec((DEC_PER_STEP, KV_WIDTH, WINDOW), lambda i: (i, 0, 0))
    st_spec = pl.BlockSpec((DEC_PER_STEP, CONV_K - 1, D_MODEL), lambda i: (i, 0, 0))
    consts = (meta, tabs0[0], tabs0[1], tabs_s[0], tabs_s[1], nw, qnw, knw, cw,
              wq, wk, wza, wpa, wpb, wo, bd)
    in_specs = [pl.BlockSpec(memory_space=pltpu.SMEM),
                row_spec(tm, D_MODEL), row_spec(TS, D_MODEL), row_spec(tm, LANES), row_spec(tm, LANES),
                cache_spec, cache_spec, st_spec, pl.BlockSpec(memory_space=pl.ANY)]
    in_specs += [_const_spec(a.shape) for a in consts]
    resident = lambda shape: pl.BlockSpec(shape, lambda i: (0, 0))
    out_specs = [row_spec(tm, D_MODEL), row_spec(TS, D_MODEL),
                 resident((KV_WIDTH, BLOCK)), resident((KV_WIDTH, BLOCK)),
                 resident((CONV_K - 1, D_MODEL)), cache_spec, cache_spec, st_spec]
    cache_sds = jax.ShapeDtypeStruct((DEC_BATCH, KV_WIDTH, WINDOW), F32)
    out_shape = [jax.ShapeDtypeStruct((SEQ, D_MODEL), F32),
                 jax.ShapeDtypeStruct((DEC_BATCH * DEC_SEQ, D_MODEL), F32),
                 jax.ShapeDtypeStruct((KV_WIDTH, BLOCK), F32),
                 jax.ShapeDtypeStruct((KV_WIDTH, BLOCK), F32),
                 jax.ShapeDtypeStruct((CONV_K - 1, D_MODEL), F32), cache_sds, cache_sds,
                 jax.ShapeDtypeStruct((DEC_BATCH, CONV_K - 1, D_MODEL), F32)]
    return pl.pallas_call(
        _layer_kernel,
        grid=(N_STEPS,),
        in_specs=in_specs,
        out_specs=out_specs,
        out_shape=out_shape,
        scratch_shapes=[pltpu.VMEM((BLOCK, KV_WIDTH), F32),
                        pltpu.VMEM((BLOCK, KV_WIDTH), F32),
                        pltpu.VMEM((SUBLANES, D_MODEL), F32),
                        pltpu.VMEM((D_MODEL, G_END - G_B), BF16),
                        pltpu.VMEM((2, GW_CHUNK, G_END - G_B), F32),
                        pltpu.SemaphoreType.DMA((2,))],
        compiler_params=pltpu.CompilerParams(
            dimension_semantics=("arbitrary",), vmem_limit_bytes=VMEM_LIMIT),
        name="hybrid_layer",
    )(sinks, x, xs, tabs[0], tabs[1], ck, cv, st, w_in, *consts)


def _rope_tables(positions):
    inv = np.power(ROPE_THETA, -np.arange(HALF, dtype=np.float64) * (2.0 / HEAD_DIM))
    ang = np.asarray(positions, dtype=np.float64)[:, None] * inv[None, :]
    cos = np.tile(np.cos(ang), (1, LANES // HALF))
    sin = np.tile(np.sin(ang), (1, LANES // HALF))
    return jnp.asarray(cos, F32), jnp.asarray(sin, F32)


def _head_mean_matrix():
    grp = np.arange(256) // HALF
    return jnp.asarray((grp[:, None] == grp[None, :]) / float(HEAD_DIM), BF16)


def kernel(x_prompt, x_sample, cache_k, cache_v, state_conv, meta_tokens, norm_w, w_in,
           q_norm_w, k_norm_w, sinks, conv_w, w_proj_a, w_proj_b, w_out):
    assert x_prompt.shape == (1, SEQ, D_MODEL) and x_sample.shape == (DEC_BATCH, DEC_SEQ, D_MODEL)
    assert w_in.shape[0] == 1, "single layer"
    wg = w_in[0, :, :G_B].astype(BF16)
    wq = (wg[:, :W_K].reshape(D_MODEL, N_KV_HEADS, GROUP, 2, HALF)
          .transpose(0, 2, 3, 1, 4).reshape(D_MODEL, D_MODEL))
    wk = (wg[:, W_K:W_V].reshape(D_MODEL, N_KV_HEADS, 2, HALF)
          .transpose(0, 2, 1, 3).reshape(D_MODEL, KV_WIDTH))
    wk = jnp.concatenate([wk, wg[:, W_V:W_ZA]], axis=1)
    wza = (wg[:, W_ZA:G_B].reshape(D_MODEL, N_KV_HEADS, GROUP, HEAD_DIM)
           .transpose(0, 2, 1, 3).reshape(D_MODEL, D_MODEL))
    wpa = (w_proj_a[0].astype(BF16).reshape(N_KV_HEADS, GROUP, HEAD_DIM, D_MODEL)
           .transpose(1, 0, 2, 3).reshape(D_MODEL, D_MODEL))
    wpb = w_proj_b[0].astype(BF16)
    wo = w_out[0].astype(BF16)

    nw = norm_w[0].reshape(1, D_MODEL)
    split = lambda v: jnp.broadcast_to(v.reshape(2, 1, HALF), (2, N_KV_HEADS, HALF)).reshape(1, 256)

    to_tiles = lambda c: c[0].transpose(0, 2, 3, 1).reshape(DEC_BATCH, KV_WIDTH, WINDOW)
    from_tiles = lambda c: (c.reshape(DEC_BATCH, N_KV_HEADS, HEAD_DIM, WINDOW)
                            .transpose(0, 3, 1, 2)[None])
    from_tile = lambda t: t.reshape(N_KV_HEADS, HEAD_DIM, WINDOW).transpose(2, 0, 1)

    tabs_p = _rope_tables(np.arange(SEQ) + N_META)
    tabs_0 = _rope_tables(np.arange(BLOCK) - LEAD)
    tabs_s = _rope_tables(PAST_LEN + (np.arange(TS) % DEC_SEQ))
    y_p, y_s, nkt_p, nvt_p, nc_p, nk_s, nv_s, nc_s = _layer_call(
        x_prompt[0], x_sample.reshape(DEC_BATCH * DEC_SEQ, D_MODEL), tabs_p,
        to_tiles(cache_k), to_tiles(cache_v), state_conv[0], w_in,
        meta_tokens.astype(x_prompt.dtype), tabs_0, tabs_s,
        nw, split(q_norm_w), split(k_norm_w), conv_w, sinks,
        wq, wk, wza, wpa, wpb, wo, _head_mean_matrix())

    return (y_p.reshape(1, SEQ, D_MODEL),
            y_s.reshape(DEC_BATCH, DEC_SEQ, D_MODEL),
            from_tile(nkt_p).reshape(1, 1, WINDOW, N_KV_HEADS, HEAD_DIM),
            from_tile(nvt_p).reshape(1, 1, WINDOW, N_KV_HEADS, HEAD_DIM),
            nc_p.reshape(1, 1, CONV_K - 1, D_MODEL),
            from_tiles(nk_s),
            from_tiles(nv_s),
            nc_s.reshape(1, DEC_BATCH, CONV_K - 1, D_MODEL))
```

```python
import functools

import numpy as np
import jax
import jax.numpy as jnp
from jax import lax
from jax.experimental import pallas as pl
from jax.experimental.pallas import tpu as pltpu

D_MODEL = 1024
SEQ = 16384
DEC_BATCH = 128
DEC_SEQ = 8
PAST_LEN = 16384
N_HEADS = 16
N_KV_HEADS = 4
GROUP = N_HEADS // N_KV_HEADS
HEAD_DIM = 64
HALF = HEAD_DIM // 2
KV_WIDTH = N_KV_HEADS * HEAD_DIM
WINDOW = 128
BLOCK = 128
ROPE_THETA = 10000.0
CONV_K = 3
N_META = 16
LEAD = BLOCK - N_META
EPS = 1e-6
NEG_INF = -1e30
Q_SCALE = HEAD_DIM ** -0.5

W_K = D_MODEL
W_V = W_K + KV_WIDTH
W_ZA = W_V + KV_WIDTH
G_B = W_ZA + D_MODEL
G_C = G_B + D_MODEL
G_H = G_C + D_MODEL
G_ZB = G_H + D_MODEL
G_GA = G_ZB + D_MODEL
G_GB = G_GA + D_MODEL
G_END = G_GB + D_MODEL

LANES = 128
SUBLANES = 8
VMEM_BYTES_V7X = 64 * 1024 * 1024

TM_PROMPT = 512
N_STEPS = SEQ // TM_PROMPT
DEC_PER_STEP = DEC_BATCH // N_STEPS
TS = DEC_PER_STEP * DEC_SEQ
GW_CHUNK = 64
GW_SLOTS = 4
SUB_TILES = 2
TM_SUB = TM_PROMPT // SUB_TILES
VMEM_LIMIT = VMEM_BYTES_V7X - 4 * 1024 * 1024

BF16 = jnp.bfloat16
F32 = jnp.float32


def _dot(a, b):
    return jnp.dot(a, b, preferred_element_type=F32)


def _dot_nt(a, b):
    return lax.dot_general(a, b, (((1,), (1,)), ((), ())), preferred_element_type=F32)


def _rms_rows(x, w):
    ms = jnp.mean(x * x, axis=-1, keepdims=True)
    return (x * lax.rsqrt(ms + EPS)) * w


def _head_rms_split(t, bd, w):
    n = t.shape[1] // 256
    r = t.shape[0]
    halves = [(t[:, 256 * c:256 * c + LANES], t[:, 256 * c + LANES:256 * (c + 1)]) for c in range(n)]
    pieces = [a * a + b * b for a, b in halves]
    if n == 1:
        pieces = [pieces[0], pieces[0]]
    pair_ms = [_dot(jnp.concatenate(pieces[2 * i:2 * i + 2], axis=1).astype(BF16), bd)
               for i in range(len(pieces) // 2)]
    ms = [pair_ms[i // 2][:, LANES * (i % 2):LANES * (i % 2 + 1)] for i in range(n)]
    out = []
    for (a, b), m in zip(halves, ms):
        scale = lax.rsqrt(m + EPS)
        out.append((a * scale) * w[:, :LANES])
        out.append((b * scale) * w[:, LANES:])
    return jnp.concatenate(out, axis=1)


def _rope_split(t, cos, sin):
    out = []
    for c in range(t.shape[1] // 256):
        a = t[:, 256 * c:256 * c + LANES]
        b = t[:, 256 * c + LANES:256 * (c + 1)]
        out.append(a * cos - b * sin)
        out.append(b * cos + a * sin)
    return jnp.concatenate(out, axis=1)


def _lane_group_masks(group_of_lane, dtype):
    return [(group_of_lane == g).astype(dtype) for g in range(N_KV_HEADS)]


def _softmax_parts(s, sk):
    m = jnp.maximum(jnp.max(s, axis=-1, keepdims=True), sk)
    p = jnp.exp(s - m)
    den = jnp.sum(p, axis=-1, keepdims=True) + jnp.exp(sk - m)
    return p, 1.0 / den


def _silu(z):
    return z * jax.nn.sigmoid(z)


def _rows_split_to_natural(t):
    return jnp.concatenate(
        [t[LANES * half + HALF * g:LANES * half + HALF * (g + 1)]
         for g in range(N_KV_HEADS) for half in range(2)], axis=0)


def _rows_natural_to_split(t):
    return jnp.concatenate(
        [t[HEAD_DIM * g + HALF * half:HEAD_DIM * g + HALF * (half + 1)]
         for half in range(2) for g in range(N_KV_HEADS)], axis=0)


def _prompt_attention(q, kb, vb, kprev, vprev, sinks_ref, first_step):
    tm = q.shape[0]
    lane = lax.broadcasted_iota(jnp.int32, (1, 2 * BLOCK), 1)
    kmasks = _lane_group_masks((lane & (LANES - 1)) >> 5, BF16)
    vmasks = _lane_group_masks(lane >> 6, BF16)
    vgroup = lane >> 6
    r_io = lax.broadcasted_iota(jnp.int32, (BLOCK, 2 * BLOCK), 0)
    c_io = lax.broadcasted_iota(jnp.int32, (BLOCK, 2 * BLOCK), 1)
    band = (c_io >= r_io) & (c_io <= r_io + WINDOW)
    first_band = band & (c_io >= jnp.where(first_step, LEAD, 0))

    o_blocks = []
    for b in range(tm // BLOCK):
        lo = b * BLOCK
        if b == 0:
            kpair = jnp.concatenate([kprev, kb[:BLOCK]], axis=0)
            vpair = jnp.concatenate([vprev, vb[:BLOCK]], axis=0)
            mask = first_band
        else:
            kpair = kb[lo - BLOCK:lo + BLOCK]
            vpair = vb[lo - BLOCK:lo + BLOCK]
            mask = band
        kbd = jnp.concatenate([kpair * km for km in kmasks], axis=0)
        vbd = jnp.concatenate([vpair * vm for vm in vmasks], axis=0)
        qs = jnp.concatenate([q[lo:lo + BLOCK, 256 * j:256 * (j + 1)] for j in range(GROUP)],
                             axis=0)
        s = _dot_nt(qs, kbd)
        p_rows = []
        mults = []
        for j in range(GROUP):
            p_cols = []
            invs = []
            for g in range(N_KV_HEADS):
                sl = s[BLOCK * j:BLOCK * (j + 1), 256 * g:256 * (g + 1)]
                sl = jnp.where(mask, sl, NEG_INF)
                p, inv = _softmax_parts(sl, sinks_ref[0, GROUP * g + j])
                p_cols.append(p.astype(BF16))
                invs.append(inv)
            p_rows.append(jnp.concatenate(p_cols, axis=1))
            mults.append(jnp.where(vgroup == 0, invs[0],
                                   jnp.where(vgroup == 1, invs[1],
                                             jnp.where(vgroup == 2, invs[2], invs[3]))))
        p_all = jnp.concatenate(p_rows, axis=0)
        half_k = 2 * 2 * BLOCK
        o = _dot(p_all[:, :half_k], vbd[:half_k]) + _dot(p_all[:, half_k:], vbd[half_k:])
        o_blocks.append(jnp.concatenate(
            [o[BLOCK * j:BLOCK * (j + 1)] * mults[j] for j in range(GROUP)], axis=1))
    return jnp.concatenate(o_blocks, axis=0)


def _decode_attention(qs, ks, vs, ck_ref, cv_ref, nck_ref, ncv_ref, sinks_ref, first, fillers):
    t_len = DEC_SEQ
    n_exp = GROUP * N_KV_HEADS * t_len
    pad = jnp.zeros((LANES - qs.shape[0], KV_WIDTH), F32)
    kst = jnp.concatenate([ks, pad], axis=0).T
    vst = jnp.concatenate([vs, pad], axis=0).T
    kst_nat = _rows_split_to_natural(kst)
    kst_b = kst.astype(BF16)
    vst_b = vst.astype(BF16)

    lane = lax.broadcasted_iota(jnp.int32, (1, KV_WIDTH), 1)
    kgroup = (lane & (LANES - 1)) >> 5
    vgroup = lane >> 6
    t_io = lax.broadcasted_iota(jnp.int32, (n_exp, WINDOW), 0) & (t_len - 1)
    c_io = lax.broadcasted_iota(jnp.int32, (n_exp, WINDOW), 1)
    old_ok = c_io >= t_io
    new_lane = lax.broadcasted_iota(jnp.int32, (KV_WIDTH, WINDOW), 1) >= WINDOW - t_len
    skcol = jnp.concatenate(
        [jnp.full((t_len, 1), sinks_ref[0, GROUP * g + j], F32)
         for j in range(GROUP) for g in range(N_KV_HEADS)], axis=0)

    outs = []
    filled = []
    for b in range(qs.shape[0] // t_len):
        if b < len(fillers):
            filled.append(fillers[b]())
        r0 = b * t_len
        qn = qs[r0:r0 + t_len]
        q_exp = jnp.concatenate(
            [jnp.where(kgroup == g, qn[:, 256 * j:256 * (j + 1)], 0.0)
             for j in range(GROUP) for g in range(N_KV_HEADS)], axis=0).astype(BF16)
        kt_old = ck_ref[first + b]
        vt_old = cv_ref[first + b]
        keys = jnp.concatenate([_rows_natural_to_split(kt_old).astype(BF16), kst_b], axis=1)
        vals = jnp.concatenate([vt_old.astype(BF16), vst_b], axis=1)
        s = _dot(q_exp, keys)
        new_ok = (c_io >= r0) & (c_io - t_io <= r0)
        s = jnp.where(jnp.concatenate([old_ok, new_ok], axis=1), s, NEG_INF)
        p, inv = _softmax_parts(s, skcol)
        o = _dot_nt(p.astype(BF16), vals) * inv
        o_j = []
        for j in range(GROUP):
            acc = None
            for g in range(N_KV_HEADS):
                r = (j * N_KV_HEADS + g) * t_len
                piece = jnp.where(vgroup == g, o[r:r + t_len], 0.0)
                acc = piece if acc is None else acc + piece
            o_j.append(acc)
        outs.append(jnp.concatenate(o_j, axis=1))
        shift_new = WINDOW - t_len - r0
        nck_ref[first + b] = jnp.where(new_lane, pltpu.roll(kst_nat, shift_new, 1),
                                       pltpu.roll(kt_old, WINDOW - t_len, 1))
        ncv_ref[first + b] = jnp.where(new_lane, pltpu.roll(vst, shift_new, 1),
                                       pltpu.roll(vt_old, WINDOW - t_len, 1))
    filled += [f() for f in fillers[len(filled):]]
    return jnp.concatenate(outs, axis=0), filled


def _gate_weight_copy(w_hbm, wstage, wsem, chunk):
    slot = chunk % GW_SLOTS
    return pltpu.make_async_copy(
        w_hbm.at[0, pl.ds(GW_CHUNK * chunk, GW_CHUNK), pl.ds(G_B, G_END - G_B)],
        wstage.at[slot], wsem.at[slot])


def _load_gate_weights(w_hbm, wg_s, wstage, wsem):
    n_chunks = D_MODEL // GW_CHUNK
    ahead = GW_SLOTS - 1
    for c in range(ahead):
        _gate_weight_copy(w_hbm, wstage, wsem, c).start()
    for c in range(n_chunks):
        if c + ahead < n_chunks:
            _gate_weight_copy(w_hbm, wstage, wsem, c + ahead).start()
        _gate_weight_copy(w_hbm, wstage, wsem, c).wait()
        wg_s[GW_CHUNK * c:GW_CHUNK * (c + 1), :] = wstage[c % GW_SLOTS].astype(BF16)


def _layer_kernel(sinks_ref, x_ref, xs_ref, cos_ref, sin_ref, ck_ref, cv_ref, st_ref, w_hbm,
                  meta_ref, cos0_ref, sin0_ref, coss_ref, sins_ref, nw_ref, qnw_ref, knw_ref, cw_ref,
                  wq_ref, wkv_ref, wza_ref, wpa_ref, wpb_ref, wo_ref, bd_ref,
                  y_ref, ys_ref, nkt_ref, nvt_ref, nc_ref, nck_ref, ncv_ref, ncs_ref,
                  kprev, vprev, utail, wg_s, wstage, wsem):
    i = pl.program_id(0)
    nw = nw_ref[...]
    bd = bd_ref[...]
    knw = knw_ref[...]

    @pl.when(i == 0)
    def _init():
        _load_gate_weights(w_hbm, wg_s, wstage, wsem)
        x0 = jnp.concatenate([jnp.zeros((LEAD, D_MODEL), F32), meta_ref[...]], axis=0)
        xn0 = _rms_rows(x0, nw).astype(BF16)
        kv0 = _dot(xn0, wkv_ref[...])
        k0 = _rope_split(_head_rms_split(kv0[:, :KV_WIDTH], bd, knw), cos0_ref[...], sin0_ref[...])
        v0 = kv0[:, KV_WIDTH:]
        c0 = _dot(xn0, wg_s[:, G_C - G_B:G_H - G_B])
        h0 = _dot(xn0, wg_s[:, G_H - G_B:G_ZB - G_B])
        row = lax.broadcasted_iota(jnp.int32, (BLOCK, D_MODEL), 0)
        u0 = jnp.where(row >= LEAD, c0 * h0, 0.0)
        kprev[...] = k0
        vprev[...] = v0
        utail[...] = u0[BLOCK - SUBLANES:BLOCK]

    cw = cw_ref[0]
    row8 = lax.broadcasted_iota(jnp.int32, (SUBLANES, D_MODEL), 0)
    t_io = lax.broadcasted_iota(jnp.int32, (DEC_PER_STEP, DEC_SEQ, D_MODEL), 1)
    k_prev = kprev[...].astype(BF16)
    v_prev = vprev[...].astype(BF16)
    tail = utail[...]
    for h in range(SUB_TILES):
        ps = slice(TM_SUB * h, TM_SUB * (h + 1))
        with_dec = h == SUB_TILES - 1
        if with_dec:
            x = jnp.concatenate([x_ref[ps, :], xs_ref[...]], axis=0)
            cos = jnp.concatenate([cos_ref[ps, :], coss_ref[...]], axis=0)
            sin = jnp.concatenate([sin_ref[ps, :], sins_ref[...]], axis=0)
        else:
            x = x_ref[ps, :]
            cos = cos_ref[ps, :]
            sin = sin_ref[ps, :]
        xn = _rms_rows(x, nw).astype(BF16)

        q = _head_rms_split(_dot(xn, wq_ref[...]), bd, qnw_ref[...])
        q = _rope_split(q, cos * Q_SCALE, sin * Q_SCALE)
        kv = _dot(xn, wkv_ref[...])
        k = _rope_split(_head_rms_split(kv[:, :KV_WIDTH], bd, knw), cos, sin)
        v = kv[:, KV_WIDTH:]
        kb = k[:TM_SUB].astype(BF16)
        vb = v[:TM_SUB].astype(BF16)

        first = (i == 0) if h == 0 else False
        o_a = _prompt_attention(q[:TM_SUB].astype(BF16), kb, vb, k_prev, v_prev, sinks_ref, first)
        wide = [lambda lo=lo: _dot(xn, wg_s[:, lo - G_B:lo - G_B + D_MODEL])
                for lo in (G_C, G_H, G_B, G_ZB)]
        if with_dec:
            o_s, (c, hc, b, z_b) = _decode_attention(
                q[TM_SUB:], k[TM_SUB:], v[TM_SUB:], ck_ref, cv_ref, nck_ref, ncv_ref, sinks_ref, 0,
                wide)
            o_a = jnp.concatenate([o_a, o_s], axis=0)
        else:
            c, hc, b, z_b = [f() for f in wide]
        k_prev = kb[TM_SUB - BLOCK:]
        v_prev = vb[TM_SUB - BLOCK:]
        k_last = k[TM_SUB - BLOCK:TM_SUB]
        v_last = v[TM_SUB - BLOCK:TM_SUB]

        u_all = c * hc
        u = u_all[:TM_SUB]
        prev1 = tail[SUBLANES - 1:SUBLANES]
        prev2 = tail[SUBLANES - 2:SUBLANES - 1]
        r1 = pltpu.roll(u, 1, 0)
        r2 = pltpu.roll(u, 2, 0)
        um1 = jnp.concatenate(
            [jnp.where(row8 == 0, prev1, r1[:SUBLANES]), r1[SUBLANES:]], axis=0)
        um2 = jnp.concatenate(
            [jnp.where(row8 == 0, prev2, jnp.where(row8 == 1, prev1, r2[:SUBLANES])),
             r2[SUBLANES:]], axis=0)
        conv = cw[0:1] * um2 + cw[1:2] * um1 + cw[2:3] * u
        tail = u[TM_SUB - SUBLANES:]
        if with_dec:
            u3 = u_all[TM_SUB:].reshape(DEC_PER_STEP, DEC_SEQ, D_MODEL)
            st = st_ref[...]
            s_m2 = st[:, 0:1, :]
            s_m1 = st[:, 1:2, :]
            us1 = jnp.where(t_io == 0, s_m1, pltpu.roll(u3, 1, 1))
            us2 = jnp.where(t_io == 0, s_m2, jnp.where(t_io == 1, s_m1, pltpu.roll(u3, 2, 1)))
            conv_s = (cw[0:1] * us2 + cw[1:2] * us1 + cw[2:3] * u3).reshape(TS, D_MODEL)
            ncs_ref[...] = u3[:, DEC_SEQ - (CONV_K - 1):, :]
            conv = jnp.concatenate([conv, conv_s], axis=0)

        z_a = _dot(xn, wza_ref[...])
        br_a = _dot((o_a * _silu(z_a)).astype(BF16), wpa_ref[...])
        br_b = _dot(((b * conv) * _silu(z_b)).astype(BF16), wpb_ref[...])
        g_a = _dot(xn, wg_s[:, G_GA - G_B:G_GB - G_B])
        g_b = _dot(xn, wg_s[:, G_GB - G_B:G_END - G_B])
        mixed = (jax.nn.sigmoid(g_a) * br_a + jax.nn.sigmoid(g_b) * br_b).astype(BF16)
        y = x + _dot(mixed, wo_ref[...])
        y_ref[ps, :] = y[:TM_SUB]
        if with_dec:
            ys_ref[...] = y[TM_SUB:]

    kprev[...] = k_last
    vprev[...] = v_last
    utail[...] = tail
    nkt_ref[...] = _rows_split_to_natural(k_last.T)
    nvt_ref[...] = v_last.T
    nc_ref[...] = tail[SUBLANES - (CONV_K - 1):]


def _const_spec(shape):
    zeros = (0,) * len(shape)
    return pl.BlockSpec(shape, lambda i: zeros, pipeline_mode=pl.Buffered(1))


def _layer_call(x, xs, tabs, ck, cv, st, w_in, meta, tabs0, tabs_s, nw, qnw, knw, cw, sinks,
                wq, wk, wza, wpa, wpb, wo, bd):
    tm = TM_PROMPT
    assert N_STEPS * DEC_PER_STEP == DEC_BATCH and TS % (2 * SUBLANES) == 0 and TS <= LANES
    assert TM_SUB % BLOCK == 0
    row_spec = lambda rows, w: pl.BlockSpec((rows, w), lambda i: (i, 0))
    cache_spec = pl.BlockSpec((DEC_PER_STEP, KV_WIDTH, WINDOW), lambda i: (i, 0, 0))
    st_spec = pl.BlockSpec((DEC_PER_STEP, CONV_K - 1, D_MODEL), lambda i: (i, 0, 0))
    consts = (meta, tabs0[0], tabs0[1], tabs_s[0], tabs_s[1], nw, qnw, knw, cw,
              wq, wk, wza, wpa, wpb, wo, bd)
    in_specs = [pl.BlockSpec(memory_space=pltpu.SMEM),
                row_spec(tm, D_MODEL), row_spec(TS, D_MODEL), row_spec(tm, LANES), row_spec(tm, LANES),
                cache_spec, cache_spec, st_spec, pl.BlockSpec(memory_space=pl.ANY)]
    in_specs += [_const_spec(a.shape) for a in consts]
    resident = lambda shape: pl.BlockSpec(shape, lambda i: (0, 0))
    out_specs = [row_spec(tm, D_MODEL), row_spec(TS, D_MODEL),
                 resident((KV_WIDTH, BLOCK)), resident((KV_WIDTH, BLOCK)),
                 resident((CONV_K - 1, D_MODEL)), cache_spec, cache_spec, st_spec]
    cache_sds = jax.ShapeDtypeStruct((DEC_BATCH, KV_WIDTH, WINDOW), F32)
    out_shape = [jax.ShapeDtypeStruct((SEQ, D_MODEL), F32),
                 jax.ShapeDtypeStruct((DEC_BATCH * DEC_SEQ, D_MODEL), F32),
                 jax.ShapeDtypeStruct((KV_WIDTH, BLOCK), F32),
                 jax.ShapeDtypeStruct((KV_WIDTH, BLOCK), F32),
                 jax.ShapeDtypeStruct((CONV_K - 1, D_MODEL), F32), cache_sds, cache_sds,
                 jax.ShapeDtypeStruct((DEC_BATCH, CONV_K - 1, D_MODEL), F32)]
    return pl.pallas_call(
        _layer_kernel,
        grid=(N_STEPS,),
        in_specs=in_specs,
        out_specs=out_specs,
        out_shape=out_shape,
        scratch_shapes=[pltpu.VMEM((BLOCK, KV_WIDTH), F32),
                        pltpu.VMEM((BLOCK, KV_WIDTH), F32),
                        pltpu.VMEM((SUBLANES, D_MODEL), F32),
                        pltpu.VMEM((D_MODEL, G_END - G_B), BF16),
                        pltpu.VMEM((GW_SLOTS, GW_CHUNK, G_END - G_B), F32),
                        pltpu.SemaphoreType.DMA((GW_SLOTS,))],
        compiler_params=pltpu.CompilerParams(
            dimension_semantics=("arbitrary",), vmem_limit_bytes=VMEM_LIMIT),
        name="hybrid_layer",
    )(sinks, x, xs, tabs[0], tabs[1], ck, cv, st, w_in, *consts)


def _rope_tables(positions):
    inv = np.power(ROPE_THETA, -np.arange(HALF, dtype=np.float64) * (2.0 / HEAD_DIM))
    ang = np.asarray(positions, dtype=np.float64)[:, None] * inv[None, :]
    cos = np.tile(np.cos(ang), (1, LANES // HALF))
    sin = np.tile(np.sin(ang), (1, LANES // HALF))
    return jnp.asarray(cos, F32), jnp.asarray(sin, F32)


def _head_mean_matrix():
    grp = np.arange(256) // HALF
    return jnp.asarray((grp[:, None] == grp[None, :]) / float(HEAD_DIM), BF16)


def kernel(x_prompt, x_sample, cache_k, cache_v, state_conv, meta_tokens, norm_w, w_in,
           q_norm_w, k_norm_w, sinks, conv_w, w_proj_a, w_proj_b, w_out):
    assert x_prompt.shape == (1, SEQ, D_MODEL) and x_sample.shape == (DEC_BATCH, DEC_SEQ, D_MODEL)
    assert w_in.shape[0] == 1, "single layer"
    wg = w_in[0, :, :G_B].astype(BF16)
    wq = (wg[:, :W_K].reshape(D_MODEL, N_KV_HEADS, GROUP, 2, HALF)
          .transpose(0, 2, 3, 1, 4).reshape(D_MODEL, D_MODEL))
    wk = (wg[:, W_K:W_V].reshape(D_MODEL, N_KV_HEADS, 2, HALF)
          .transpose(0, 2, 1, 3).reshape(D_MODEL, KV_WIDTH))
    wk = jnp.concatenate([wk, wg[:, W_V:W_ZA]], axis=1)
    wza = (wg[:, W_ZA:G_B].reshape(D_MODEL, N_KV_HEADS, GROUP, HEAD_DIM)
           .transpose(0, 2, 1, 3).reshape(D_MODEL, D_MODEL))
    wpa = (w_proj_a[0].astype(BF16).reshape(N_KV_HEADS, GROUP, HEAD_DIM, D_MODEL)
           .transpose(1, 0, 2, 3).reshape(D_MODEL, D_MODEL))
    wpb = w_proj_b[0].astype(BF16)
    wo = w_out[0].astype(BF16)

    nw = norm_w[0].reshape(1, D_MODEL)
    split = lambda v: jnp.broadcast_to(v.reshape(2, 1, HALF), (2, N_KV_HEADS, HALF)).reshape(1, 256)

    to_tiles = lambda c: c[0].transpose(0, 2, 3, 1).reshape(DEC_BATCH, KV_WIDTH, WINDOW)
    from_tiles = lambda c: (c.reshape(DEC_BATCH, N_KV_HEADS, HEAD_DIM, WINDOW)
                            .transpose(0, 3, 1, 2)[None])
    from_tile = lambda t: t.reshape(N_KV_HEADS, HEAD_DIM, WINDOW).transpose(2, 0, 1)

    tabs_p = _rope_tables(np.arange(SEQ) + N_META)
    tabs_0 = _rope_tables(np.arange(BLOCK) - LEAD)
    tabs_s = _rope_tables(PAST_LEN + (np.arange(TS) % DEC_SEQ))
    y_p, y_s, nkt_p, nvt_p, nc_p, nk_s, nv_s, nc_s = _layer_call(
        x_prompt[0], x_sample.reshape(DEC_BATCH * DEC_SEQ, D_MODEL), tabs_p,
        to_tiles(cache_k), to_tiles(cache_v), state_conv[0], w_in,
        meta_tokens.astype(x_prompt.dtype), tabs_0, tabs_s,
        nw, split(q_norm_w), split(k_norm_w), conv_w, sinks,
        wq, wk, wza, wpa, wpb, wo, _head_mean_matrix())

    return (y_p.reshape(1, SEQ, D_MODEL),
            y_s.reshape(DEC_BATCH, DEC_SEQ, D_MODEL),
            from_tile(nkt_p).reshape(1, 1, WINDOW, N_KV_HEADS, HEAD_DIM),
            from_tile(nvt_p).reshape(1, 1, WINDOW, N_KV_HEADS, HEAD_DIM),
            nc_p.reshape(1, 1, CONV_K - 1, D_MODEL),
            from_tiles(nk_s),
            from_tiles(nv_s),
            nc_s.reshape(1, DEC_BATCH, CONV_K - 1, D_MODEL))
```

```python
import functools

import numpy as np
import jax
import jax.numpy as jnp
from jax import lax
from jax.experimental import pallas as pl
from jax.experimental.pallas import tpu as pltpu

D_MODEL = 1024
SEQ = 16384
DEC_BATCH = 128
DEC_SEQ = 8
PAST_LEN = 16384
N_HEADS = 16
N_KV_HEADS = 4
GROUP = N_HEADS // N_KV_HEADS
HEAD_DIM = 64
HALF = HEAD_DIM // 2
KV_WIDTH = N_KV_HEADS * HEAD_DIM
WINDOW = 128
BLOCK = 128
ROPE_THETA = 10000.0
CONV_K = 3
N_META = 16
LEAD = BLOCK - N_META
EPS = 1e-6
NEG_INF = -1e30
Q_SCALE = HEAD_DIM ** -0.5

W_K = D_MODEL
W_V = W_K + KV_WIDTH
W_ZA = W_V + KV_WIDTH
G_B = W_ZA + D_MODEL
G_C = G_B + D_MODEL
G_H = G_C + D_MODEL
G_ZB = G_H + D_MODEL
G_GA = G_ZB + D_MODEL
G_GB = G_GA + D_MODEL
G_END = G_GB + D_MODEL

LANES = 128
SUBLANES = 8
VMEM_BYTES_V7X = 64 * 1024 * 1024

TM_PROMPT = 512
N_STEPS = SEQ // TM_PROMPT
DEC_PER_STEP = DEC_BATCH // N_STEPS
TS = DEC_PER_STEP * DEC_SEQ
GW_CHUNK = 64
GW_SLOTS = 4
SUB_TILES = 2
TM_SUB = TM_PROMPT // SUB_TILES
VMEM_LIMIT = VMEM_BYTES_V7X - 4 * 1024 * 1024

BF16 = jnp.bfloat16
F32 = jnp.float32


def _dot(a, b):
    return jnp.dot(a, b, preferred_element_type=F32)


def _dot_nt(a, b):
    return lax.dot_general(a, b, (((1,), (1,)), ((), ())), preferred_element_type=F32)


def _rms_rows(x, w):
    ms = jnp.mean(x * x, axis=-1, keepdims=True)
    return (x * lax.rsqrt(ms + EPS)) * w


def _head_rms_split(t, bd, w):
    n = t.shape[1] // 256
    r = t.shape[0]
    halves = [(t[:, 256 * c:256 * c + LANES], t[:, 256 * c + LANES:256 * (c + 1)]) for c in range(n)]
    pieces = [a * a + b * b for a, b in halves]
    if n == 1:
        pieces = [pieces[0], pieces[0]]
    pair_ms = [_dot(jnp.concatenate(pieces[2 * i:2 * i + 2], axis=1).astype(BF16), bd)
               for i in range(len(pieces) // 2)]
    ms = [pair_ms[i // 2][:, LANES * (i % 2):LANES * (i % 2 + 1)] for i in range(n)]
    out = []
    for (a, b), m in zip(halves, ms):
        scale = lax.rsqrt(m + EPS)
        out.append((a * scale) * w[:, :LANES])
        out.append((b * scale) * w[:, LANES:])
    return jnp.concatenate(out, axis=1)


def _rope_split(t, cos, sin):
    out = []
    for c in range(t.shape[1] // 256):
        a = t[:, 256 * c:256 * c + LANES]
        b = t[:, 256 * c + LANES:256 * (c + 1)]
        out.append(a * cos - b * sin)
        out.append(b * cos + a * sin)
    return jnp.concatenate(out, axis=1)


def _lane_group_masks(group_of_lane, dtype):
    return [(group_of_lane == g).astype(dtype) for g in range(N_KV_HEADS)]


def _softmax_parts(s, sk):
    m = jnp.maximum(jnp.max(s, axis=-1, keepdims=True), sk)
    p = jnp.exp(s - m)
    den = jnp.sum(p, axis=-1, keepdims=True) + jnp.exp(sk - m)
    return p, 1.0 / den


def _silu(z):
    return z * jax.nn.sigmoid(z)


def _rows_split_to_natural(t):
    return jnp.concatenate(
        [t[LANES * half + HALF * g:LANES * half + HALF * (g + 1)]
         for g in range(N_KV_HEADS) for half in range(2)], axis=0)


def _rows_natural_to_split(t):
    return jnp.concatenate(
        [t[HEAD_DIM * g + HALF * half:HEAD_DIM * g + HALF * (half + 1)]
         for half in range(2) for g in range(N_KV_HEADS)], axis=0)


def _prompt_scores(q, kb, vb, kprev, vprev, first_step):
    tm = q.shape[0]
    lane = lax.broadcasted_iota(jnp.int32, (1, 2 * BLOCK), 1)
    kmasks = _lane_group_masks((lane & (LANES - 1)) >> 5, BF16)
    vmasks = _lane_group_masks(lane >> 6, BF16)
    r_io = lax.broadcasted_iota(jnp.int32, (BLOCK, 2 * BLOCK), 0)
    c_io = lax.broadcasted_iota(jnp.int32, (BLOCK, 2 * BLOCK), 1)
    band = (c_io >= r_io) & (c_io <= r_io + WINDOW)
    first_band = band & (c_io >= jnp.where(first_step, LEAD, 0))

    blocks = []
    for b in range(tm // BLOCK):
        lo = b * BLOCK
        if b == 0:
            kpair = jnp.concatenate([kprev, kb[:BLOCK]], axis=0)
            vpair = jnp.concatenate([vprev, vb[:BLOCK]], axis=0)
            mask = first_band
        else:
            kpair = kb[lo - BLOCK:lo + BLOCK]
            vpair = vb[lo - BLOCK:lo + BLOCK]
            mask = band
        kbd = jnp.concatenate([kpair * km for km in kmasks], axis=0)
        vbd = jnp.concatenate([vpair * vm for vm in vmasks], axis=0)
        qs = jnp.concatenate([q[lo:lo + BLOCK, 256 * j:256 * (j + 1)] for j in range(GROUP)],
                             axis=0)
        blocks.append((_dot_nt(qs, kbd), vbd, mask))
    return blocks


def _prompt_outputs(blocks, sinks_ref):
    vgroup = lax.broadcasted_iota(jnp.int32, (1, 2 * BLOCK), 1) >> 6
    o_blocks = []
    for s, vbd, mask in blocks:
        p_rows = []
        mults = []
        for j in range(GROUP):
            p_cols = []
            invs = []
            for g in range(N_KV_HEADS):
                sl = s[BLOCK * j:BLOCK * (j + 1), 256 * g:256 * (g + 1)]
                sl = jnp.where(mask, sl, NEG_INF)
                p, inv = _softmax_parts(sl, sinks_ref[0, GROUP * g + j])
                p_cols.append(p.astype(BF16))
                invs.append(inv)
            p_rows.append(jnp.concatenate(p_cols, axis=1))
            mults.append(jnp.where(vgroup == 0, invs[0],
                                   jnp.where(vgroup == 1, invs[1],
                                             jnp.where(vgroup == 2, invs[2], invs[3]))))
        p_all = jnp.concatenate(p_rows, axis=0)
        half_k = 2 * 2 * BLOCK
        o = _dot(p_all[:, :half_k], vbd[:half_k]) + _dot(p_all[:, half_k:], vbd[half_k:])
        o_blocks.append(jnp.concatenate(
            [o[BLOCK * j:BLOCK * (j + 1)] * mults[j] for j in range(GROUP)], axis=1))
    return jnp.concatenate(o_blocks, axis=0)


def _decode_scores(qs, ks, vs, ck_ref, cv_ref, nck_ref, ncv_ref):
    t_len = DEC_SEQ
    n_exp = GROUP * N_KV_HEADS * t_len
    pad = jnp.zeros((LANES - qs.shape[0], KV_WIDTH), F32)
    kst = jnp.concatenate([ks, pad], axis=0).T
    vst = jnp.concatenate([vs, pad], axis=0).T
    kst_nat = _rows_split_to_natural(kst)
    kst_b = kst.astype(BF16)
    vst_b = vst.astype(BF16)

    lane = lax.broadcasted_iota(jnp.int32, (1, KV_WIDTH), 1)
    kgroup = (lane & (LANES - 1)) >> 5
    t_io = lax.broadcasted_iota(jnp.int32, (n_exp, WINDOW), 0) & (t_len - 1)
    c_io = lax.broadcasted_iota(jnp.int32, (n_exp, WINDOW), 1)
    old_ok = c_io >= t_io
    new_lane = lax.broadcasted_iota(jnp.int32, (KV_WIDTH, WINDOW), 1) >= WINDOW - t_len

    elems = []
    for b in range(qs.shape[0] // t_len):
        r0 = b * t_len
        qn = qs[r0:r0 + t_len]
        q_exp = jnp.concatenate(
            [jnp.where(kgroup == g, qn[:, 256 * j:256 * (j + 1)], 0.0)
             for j in range(GROUP) for g in range(N_KV_HEADS)], axis=0).astype(BF16)
        kt_old = ck_ref[b]
        vt_old = cv_ref[b]
        keys = jnp.concatenate([_rows_natural_to_split(kt_old).astype(BF16), kst_b], axis=1)
        vals = jnp.concatenate([vt_old.astype(BF16), vst_b], axis=1)
        s = _dot(q_exp, keys)
        new_ok = (c_io >= r0) & (c_io - t_io <= r0)
        s = jnp.where(jnp.concatenate([old_ok, new_ok], axis=1), s, NEG_INF)
        elems.append((s, vals))
        shift_new = WINDOW - t_len - r0
        nck_ref[b] = jnp.where(new_lane, pltpu.roll(kst_nat, shift_new, 1),
                               pltpu.roll(kt_old, WINDOW - t_len, 1))
        ncv_ref[b] = jnp.where(new_lane, pltpu.roll(vst, shift_new, 1),
                               pltpu.roll(vt_old, WINDOW - t_len, 1))
    return elems


def _decode_outputs(elems, sinks_ref):
    t_len = DEC_SEQ
    vgroup = lax.broadcasted_iota(jnp.int32, (1, KV_WIDTH), 1) >> 6
    skcol = jnp.concatenate(
        [jnp.full((t_len, 1), sinks_ref[0, GROUP * g + j], F32)
         for j in range(GROUP) for g in range(N_KV_HEADS)], axis=0)
    outs = []
    for s, vals in elems:
        p, inv = _softmax_parts(s, skcol)
        o = _dot_nt(p.astype(BF16), vals) * inv
        o_j = []
        for j in range(GROUP):
            acc = None
            for g in range(N_KV_HEADS):
                r = (j * N_KV_HEADS + g) * t_len
                piece = jnp.where(vgroup == g, o[r:r + t_len], 0.0)
                acc = piece if acc is None else acc + piece
            o_j.append(acc)
        outs.append(jnp.concatenate(o_j, axis=1))
    return jnp.concatenate(outs, axis=0)


def _gate_weight_copy(w_hbm, wstage, wsem, chunk):
    slot = chunk % GW_SLOTS
    return pltpu.make_async_copy(
        w_hbm.at[0, pl.ds(GW_CHUNK * chunk, GW_CHUNK), pl.ds(G_B, G_END - G_B)],
        wstage.at[slot], wsem.at[slot])


def _load_gate_weights(w_hbm, wg_s, wstage, wsem):
    n_chunks = D_MODEL // GW_CHUNK
    ahead = GW_SLOTS - 1
    for c in range(ahead):
        _gate_weight_copy(w_hbm, wstage, wsem, c).start()
    for c in range(n_chunks):
        if c + ahead < n_chunks:
            _gate_weight_copy(w_hbm, wstage, wsem, c + ahead).start()
        _gate_weight_copy(w_hbm, wstage, wsem, c).wait()
        wg_s[GW_CHUNK * c:GW_CHUNK * (c + 1), :] = wstage[c % GW_SLOTS].astype(BF16)


def _layer_kernel(sinks_ref, x_ref, xs_ref, cos_ref, sin_ref, ck_ref, cv_ref, st_ref, w_hbm,
                  meta_ref, cos0_ref, sin0_ref, coss_ref, sins_ref, nw_ref, qnw_ref, knw_ref, cw_ref,
                  wq_ref, wkv_ref, wza_ref, wpa_ref, wpb_ref, wo_ref, bd_ref,
                  y_ref, ys_ref, nkt_ref, nvt_ref, nc_ref, nck_ref, ncv_ref, ncs_ref,
                  kprev, vprev, utail, wg_s, wstage, wsem):
    i = pl.program_id(0)
    nw = nw_ref[...]
    bd = bd_ref[...]
    knw = knw_ref[...]

    @pl.when(i == 0)
    def _init():
        _load_gate_weights(w_hbm, wg_s, wstage, wsem)
        x0 = jnp.concatenate([jnp.zeros((LEAD, D_MODEL), F32), meta_ref[...]], axis=0)
        xn0 = _rms_rows(x0, nw).astype(BF16)
        kv0 = _dot(xn0, wkv_ref[...])
        k0 = _rope_split(_head_rms_split(kv0[:, :KV_WIDTH], bd, knw), cos0_ref[...], sin0_ref[...])
        v0 = kv0[:, KV_WIDTH:]
        c0 = _dot(xn0, wg_s[:, G_C - G_B:G_H - G_B])
        h0 = _dot(xn0, wg_s[:, G_H - G_B:G_ZB - G_B])
        row = lax.broadcasted_iota(jnp.int32, (BLOCK, D_MODEL), 0)
        u0 = jnp.where(row >= LEAD, c0 * h0, 0.0)
        kprev[...] = k0
        vprev[...] = v0
        utail[...] = u0[BLOCK - SUBLANES:BLOCK]

    cw = cw_ref[0]
    row8 = lax.broadcasted_iota(jnp.int32, (SUBLANES, D_MODEL), 0)
    t_io = lax.broadcasted_iota(jnp.int32, (DEC_PER_STEP, DEC_SEQ, D_MODEL), 1)
    k_prev = kprev[...].astype(BF16)
    v_prev = vprev[...].astype(BF16)
    tail = utail[...]
    for h in range(SUB_TILES):
        ps = slice(TM_SUB * h, TM_SUB * (h + 1))
        with_dec = h == SUB_TILES - 1
        if with_dec:
            x = jnp.concatenate([x_ref[ps, :], xs_ref[...]], axis=0)
            cos = jnp.concatenate([cos_ref[ps, :], coss_ref[...]], axis=0)
            sin = jnp.concatenate([sin_ref[ps, :], sins_ref[...]], axis=0)
        else:
            x = x_ref[ps, :]
            cos = cos_ref[ps, :]
            sin = sin_ref[ps, :]
        xn = _rms_rows(x, nw).astype(BF16)

        q = _head_rms_split(_dot(xn, wq_ref[...]), bd, qnw_ref[...])
        q = _rope_split(q, cos * Q_SCALE, sin * Q_SCALE)
        kv = _dot(xn, wkv_ref[...])
        k = _rope_split(_head_rms_split(kv[:, :KV_WIDTH], bd, knw), cos, sin)
        v = kv[:, KV_WIDTH:]
        kb = k[:TM_SUB].astype(BF16)
        vb = v[:TM_SUB].astype(BF16)

        first = (i == 0) if h == 0 else False
        blocks = _prompt_scores(q[:TM_SUB].astype(BF16), kb, vb, k_prev, v_prev, first)
        if with_dec:
            elems = _decode_scores(q[TM_SUB:], k[TM_SUB:], v[TM_SUB:], ck_ref, cv_ref, nck_ref,
                                   ncv_ref)
        c, hc, b, z_b = [_dot(xn, wg_s[:, lo - G_B:lo - G_B + D_MODEL])
                         for lo in (G_C, G_H, G_B, G_ZB)]
        o_a = _prompt_outputs(blocks, sinks_ref)
        if with_dec:
            o_a = jnp.concatenate([o_a, _decode_outputs(elems, sinks_ref)], axis=0)
        k_prev = kb[TM_SUB - BLOCK:]
        v_prev = vb[TM_SUB - BLOCK:]
        k_last = k[TM_SUB - BLOCK:TM_SUB]
        v_last = v[TM_SUB - BLOCK:TM_SUB]

        u_all = c * hc
        u = u_all[:TM_SUB]
        prev1 = tail[SUBLANES - 1:SUBLANES]
        prev2 = tail[SUBLANES - 2:SUBLANES - 1]
        r1 = pltpu.roll(u, 1, 0)
        r2 = pltpu.roll(u, 2, 0)
        um1 = jnp.concatenate(
            [jnp.where(row8 == 0, prev1, r1[:SUBLANES]), r1[SUBLANES:]], axis=0)
        um2 = jnp.concatenate(
            [jnp.where(row8 == 0, prev2, jnp.where(row8 == 1, prev1, r2[:SUBLANES])),
             r2[SUBLANES:]], axis=0)
        conv = cw[0:1] * um2 + cw[1:2] * um1 + cw[2:3] * u
        tail = u[TM_SUB - SUBLANES:]
        if with_dec:
            u3 = u_all[TM_SUB:].reshape(DEC_PER_STEP, DEC_SEQ, D_MODEL)
            st = st_ref[...]
            s_m2 = st[:, 0:1, :]
            s_m1 = st[:, 1:2, :]
            us1 = jnp.where(t_io == 0, s_m1, pltpu.roll(u3, 1, 1))
            us2 = jnp.where(t_io == 0, s_m2, jnp.where(t_io == 1, s_m1, pltpu.roll(u3, 2, 1)))
            conv_s = (cw[0:1] * us2 + cw[1:2] * us1 + cw[2:3] * u3).reshape(TS, D_MODEL)
            ncs_ref[...] = u3[:, DEC_SEQ - (CONV_K - 1):, :]
            conv = jnp.concatenate([conv, conv_s], axis=0)

        z_a = _dot(xn, wza_ref[...])
        br_a = _dot((o_a * _silu(z_a)).astype(BF16), wpa_ref[...])
        br_b = _dot(((b * conv) * _silu(z_b)).astype(BF16), wpb_ref[...])
        g_a = _dot(xn, wg_s[:, G_GA - G_B:G_GB - G_B])
        g_b = _dot(xn, wg_s[:, G_GB - G_B:G_END - G_B])
        mixed = (jax.nn.sigmoid(g_a) * br_a + jax.nn.sigmoid(g_b) * br_b).astype(BF16)
        y = x + _dot(mixed, wo_ref[...])
        y_ref[ps, :] = y[:TM_SUB]
        if with_dec:
            ys_ref[...] = y[TM_SUB:]

    kprev[...] = k_last
    vprev[...] = v_last
    utail[...] = tail
    nkt_ref[...] = _rows_split_to_natural(k_last.T)
    nvt_ref[...] = v_last.T
    nc_ref[...] = tail[SUBLANES - (CONV_K - 1):]


def _const_spec(shape):
    zeros = (0,) * len(shape)
    return pl.BlockSpec(shape, lambda i: zeros, pipeline_mode=pl.Buffered(1))


def _layer_call(x, xs, tabs, ck, cv, st, w_in, meta, tabs0, tabs_s, nw, qnw, knw, cw, sinks,
                wq, wk, wza, wpa, wpb, wo, bd):
    tm = TM_PROMPT
    assert N_STEPS * DEC_PER_STEP == DEC_BATCH and TS % (2 * SUBLANES) == 0 and TS <= LANES
    assert TM_SUB % BLOCK == 0
    row_spec = lambda rows, w: pl.BlockSpec((rows, w), lambda i: (i, 0))
    cache_spec = pl.BlockSpec((DEC_PER_STEP, KV_WIDTH, WINDOW), lambda i: (i, 0, 0))
    st_spec = pl.BlockSpec((DEC_PER_STEP, CONV_K - 1, D_MODEL), lambda i: (i, 0, 0))
    consts = (meta, tabs0[0], tabs0[1], tabs_s[0], tabs_s[1], nw, qnw, knw, cw,
              wq, wk, wza, wpa, wpb, wo, bd)
    in_specs = [pl.BlockSpec(memory_space=pltpu.SMEM),
                row_spec(tm, D_MODEL), row_spec(TS, D_MODEL), row_spec(tm, LANES), row_spec(tm, LANES),
                cache_spec, cache_spec, st_spec, pl.BlockSpec(memory_space=pl.ANY)]
    in_specs += [_const_spec(a.shape) for a in consts]
    resident = lambda shape: pl.BlockSpec(shape, lambda i: (0, 0))
    out_specs = [row_spec(tm, D_MODEL), row_spec(TS, D_MODEL),
                 resident((KV_WIDTH, BLOCK)), resident((KV_WIDTH, BLOCK)),
                 resident((CONV_K - 1, D_MODEL)), cache_spec, cache_spec, st_spec]
    cache_sds = jax.ShapeDtypeStruct((DEC_BATCH, KV_WIDTH, WINDOW), F32)
    out_shape = [jax.ShapeDtypeStruct((SEQ, D_MODEL), F32),
                 jax.ShapeDtypeStruct((DEC_BATCH * DEC_SEQ, D_MODEL), F32),
                 jax.ShapeDtypeStruct((KV_WIDTH, BLOCK), F32),
                 jax.ShapeDtypeStruct((KV_WIDTH, BLOCK), F32),
                 jax.ShapeDtypeStruct((CONV_K - 1, D_MODEL), F32), cache_sds, cache_sds,
                 jax.ShapeDtypeStruct((DEC_BATCH, CONV_K - 1, D_MODEL), F32)]
    return pl.pallas_call(
        _layer_kernel,
        grid=(N_STEPS,),
        in_specs=in_specs,
        out_specs=out_specs,
        out_shape=out_shape,
        scratch_shapes=[pltpu.VMEM((BLOCK, KV_WIDTH), F32),
                        pltpu.VMEM((BLOCK, KV_WIDTH), F32),
                        pltpu.VMEM((SUBLANES, D_MODEL), F32),
                        pltpu.VMEM((D_MODEL, G_END - G_B), BF16),
                        pltpu.VMEM((GW_SLOTS, GW_CHUNK, G_END - G_B), F32),
                        pltpu.SemaphoreType.DMA((GW_SLOTS,))],
        compiler_params=pltpu.CompilerParams(
            dimension_semantics=("arbitrary",), vmem_limit_bytes=VMEM_LIMIT),
        name="hybrid_layer",
    )(sinks, x, xs, tabs[0], tabs[1], ck, cv, st, w_in, *consts)


def _rope_tables(positions):
    inv = np.power(ROPE_THETA, -np.arange(HALF, dtype=np.float64) * (2.0 / HEAD_DIM))
    ang = np.asarray(positions, dtype=np.float64)[:, None] * inv[None, :]
    cos = np.tile(np.cos(ang), (1, LANES // HALF))
    sin = np.tile(np.sin(ang), (1, LANES // HALF))
    return jnp.asarray(cos, F32), jnp.asarray(sin, F32)


def _head_mean_matrix():
    grp = np.arange(256) // HALF
    return jnp.asarray((grp[:, None] == grp[None, :]) / float(HEAD_DIM), BF16)


def kernel(x_prompt, x_sample, cache_k, cache_v, state_conv, meta_tokens, norm_w, w_in,
           q_norm_w, k_norm_w, sinks, conv_w, w_proj_a, w_proj_b, w_out):
    assert x_prompt.shape == (1, SEQ, D_MODEL) and x_sample.shape == (DEC_BATCH, DEC_SEQ, D_MODEL)
    assert w_in.shape[0] == 1, "single layer"
    wg = w_in[0, :, :G_B].astype(BF16)
    wq = (wg[:, :W_K].reshape(D_MODEL, N_KV_HEADS, GROUP, 2, HALF)
          .transpose(0, 2, 3, 1, 4).reshape(D_MODEL, D_MODEL))
    wk = (wg[:, W_K:W_V].reshape(D_MODEL, N_KV_HEADS, 2, HALF)
          .transpose(0, 2, 1, 3).reshape(D_MODEL, KV_WIDTH))
    wk = jnp.concatenate([wk, wg[:, W_V:W_ZA]], axis=1)
    wza = (wg[:, W_ZA:G_B].reshape(D_MODEL, N_KV_HEADS, GROUP, HEAD_DIM)
           .transpose(0, 2, 1, 3).reshape(D_MODEL, D_MODEL))
    wpa = (w_proj_a[0].astype(BF16).reshape(N_KV_HEADS, GROUP, HEAD_DIM, D_MODEL)
           .transpose(1, 0, 2, 3).reshape(D_MODEL, D_MODEL))
    wpb = w_proj_b[0].astype(BF16)
    wo = w_out[0].astype(BF16)

    nw = norm_w[0].reshape(1, D_MODEL)
    split = lambda v: jnp.broadcast_to(v.reshape(2, 1, HALF), (2, N_KV_HEADS, HALF)).reshape(1, 256)

    to_tiles = lambda c: c[0].transpose(0, 2, 3, 1).reshape(DEC_BATCH, KV_WIDTH, WINDOW)
    from_tiles = lambda c: (c.reshape(DEC_BATCH, N_KV_HEADS, HEAD_DIM, WINDOW)
                            .transpose(0, 3, 1, 2)[None])
    from_tile = lambda t: t.reshape(N_KV_HEADS, HEAD_DIM, WINDOW).transpose(2, 0, 1)

    tabs_p = _rope_tables(np.arange(SEQ) + N_META)
    tabs_0 = _rope_tables(np.arange(BLOCK) - LEAD)
    tabs_s = _rope_tables(PAST_LEN + (np.arange(TS) % DEC_SEQ))
    y_p, y_s, nkt_p, nvt_p, nc_p, nk_s, nv_s, nc_s = _layer_call(
        x_prompt[0], x_sample.reshape(DEC_BATCH * DEC_SEQ, D_MODEL), tabs_p,
        to_tiles(cache_k), to_tiles(cache_v), state_conv[0], w_in,
        meta_tokens.astype(x_prompt.dtype), tabs_0, tabs_s,
        nw, split(q_norm_w), split(k_norm_w), conv_w, sinks,
        wq, wk, wza, wpa, wpb, wo, _head_mean_matrix())

    return (y_p.reshape(1, SEQ, D_MODEL),
            y_s.reshape(DEC_BATCH, DEC_SEQ, D_MODEL),
            from_tile(nkt_p).reshape(1, 1, WINDOW, N_KV_HEADS, HEAD_DIM),
            from_tile(nvt_p).reshape(1, 1, WINDOW, N_KV_HEADS, HEAD_DIM),
            nc_p.reshape(1, 1, CONV_K - 1, D_MODEL),
            from_tiles(nk_s),
            from_tiles(nv_s),
            nc_s.reshape(1, DEC_BATCH, CONV_K - 1, D_MODEL))
```

```python
import functools

import numpy as np
import jax
import jax.numpy as jnp
from jax import lax
from jax.experimental import pallas as pl
from jax.experimental.pallas import tpu as pltpu

D_MODEL = 1024
SEQ = 16384
DEC_BATCH = 128
DEC_SEQ = 8
PAST_LEN = 16384
N_HEADS = 16
N_KV_HEADS = 4
GROUP = N_HEADS // N_KV_HEADS
HEAD_DIM = 64
HALF = HEAD_DIM // 2
KV_WIDTH = N_KV_HEADS * HEAD_DIM
WINDOW = 128
BLOCK = 128
ROPE_THETA = 10000.0
CONV_K = 3
N_META = 16
LEAD = BLOCK - N_META
EPS = 1e-6
NEG_INF = -1e30
Q_SCALE = HEAD_DIM ** -0.5

W_K = D_MODEL
W_V = W_K + KV_WIDTH
W_ZA = W_V + KV_WIDTH
G_B = W_ZA + D_MODEL
G_C = G_B + D_MODEL
G_H = G_C + D_MODEL
G_ZB = G_H + D_MODEL
G_GA = G_ZB + D_MODEL
G_GB = G_GA + D_MODEL
G_END = G_GB + D_MODEL

LANES = 128
SUBLANES = 8
VMEM_BYTES_V7X = 64 * 1024 * 1024

TM_PROMPT = 512
N_STEPS = SEQ // TM_PROMPT
DEC_PER_STEP = DEC_BATCH // N_STEPS
TS = DEC_PER_STEP * DEC_SEQ
GW_CHUNK = 64
GW_SLOTS = 4
SUB_TILES = 2
TM_SUB = TM_PROMPT // SUB_TILES
VMEM_LIMIT = VMEM_BYTES_V7X - 4 * 1024 * 1024

BF16 = jnp.bfloat16
F32 = jnp.float32


def _dot(a, b):
    return jnp.dot(a, b, preferred_element_type=F32)


def _dot_nt(a, b):
    return lax.dot_general(a, b, (((1,), (1,)), ((), ())), preferred_element_type=F32)


def _rms_rows(x, w):
    ms = jnp.mean(x * x, axis=-1, keepdims=True)
    return (x * lax.rsqrt(ms + EPS)) * w


def _head_rms_split(t, bd, w):
    n = t.shape[1] // 256
    r = t.shape[0]
    halves = [(t[:, 256 * c:256 * c + LANES], t[:, 256 * c + LANES:256 * (c + 1)]) for c in range(n)]
    pieces = [a * a + b * b for a, b in halves]
    if n == 1:
        pieces = [pieces[0], pieces[0]]
    pair_ms = [_dot(jnp.concatenate(pieces[2 * i:2 * i + 2], axis=1).astype(BF16), bd)
               for i in range(len(pieces) // 2)]
    ms = [pair_ms[i // 2][:, LANES * (i % 2):LANES * (i % 2 + 1)] for i in range(n)]
    out = []
    for (a, b), m in zip(halves, ms):
        scale = lax.rsqrt(m + EPS)
        out.append((a * scale) * w[:, :LANES])
        out.append((b * scale) * w[:, LANES:])
    return jnp.concatenate(out, axis=1)


def _rope_split(t, cos, sin):
    out = []
    for c in range(t.shape[1] // 256):
        a = t[:, 256 * c:256 * c + LANES]
        b = t[:, 256 * c + LANES:256 * (c + 1)]
        out.append(a * cos - b * sin)
        out.append(b * cos + a * sin)
    return jnp.concatenate(out, axis=1)


def _lane_group_masks(group_of_lane, dtype):
    return [(group_of_lane == g).astype(dtype) for g in range(N_KV_HEADS)]


def _softmax_parts(s, sk):
    m = jnp.maximum(jnp.max(s, axis=-1, keepdims=True), sk)
    p = jnp.exp(s - m)
    den = jnp.sum(p, axis=-1, keepdims=True) + jnp.exp(sk - m)
    return p, 1.0 / den


def _silu(z):
    return z * jax.nn.sigmoid(z)


def _rows_split_to_natural(t):
    return jnp.concatenate(
        [t[LANES * half + HALF * g:LANES * half + HALF * (g + 1)]
         for g in range(N_KV_HEADS) for half in range(2)], axis=0)


def _rows_natural_to_split(t):
    return jnp.concatenate(
        [t[HEAD_DIM * g + HALF * half:HEAD_DIM * g + HALF * (half + 1)]
         for half in range(2) for g in range(N_KV_HEADS)], axis=0)


def _prompt_attention(q, kb, vb, kprev, vprev, sinks_ref, first_step):
    tm = q.shape[0]
    lane = lax.broadcasted_iota(jnp.int32, (1, 2 * BLOCK), 1)
    kmasks = _lane_group_masks((lane & (LANES - 1)) >> 5, BF16)
    vmasks = _lane_group_masks(lane >> 6, BF16)
    vgroup = lane >> 6
    r_io = lax.broadcasted_iota(jnp.int32, (BLOCK, 2 * BLOCK), 0)
    c_io = lax.broadcasted_iota(jnp.int32, (BLOCK, 2 * BLOCK), 1)
    band = (c_io >= r_io) & (c_io <= r_io + WINDOW)
    first_band = band & (c_io >= jnp.where(first_step, LEAD, 0))

    o_blocks = []
    for b in range(tm // BLOCK):
        lo = b * BLOCK
        if b == 0:
            kpair = jnp.concatenate([kprev, kb[:BLOCK]], axis=0)
            vpair = jnp.concatenate([vprev, vb[:BLOCK]], axis=0)
            mask = first_band
        else:
            kpair = kb[lo - BLOCK:lo + BLOCK]
            vpair = vb[lo - BLOCK:lo + BLOCK]
            mask = band
        kbd = jnp.concatenate([kpair * km for km in kmasks], axis=0)
        vbd = jnp.concatenate([vpair * vm for vm in vmasks], axis=0)
        qs = jnp.concatenate([q[lo:lo + BLOCK, 256 * j:256 * (j + 1)] for j in range(GROUP)],
                             axis=0)
        s = _dot_nt(qs, kbd)
        p_rows = []
        mults = []
        for j in range(GROUP):
            p_cols = []
            invs = []
            for g in range(N_KV_HEADS):
                sl = s[BLOCK * j:BLOCK * (j + 1), 256 * g:256 * (g + 1)]
                sl = jnp.where(mask, sl, NEG_INF)
                p, inv = _softmax_parts(sl, sinks_ref[0, GROUP * g + j])
                p_cols.append(p.astype(BF16))
                invs.append(inv)
            p_rows.append(jnp.concatenate(p_cols, axis=1))
            mults.append(jnp.where(vgroup == 0, invs[0],
                                   jnp.where(vgroup == 1, invs[1],
                                             jnp.where(vgroup == 2, invs[2], invs[3]))))
        p_all = jnp.concatenate(p_rows, axis=0)
        half_k = 2 * 2 * BLOCK
        o = _dot(p_all[:, :half_k], vbd[:half_k]) + _dot(p_all[:, half_k:], vbd[half_k:])
        o_blocks.append(jnp.concatenate(
            [o[BLOCK * j:BLOCK * (j + 1)] * mults[j] for j in range(GROUP)], axis=1))
    return jnp.concatenate(o_blocks, axis=0)


def _decode_scores(qs, ks, vs, ck_ref, cv_ref, nck_ref, ncv_ref):
    t_len = DEC_SEQ
    n_exp = GROUP * N_KV_HEADS * t_len
    pad = jnp.zeros((LANES - qs.shape[0], KV_WIDTH), F32)
    kst = jnp.concatenate([ks, pad], axis=0).T
    vst = jnp.concatenate([vs, pad], axis=0).T
    kst_nat = _rows_split_to_natural(kst)
    kst_b = kst.astype(BF16)
    vst_b = vst.astype(BF16)

    lane = lax.broadcasted_iota(jnp.int32, (1, KV_WIDTH), 1)
    kgroup = (lane & (LANES - 1)) >> 5
    t_io = lax.broadcasted_iota(jnp.int32, (n_exp, WINDOW), 0) & (t_len - 1)
    c_io = lax.broadcasted_iota(jnp.int32, (n_exp, WINDOW), 1)
    old_ok = c_io >= t_io
    new_lane = lax.broadcasted_iota(jnp.int32, (KV_WIDTH, WINDOW), 1) >= WINDOW - t_len

    elems = []
    for b in range(qs.shape[0] // t_len):
        r0 = b * t_len
        qn = qs[r0:r0 + t_len]
        q_exp = jnp.concatenate(
            [jnp.where(kgroup == g, qn[:, 256 * j:256 * (j + 1)], 0.0)
             for j in range(GROUP) for g in range(N_KV_HEADS)], axis=0).astype(BF16)
        kt_old = ck_ref[b]
        vt_old = cv_ref[b]
        keys = jnp.concatenate([_rows_natural_to_split(kt_old).astype(BF16), kst_b], axis=1)
        vals = jnp.concatenate([vt_old.astype(BF16), vst_b], axis=1)
        s = _dot(q_exp, keys)
        new_ok = (c_io >= r0) & (c_io - t_io <= r0)
        s = jnp.where(jnp.concatenate([old_ok, new_ok], axis=1), s, NEG_INF)
        elems.append((s, vals))
        shift_new = WINDOW - t_len - r0
        nck_ref[b] = jnp.where(new_lane, pltpu.roll(kst_nat, shift_new, 1),
                               pltpu.roll(kt_old, WINDOW - t_len, 1))
        ncv_ref[b] = jnp.where(new_lane, pltpu.roll(vst, shift_new, 1),
                               pltpu.roll(vt_old, WINDOW - t_len, 1))
    return elems


def _decode_outputs(elems, sinks_ref):
    t_len = DEC_SEQ
    vgroup = lax.broadcasted_iota(jnp.int32, (1, KV_WIDTH), 1) >> 6
    skcol = jnp.concatenate(
        [jnp.full((t_len, 1), sinks_ref[0, GROUP * g + j], F32)
         for j in range(GROUP) for g in range(N_KV_HEADS)], axis=0)
    outs = []
    for s, vals in elems:
        p, inv = _softmax_parts(s, skcol)
        o = _dot_nt(p.astype(BF16), vals) * inv
        o_j = []
        for j in range(GROUP):
            acc = None
            for g in range(N_KV_HEADS):
                r = (j * N_KV_HEADS + g) * t_len
                piece = jnp.where(vgroup == g, o[r:r + t_len], 0.0)
                acc = piece if acc is None else acc + piece
            o_j.append(acc)
        outs.append(jnp.concatenate(o_j, axis=1))
    return jnp.concatenate(outs, axis=0)


def _gate_weight_copy(w_hbm, wstage, wsem, chunk):
    slot = chunk % GW_SLOTS
    return pltpu.make_async_copy(
        w_hbm.at[0, pl.ds(GW_CHUNK * chunk, GW_CHUNK), pl.ds(G_B, G_END - G_B)],
        wstage.at[slot], wsem.at[slot])


def _load_gate_weights(w_hbm, wg_s, wstage, wsem):
    n_chunks = D_MODEL // GW_CHUNK
    ahead = GW_SLOTS - 1
    for c in range(ahead):
        _gate_weight_copy(w_hbm, wstage, wsem, c).start()
    for c in range(n_chunks):
        if c + ahead < n_chunks:
            _gate_weight_copy(w_hbm, wstage, wsem, c + ahead).start()
        _gate_weight_copy(w_hbm, wstage, wsem, c).wait()
        wg_s[GW_CHUNK * c:GW_CHUNK * (c + 1), :] = wstage[c % GW_SLOTS].astype(BF16)


def _layer_kernel(sinks_ref, x_ref, xs_ref, cos_ref, sin_ref, ck_ref, cv_ref, st_ref, w_hbm,
                  meta_ref, cos0_ref, sin0_ref, coss_ref, sins_ref, nw_ref, qnw_ref, knw_ref, cw_ref,
                  wq_ref, wkv_ref, wza_ref, wpa_ref, wpb_ref, wo_ref, bd_ref,
                  y_ref, ys_ref, nkt_ref, nvt_ref, nc_ref, nck_ref, ncv_ref, ncs_ref,
                  kprev, vprev, utail, wg_s, wstage, wsem):
    i = pl.program_id(0)
    nw = nw_ref[...]
    bd = bd_ref[...]
    knw = knw_ref[...]

    @pl.when(i == 0)
    def _init():
        _load_gate_weights(w_hbm, wg_s, wstage, wsem)
        x0 = jnp.concatenate([jnp.zeros((LEAD, D_MODEL), F32), meta_ref[...]], axis=0)
        xn0 = _rms_rows(x0, nw).astype(BF16)
        kv0 = _dot(xn0, wkv_ref[...])
        k0 = _rope_split(_head_rms_split(kv0[:, :KV_WIDTH], bd, knw), cos0_ref[...], sin0_ref[...])
        v0 = kv0[:, KV_WIDTH:]
        c0 = _dot(xn0, wg_s[:, G_C - G_B:G_H - G_B])
        h0 = _dot(xn0, wg_s[:, G_H - G_B:G_ZB - G_B])
        row = lax.broadcasted_iota(jnp.int32, (BLOCK, D_MODEL), 0)
        u0 = jnp.where(row >= LEAD, c0 * h0, 0.0)
        kprev[...] = k0
        vprev[...] = v0
        utail[...] = u0[BLOCK - SUBLANES:BLOCK]

    cw = cw_ref[0]
    row8 = lax.broadcasted_iota(jnp.int32, (SUBLANES, D_MODEL), 0)
    t_io = lax.broadcasted_iota(jnp.int32, (DEC_PER_STEP, DEC_SEQ, D_MODEL), 1)
    k_prev = kprev[...].astype(BF16)
    v_prev = vprev[...].astype(BF16)
    tail = utail[...]
    for h in range(SUB_TILES):
        ps = slice(TM_SUB * h, TM_SUB * (h + 1))
        with_dec = h == SUB_TILES - 1
        if with_dec:
            x = jnp.concatenate([x_ref[ps, :], xs_ref[...]], axis=0)
            cos = jnp.concatenate([cos_ref[ps, :], coss_ref[...]], axis=0)
            sin = jnp.concatenate([sin_ref[ps, :], sins_ref[...]], axis=0)
        else:
            x = x_ref[ps, :]
            cos = cos_ref[ps, :]
            sin = sin_ref[ps, :]
        xn = _rms_rows(x, nw).astype(BF16)

        q = _head_rms_split(_dot(xn, wq_ref[...]), bd, qnw_ref[...])
        q = _rope_split(q, cos * Q_SCALE, sin * Q_SCALE)
        kv = _dot(xn, wkv_ref[...])
        k = _rope_split(_head_rms_split(kv[:, :KV_WIDTH], bd, knw), cos, sin)
        v = kv[:, KV_WIDTH:]
        kb = k[:TM_SUB].astype(BF16)
        vb = v[:TM_SUB].astype(BF16)

        first = (i == 0) if h == 0 else False
        o_a = _prompt_attention(q[:TM_SUB].astype(BF16), kb, vb, k_prev, v_prev, sinks_ref, first)
        if with_dec:
            elems = _decode_scores(q[TM_SUB:], k[TM_SUB:], v[TM_SUB:], ck_ref, cv_ref, nck_ref,
                                   ncv_ref)
        c, hc, b, z_b = [_dot(xn, wg_s[:, lo - G_B:lo - G_B + D_MODEL])
                         for lo in (G_C, G_H, G_B, G_ZB)]
        if with_dec:
            o_a = jnp.concatenate([o_a, _decode_outputs(elems, sinks_ref)], axis=0)
        k_prev = kb[TM_SUB - BLOCK:]
        v_prev = vb[TM_SUB - BLOCK:]
        k_last = k[TM_SUB - BLOCK:TM_SUB]
        v_last = v[TM_SUB - BLOCK:TM_SUB]

        u_all = c * hc
        u = u_all[:TM_SUB]
        prev1 = tail[SUBLANES - 1:SUBLANES]
        prev2 = tail[SUBLANES - 2:SUBLANES - 1]
        r1 = pltpu.roll(u, 1, 0)
        r2 = pltpu.roll(u, 2, 0)
        um1 = jnp.concatenate(
            [jnp.where(row8 == 0, prev1, r1[:SUBLANES]), r1[SUBLANES:]], axis=0)
        um2 = jnp.concatenate(
            [jnp.where(row8 == 0, prev2, jnp.where(row8 == 1, prev1, r2[:SUBLANES])),
             r2[SUBLANES:]], axis=0)
        conv = cw[0:1] * um2 + cw[1:2] * um1 + cw[2:3] * u
        tail = u[TM_SUB - SUBLANES:]
        if with_dec:
            u3 = u_all[TM_SUB:].reshape(DEC_PER_STEP, DEC_SEQ, D_MODEL)
            st = st_ref[...]
            s_m2 = st[:, 0:1, :]
            s_m1 = st[:, 1:2, :]
            us1 = jnp.where(t_io == 0, s_m1, pltpu.roll(u3, 1, 1))
            us2 = jnp.where(t_io == 0, s_m2, jnp.where(t_io == 1, s_m1, pltpu.roll(u3, 2, 1)))
            conv_s = (cw[0:1] * us2 + cw[1:2] * us1 + cw[2:3] * u3).reshape(TS, D_MODEL)
            ncs_ref[...] = u3[:, DEC_SEQ - (CONV_K - 1):, :]
            conv = jnp.concatenate([conv, conv_s], axis=0)

        z_a = _dot(xn, wza_ref[...])
        br_a = _dot((o_a * _silu(z_a)).astype(BF16), wpa_ref[...])
        br_b = _dot(((b * conv) * _silu(z_b)).astype(BF16), wpb_ref[...])
        g_a = _dot(xn, wg_s[:, G_GA - G_B:G_GB - G_B])
        g_b = _dot(xn, wg_s[:, G_GB - G_B:G_END - G_B])
        mixed = (jax.nn.sigmoid(g_a) * br_a + jax.nn.sigmoid(g_b) * br_b).astype(BF16)
        y = x + _dot(mixed, wo_ref[...])
        y_ref[ps, :] = y[:TM_SUB]
        if with_dec:
            ys_ref[...] = y[TM_SUB:]

    kprev[...] = k_last
    vprev[...] = v_last
    utail[...] = tail
    nkt_ref[...] = _rows_split_to_natural(k_last.T)
    nvt_ref[...] = v_last.T
    nc_ref[...] = tail[SUBLANES - (CONV_K - 1):]


def _const_spec(shape):
    zeros = (0,) * len(shape)
    return pl.BlockSpec(shape, lambda i: zeros, pipeline_mode=pl.Buffered(1))


def _layer_call(x, xs, tabs, ck, cv, st, w_in, meta, tabs0, tabs_s, nw, qnw, knw, cw, sinks,
                wq, wk, wza, wpa, wpb, wo, bd):
    tm = TM_PROMPT
    assert N_STEPS * DEC_PER_STEP == DEC_BATCH and TS % (2 * SUBLANES) == 0 and TS <= LANES
    assert TM_SUB % BLOCK == 0
    row_spec = lambda rows, w: pl.BlockSpec((rows, w), lambda i: (i, 0))
    cache_spec = pl.BlockSpec((DEC_PER_STEP, KV_WIDTH, WINDOW), lambda i: (i, 0, 0))
    st_spec = pl.BlockSpec((DEC_PER_STEP, CONV_K - 1, D_MODEL), lambda i: (i, 0, 0))
    consts = (meta, tabs0[0], tabs0[1], tabs_s[0], tabs_s[1], nw, qnw, knw, cw,
              wq, wk, wza, wpa, wpb, wo, bd)
    in_specs = [pl.BlockSpec(memory_space=pltpu.SMEM),
                row_spec(tm, D_MODEL), row_spec(TS, D_MODEL), row_spec(tm, LANES), row_spec(tm, LANES),
                cache_spec, cache_spec, st_spec, pl.BlockSpec(memory_space=pl.ANY)]
    in_specs += [_const_spec(a.shape) for a in consts]
    resident = lambda shape: pl.BlockSpec(shape, lambda i: (0, 0))
    out_specs = [row_spec(tm, D_MODEL), row_spec(TS, D_MODEL),
                 resident((KV_WIDTH, BLOCK)), resident((KV_WIDTH, BLOCK)),
                 resident((CONV_K - 1, D_MODEL)), cache_spec, cache_spec, st_spec]
    cache_sds = jax.ShapeDtypeStruct((DEC_BATCH, KV_WIDTH, WINDOW), F32)
    out_shape = [jax.ShapeDtypeStruct((SEQ, D_MODEL), F32),
                 jax.ShapeDtypeStruct((DEC_BATCH * DEC_SEQ, D_MODEL), F32),
                 jax.ShapeDtypeStruct((KV_WIDTH, BLOCK), F32),
                 jax.ShapeDtypeStruct((KV_WIDTH, BLOCK), F32),
                 jax.ShapeDtypeStruct((CONV_K - 1, D_MODEL), F32), cache_sds, cache_sds,
                 jax.ShapeDtypeStruct((DEC_BATCH, CONV_K - 1, D_MODEL), F32)]
    return pl.pallas_call(
        _layer_kernel,
        grid=(N_STEPS,),
        in_specs=in_specs,
        out_specs=out_specs,
        out_shape=out_shape,
        scratch_shapes=[pltpu.VMEM((BLOCK, KV_WIDTH), F32),
                        pltpu.VMEM((BLOCK, KV_WIDTH), F32),
                        pltpu.VMEM((SUBLANES, D_MODEL), F32),
                        pltpu.VMEM((D_MODEL, G_END - G_B), BF16),
                        pltpu.VMEM((GW_SLOTS, GW_CHUNK, G_END - G_B), F32),
                        pltpu.SemaphoreType.DMA((GW_SLOTS,))],
        compiler_params=pltpu.CompilerParams(
            dimension_semantics=("arbitrary",), vmem_limit_bytes=VMEM_LIMIT),
        name="hybrid_layer",
    )(sinks, x, xs, tabs[0], tabs[1], ck, cv, st, w_in, *consts)


def _rope_tables(positions):
    inv = np.power(ROPE_THETA, -np.arange(HALF, dtype=np.float64) * (2.0 / HEAD_DIM))
    ang = np.asarray(positions, dtype=np.float64)[:, None] * inv[None, :]
    cos = np.tile(np.cos(ang), (1, LANES // HALF))
    sin = np.tile(np.sin(ang), (1, LANES // HALF))
    return jnp.asarray(cos, F32), jnp.asarray(sin, F32)


def _head_mean_matrix():
    grp = np.arange(256) // HALF
    return jnp.asarray((grp[:, None] == grp[None, :]) / float(HEAD_DIM), BF16)


def kernel(x_prompt, x_sample, cache_k, cache_v, state_conv, meta_tokens, norm_w, w_in,
           q_norm_w, k_norm_w, sinks, conv_w, w_proj_a, w_proj_b, w_out):
    assert x_prompt.shape == (1, SEQ, D_MODEL) and x_sample.shape == (DEC_BATCH, DEC_SEQ, D_MODEL)
    assert w_in.shape[0] == 1, "single layer"
    wg = w_in[0, :, :G_B].astype(BF16)
    wq = (wg[:, :W_K].reshape(D_MODEL, N_KV_HEADS, GROUP, 2, HALF)
          .transpose(0, 2, 3, 1, 4).reshape(D_MODEL, D_MODEL))
    wk = (wg[:, W_K:W_V].reshape(D_MODEL, N_KV_HEADS, 2, HALF)
          .transpose(0, 2, 1, 3).reshape(D_MODEL, KV_WIDTH))
    wk = jnp.concatenate([wk, wg[:, W_V:W_ZA]], axis=1)
    wza = (wg[:, W_ZA:G_B].reshape(D_MODEL, N_KV_HEADS, GROUP, HEAD_DIM)
           .transpose(0, 2, 1, 3).reshape(D_MODEL, D_MODEL))
    wpa = (w_proj_a[0].astype(BF16).reshape(N_KV_HEADS, GROUP, HEAD_DIM, D_MODEL)
           .transpose(1, 0, 2, 3).reshape(D_MODEL, D_MODEL))
    wpb = w_proj_b[0].astype(BF16)
    wo = w_out[0].astype(BF16)

    nw = norm_w[0].reshape(1, D_MODEL)
    split = lambda v: jnp.broadcast_to(v.reshape(2, 1, HALF), (2, N_KV_HEADS, HALF)).reshape(1, 256)

    to_tiles = lambda c: c[0].transpose(0, 2, 3, 1).reshape(DEC_BATCH, KV_WIDTH, WINDOW)
    from_tiles = lambda c: (c.reshape(DEC_BATCH, N_KV_HEADS, HEAD_DIM, WINDOW)
                            .transpose(0, 3, 1, 2)[None])
    from_tile = lambda t: t.reshape(N_KV_HEADS, HEAD_DIM, WINDOW).transpose(2, 0, 1)

    tabs_p = _rope_tables(np.arange(SEQ) + N_META)
    tabs_0 = _rope_tables(np.arange(BLOCK) - LEAD)
    tabs_s = _rope_tables(PAST_LEN + (np.arange(TS) % DEC_SEQ))
    y_p, y_s, nkt_p, nvt_p, nc_p, nk_s, nv_s, nc_s = _layer_call(
        x_prompt[0], x_sample.reshape(DEC_BATCH * DEC_SEQ, D_MODEL), tabs_p,
        to_tiles(cache_k), to_tiles(cache_v), state_conv[0], w_in,
        meta_tokens.astype(x_prompt.dtype), tabs_0, tabs_s,
        nw, split(q_norm_w), split(k_norm_w), conv_w, sinks,
        wq, wk, wza, wpa, wpb, wo, _head_mean_matrix())

    return (y_p.reshape(1, SEQ, D_MODEL),
            y_s.reshape(DEC_BATCH, DEC_SEQ, D_MODEL),
            from_tile(nkt_p).reshape(1, 1, WINDOW, N_KV_HEADS, HEAD_DIM),
            from_tile(nvt_p).reshape(1, 1, WINDOW, N_KV_HEADS, HEAD_DIM),
            nc_p.reshape(1, 1, CONV_K - 1, D_MODEL),
            from_tiles(nk_s),
            from_tiles(nv_s),
            nc_s.reshape(1, DEC_BATCH, CONV_K - 1, D_MODEL))
```

```python
import functools

import numpy as np
import jax
import jax.numpy as jnp
from jax import lax
from jax.experimental import pallas as pl
from jax.experimental.pallas import tpu as pltpu

D_MODEL = 1024
SEQ = 16384
DEC_BATCH = 128
DEC_SEQ = 8
PAST_LEN = 16384
N_HEADS = 16
N_KV_HEADS = 4
GROUP = N_HEADS // N_KV_HEADS
HEAD_DIM = 64
HALF = HEAD_DIM // 2
KV_WIDTH = N_KV_HEADS * HEAD_DIM
WINDOW = 128
BLOCK = 128
ROPE_THETA = 10000.0
CONV_K = 3
N_META = 16
LEAD = BLOCK - N_META
EPS = 1e-6
NEG_INF = -1e30
Q_SCALE = HEAD_DIM ** -0.5

W_K = D_MODEL
W_V = W_K + KV_WIDTH
W_ZA = W_V + KV_WIDTH
G_B = W_ZA + D_MODEL
G_C = G_B + D_MODEL
G_H = G_C + D_MODEL
G_ZB = G_H + D_MODEL
G_GA = G_ZB + D_MODEL
G_GB = G_GA + D_MODEL
G_END = G_GB + D_MODEL

LANES = 128
SUBLANES = 8
VMEM_BYTES_V7X = 64 * 1024 * 1024

TM_PROMPT = 512
N_STEPS = SEQ // TM_PROMPT
DEC_PER_STEP = DEC_BATCH // N_STEPS
TS = DEC_PER_STEP * DEC_SEQ
GW_CHUNK = 64
GW_SLOTS = 4
SUB_TILES = 2
TM_SUB = TM_PROMPT // SUB_TILES
VMEM_LIMIT = VMEM_BYTES_V7X - 4 * 1024 * 1024

BF16 = jnp.bfloat16
F32 = jnp.float32


def _dot(a, b):
    return jnp.dot(a, b, preferred_element_type=F32)


def _dot_nt(a, b):
    return lax.dot_general(a, b, (((1,), (1,)), ((), ())), preferred_element_type=F32)


def _rms_rows(x, w):
    ms = jnp.mean(x * x, axis=-1, keepdims=True)
    return (x * lax.rsqrt(ms + EPS)) * w


def _head_rms_split(t, bd, w):
    n = t.shape[1] // 256
    r = t.shape[0]
    halves = [(t[:, 256 * c:256 * c + LANES], t[:, 256 * c + LANES:256 * (c + 1)]) for c in range(n)]
    pieces = [a * a + b * b for a, b in halves]
    if n == 1:
        pieces = [pieces[0], pieces[0]]
    pair_ms = [_dot(jnp.concatenate(pieces[2 * i:2 * i + 2], axis=1).astype(BF16), bd)
               for i in range(len(pieces) // 2)]
    ms = [pair_ms[i // 2][:, LANES * (i % 2):LANES * (i % 2 + 1)] for i in range(n)]
    out = []
    for (a, b), m in zip(halves, ms):
        scale = lax.rsqrt(m + EPS)
        out.append((a * scale) * w[:, :LANES])
        out.append((b * scale) * w[:, LANES:])
    return jnp.concatenate(out, axis=1)


def _rope_split(t, cos, sin):
    out = []
    for c in range(t.shape[1] // 256):
        a = t[:, 256 * c:256 * c + LANES]
        b = t[:, 256 * c + LANES:256 * (c + 1)]
        out.append(a * cos - b * sin)
        out.append(b * cos + a * sin)
    return jnp.concatenate(out, axis=1)


def _lane_group_masks(group_of_lane, dtype):
    return [(group_of_lane == g).astype(dtype) for g in range(N_KV_HEADS)]


def _softmax_parts(s, sk):
    m = jnp.maximum(jnp.max(s, axis=-1, keepdims=True), sk)
    p = jnp.exp(s - m)
    den = jnp.sum(p, axis=-1, keepdims=True) + jnp.exp(sk - m)
    return p, 1.0 / den


def _silu(z):
    return z * jax.nn.sigmoid(z)


def _rows_split_to_natural(t):
    return jnp.concatenate(
        [t[LANES * half + HALF * g:LANES * half + HALF * (g + 1)]
         for g in range(N_KV_HEADS) for half in range(2)], axis=0)


def _rows_natural_to_split(t):
    return jnp.concatenate(
        [t[HEAD_DIM * g + HALF * half:HEAD_DIM * g + HALF * (half + 1)]
         for half in range(2) for g in range(N_KV_HEADS)], axis=0)


def _prompt_attention(q, kb, vb, kprev, vprev, sinks_ref, first_step):
    tm = q.shape[0]
    lane = lax.broadcasted_iota(jnp.int32, (1, 2 * BLOCK), 1)
    kmasks = _lane_group_masks((lane & (LANES - 1)) >> 5, BF16)
    vmasks = _lane_group_masks(lane >> 6, BF16)
    vgroup = lane >> 6
    r_io = lax.broadcasted_iota(jnp.int32, (BLOCK, 2 * BLOCK), 0)
    c_io = lax.broadcasted_iota(jnp.int32, (BLOCK, 2 * BLOCK), 1)
    band = (c_io >= r_io) & (c_io <= r_io + WINDOW)
    first_band = band & (c_io >= jnp.where(first_step, LEAD, 0))

    o_blocks = []
    for b in range(tm // BLOCK):
        lo = b * BLOCK
        if b == 0:
            kpair = jnp.concatenate([kprev, kb[:BLOCK]], axis=0)
            vpair = jnp.concatenate([vprev, vb[:BLOCK]], axis=0)
            mask = first_band
        else:
            kpair = kb[lo - BLOCK:lo + BLOCK]
            vpair = vb[lo - BLOCK:lo + BLOCK]
            mask = band
        kbd = jnp.concatenate([kpair * km for km in kmasks], axis=0)
        vbd = jnp.concatenate([vpair * vm for vm in vmasks], axis=0)
        qs = jnp.concatenate([q[lo:lo + BLOCK, 256 * j:256 * (j + 1)] for j in range(GROUP)],
                             axis=0)
        s = _dot_nt(qs, kbd)
        p_rows = []
        mults = []
        for j in range(GROUP):
            p_cols = []
            invs = []
            for g in range(N_KV_HEADS):
                sl = s[BLOCK * j:BLOCK * (j + 1), 256 * g:256 * (g + 1)]
                sl = jnp.where(mask, sl, NEG_INF)
                p, inv = _softmax_parts(sl, sinks_ref[0, GROUP * g + j])
                p_cols.append(p.astype(BF16))
                invs.append(inv)
            p_rows.append(jnp.concatenate(p_cols, axis=1))
            mults.append(jnp.where(vgroup == 0, invs[0],
                                   jnp.where(vgroup == 1, invs[1],
                                             jnp.where(vgroup == 2, invs[2], invs[3]))))
        p_all = jnp.concatenate(p_rows, axis=0)
        half_k = 2 * 2 * BLOCK
        o = _dot(p_all[:, :half_k], vbd[:half_k]) + _dot(p_all[:, half_k:], vbd[half_k:])
        o_blocks.append(jnp.concatenate(
            [o[BLOCK * j:BLOCK * (j + 1)] * mults[j] for j in range(GROUP)], axis=1))
    return jnp.concatenate(o_blocks, axis=0)


def _decode_scores(qs, ks, vs, ck_ref, cv_ref, nck_ref, ncv_ref):
    t_len = DEC_SEQ
    n_exp = GROUP * N_KV_HEADS * t_len
    pad = jnp.zeros((LANES - qs.shape[0], KV_WIDTH), F32)
    kst = jnp.concatenate([ks, pad], axis=0).T
    vst = jnp.concatenate([vs, pad], axis=0).T
    kst_nat = _rows_split_to_natural(kst)
    kst_b = kst.astype(BF16)
    vst_b = vst.astype(BF16)

    lane = lax.broadcasted_iota(jnp.int32, (1, KV_WIDTH), 1)
    kgroup = (lane & (LANES - 1)) >> 5
    t_io = lax.broadcasted_iota(jnp.int32, (n_exp, WINDOW), 0) & (t_len - 1)
    c_io = lax.broadcasted_iota(jnp.int32, (n_exp, WINDOW), 1)
    old_ok = c_io >= t_io
    new_lane = lax.broadcasted_iota(jnp.int32, (KV_WIDTH, WINDOW), 1) >= WINDOW - t_len

    elems = []
    for b in range(qs.shape[0] // t_len):
        r0 = b * t_len
        qn = qs[r0:r0 + t_len]
        q_exp = jnp.concatenate(
            [jnp.where(kgroup == g, qn[:, 256 * j:256 * (j + 1)], 0.0)
             for j in range(GROUP) for g in range(N_KV_HEADS)], axis=0).astype(BF16)
        kt_old = ck_ref[b]
        vt_old = cv_ref[b]
        keys = jnp.concatenate([_rows_natural_to_split(kt_old).astype(BF16), kst_b], axis=1)
        vals = jnp.concatenate([vt_old.astype(BF16), vst_b], axis=1)
        s = _dot(q_exp, keys)
        new_ok = (c_io >= r0) & (c_io - t_io <= r0)
        s = jnp.where(jnp.concatenate([old_ok, new_ok], axis=1), s, NEG_INF)
        elems.append((s, vals))
        shift_new = WINDOW - t_len - r0
        nck_ref[b] = jnp.where(new_lane, pltpu.roll(kst_nat, shift_new, 1),
                               pltpu.roll(kt_old, WINDOW - t_len, 1))
        ncv_ref[b] = jnp.where(new_lane, pltpu.roll(vst, shift_new, 1),
                               pltpu.roll(vt_old, WINDOW - t_len, 1))
    return elems


def _decode_outputs(elems, sinks_ref):
    t_len = DEC_SEQ
    vgroup = lax.broadcasted_iota(jnp.int32, (1, KV_WIDTH), 1) >> 6
    skcol = jnp.concatenate(
        [jnp.full((t_len, 1), sinks_ref[0, GROUP * g + j], F32)
         for j in range(GROUP) for g in range(N_KV_HEADS)], axis=0)
    outs = []
    for s, vals in elems:
        p, inv = _softmax_parts(s, skcol)
        o = _dot_nt(p.astype(BF16), vals) * inv
        o_j = []
        for j in range(GROUP):
            acc = None
            for g in range(N_KV_HEADS):
                r = (j * N_KV_HEADS + g) * t_len
                piece = jnp.where(vgroup == g, o[r:r + t_len], 0.0)
                acc = piece if acc is None else acc + piece
            o_j.append(acc)
        outs.append(jnp.concatenate(o_j, axis=1))
    return jnp.concatenate(outs, axis=0)


def _gate_weight_copy(w_hbm, wstage, wsem, chunk):
    slot = chunk % GW_SLOTS
    return pltpu.make_async_copy(
        w_hbm.at[0, pl.ds(GW_CHUNK * chunk, GW_CHUNK), pl.ds(G_B, G_END - G_B)],
        wstage.at[slot], wsem.at[slot])


def _load_gate_weights(w_hbm, wg_s, wstage, wsem):
    n_chunks = D_MODEL // GW_CHUNK
    ahead = GW_SLOTS - 1
    for c in range(ahead):
        _gate_weight_copy(w_hbm, wstage, wsem, c).start()
    for c in range(n_chunks):
        if c + ahead < n_chunks:
            _gate_weight_copy(w_hbm, wstage, wsem, c + ahead).start()
        _gate_weight_copy(w_hbm, wstage, wsem, c).wait()
        wg_s[GW_CHUNK * c:GW_CHUNK * (c + 1), :] = wstage[c % GW_SLOTS].astype(BF16)


def _layer_kernel(sinks_ref, x_ref, xs_ref, cos_ref, sin_ref, ck_ref, cv_ref, st_ref, w_hbm,
                  meta_ref, cos0_ref, sin0_ref, coss_ref, sins_ref, nw_ref, qnw_ref, knw_ref, cw_ref,
                  wq_ref, wkv_ref, wza_ref, wpa_ref, wpb_ref, wo_ref, bd_ref,
                  y_ref, ys_ref, nkt_ref, nvt_ref, nc_ref, nck_ref, ncv_ref, ncs_ref,
                  kprev, vprev, utail, wg_s, wstage, wsem):
    i = pl.program_id(0)
    nw = nw_ref[...]
    bd = bd_ref[...]
    knw = knw_ref[...]

    @pl.when(i == 0)
    def _init():
        _load_gate_weights(w_hbm, wg_s, wstage, wsem)
        x0 = jnp.concatenate([jnp.zeros((LEAD, D_MODEL), F32), meta_ref[...]], axis=0)
        xn0 = _rms_rows(x0, nw).astype(BF16)
        kv0 = _dot(xn0, wkv_ref[...])
        k0 = _rope_split(_head_rms_split(kv0[:, :KV_WIDTH], bd, knw), cos0_ref[...], sin0_ref[...])
        v0 = kv0[:, KV_WIDTH:]
        c0 = _dot(xn0, wg_s[:, G_C - G_B:G_H - G_B])
        h0 = _dot(xn0, wg_s[:, G_H - G_B:G_ZB - G_B])
        row = lax.broadcasted_iota(jnp.int32, (BLOCK, D_MODEL), 0)
        u0 = jnp.where(row >= LEAD, c0 * h0, 0.0)
        kprev[...] = k0
        vprev[...] = v0
        utail[...] = u0[BLOCK - SUBLANES:BLOCK]

    cw = cw_ref[0]
    row8 = lax.broadcasted_iota(jnp.int32, (SUBLANES, D_MODEL), 0)
    t_io = lax.broadcasted_iota(jnp.int32, (DEC_PER_STEP, DEC_SEQ, D_MODEL), 1)
    k_prev = kprev[...].astype(BF16)
    v_prev = vprev[...].astype(BF16)
    tail = utail[...]
    for h in range(SUB_TILES):
        ps = slice(TM_SUB * h, TM_SUB * (h + 1))
        with_dec = h == SUB_TILES - 1
        if with_dec:
            x = jnp.concatenate([x_ref[ps, :], xs_ref[...]], axis=0)
            cos = jnp.concatenate([cos_ref[ps, :], coss_ref[...]], axis=0)
            sin = jnp.concatenate([sin_ref[ps, :], sins_ref[...]], axis=0)
        else:
            x = x_ref[ps, :]
            cos = cos_ref[ps, :]
            sin = sin_ref[ps, :]
        xn = _rms_rows(x, nw).astype(BF16)

        q = _head_rms_split(_dot(xn, wq_ref[...]), bd, qnw_ref[...])
        q = _rope_split(q, cos * Q_SCALE, sin * Q_SCALE)
        kv = _dot(xn, wkv_ref[...])
        k = _rope_split(_head_rms_split(kv[:, :KV_WIDTH], bd, knw), cos, sin)
        v = kv[:, KV_WIDTH:]
        kb = k[:TM_SUB].astype(BF16)
        vb = v[:TM_SUB].astype(BF16)

        first = (i == 0) if h == 0 else False
        o_a = _prompt_attention(q[:TM_SUB].astype(BF16), kb, vb, k_prev, v_prev, sinks_ref, first)
        if with_dec:
            elems = _decode_scores(q[TM_SUB:], k[TM_SUB:], v[TM_SUB:], ck_ref, cv_ref, nck_ref,
                                   ncv_ref)
        c, hc, b, z_b = [_dot(xn, wg_s[:, lo - G_B:lo - G_B + D_MODEL])
                         for lo in (G_C, G_H, G_B, G_ZB)]
        z_a = _dot(xn, wza_ref[...])
        if with_dec:
            o_a = jnp.concatenate([o_a, _decode_outputs(elems, sinks_ref)], axis=0)
        k_prev = kb[TM_SUB - BLOCK:]
        v_prev = vb[TM_SUB - BLOCK:]
        k_last = k[TM_SUB - BLOCK:TM_SUB]
        v_last = v[TM_SUB - BLOCK:TM_SUB]

        u_all = c * hc
        u = u_all[:TM_SUB]
        prev1 = tail[SUBLANES - 1:SUBLANES]
        prev2 = tail[SUBLANES - 2:SUBLANES - 1]
        r1 = pltpu.roll(u, 1, 0)
        r2 = pltpu.roll(u, 2, 0)
        um1 = jnp.concatenate(
            [jnp.where(row8 == 0, prev1, r1[:SUBLANES]), r1[SUBLANES:]], axis=0)
        um2 = jnp.concatenate(
            [jnp.where(row8 == 0, prev2, jnp.where(row8 == 1, prev1, r2[:SUBLANES])),
             r2[SUBLANES:]], axis=0)
        conv = cw[0:1] * um2 + cw[1:2] * um1 + cw[2:3] * u
        tail = u[TM_SUB - SUBLANES:]
        if with_dec:
            u3 = u_all[TM_SUB:].reshape(DEC_PER_STEP, DEC_SEQ, D_MODEL)
            st = st_ref[...]
            s_m2 = st[:, 0:1, :]
            s_m1 = st[:, 1:2, :]
            us1 = jnp.where(t_io == 0, s_m1, pltpu.roll(u3, 1, 1))
            us2 = jnp.where(t_io == 0, s_m2, jnp.where(t_io == 1, s_m1, pltpu.roll(u3, 2, 1)))
            conv_s = (cw[0:1] * us2 + cw[1:2] * us1 + cw[2:3] * u3).reshape(TS, D_MODEL)
            ncs_ref[...] = u3[:, DEC_SEQ - (CONV_K - 1):, :]
            conv = jnp.concatenate([conv, conv_s], axis=0)

        br_a = _dot((o_a * _silu(z_a)).astype(BF16), wpa_ref[...])
        br_b = _dot(((b * conv) * _silu(z_b)).astype(BF16), wpb_ref[...])
        g_a = _dot(xn, wg_s[:, G_GA - G_B:G_GB - G_B])
        g_b = _dot(xn, wg_s[:, G_GB - G_B:G_END - G_B])
        mixed = (jax.nn.sigmoid(g_a) * br_a + jax.nn.sigmoid(g_b) * br_b).astype(BF16)
        y = x + _dot(mixed, wo_ref[...])
        y_ref[ps, :] = y[:TM_SUB]
        if with_dec:
            ys_ref[...] = y[TM_SUB:]

    kprev[...] = k_last
    vprev[...] = v_last
    utail[...] = tail
    nkt_ref[...] = _rows_split_to_natural(k_last.T)
    nvt_ref[...] = v_last.T
    nc_ref[...] = tail[SUBLANES - (CONV_K - 1):]


def _const_spec(shape):
    zeros = (0,) * len(shape)
    return pl.BlockSpec(shape, lambda i: zeros, pipeline_mode=pl.Buffered(1))


def _layer_call(x, xs, tabs, ck, cv, st, w_in, meta, tabs0, tabs_s, nw, qnw, knw, cw, sinks,
                wq, wk, wza, wpa, wpb, wo, bd):
    tm = TM_PROMPT
    assert N_STEPS * DEC_PER_STEP == DEC_BATCH and TS % (2 * SUBLANES) == 0 and TS <= LANES
    assert TM_SUB % BLOCK == 0
    row_spec = lambda rows, w: pl.BlockSpec((rows, w), lambda i: (i, 0))
    cache_spec = pl.BlockSpec((DEC_PER_STEP, KV_WIDTH, WINDOW), lambda i: (i, 0, 0))
    st_spec = pl.BlockSpec((DEC_PER_STEP, CONV_K - 1, D_MODEL), lambda i: (i, 0, 0))
    consts = (meta, tabs0[0], tabs0[1], tabs_s[0], tabs_s[1], nw, qnw, knw, cw,
              wq, wk, wza, wpa, wpb, wo, bd)
    in_specs = [pl.BlockSpec(memory_space=pltpu.SMEM),
                row_spec(tm, D_MODEL), row_spec(TS, D_MODEL), row_spec(tm, LANES), row_spec(tm, LANES),
                cache_spec, cache_spec, st_spec, pl.BlockSpec(memory_space=pl.ANY)]
    in_specs += [_const_spec(a.shape) for a in consts]
    resident = lambda shape: pl.BlockSpec(shape, lambda i: (0, 0))
    out_specs = [row_spec(tm, D_MODEL), row_spec(TS, D_MODEL),
                 resident((KV_WIDTH, BLOCK)), resident((KV_WIDTH, BLOCK)),
                 resident((CONV_K - 1, D_MODEL)), cache_spec, cache_spec, st_spec]
    cache_sds = jax.ShapeDtypeStruct((DEC_BATCH, KV_WIDTH, WINDOW), F32)
    out_shape = [jax.ShapeDtypeStruct((SEQ, D_MODEL), F32),
                 jax.ShapeDtypeStruct((DEC_BATCH * DEC_SEQ, D_MODEL), F32),
                 jax.ShapeDtypeStruct((KV_WIDTH, BLOCK), F32),
                 jax.ShapeDtypeStruct((KV_WIDTH, BLOCK), F32),
                 jax.ShapeDtypeStruct((CONV_K - 1, D_MODEL), F32), cache_sds, cache_sds,
                 jax.ShapeDtypeStruct((DEC_BATCH, CONV_K - 1, D_MODEL), F32)]
    return pl.pallas_call(
        _layer_kernel,
        grid=(N_STEPS,),
        in_specs=in_specs,
        out_specs=out_specs,
        out_shape=out_shape,
        scratch_shapes=[pltpu.VMEM((BLOCK, KV_WIDTH), F32),
                        pltpu.VMEM((BLOCK, KV_WIDTH), F32),
                        pltpu.VMEM((SUBLANES, D_MODEL), F32),
                        pltpu.VMEM((D_MODEL, G_END - G_B), BF16),
                        pltpu.VMEM((GW_SLOTS, GW_CHUNK, G_END - G_B), F32),
                        pltpu.SemaphoreType.DMA((GW_SLOTS,))],
        compiler_params=pltpu.CompilerParams(
            dimension_semantics=("arbitrary",), vmem_limit_bytes=VMEM_LIMIT),
        name="hybrid_layer",
    )(sinks, x, xs, tabs[0], tabs[1], ck, cv, st, w_in, *consts)


def _rope_tables(positions):
    inv = np.power(ROPE_THETA, -np.arange(HALF, dtype=np.float64) * (2.0 / HEAD_DIM))
    ang = np.asarray(positions, dtype=np.float64)[:, None] * inv[None, :]
    cos = np.tile(np.cos(ang), (1, LANES // HALF))
    sin = np.tile(np.sin(ang), (1, LANES // HALF))
    return jnp.asarray(cos, F32), jnp.asarray(sin, F32)


def _head_mean_matrix():
    grp = np.arange(256) // HALF
    return jnp.asarray((grp[:, None] == grp[None, :]) / float(HEAD_DIM), BF16)


def kernel(x_prompt, x_sample, cache_k, cache_v, state_conv, meta_tokens, norm_w, w_in,
           q_norm_w, k_norm_w, sinks, conv_w, w_proj_a, w_proj_b, w_out):
    assert x_prompt.shape == (1, SEQ, D_MODEL) and x_sample.shape == (DEC_BATCH, DEC_SEQ, D_MODEL)
    assert w_in.shape[0] == 1, "single layer"
    wg = w_in[0, :, :G_B].astype(BF16)
    wq = (wg[:, :W_K].reshape(D_MODEL, N_KV_HEADS, GROUP, 2, HALF)
          .transpose(0, 2, 3, 1, 4).reshape(D_MODEL, D_MODEL))
    wk = (wg[:, W_K:W_V].reshape(D_MODEL, N_KV_HEADS, 2, HALF)
          .transpose(0, 2, 1, 3).reshape(D_MODEL, KV_WIDTH))
    wk = jnp.concatenate([wk, wg[:, W_V:W_ZA]], axis=1)
    wza = (wg[:, W_ZA:G_B].reshape(D_MODEL, N_KV_HEADS, GROUP, HEAD_DIM)
           .transpose(0, 2, 1, 3).reshape(D_MODEL, D_MODEL))
    wpa = (w_proj_a[0].astype(BF16).reshape(N_KV_HEADS, GROUP, HEAD_DIM, D_MODEL)
           .transpose(1, 0, 2, 3).reshape(D_MODEL, D_MODEL))
    wpb = w_proj_b[0].astype(BF16)
    wo = w_out[0].astype(BF16)

    nw = norm_w[0].reshape(1, D_MODEL)
    split = lambda v: jnp.broadcast_to(v.reshape(2, 1, HALF), (2, N_KV_HEADS, HALF)).reshape(1, 256)

    to_tiles = lambda c: c[0].transpose(0, 2, 3, 1).reshape(DEC_BATCH, KV_WIDTH, WINDOW)
    from_tiles = lambda c: (c.reshape(DEC_BATCH, N_KV_HEADS, HEAD_DIM, WINDOW)
                            .transpose(0, 3, 1, 2)[None])
    from_tile = lambda t: t.reshape(N_KV_HEADS, HEAD_DIM, WINDOW).transpose(2, 0, 1)

    tabs_p = _rope_tables(np.arange(SEQ) + N_META)
    tabs_0 = _rope_tables(np.arange(BLOCK) - LEAD)
    tabs_s = _rope_tables(PAST_LEN + (np.arange(TS) % DEC_SEQ))
    y_p, y_s, nkt_p, nvt_p, nc_p, nk_s, nv_s, nc_s = _layer_call(
        x_prompt[0], x_sample.reshape(DEC_BATCH * DEC_SEQ, D_MODEL), tabs_p,
        to_tiles(cache_k), to_tiles(cache_v), state_conv[0], w_in,
        meta_tokens.astype(x_prompt.dtype), tabs_0, tabs_s,
        nw, split(q_norm_w), split(k_norm_w), conv_w, sinks,
        wq, wk, wza, wpa, wpb, wo, _head_mean_matrix())

    return (y_p.reshape(1, SEQ, D_MODEL),
            y_s.reshape(DEC_BATCH, DEC_SEQ, D_MODEL),
            from_tile(nkt_p).reshape(1, 1, WINDOW, N_KV_HEADS, HEAD_DIM),
            from_tile(nvt_p).reshape(1, 1, WINDOW, N_KV_HEADS, HEAD_DIM),
            nc_p.reshape(1, 1, CONV_K - 1, D_MODEL),
            from_tiles(nk_s),
            from_tiles(nv_s),
            nc_s.reshape(1, DEC_BATCH, CONV_K - 1, D_MODEL))
```

```python
import functools

import numpy as np
import jax
import jax.numpy as jnp
from jax import lax
from jax.experimental import pallas as pl
from jax.experimental.pallas import tpu as pltpu

D_MODEL = 1024
SEQ = 16384
DEC_BATCH = 128
DEC_SEQ = 8
PAST_LEN = 16384
N_HEADS = 16
N_KV_HEADS = 4
GROUP = N_HEADS // N_KV_HEADS
HEAD_DIM = 64
HALF = HEAD_DIM // 2
KV_WIDTH = N_KV_HEADS * HEAD_DIM
WINDOW = 128
BLOCK = 128
ROPE_THETA = 10000.0
CONV_K = 3
N_META = 16
LEAD = BLOCK - N_META
EPS = 1e-6
NEG_INF = -1e30
Q_SCALE = HEAD_DIM ** -0.5

W_K = D_MODEL
W_V = W_K + KV_WIDTH
W_ZA = W_V + KV_WIDTH
G_B = W_ZA + D_MODEL
G_C = G_B + D_MODEL
G_H = G_C + D_MODEL
G_ZB = G_H + D_MODEL
G_GA = G_ZB + D_MODEL
G_GB = G_GA + D_MODEL
G_END = G_GB + D_MODEL

LANES = 128
SUBLANES = 8
VMEM_BYTES_V7X = 64 * 1024 * 1024

TM_PROMPT = 512
N_STEPS = SEQ // TM_PROMPT
DEC_PER_STEP = DEC_BATCH // N_STEPS
TS = DEC_PER_STEP * DEC_SEQ
GW_CHUNK = 64
GW_SLOTS = 4
SUB_TILES = 2
TM_SUB = TM_PROMPT // SUB_TILES
VMEM_LIMIT = VMEM_BYTES_V7X - 4 * 1024 * 1024

BF16 = jnp.bfloat16
F32 = jnp.float32


def _dot(a, b):
    return jnp.dot(a, b, preferred_element_type=F32)


def _dot_nt(a, b):
    return lax.dot_general(a, b, (((1,), (1,)), ((), ())), preferred_element_type=F32)


def _rms_rows(x, w):
    ms = jnp.mean(x * x, axis=-1, keepdims=True)
    return (x * lax.rsqrt(ms + EPS)) * w


def _head_rms_split(t, bd, w):
    n = t.shape[1] // 256
    r = t.shape[0]
    halves = [(t[:, 256 * c:256 * c + LANES], t[:, 256 * c + LANES:256 * (c + 1)]) for c in range(n)]
    pieces = [a * a + b * b for a, b in halves]
    if n == 1:
        pieces = [pieces[0], pieces[0]]
    pair_ms = [_dot(jnp.concatenate(pieces[2 * i:2 * i + 2], axis=1).astype(BF16), bd)
               for i in range(len(pieces) // 2)]
    ms = [pair_ms[i // 2][:, LANES * (i % 2):LANES * (i % 2 + 1)] for i in range(n)]
    out = []
    for (a, b), m in zip(halves, ms):
        scale = lax.rsqrt(m + EPS)
        out.append((a * scale) * w[:, :LANES])
        out.append((b * scale) * w[:, LANES:])
    return jnp.concatenate(out, axis=1)


def _rope_split(t, cos, sin):
    out = []
    for c in range(t.shape[1] // 256):
        a = t[:, 256 * c:256 * c + LANES]
        b = t[:, 256 * c + LANES:256 * (c + 1)]
        out.append(a * cos - b * sin)
        out.append(b * cos + a * sin)
    return jnp.concatenate(out, axis=1)


def _lane_group_masks(group_of_lane, dtype):
    return [(group_of_lane == g).astype(dtype) for g in range(N_KV_HEADS)]


def _softmax_parts(s, sk):
    m = jnp.maximum(jnp.max(s, axis=-1, keepdims=True), sk)
    p = jnp.exp(s - m)
    den = jnp.sum(p, axis=-1, keepdims=True) + jnp.exp(sk - m)
    return p, 1.0 / den


def _silu(z):
    return z * jax.nn.sigmoid(z)


def _rows_split_to_natural(t):
    return jnp.concatenate(
        [t[LANES * half + HALF * g:LANES * half + HALF * (g + 1)]
         for g in range(N_KV_HEADS) for half in range(2)], axis=0)


def _rows_natural_to_split(t):
    return jnp.concatenate(
        [t[HEAD_DIM * g + HALF * half:HEAD_DIM * g + HALF * (half + 1)]
         for half in range(2) for g in range(N_KV_HEADS)], axis=0)


def _prompt_attention(q, kb, vb, kprev, vprev, sinks_ref, first_step):
    tm = q.shape[0]
    lane = lax.broadcasted_iota(jnp.int32, (1, 2 * BLOCK), 1)
    kmasks = _lane_group_masks((lane & (LANES - 1)) >> 5, BF16)
    vmasks = _lane_group_masks(lane >> 6, BF16)
    vgroup = lane >> 6
    r_io = lax.broadcasted_iota(jnp.int32, (BLOCK, 2 * BLOCK), 0)
    c_io = lax.broadcasted_iota(jnp.int32, (BLOCK, 2 * BLOCK), 1)
    band = (c_io >= r_io) & (c_io <= r_io + WINDOW)
    first_band = band & (c_io >= jnp.where(first_step, LEAD, 0))

    o_blocks = []
    for b in range(tm // BLOCK):
        lo = b * BLOCK
        if b == 0:
            kpair = jnp.concatenate([kprev, kb[:BLOCK]], axis=0)
            vpair = jnp.concatenate([vprev, vb[:BLOCK]], axis=0)
            mask = first_band
        else:
            kpair = kb[lo - BLOCK:lo + BLOCK]
            vpair = vb[lo - BLOCK:lo + BLOCK]
            mask = band
        kbd = jnp.concatenate([kpair * km for km in kmasks], axis=0)
        vbd = jnp.concatenate([vpair * vm for vm in vmasks], axis=0)
        qs = jnp.concatenate([q[lo:lo + BLOCK, 256 * j:256 * (j + 1)] for j in range(GROUP)],
                             axis=0)
        s = _dot_nt(qs, kbd)
        p_rows = []
        mults = []
        for j in range(GROUP):
            p_cols = []
            invs = []
            for g in range(N_KV_HEADS):
                sl = s[BLOCK * j:BLOCK * (j + 1), 256 * g:256 * (g + 1)]
                sl = jnp.where(mask, sl, NEG_INF)
                p, inv = _softmax_parts(sl, sinks_ref[0, GROUP * g + j])
                p_cols.append(p.astype(BF16))
                invs.append(inv)
            p_rows.append(jnp.concatenate(p_cols, axis=1))
            mults.append(jnp.where(vgroup == 0, invs[0],
                                   jnp.where(vgroup == 1, invs[1],
                                             jnp.where(vgroup == 2, invs[2], invs[3]))))
        p_all = jnp.concatenate(p_rows, axis=0)
        half_k = 2 * 2 * BLOCK
        o = _dot(p_all[:, :half_k], vbd[:half_k]) + _dot(p_all[:, half_k:], vbd[half_k:])
        o_blocks.append(jnp.concatenate(
            [o[BLOCK * j:BLOCK * (j + 1)] * mults[j] for j in range(GROUP)], axis=1))
    return jnp.concatenate(o_blocks, axis=0)


def _decode_scores(qs, ks, vs, ck_ref, cv_ref, nck_ref, ncv_ref):
    t_len = DEC_SEQ
    n_exp = GROUP * N_KV_HEADS * t_len
    pad = jnp.zeros((LANES - qs.shape[0], KV_WIDTH), F32)
    kst = jnp.concatenate([ks, pad], axis=0).T
    vst = jnp.concatenate([vs, pad], axis=0).T
    kst_nat = _rows_split_to_natural(kst)
    kst_b = kst.astype(BF16)
    vst_b = vst.astype(BF16)

    lane = lax.broadcasted_iota(jnp.int32, (1, KV_WIDTH), 1)
    kgroup = (lane & (LANES - 1)) >> 5
    t_io = lax.broadcasted_iota(jnp.int32, (n_exp, WINDOW), 0) & (t_len - 1)
    c_io = lax.broadcasted_iota(jnp.int32, (n_exp, WINDOW), 1)
    old_ok = c_io >= t_io
    new_lane = lax.broadcasted_iota(jnp.int32, (KV_WIDTH, WINDOW), 1) >= WINDOW - t_len

    elems = []
    for b in range(qs.shape[0] // t_len):
        r0 = b * t_len
        qn = qs[r0:r0 + t_len]
        q_exp = jnp.concatenate(
            [jnp.where(kgroup == g, qn[:, 256 * j:256 * (j + 1)], 0.0)
             for j in range(GROUP) for g in range(N_KV_HEADS)], axis=0).astype(BF16)
        kt_old = ck_ref[b]
        vt_old = cv_ref[b]
        keys = jnp.concatenate([_rows_natural_to_split(kt_old).astype(BF16), kst_b], axis=1)
        vals = jnp.concatenate([vt_old.astype(BF16), vst_b], axis=1)
        s = _dot(q_exp, keys)
        new_ok = (c_io >= r0) & (c_io - t_io <= r0)
        s = jnp.where(jnp.concatenate([old_ok, new_ok], axis=1), s, NEG_INF)
        elems.append((s, vals))
        shift_new = WINDOW - t_len - r0
        nck_ref[b] = jnp.where(new_lane, pltpu.roll(kst_nat, shift_new, 1),
                               pltpu.roll(kt_old, WINDOW - t_len, 1))
        ncv_ref[b] = jnp.where(new_lane, pltpu.roll(vst, shift_new, 1),
                               pltpu.roll(vt_old, WINDOW - t_len, 1))
    return elems


def _decode_outputs(elems, sinks_ref):
    t_len = DEC_SEQ
    vgroup = lax.broadcasted_iota(jnp.int32, (1, KV_WIDTH), 1) >> 6
    skcol = jnp.concatenate(
        [jnp.full((t_len, 1), sinks_ref[0, GROUP * g + j], F32)
         for j in range(GROUP) for g in range(N_KV_HEADS)], axis=0)
    outs = []
    for s, vals in elems:
        p, inv = _softmax_parts(s, skcol)
        o = _dot_nt(p.astype(BF16), vals) * inv
        o_j = []
        for j in range(GROUP):
            acc = None
            for g in range(N_KV_HEADS):
                r = (j * N_KV_HEADS + g) * t_len
                piece = jnp.where(vgroup == g, o[r:r + t_len], 0.0)
                acc = piece if acc is None else acc + piece
            o_j.append(acc)
        outs.append(jnp.concatenate(o_j, axis=1))
    return jnp.concatenate(outs, axis=0)


def _gate_weight_copy(w_hbm, wstage, wsem, chunk):
    slot = chunk % GW_SLOTS
    return pltpu.make_async_copy(
        w_hbm.at[0, pl.ds(GW_CHUNK * chunk, GW_CHUNK), pl.ds(G_B, G_END - G_B)],
        wstage.at[slot], wsem.at[slot])


def _load_gate_weights(w_hbm, wg_s, wstage, wsem):
    n_chunks = D_MODEL // GW_CHUNK
    ahead = GW_SLOTS - 1
    for c in range(ahead):
        _gate_weight_copy(w_hbm, wstage, wsem, c).start()
    for c in range(n_chunks):
        if c + ahead < n_chunks:
            _gate_weight_copy(w_hbm, wstage, wsem, c + ahead).start()
        _gate_weight_copy(w_hbm, wstage, wsem, c).wait()
        wg_s[GW_CHUNK * c:GW_CHUNK * (c + 1), :] = wstage[c % GW_SLOTS].astype(BF16)


def _layer_kernel(sinks_ref, x_ref, xs_ref, cos_ref, sin_ref, ck_ref, cv_ref, st_ref, w_hbm,
                  meta_ref, cos0_ref, sin0_ref, coss_ref, sins_ref, nw_ref, qnw_ref, knw_ref, cw_ref,
                  wq_ref, wkv_ref, wza_ref, wpa_ref, wpb_ref, wo_ref, bd_ref,
                  y_ref, ys_ref, nkt_ref, nvt_ref, nc_ref, nck_ref, ncv_ref, ncs_ref,
                  kprev, vprev, utail, wg_s, wstage, wsem):
    i = pl.program_id(0)
    nw = nw_ref[...]
    bd = bd_ref[...]
    knw = knw_ref[...]

    @pl.when(i == 0)
    def _init():
        _load_gate_weights(w_hbm, wg_s, wstage, wsem)
        x0 = jnp.concatenate([jnp.zeros((LEAD, D_MODEL), F32), meta_ref[...]], axis=0)
        xn0 = _rms_rows(x0, nw).astype(BF16)
        kv0 = _dot(xn0, wkv_ref[...])
        k0 = _rope_split(_head_rms_split(kv0[:, :KV_WIDTH], bd, knw), cos0_ref[...], sin0_ref[...])
        v0 = kv0[:, KV_WIDTH:]
        c0 = _dot(xn0, wg_s[:, G_C - G_B:G_H - G_B])
        h0 = _dot(xn0, wg_s[:, G_H - G_B:G_ZB - G_B])
        row = lax.broadcasted_iota(jnp.int32, (BLOCK, D_MODEL), 0)
        u0 = jnp.where(row >= LEAD, c0 * h0, 0.0)
        kprev[...] = k0
        vprev[...] = v0
        utail[...] = u0[BLOCK - SUBLANES:BLOCK]

    cw = cw_ref[0]
    row8 = lax.broadcasted_iota(jnp.int32, (SUBLANES, D_MODEL), 0)
    t_io = lax.broadcasted_iota(jnp.int32, (DEC_PER_STEP, DEC_SEQ, D_MODEL), 1)
    k_prev = kprev[...].astype(BF16)
    v_prev = vprev[...].astype(BF16)
    tail = utail[...]
    def front(h):
        ps = slice(TM_SUB * h, TM_SUB * (h + 1))
        with_dec = h == SUB_TILES - 1
        if with_dec:
            x = jnp.concatenate([x_ref[ps, :], xs_ref[...]], axis=0)
            cos = jnp.concatenate([cos_ref[ps, :], coss_ref[...]], axis=0)
            sin = jnp.concatenate([sin_ref[ps, :], sins_ref[...]], axis=0)
        else:
            x = x_ref[ps, :]
            cos = cos_ref[ps, :]
            sin = sin_ref[ps, :]
        xn = _rms_rows(x, nw).astype(BF16)

        q = _head_rms_split(_dot(xn, wq_ref[...]), bd, qnw_ref[...])
        q = _rope_split(q, cos * Q_SCALE, sin * Q_SCALE)
        kv = _dot(xn, wkv_ref[...])
        k = _rope_split(_head_rms_split(kv[:, :KV_WIDTH], bd, knw), cos, sin)
        v = kv[:, KV_WIDTH:]
        kb = k[:TM_SUB].astype(BF16)
        vb = v[:TM_SUB].astype(BF16)
        return ps, with_dec, x, xn, q, k, v, kb, vb

    def mid(h, fr, k_prev, v_prev, tail):
        ps, with_dec, x, xn, q, k, v, kb, vb = fr
        first = (i == 0) if h == 0 else False
        o_a = _prompt_attention(q[:TM_SUB].astype(BF16), kb, vb, k_prev, v_prev, sinks_ref, first)
        if with_dec:
            elems = _decode_scores(q[TM_SUB:], k[TM_SUB:], v[TM_SUB:], ck_ref, cv_ref, nck_ref,
                                   ncv_ref)
        c, hc, b, z_b = [_dot(xn, wg_s[:, lo - G_B:lo - G_B + D_MODEL])
                         for lo in (G_C, G_H, G_B, G_ZB)]
        if with_dec:
            o_a = jnp.concatenate([o_a, _decode_outputs(elems, sinks_ref)], axis=0)
        k_last = k[TM_SUB - BLOCK:TM_SUB]
        v_last = v[TM_SUB - BLOCK:TM_SUB]

        u_all = c * hc
        u = u_all[:TM_SUB]
        prev1 = tail[SUBLANES - 1:SUBLANES]
        prev2 = tail[SUBLANES - 2:SUBLANES - 1]
        r1 = pltpu.roll(u, 1, 0)
        r2 = pltpu.roll(u, 2, 0)
        um1 = jnp.concatenate(
            [jnp.where(row8 == 0, prev1, r1[:SUBLANES]), r1[SUBLANES:]], axis=0)
        um2 = jnp.concatenate(
            [jnp.where(row8 == 0, prev2, jnp.where(row8 == 1, prev1, r2[:SUBLANES])),
             r2[SUBLANES:]], axis=0)
        conv = cw[0:1] * um2 + cw[1:2] * um1 + cw[2:3] * u
        tail = u[TM_SUB - SUBLANES:]
        if with_dec:
            u3 = u_all[TM_SUB:].reshape(DEC_PER_STEP, DEC_SEQ, D_MODEL)
            st = st_ref[...]
            s_m2 = st[:, 0:1, :]
            s_m1 = st[:, 1:2, :]
            us1 = jnp.where(t_io == 0, s_m1, pltpu.roll(u3, 1, 1))
            us2 = jnp.where(t_io == 0, s_m2, jnp.where(t_io == 1, s_m1, pltpu.roll(u3, 2, 1)))
            conv_s = (cw[0:1] * us2 + cw[1:2] * us1 + cw[2:3] * u3).reshape(TS, D_MODEL)
            ncs_ref[...] = u3[:, DEC_SEQ - (CONV_K - 1):, :]
            conv = jnp.concatenate([conv, conv_s], axis=0)
        return o_a, conv, b, z_b, k_last, v_last, tail

    def back(fr, md):
        ps, with_dec, x, xn = fr[:4]
        o_a, conv, b, z_b = md[:4]
        z_a = _dot(xn, wza_ref[...])
        br_a = _dot((o_a * _silu(z_a)).astype(BF16), wpa_ref[...])
        br_b = _dot(((b * conv) * _silu(z_b)).astype(BF16), wpb_ref[...])
        g_a = _dot(xn, wg_s[:, G_GA - G_B:G_GB - G_B])
        g_b = _dot(xn, wg_s[:, G_GB - G_B:G_END - G_B])
        mixed = (jax.nn.sigmoid(g_a) * br_a + jax.nn.sigmoid(g_b) * br_b).astype(BF16)
        y = x + _dot(mixed, wo_ref[...])
        y_ref[ps, :] = y[:TM_SUB]
        if with_dec:
            ys_ref[...] = y[TM_SUB:]

    fronts = {0: front(0)}
    mids = {}
    for h in range(SUB_TILES):
        if h:
            k_prev = fronts[h - 1][7][TM_SUB - BLOCK:]
            v_prev = fronts[h - 1][8][TM_SUB - BLOCK:]
            tail = mids[h - 1][6]
        mids[h] = mid(h, fronts[h], k_prev, v_prev, tail)
        if h + 1 < SUB_TILES:
            fronts[h + 1] = front(h + 1)
        back(fronts[h], mids[h])
    k_last, v_last, tail = mids[SUB_TILES - 1][4:7]

    kprev[...] = k_last
    vprev[...] = v_last
    utail[...] = tail
    nkt_ref[...] = _rows_split_to_natural(k_last.T)
    nvt_ref[...] = v_last.T
    nc_ref[...] = tail[SUBLANES - (CONV_K - 1):]


def _const_spec(shape):
    zeros = (0,) * len(shape)
    return pl.BlockSpec(shape, lambda i: zeros, pipeline_mode=pl.Buffered(1))


def _layer_call(x, xs, tabs, ck, cv, st, w_in, meta, tabs0, tabs_s, nw, qnw, knw, cw, sinks,
                wq, wk, wza, wpa, wpb, wo, bd):
    tm = TM_PROMPT
    assert N_STEPS * DEC_PER_STEP == DEC_BATCH and TS % (2 * SUBLANES) == 0 and TS <= LANES
    assert TM_SUB % BLOCK == 0
    row_spec = lambda rows, w: pl.BlockSpec((rows, w), lambda i: (i, 0))
    cache_spec = pl.BlockSpec((DEC_PER_STEP, KV_WIDTH, WINDOW), lambda i: (i, 0, 0))
    st_spec = pl.BlockSpec((DEC_PER_STEP, CONV_K - 1, D_MODEL), lambda i: (i, 0, 0))
    consts = (meta, tabs0[0], tabs0[1], tabs_s[0], tabs_s[1], nw, qnw, knw, cw,
              wq, wk, wza, wpa, wpb, wo, bd)
    in_specs = [pl.BlockSpec(memory_space=pltpu.SMEM),
                row_spec(tm, D_MODEL), row_spec(TS, D_MODEL), row_spec(tm, LANES), row_spec(tm, LANES),
                cache_spec, cache_spec, st_spec, pl.BlockSpec(memory_space=pl.ANY)]
    in_specs += [_const_spec(a.shape) for a in consts]
    resident = lambda shape: pl.BlockSpec(shape, lambda i: (0, 0))
    out_specs = [row_spec(tm, D_MODEL), row_spec(TS, D_MODEL),
                 resident((KV_WIDTH, BLOCK)), resident((KV_WIDTH, BLOCK)),
                 resident((CONV_K - 1, D_MODEL)), cache_spec, cache_spec, st_spec]
    cache_sds = jax.ShapeDtypeStruct((DEC_BATCH, KV_WIDTH, WINDOW), F32)
    out_shape = [jax.ShapeDtypeStruct((SEQ, D_MODEL), F32),
                 jax.ShapeDtypeStruct((DEC_BATCH * DEC_SEQ, D_MODEL), F32),
                 jax.ShapeDtypeStruct((KV_WIDTH, BLOCK), F32),
                 jax.ShapeDtypeStruct((KV_WIDTH, BLOCK), F32),
                 jax.ShapeDtypeStruct((CONV_K - 1, D_MODEL), F32), cache_sds, cache_sds,
                 jax.ShapeDtypeStruct((DEC_BATCH, CONV_K - 1, D_MODEL), F32)]
    return pl.pallas_call(
        _layer_kernel,
        grid=(N_STEPS,),
        in_specs=in_specs,
        out_specs=out_specs,
        out_shape=out_shape,
        scratch_shapes=[pltpu.VMEM((BLOCK, KV_WIDTH), F32),
                        pltpu.VMEM((BLOCK, KV_WIDTH), F32),
                        pltpu.VMEM((SUBLANES, D_MODEL), F32),
                        pltpu.VMEM((D_MODEL, G_END - G_B), BF16),
                        pltpu.VMEM((GW_SLOTS, GW_CHUNK, G_END - G_B), F32),
                        pltpu.SemaphoreType.DMA((GW_SLOTS,))],
        compiler_params=pltpu.CompilerParams(
            dimension_semantics=("arbitrary",), vmem_limit_bytes=VMEM_LIMIT),
        name="hybrid_layer",
    )(sinks, x, xs, tabs[0], tabs[1], ck, cv, st, w_in, *consts)


def _rope_tables(positions):
    inv = np.power(ROPE_THETA, -np.arange(HALF, dtype=np.float64) * (2.0 / HEAD_DIM))
    ang = np.asarray(positions, dtype=np.float64)[:, None] * inv[None, :]
    cos = np.tile(np.cos(ang), (1, LANES // HALF))
    sin = np.tile(np.sin(ang), (1, LANES // HALF))
    return jnp.asarray(cos, F32), jnp.asarray(sin, F32)


def _head_mean_matrix():
    grp = np.arange(256) // HALF
    return jnp.asarray((grp[:, None] == grp[None, :]) / float(HEAD_DIM), BF16)


def kernel(x_prompt, x_sample, cache_k, cache_v, state_conv, meta_tokens, norm_w, w_in,
           q_norm_w, k_norm_w, sinks, conv_w, w_proj_a, w_proj_b, w_out):
    assert x_prompt.shape == (1, SEQ, D_MODEL) and x_sample.shape == (DEC_BATCH, DEC_SEQ, D_MODEL)
    assert w_in.shape[0] == 1, "single layer"
    wg = w_in[0, :, :G_B].astype(BF16)
    wq = (wg[:, :W_K].reshape(D_MODEL, N_KV_HEADS, GROUP, 2, HALF)
          .transpose(0, 2, 3, 1, 4).reshape(D_MODEL, D_MODEL))
    wk = (wg[:, W_K:W_V].reshape(D_MODEL, N_KV_HEADS, 2, HALF)
          .transpose(0, 2, 1, 3).reshape(D_MODEL, KV_WIDTH))
    wk = jnp.concatenate([wk, wg[:, W_V:W_ZA]], axis=1)
    wza = (wg[:, W_ZA:G_B].reshape(D_MODEL, N_KV_HEADS, GROUP, HEAD_DIM)
           .transpose(0, 2, 1, 3).reshape(D_MODEL, D_MODEL))
    wpa = (w_proj_a[0].astype(BF16).reshape(N_KV_HEADS, GROUP, HEAD_DIM, D_MODEL)
           .transpose(1, 0, 2, 3).reshape(D_MODEL, D_MODEL))
    wpb = w_proj_b[0].astype(BF16)
    wo = w_out[0].astype(BF16)

    nw = norm_w[0].reshape(1, D_MODEL)
    split = lambda v: jnp.broadcast_to(v.reshape(2, 1, HALF), (2, N_KV_HEADS, HALF)).reshape(1, 256)

    to_tiles = lambda c: c[0].transpose(0, 2, 3, 1).reshape(DEC_BATCH, KV_WIDTH, WINDOW)
    from_tiles = lambda c: (c.reshape(DEC_BATCH, N_KV_HEADS, HEAD_DIM, WINDOW)
                            .transpose(0, 3, 1, 2)[None])
    from_tile = lambda t: t.reshape(N_KV_HEADS, HEAD_DIM, WINDOW).transpose(2, 0, 1)

    tabs_p = _rope_tables(np.arange(SEQ) + N_META)
    tabs_0 = _rope_tables(np.arange(BLOCK) - LEAD)
    tabs_s = _rope_tables(PAST_LEN + (np.arange(TS) % DEC_SEQ))
    y_p, y_s, nkt_p, nvt_p, nc_p, nk_s, nv_s, nc_s = _layer_call(
        x_prompt[0], x_sample.reshape(DEC_BATCH * DEC_SEQ, D_MODEL), tabs_p,
        to_tiles(cache_k), to_tiles(cache_v), state_conv[0], w_in,
        meta_tokens.astype(x_prompt.dtype), tabs_0, tabs_s,
        nw, split(q_norm_w), split(k_norm_w), conv_w, sinks,
        wq, wk, wza, wpa, wpb, wo, _head_mean_matrix())

    return (y_p.reshape(1, SEQ, D_MODEL),
            y_s.reshape(DEC_BATCH, DEC_SEQ, D_MODEL),
            from_tile(nkt_p).reshape(1, 1, WINDOW, N_KV_HEADS, HEAD_DIM),
            from_tile(nvt_p).reshape(1, 1, WINDOW, N_KV_HEADS, HEAD_DIM),
            nc_p.reshape(1, 1, CONV_K - 1, D_MODEL),
            from_tiles(nk_s),
            from_tiles(nv_s),
            nc_s.reshape(1, DEC_BATCH, CONV_K - 1, D_MODEL))
```

```python
import functools

import numpy as np
import jax
import jax.numpy as jnp
from jax import lax
from jax.experimental import pallas as pl
from jax.experimental.pallas import tpu as pltpu

D_MODEL = 1024
SEQ = 16384
DEC_BATCH = 128
DEC_SEQ = 8
PAST_LEN = 16384
N_HEADS = 16
N_KV_HEADS = 4
GROUP = N_HEADS // N_KV_HEADS
HEAD_DIM = 64
HALF = HEAD_DIM // 2
KV_WIDTH = N_KV_HEADS * HEAD_DIM
WINDOW = 128
BLOCK = 128
ROPE_THETA = 10000.0
CONV_K = 3
N_META = 16
LEAD = BLOCK - N_META
EPS = 1e-6
NEG_INF = -1e30
Q_SCALE = HEAD_DIM ** -0.5

W_K = D_MODEL
W_V = W_K + KV_WIDTH
W_ZA = W_V + KV_WIDTH
G_B = W_ZA + D_MODEL
G_C = G_B + D_MODEL
G_H = G_C + D_MODEL
G_ZB = G_H + D_MODEL
G_GA = G_ZB + D_MODEL
G_GB = G_GA + D_MODEL
G_END = G_GB + D_MODEL

LANES = 128
SUBLANES = 8
VMEM_BYTES_V7X = 64 * 1024 * 1024

TM_PROMPT = 512
N_STEPS = SEQ // TM_PROMPT
DEC_PER_STEP = DEC_BATCH // N_STEPS
TS = DEC_PER_STEP * DEC_SEQ
GW_CHUNK = 64
GW_SLOTS = 4
SUB_TILES = 2
TM_SUB = TM_PROMPT // SUB_TILES
VMEM_LIMIT = VMEM_BYTES_V7X - 4 * 1024 * 1024

BF16 = jnp.bfloat16
F32 = jnp.float32


def _dot(a, b):
    return jnp.dot(a, b, preferred_element_type=F32)


def _dot_nt(a, b):
    return lax.dot_general(a, b, (((1,), (1,)), ((), ())), preferred_element_type=F32)


def _rms_rows(x, w):
    ms = jnp.mean(x * x, axis=-1, keepdims=True)
    return (x * lax.rsqrt(ms + EPS)) * w


def _head_rms_split(t, bd, w):
    n = t.shape[1] // 256
    r = t.shape[0]
    halves = [(t[:, 256 * c:256 * c + LANES], t[:, 256 * c + LANES:256 * (c + 1)]) for c in range(n)]
    pieces = [a * a + b * b for a, b in halves]
    if n == 1:
        pieces = [pieces[0], pieces[0]]
    pair_ms = [_dot(jnp.concatenate(pieces[2 * i:2 * i + 2], axis=1).astype(BF16), bd)
               for i in range(len(pieces) // 2)]
    ms = [pair_ms[i // 2][:, LANES * (i % 2):LANES * (i % 2 + 1)] for i in range(n)]
    out = []
    for (a, b), m in zip(halves, ms):
        scale = lax.rsqrt(m + EPS)
        out.append((a * scale) * w[:, :LANES])
        out.append((b * scale) * w[:, LANES:])
    return jnp.concatenate(out, axis=1)


def _rope_split(t, cos, sin):
    out = []
    for c in range(t.shape[1] // 256):
        a = t[:, 256 * c:256 * c + LANES]
        b = t[:, 256 * c + LANES:256 * (c + 1)]
        out.append(a * cos - b * sin)
        out.append(b * cos + a * sin)
    return jnp.concatenate(out, axis=1)


def _lane_group_masks(group_of_lane, dtype):
    return [(group_of_lane == g).astype(dtype) for g in range(N_KV_HEADS)]


def _softmax_parts(s, sk):
    m = jnp.maximum(jnp.max(s, axis=-1, keepdims=True), sk)
    p = jnp.exp(s - m)
    den = jnp.sum(p, axis=-1, keepdims=True) + jnp.exp(sk - m)
    return p, 1.0 / den


def _silu(z):
    return z * jax.nn.sigmoid(z)


def _rows_split_to_natural(t):
    return jnp.concatenate(
        [t[LANES * half + HALF * g:LANES * half + HALF * (g + 1)]
         for g in range(N_KV_HEADS) for half in range(2)], axis=0)


def _rows_natural_to_split(t):
    return jnp.concatenate(
        [t[HEAD_DIM * g + HALF * half:HEAD_DIM * g + HALF * (half + 1)]
         for half in range(2) for g in range(N_KV_HEADS)], axis=0)


def _prompt_attention(q, kb, vb, kprev, vprev, sinks_ref, first_step):
    tm = q.shape[0]
    lane = lax.broadcasted_iota(jnp.int32, (1, 2 * BLOCK), 1)
    kmasks = _lane_group_masks((lane & (LANES - 1)) >> 5, BF16)
    vmasks = _lane_group_masks(lane >> 6, BF16)
    vgroup = lane >> 6
    r_io = lax.broadcasted_iota(jnp.int32, (BLOCK, 2 * BLOCK), 0)
    c_io = lax.broadcasted_iota(jnp.int32, (BLOCK, 2 * BLOCK), 1)
    band = (c_io >= r_io) & (c_io <= r_io + WINDOW)
    first_band = band & (c_io >= jnp.where(first_step, LEAD, 0))

    o_blocks = []
    for b in range(tm // BLOCK):
        lo = b * BLOCK
        if b == 0:
            kpair = jnp.concatenate([kprev, kb[:BLOCK]], axis=0)
            vpair = jnp.concatenate([vprev, vb[:BLOCK]], axis=0)
            mask = first_band
        else:
            kpair = kb[lo - BLOCK:lo + BLOCK]
            vpair = vb[lo - BLOCK:lo + BLOCK]
            mask = band
        kbd = jnp.concatenate([kpair * km for km in kmasks], axis=0)
        vbd = jnp.concatenate([vpair * vm for vm in vmasks], axis=0)
        qs = jnp.concatenate([q[lo:lo + BLOCK, 256 * j:256 * (j + 1)] for j in range(GROUP)],
                             axis=0)
        s = _dot_nt(qs, kbd)
        p_rows = []
        mults = []
        for j in range(GROUP):
            p_cols = []
            invs = []
            for g in range(N_KV_HEADS):
                sl = s[BLOCK * j:BLOCK * (j + 1), 256 * g:256 * (g + 1)]
                sl = jnp.where(mask, sl, NEG_INF)
                p, inv = _softmax_parts(sl, sinks_ref[0, GROUP * g + j])
                p_cols.append(p.astype(BF16))
                invs.append(inv)
            p_rows.append(jnp.concatenate(p_cols, axis=1))
            mults.append(jnp.where(vgroup == 0, invs[0],
                                   jnp.where(vgroup == 1, invs[1],
                                             jnp.where(vgroup == 2, invs[2], invs[3]))))
        p_all = jnp.concatenate(p_rows, axis=0)
        half_k = 2 * 2 * BLOCK
        o = _dot(p_all[:, :half_k], vbd[:half_k]) + _dot(p_all[:, half_k:], vbd[half_k:])
        o_blocks.append(jnp.concatenate(
            [o[BLOCK * j:BLOCK * (j + 1)] * mults[j] for j in range(GROUP)], axis=1))
    return jnp.concatenate(o_blocks, axis=0)


def _decode_scores(qs, ks, vs, ck_ref, cv_ref, nck_ref, ncv_ref):
    t_len = DEC_SEQ
    n_exp = GROUP * N_KV_HEADS * t_len
    pad = jnp.zeros((LANES - qs.shape[0], KV_WIDTH), F32)
    kst = jnp.concatenate([ks, pad], axis=0).T
    vst = jnp.concatenate([vs, pad], axis=0).T
    kst_nat = _rows_split_to_natural(kst)
    kst_b = kst.astype(BF16)
    vst_b = vst.astype(BF16)

    lane = lax.broadcasted_iota(jnp.int32, (1, KV_WIDTH), 1)
    kgroup = (lane & (LANES - 1)) >> 5
    t_io = lax.broadcasted_iota(jnp.int32, (n_exp, WINDOW), 0) & (t_len - 1)
    c_io = lax.broadcasted_iota(jnp.int32, (n_exp, WINDOW), 1)
    old_ok = c_io >= t_io
    new_lane = lax.broadcasted_iota(jnp.int32, (KV_WIDTH, WINDOW), 1) >= WINDOW - t_len

    elems = []
    for b in range(qs.shape[0] // t_len):
        r0 = b * t_len
        qn = qs[r0:r0 + t_len]
        q_exp = jnp.concatenate(
            [jnp.where(kgroup == g, qn[:, 256 * j:256 * (j + 1)], 0.0)
             for j in range(GROUP) for g in range(N_KV_HEADS)], axis=0).astype(BF16)
        kt_old = ck_ref[b]
        vt_old = cv_ref[b]
        keys = jnp.concatenate([_rows_natural_to_split(kt_old).astype(BF16), kst_b], axis=1)
        vals = jnp.concatenate([vt_old.astype(BF16), vst_b], axis=1)
        s = _dot(q_exp, keys)
        new_ok = (c_io >= r0) & (c_io - t_io <= r0)
        s = jnp.where(jnp.concatenate([old_ok, new_ok], axis=1), s, NEG_INF)
        elems.append((s, vals))
        shift_new = WINDOW - t_len - r0
        nck_ref[b] = jnp.where(new_lane, pltpu.roll(kst_nat, shift_new, 1),
                               pltpu.roll(kt_old, WINDOW - t_len, 1))
        ncv_ref[b] = jnp.where(new_lane, pltpu.roll(vst, shift_new, 1),
                               pltpu.roll(vt_old, WINDOW - t_len, 1))
    return elems


def _decode_outputs(elems, sinks_ref):
    t_len = DEC_SEQ
    vgroup = lax.broadcasted_iota(jnp.int32, (1, KV_WIDTH), 1) >> 6
    skcol = jnp.concatenate(
        [jnp.full((t_len, 1), sinks_ref[0, GROUP * g + j], F32)
         for j in range(GROUP) for g in range(N_KV_HEADS)], axis=0)
    outs = []
    for s, vals in elems:
        p, inv = _softmax_parts(s, skcol)
        o = _dot_nt(p.astype(BF16), vals) * inv
        o_j = []
        for j in range(GROUP):
            acc = None
            for g in range(N_KV_HEADS):
                r = (j * N_KV_HEADS + g) * t_len
                piece = jnp.where(vgroup == g, o[r:r + t_len], 0.0)
                acc = piece if acc is None else acc + piece
            o_j.append(acc)
        outs.append(jnp.concatenate(o_j, axis=1))
    return jnp.concatenate(outs, axis=0)


def _gate_weight_copy(w_hbm, wstage, wsem, chunk):
    slot = chunk % GW_SLOTS
    return pltpu.make_async_copy(
        w_hbm.at[0, pl.ds(GW_CHUNK * chunk, GW_CHUNK), pl.ds(G_B, G_END - G_B)],
        wstage.at[slot], wsem.at[slot])


def _load_gate_weights(w_hbm, wg_s, wstage, wsem):
    n_chunks = D_MODEL // GW_CHUNK
    ahead = GW_SLOTS - 1
    for c in range(ahead):
        _gate_weight_copy(w_hbm, wstage, wsem, c).start()
    for c in range(n_chunks):
        if c + ahead < n_chunks:
            _gate_weight_copy(w_hbm, wstage, wsem, c + ahead).start()
        _gate_weight_copy(w_hbm, wstage, wsem, c).wait()
        wg_s[GW_CHUNK * c:GW_CHUNK * (c + 1), :] = wstage[c % GW_SLOTS].astype(BF16)


def _layer_kernel(sinks_ref, x_ref, xs_ref, cos_ref, sin_ref, ck_ref, cv_ref, st_ref, w_hbm,
                  meta_ref, cos0_ref, sin0_ref, coss_ref, sins_ref, nw_ref, qnw_ref, knw_ref, cw_ref,
                  wq_ref, wkv_ref, wza_ref, wpa_ref, wpb_ref, wo_ref, bd_ref,
                  y_ref, ys_ref, nkt_ref, nvt_ref, nc_ref, nck_ref, ncv_ref, ncs_ref,
                  kprev, vprev, utail, wg_s, wstage, wsem):
    i = pl.program_id(0)
    nw = nw_ref[...]
    bd = bd_ref[...]
    knw = knw_ref[...]

    @pl.when(i == 0)
    def _init():
        _load_gate_weights(w_hbm, wg_s, wstage, wsem)
        x0 = jnp.concatenate([jnp.zeros((LEAD, D_MODEL), F32), meta_ref[...]], axis=0)
        xn0 = _rms_rows(x0, nw).astype(BF16)
        kv0 = _dot(xn0, wkv_ref[...])
        k0 = _rope_split(_head_rms_split(kv0[:, :KV_WIDTH], bd, knw), cos0_ref[...], sin0_ref[...])
        v0 = kv0[:, KV_WIDTH:]
        c0 = _dot(xn0, wg_s[:, G_C - G_B:G_H - G_B])
        h0 = _dot(xn0, wg_s[:, G_H - G_B:G_ZB - G_B])
        row = lax.broadcasted_iota(jnp.int32, (BLOCK, D_MODEL), 0)
        u0 = jnp.where(row >= LEAD, c0 * h0, 0.0)
        kprev[...] = k0
        vprev[...] = v0
        utail[...] = u0[BLOCK - SUBLANES:BLOCK]

    cw = cw_ref[0]
    row8 = lax.broadcasted_iota(jnp.int32, (SUBLANES, D_MODEL), 0)
    t_io = lax.broadcasted_iota(jnp.int32, (DEC_PER_STEP, DEC_SEQ, D_MODEL), 1)
    k_prev = kprev[...].astype(BF16)
    v_prev = vprev[...].astype(BF16)
    tail = utail[...]
    def front(h):
        ps = slice(TM_SUB * h, TM_SUB * (h + 1))
        with_dec = h == 0
        if with_dec:
            x = jnp.concatenate([x_ref[ps, :], xs_ref[...]], axis=0)
            cos = jnp.concatenate([cos_ref[ps, :], coss_ref[...]], axis=0)
            sin = jnp.concatenate([sin_ref[ps, :], sins_ref[...]], axis=0)
        else:
            x = x_ref[ps, :]
            cos = cos_ref[ps, :]
            sin = sin_ref[ps, :]
        xn = _rms_rows(x, nw).astype(BF16)

        q = _head_rms_split(_dot(xn, wq_ref[...]), bd, qnw_ref[...])
        q = _rope_split(q, cos * Q_SCALE, sin * Q_SCALE)
        kv = _dot(xn, wkv_ref[...])
        k = _rope_split(_head_rms_split(kv[:, :KV_WIDTH], bd, knw), cos, sin)
        v = kv[:, KV_WIDTH:]
        kb = k[:TM_SUB].astype(BF16)
        vb = v[:TM_SUB].astype(BF16)
        return ps, with_dec, x, xn, q, k, v, kb, vb

    def mid(h, fr, k_prev, v_prev, tail):
        ps, with_dec, x, xn, q, k, v, kb, vb = fr
        first = (i == 0) if h == 0 else False
        o_a = _prompt_attention(q[:TM_SUB].astype(BF16), kb, vb, k_prev, v_prev, sinks_ref, first)
        if with_dec:
            elems = _decode_scores(q[TM_SUB:], k[TM_SUB:], v[TM_SUB:], ck_ref, cv_ref, nck_ref,
                                   ncv_ref)
        c, hc, b, z_b = [_dot(xn, wg_s[:, lo - G_B:lo - G_B + D_MODEL])
                         for lo in (G_C, G_H, G_B, G_ZB)]
        if with_dec:
            o_a = jnp.concatenate([o_a, _decode_outputs(elems, sinks_ref)], axis=0)
        k_last = k[TM_SUB - BLOCK:TM_SUB]
        v_last = v[TM_SUB - BLOCK:TM_SUB]

        u_all = c * hc
        u = u_all[:TM_SUB]
        prev1 = tail[SUBLANES - 1:SUBLANES]
        prev2 = tail[SUBLANES - 2:SUBLANES - 1]
        r1 = pltpu.roll(u, 1, 0)
        r2 = pltpu.roll(u, 2, 0)
        um1 = jnp.concatenate(
            [jnp.where(row8 == 0, prev1, r1[:SUBLANES]), r1[SUBLANES:]], axis=0)
        um2 = jnp.concatenate(
            [jnp.where(row8 == 0, prev2, jnp.where(row8 == 1, prev1, r2[:SUBLANES])),
             r2[SUBLANES:]], axis=0)
        conv = cw[0:1] * um2 + cw[1:2] * um1 + cw[2:3] * u
        tail = u[TM_SUB - SUBLANES:]
        if with_dec:
            u3 = u_all[TM_SUB:].reshape(DEC_PER_STEP, DEC_SEQ, D_MODEL)
            st = st_ref[...]
            s_m2 = st[:, 0:1, :]
            s_m1 = st[:, 1:2, :]
            us1 = jnp.where(t_io == 0, s_m1, pltpu.roll(u3, 1, 1))
            us2 = jnp.where(t_io == 0, s_m2, jnp.where(t_io == 1, s_m1, pltpu.roll(u3, 2, 1)))
            conv_s = (cw[0:1] * us2 + cw[1:2] * us1 + cw[2:3] * u3).reshape(TS, D_MODEL)
            ncs_ref[...] = u3[:, DEC_SEQ - (CONV_K - 1):, :]
            conv = jnp.concatenate([conv, conv_s], axis=0)
        return o_a, conv, b, z_b, k_last, v_last, tail

    def back(fr, md):
        ps, with_dec, x, xn = fr[:4]
        o_a, conv, b, z_b = md[:4]
        z_a = _dot(xn, wza_ref[...])
        br_a = _dot((o_a * _silu(z_a)).astype(BF16), wpa_ref[...])
        br_b = _dot(((b * conv) * _silu(z_b)).astype(BF16), wpb_ref[...])
        g_a = _dot(xn, wg_s[:, G_GA - G_B:G_GB - G_B])
        g_b = _dot(xn, wg_s[:, G_GB - G_B:G_END - G_B])
        mixed = (jax.nn.sigmoid(g_a) * br_a + jax.nn.sigmoid(g_b) * br_b).astype(BF16)
        y = x + _dot(mixed, wo_ref[...])
        y_ref[ps, :] = y[:TM_SUB]
        if with_dec:
            ys_ref[...] = y[TM_SUB:]

    fronts = {0: front(0)}
    mids = {}
    for h in range(SUB_TILES):
        if h:
            k_prev = fronts[h - 1][7][TM_SUB - BLOCK:]
            v_prev = fronts[h - 1][8][TM_SUB - BLOCK:]
            tail = mids[h - 1][6]
        mids[h] = mid(h, fronts[h], k_prev, v_prev, tail)
        if h + 1 < SUB_TILES:
            fronts[h + 1] = front(h + 1)
        back(fronts[h], mids[h])
    k_last, v_last, tail = mids[SUB_TILES - 1][4:7]

    kprev[...] = k_last
    vprev[...] = v_last
    utail[...] = tail
    nkt_ref[...] = _rows_split_to_natural(k_last.T)
    nvt_ref[...] = v_last.T
    nc_ref[...] = tail[SUBLANES - (CONV_K - 1):]


def _const_spec(shape):
    zeros = (0,) * len(shape)
    return pl.BlockSpec(shape, lambda i: zeros, pipeline_mode=pl.Buffered(1))


def _layer_call(x, xs, tabs, ck, cv, st, w_in, meta, tabs0, tabs_s, nw, qnw, knw, cw, sinks,
                wq, wk, wza, wpa, wpb, wo, bd):
    tm = TM_PROMPT
    assert N_STEPS * DEC_PER_STEP == DEC_BATCH and TS % (2 * SUBLANES) == 0 and TS <= LANES
    assert TM_SUB % BLOCK == 0
    row_spec = lambda rows, w: pl.BlockSpec((rows, w), lambda i: (i, 0))
    cache_spec = pl.BlockSpec((DEC_PER_STEP, KV_WIDTH, WINDOW), lambda i: (i, 0, 0))
    st_spec = pl.BlockSpec((DEC_PER_STEP, CONV_K - 1, D_MODEL), lambda i: (i, 0, 0))
    consts = (meta, tabs0[0], tabs0[1], tabs_s[0], tabs_s[1], nw, qnw, knw, cw,
              wq, wk, wza, wpa, wpb, wo, bd)
    in_specs = [pl.BlockSpec(memory_space=pltpu.SMEM),
                row_spec(tm, D_MODEL), row_spec(TS, D_MODEL), row_spec(tm, LANES), row_spec(tm, LANES),
                cache_spec, cache_spec, st_spec, pl.BlockSpec(memory_space=pl.ANY)]
    in_specs += [_const_spec(a.shape) for a in consts]
    resident = lambda shape: pl.BlockSpec(shape, lambda i: (0, 0))
    out_specs = [row_spec(tm, D_MODEL), row_spec(TS, D_MODEL),
                 resident((KV_WIDTH, BLOCK)), resident((KV_WIDTH, BLOCK)),
                 resident((CONV_K - 1, D_MODEL)), cache_spec, cache_spec, st_spec]
    cache_sds = jax.ShapeDtypeStruct((DEC_BATCH, KV_WIDTH, WINDOW), F32)
    out_shape = [jax.ShapeDtypeStruct((SEQ, D_MODEL), F32),
                 jax.ShapeDtypeStruct((DEC_BATCH * DEC_SEQ, D_MODEL), F32),
                 jax.ShapeDtypeStruct((KV_WIDTH, BLOCK), F32),
                 jax.ShapeDtypeStruct((KV_WIDTH, BLOCK), F32),
                 jax.ShapeDtypeStruct((CONV_K - 1, D_MODEL), F32), cache_sds, cache_sds,
                 jax.ShapeDtypeStruct((DEC_BATCH, CONV_K - 1, D_MODEL), F32)]
    return pl.pallas_call(
        _layer_kernel,
        grid=(N_STEPS,),
        in_specs=in_specs,
        out_specs=out_specs,
        out_shape=out_shape,
        scratch_shapes=[pltpu.VMEM((BLOCK, KV_WIDTH), F32),
                        pltpu.VMEM((BLOCK, KV_WIDTH), F32),
                        pltpu.VMEM((SUBLANES, D_MODEL), F32),
                        pltpu.VMEM((D_MODEL, G_END - G_B), BF16),
                        pltpu.VMEM((GW_SLOTS, GW_CHUNK, G_END - G_B), F32),
                        pltpu.SemaphoreType.DMA((GW_SLOTS,))],
        compiler_params=pltpu.CompilerParams(
            dimension_semantics=("arbitrary",), vmem_limit_bytes=VMEM_LIMIT),
        name="hybrid_layer",
    )(sinks, x, xs, tabs[0], tabs[1], ck, cv, st, w_in, *consts)


def _rope_tables(positions):
    inv = np.power(ROPE_THETA, -np.arange(HALF, dtype=np.float64) * (2.0 / HEAD_DIM))
    ang = np.asarray(positions, dtype=np.float64)[:, None] * inv[None, :]
    cos = np.tile(np.cos(ang), (1, LANES // HALF))
    sin = np.tile(np.sin(ang), (1, LANES // HALF))
    return jnp.asarray(cos, F32), jnp.asarray(sin, F32)


def _head_mean_matrix():
    grp = np.arange(256) // HALF
    return jnp.asarray((grp[:, None] == grp[None, :]) / float(HEAD_DIM), BF16)


def kernel(x_prompt, x_sample, cache_k, cache_v, state_conv, meta_tokens, norm_w, w_in,
           q_norm_w, k_norm_w, sinks, conv_w, w_proj_a, w_proj_b, w_out):
    assert x_prompt.shape == (1, SEQ, D_MODEL) and x_sample.shape == (DEC_BATCH, DEC_SEQ, D_MODEL)
    assert w_in.shape[0] == 1, "single layer"
    wg = w_in[0, :, :G_B].astype(BF16)
    wq = (wg[:, :W_K].reshape(D_MODEL, N_KV_HEADS, GROUP, 2, HALF)
          .transpose(0, 2, 3, 1, 4).reshape(D_MODEL, D_MODEL))
    wk = (wg[:, W_K:W_V].reshape(D_MODEL, N_KV_HEADS, 2, HALF)
          .transpose(0, 2, 1, 3).reshape(D_MODEL, KV_WIDTH))
    wk = jnp.concatenate([wk, wg[:, W_V:W_ZA]], axis=1)
    wza = (wg[:, W_ZA:G_B].reshape(D_MODEL, N_KV_HEADS, GROUP, HEAD_DIM)
           .transpose(0, 2, 1, 3).reshape(D_MODEL, D_MODEL))
    wpa = (w_proj_a[0].astype(BF16).reshape(N_KV_HEADS, GROUP, HEAD_DIM, D_MODEL)
           .transpose(1, 0, 2, 3).reshape(D_MODEL, D_MODEL))
    wpb = w_proj_b[0].astype(BF16)
    wo = w_out[0].astype(BF16)

    nw = norm_w[0].reshape(1, D_MODEL)
    split = lambda v: jnp.broadcast_to(v.reshape(2, 1, HALF), (2, N_KV_HEADS, HALF)).reshape(1, 256)

    to_tiles = lambda c: c[0].transpose(0, 2, 3, 1).reshape(DEC_BATCH, KV_WIDTH, WINDOW)
    from_tiles = lambda c: (c.reshape(DEC_BATCH, N_KV_HEADS, HEAD_DIM, WINDOW)
                            .transpose(0, 3, 1, 2)[None])
    from_tile = lambda t: t.reshape(N_KV_HEADS, HEAD_DIM, WINDOW).transpose(2, 0, 1)

    tabs_p = _rope_tables(np.arange(SEQ) + N_META)
    tabs_0 = _rope_tables(np.arange(BLOCK) - LEAD)
    tabs_s = _rope_tables(PAST_LEN + (np.arange(TS) % DEC_SEQ))
    y_p, y_s, nkt_p, nvt_p, nc_p, nk_s, nv_s, nc_s = _layer_call(
        x_prompt[0], x_sample.reshape(DEC_BATCH * DEC_SEQ, D_MODEL), tabs_p,
        to_tiles(cache_k), to_tiles(cache_v), state_conv[0], w_in,
        meta_tokens.astype(x_prompt.dtype), tabs_0, tabs_s,
        nw, split(q_norm_w), split(k_norm_w), conv_w, sinks,
        wq, wk, wza, wpa, wpb, wo, _head_mean_matrix())

    return (y_p.reshape(1, SEQ, D_MODEL),
            y_s.reshape(DEC_BATCH, DEC_SEQ, D_MODEL),
            from_tile(nkt_p).reshape(1, 1, WINDOW, N_KV_HEADS, HEAD_DIM),
            from_tile(nvt_p).reshape(1, 1, WINDOW, N_KV_HEADS, HEAD_DIM),
            nc_p.reshape(1, 1, CONV_K - 1, D_MODEL),
            from_tiles(nk_s),
            from_tiles(nv_s),
            nc_s.reshape(1, DEC_BATCH, CONV_K - 1, D_MODEL))
```

```python
import functools

import numpy as np
import jax
import jax.numpy as jnp
from jax import lax
from jax.experimental import pallas as pl
from jax.experimental.pallas import tpu as pltpu

D_MODEL = 1024
SEQ = 16384
DEC_BATCH = 128
DEC_SEQ = 8
PAST_LEN = 16384
N_HEADS = 16
N_KV_HEADS = 4
GROUP = N_HEADS // N_KV_HEADS
HEAD_DIM = 64
HALF = HEAD_DIM // 2
KV_WIDTH = N_KV_HEADS * HEAD_DIM
WINDOW = 128
BLOCK = 128
ROPE_THETA = 10000.0
CONV_K = 3
N_META = 16
LEAD = BLOCK - N_META
EPS = 1e-6
NEG_INF = -1e30
Q_SCALE = HEAD_DIM ** -0.5

W_K = D_MODEL
W_V = W_K + KV_WIDTH
W_ZA = W_V + KV_WIDTH
G_B = W_ZA + D_MODEL
G_C = G_B + D_MODEL
G_H = G_C + D_MODEL
G_ZB = G_H + D_MODEL
G_GA = G_ZB + D_MODEL
G_GB = G_GA + D_MODEL
G_END = G_GB + D_MODEL

LANES = 128
SUBLANES = 8
VMEM_BYTES_V7X = 64 * 1024 * 1024

TM_PROMPT = 512
N_STEPS = SEQ // TM_PROMPT
DEC_PER_STEP = DEC_BATCH // N_STEPS
TS = DEC_PER_STEP * DEC_SEQ
GW_CHUNK = 64
GW_SLOTS = 4
SUB_TILES = 2
TM_SUB = TM_PROMPT // SUB_TILES
VMEM_LIMIT = VMEM_BYTES_V7X - 4 * 1024 * 1024

BF16 = jnp.bfloat16
F32 = jnp.float32


def _dot(a, b):
    return jnp.dot(a, b, preferred_element_type=F32)


def _dot_nt(a, b):
    return lax.dot_general(a, b, (((1,), (1,)), ((), ())), preferred_element_type=F32)


def _rms_rows(x, w):
    ms = jnp.mean(x * x, axis=-1, keepdims=True)
    return (x * lax.rsqrt(ms + EPS)) * w


def _head_rms_split(t, bd, w):
    n = t.shape[1] // 256
    r = t.shape[0]
    halves = [(t[:, 256 * c:256 * c + LANES], t[:, 256 * c + LANES:256 * (c + 1)]) for c in range(n)]
    pieces = [a * a + b * b for a, b in halves]
    if n == 1:
        pieces = [pieces[0], pieces[0]]
    pair_ms = [_dot(jnp.concatenate(pieces[2 * i:2 * i + 2], axis=1).astype(BF16), bd)
               for i in range(len(pieces) // 2)]
    ms = [pair_ms[i // 2][:, LANES * (i % 2):LANES * (i % 2 + 1)] for i in range(n)]
    out = []
    for (a, b), m in zip(halves, ms):
        scale = lax.rsqrt(m + EPS)
        out.append((a * scale) * w[:, :LANES])
        out.append((b * scale) * w[:, LANES:])
    return jnp.concatenate(out, axis=1)


def _rope_split(t, cos, sin):
    out = []
    for c in range(t.shape[1] // 256):
        a = t[:, 256 * c:256 * c + LANES]
        b = t[:, 256 * c + LANES:256 * (c + 1)]
        out.append(a * cos - b * sin)
        out.append(b * cos + a * sin)
    return jnp.concatenate(out, axis=1)


def _lane_group_masks(group_of_lane, dtype):
    return [(group_of_lane == g).astype(dtype) for g in range(N_KV_HEADS)]


def _softmax_parts(s, sk):
    m = jnp.maximum(jnp.max(s, axis=-1, keepdims=True), sk)
    p = jnp.exp(s - m)
    den = jnp.sum(p, axis=-1, keepdims=True) + jnp.exp(sk - m)
    return p, 1.0 / den


def _silu(z):
    return z * jax.nn.sigmoid(z)


def _rows_split_to_natural(t):
    return jnp.concatenate(
        [t[LANES * half + HALF * g:LANES * half + HALF * (g + 1)]
         for g in range(N_KV_HEADS) for half in range(2)], axis=0)


def _rows_natural_to_split(t):
    return jnp.concatenate(
        [t[HEAD_DIM * g + HALF * half:HEAD_DIM * g + HALF * (half + 1)]
         for half in range(2) for g in range(N_KV_HEADS)], axis=0)


def _prompt_attention(q, kb, vb, kprev, vprev, sinks_ref, first_step):
    tm = q.shape[0]
    lane = lax.broadcasted_iota(jnp.int32, (1, 2 * BLOCK), 1)
    kmasks = _lane_group_masks((lane & (LANES - 1)) >> 5, BF16)
    vmasks = _lane_group_masks(lane >> 6, BF16)
    vgroup = lane >> 6
    r_io = lax.broadcasted_iota(jnp.int32, (BLOCK, 2 * BLOCK), 0)
    c_io = lax.broadcasted_iota(jnp.int32, (BLOCK, 2 * BLOCK), 1)
    band = (c_io >= r_io) & (c_io <= r_io + WINDOW)
    first_band = band & (c_io >= jnp.where(first_step, LEAD, 0))

    o_blocks = []
    for b in range(tm // BLOCK):
        lo = b * BLOCK
        if b == 0:
            kpair = jnp.concatenate([kprev, kb[:BLOCK]], axis=0)
            vpair = jnp.concatenate([vprev, vb[:BLOCK]], axis=0)
            mask = first_band
        else:
            kpair = kb[lo - BLOCK:lo + BLOCK]
            vpair = vb[lo - BLOCK:lo + BLOCK]
            mask = band
        kbd = jnp.concatenate([kpair * km for km in kmasks], axis=0)
        vbd = jnp.concatenate([vpair * vm for vm in vmasks], axis=0)
        qs = jnp.concatenate([q[lo:lo + BLOCK, 256 * j:256 * (j + 1)] for j in range(GROUP)],
                             axis=0)
        s = _dot_nt(qs, kbd)
        p_rows = []
        mults = []
        for j in range(GROUP):
            p_cols = []
            invs = []
            for g in range(N_KV_HEADS):
                sl = s[BLOCK * j:BLOCK * (j + 1), 256 * g:256 * (g + 1)]
                sl = jnp.where(mask, sl, NEG_INF)
                p, inv = _softmax_parts(sl, sinks_ref[0, GROUP * g + j])
                p_cols.append(p.astype(BF16))
                invs.append(inv)
            p_rows.append(jnp.concatenate(p_cols, axis=1))
            mults.append(jnp.where(vgroup == 0, invs[0],
                                   jnp.where(vgroup == 1, invs[1],
                                             jnp.where(vgroup == 2, invs[2], invs[3]))))
        p_all = jnp.concatenate(p_rows, axis=0)
        half_k = 2 * 2 * BLOCK
        o = _dot(p_all[:, :half_k], vbd[:half_k]) + _dot(p_all[:, half_k:], vbd[half_k:])
        o_blocks.append(jnp.concatenate(
            [o[BLOCK * j:BLOCK * (j + 1)] * mults[j] for j in range(GROUP)], axis=1))
    return jnp.concatenate(o_blocks, axis=0)


def _decode_scores(qs, ks, vs, ck_ref, cv_ref, nck_ref, ncv_ref):
    t_len = DEC_SEQ
    n_exp = GROUP * N_KV_HEADS * t_len
    pad = jnp.zeros((LANES - qs.shape[0], KV_WIDTH), F32)
    kst = jnp.concatenate([ks, pad], axis=0).T
    vst = jnp.concatenate([vs, pad], axis=0).T
    kst_nat = _rows_split_to_natural(kst)
    kst_b = kst.astype(BF16)
    vst_b = vst.astype(BF16)

    lane = lax.broadcasted_iota(jnp.int32, (1, KV_WIDTH), 1)
    kgroup = (lane & (LANES - 1)) >> 5
    t_io = lax.broadcasted_iota(jnp.int32, (n_exp, WINDOW), 0) & (t_len - 1)
    c_io = lax.broadcasted_iota(jnp.int32, (n_exp, WINDOW), 1)
    old_ok = c_io >= t_io
    new_lane = lax.broadcasted_iota(jnp.int32, (KV_WIDTH, WINDOW), 1) >= WINDOW - t_len

    elems = []
    for b in range(qs.shape[0] // t_len):
        r0 = b * t_len
        qn = qs[r0:r0 + t_len]
        q_exp = jnp.concatenate(
            [jnp.where(kgroup == g, qn[:, 256 * j:256 * (j + 1)], 0.0)
             for j in range(GROUP) for g in range(N_KV_HEADS)], axis=0).astype(BF16)
        kt_old = ck_ref[b]
        vt_old = cv_ref[b]
        keys = jnp.concatenate([_rows_natural_to_split(kt_old).astype(BF16), kst_b], axis=1)
        vals = jnp.concatenate([vt_old.astype(BF16), vst_b], axis=1)
        s = _dot(q_exp, keys)
        new_ok = (c_io >= r0) & (c_io - t_io <= r0)
        s = jnp.where(jnp.concatenate([old_ok, new_ok], axis=1), s, NEG_INF)
        elems.append((s, vals))
        shift_new = WINDOW - t_len - r0
        nck_ref[b] = jnp.where(new_lane, pltpu.roll(kst_nat, shift_new, 1),
                               pltpu.roll(kt_old, WINDOW - t_len, 1))
        ncv_ref[b] = jnp.where(new_lane, pltpu.roll(vst, shift_new, 1),
                               pltpu.roll(vt_old, WINDOW - t_len, 1))
    return elems


def _decode_outputs(elems, sinks_ref):
    t_len = DEC_SEQ
    vgroup = lax.broadcasted_iota(jnp.int32, (1, KV_WIDTH), 1) >> 6
    skcol = jnp.concatenate(
        [jnp.full((t_len, 1), sinks_ref[0, GROUP * g + j], F32)
         for j in range(GROUP) for g in range(N_KV_HEADS)], axis=0)
    outs = []
    for s, vals in elems:
        p, inv = _softmax_parts(s, skcol)
        o = _dot_nt(p.astype(BF16), vals) * inv
        o_j = []
        for j in range(GROUP):
            acc = None
            for g in range(N_KV_HEADS):
                r = (j * N_KV_HEADS + g) * t_len
                piece = jnp.where(vgroup == g, o[r:r + t_len], 0.0)
                acc = piece if acc is None else acc + piece
            o_j.append(acc)
        outs.append(jnp.concatenate(o_j, axis=1))
    return jnp.concatenate(outs, axis=0)


def _gate_weight_copy(w_hbm, wstage, wsem, chunk):
    slot = chunk % GW_SLOTS
    return pltpu.make_async_copy(
        w_hbm.at[0, pl.ds(GW_CHUNK * chunk, GW_CHUNK), pl.ds(G_B, G_END - G_B)],
        wstage.at[slot], wsem.at[slot])


def _load_gate_weights(w_hbm, wg_s, wstage, wsem):
    n_chunks = D_MODEL // GW_CHUNK
    ahead = GW_SLOTS - 1
    for c in range(ahead):
        _gate_weight_copy(w_hbm, wstage, wsem, c).start()
    for c in range(n_chunks):
        if c + ahead < n_chunks:
            _gate_weight_copy(w_hbm, wstage, wsem, c + ahead).start()
        _gate_weight_copy(w_hbm, wstage, wsem, c).wait()
        wg_s[GW_CHUNK * c:GW_CHUNK * (c + 1), :] = wstage[c % GW_SLOTS].astype(BF16)


def _layer_kernel(sinks_ref, x_ref, xs_ref, cos_ref, sin_ref, ck_ref, cv_ref, st_ref, w_hbm,
                  meta_ref, cos0_ref, sin0_ref, coss_ref, sins_ref, nw_ref, qnw_ref, knw_ref, cw_ref,
                  wq_ref, wkv_ref, wza_ref, wpa_ref, wpb_ref, wo_ref, bd_ref,
                  y_ref, ys_ref, nkt_ref, nvt_ref, nc_ref, nck_ref, ncv_ref, ncs_ref,
                  kprev, vprev, utail, wg_s, wstage, wsem):
    i = pl.program_id(0)
    nw = nw_ref[...]
    bd = bd_ref[...]
    knw = knw_ref[...]

    @pl.when(i == 0)
    def _init():
        _load_gate_weights(w_hbm, wg_s, wstage, wsem)
        x0 = jnp.concatenate([jnp.zeros((LEAD, D_MODEL), F32), meta_ref[...]], axis=0)
        xn0 = _rms_rows(x0, nw).astype(BF16)
        kv0 = _dot(xn0, wkv_ref[...])
        k0 = _rope_split(_head_rms_split(kv0[:, :KV_WIDTH], bd, knw), cos0_ref[...], sin0_ref[...])
        v0 = kv0[:, KV_WIDTH:]
        c0 = _dot(xn0, wg_s[:, G_C - G_B:G_H - G_B])
        h0 = _dot(xn0, wg_s[:, G_H - G_B:G_ZB - G_B])
        row = lax.broadcasted_iota(jnp.int32, (BLOCK, D_MODEL), 0)
        u0 = jnp.where(row >= LEAD, c0 * h0, 0.0)
        kprev[...] = k0
        vprev[...] = v0
        utail[...] = u0[BLOCK - SUBLANES:BLOCK]

    cw = cw_ref[0]
    row8 = lax.broadcasted_iota(jnp.int32, (SUBLANES, D_MODEL), 0)
    t_io = lax.broadcasted_iota(jnp.int32, (DEC_PER_STEP, DEC_SEQ, D_MODEL), 1)
    k_prev = kprev[...].astype(BF16)
    v_prev = vprev[...].astype(BF16)
    tail = utail[...]
    def front(h):
        ps = slice(TM_SUB * h, TM_SUB * (h + 1))
        with_dec = h == SUB_TILES - 1
        if with_dec:
            x = jnp.concatenate([x_ref[ps, :], xs_ref[...]], axis=0)
            cos = jnp.concatenate([cos_ref[ps, :], coss_ref[...]], axis=0)
            sin = jnp.concatenate([sin_ref[ps, :], sins_ref[...]], axis=0)
        else:
            x = x_ref[ps, :]
            cos = cos_ref[ps, :]
            sin = sin_ref[ps, :]
        xn = _rms_rows(x, nw).astype(BF16)

        q = _head_rms_split(_dot(xn, wq_ref[...]), bd, qnw_ref[...])
        q = _rope_split(q, cos * Q_SCALE, sin * Q_SCALE)
        kv = _dot(xn, wkv_ref[...])
        k = _rope_split(_head_rms_split(kv[:, :KV_WIDTH], bd, knw), cos, sin)
        v = kv[:, KV_WIDTH:]
        kb = k[:TM_SUB].astype(BF16)
        vb = v[:TM_SUB].astype(BF16)
        return ps, with_dec, x, xn, q, k, v, kb, vb

    def mid(h, fr, k_prev, v_prev, tail):
        ps, with_dec, x, xn, q, k, v, kb, vb = fr
        first = (i == 0) if h == 0 else False
        o_a = _prompt_attention(q[:TM_SUB].astype(BF16), kb, vb, k_prev, v_prev, sinks_ref, first)
        if with_dec:
            elems = _decode_scores(q[TM_SUB:], k[TM_SUB:], v[TM_SUB:], ck_ref, cv_ref, nck_ref,
                                   ncv_ref)
        c, hc, b, z_b = [_dot(xn, wg_s[:, lo - G_B:lo - G_B + D_MODEL])
                         for lo in (G_C, G_H, G_B, G_ZB)]
        if with_dec:
            o_a = jnp.concatenate([o_a, _decode_outputs(elems, sinks_ref)], axis=0)
        k_last = k[TM_SUB - BLOCK:TM_SUB]
        v_last = v[TM_SUB - BLOCK:TM_SUB]

        u_all = c * hc
        u = u_all[:TM_SUB]
        prev1 = tail[SUBLANES - 1:SUBLANES]
        prev2 = tail[SUBLANES - 2:SUBLANES - 1]
        r1 = pltpu.roll(u, 1, 0)
        r2 = pltpu.roll(u, 2, 0)
        um1 = jnp.concatenate(
            [jnp.where(row8 == 0, prev1, r1[:SUBLANES]), r1[SUBLANES:]], axis=0)
        um2 = jnp.concatenate(
            [jnp.where(row8 == 0, prev2, jnp.where(row8 == 1, prev1, r2[:SUBLANES])),
             r2[SUBLANES:]], axis=0)
        conv = cw[0:1] * um2 + cw[1:2] * um1 + cw[2:3] * u
        tail = u[TM_SUB - SUBLANES:]
        if with_dec:
            u3 = u_all[TM_SUB:].reshape(DEC_PER_STEP, DEC_SEQ, D_MODEL)
            st = st_ref[...]
            s_m2 = st[:, 0:1, :]
            s_m1 = st[:, 1:2, :]
            us1 = jnp.where(t_io == 0, s_m1, pltpu.roll(u3, 1, 1))
            us2 = jnp.where(t_io == 0, s_m2, jnp.where(t_io == 1, s_m1, pltpu.roll(u3, 2, 1)))
            conv_s = (cw[0:1] * us2 + cw[1:2] * us1 + cw[2:3] * u3).reshape(TS, D_MODEL)
            ncs_ref[...] = u3[:, DEC_SEQ - (CONV_K - 1):, :]
            conv = jnp.concatenate([conv, conv_s], axis=0)
        return o_a, conv, b, z_b, k_last, v_last, tail

    def back(fr, md):
        ps, with_dec, x, xn = fr[:4]
        o_a, conv, b, z_b = md[:4]
        z_a = _dot(xn, wza_ref[...])
        g_a = _dot(xn, wg_s[:, G_GA - G_B:G_GB - G_B])
        g_b = _dot(xn, wg_s[:, G_GB - G_B:G_END - G_B])
        br_b = _dot(((b * conv) * _silu(z_b)).astype(BF16), wpb_ref[...])
        br_a = _dot((o_a * _silu(z_a)).astype(BF16), wpa_ref[...])
        mixed = (jax.nn.sigmoid(g_a) * br_a + jax.nn.sigmoid(g_b) * br_b).astype(BF16)
        y = x + _dot(mixed, wo_ref[...])
        y_ref[ps, :] = y[:TM_SUB]
        if with_dec:
            ys_ref[...] = y[TM_SUB:]

    fronts = {0: front(0)}
    mids = {}
    for h in range(SUB_TILES):
        if h:
            k_prev = fronts[h - 1][7][TM_SUB - BLOCK:]
            v_prev = fronts[h - 1][8][TM_SUB - BLOCK:]
            tail = mids[h - 1][6]
        mids[h] = mid(h, fronts[h], k_prev, v_prev, tail)
        if h + 1 < SUB_TILES:
            fronts[h + 1] = front(h + 1)
        back(fronts[h], mids[h])
    k_last, v_last, tail = mids[SUB_TILES - 1][4:7]

    kprev[...] = k_last
    vprev[...] = v_last
    utail[...] = tail
    nkt_ref[...] = _rows_split_to_natural(k_last.T)
    nvt_ref[...] = v_last.T
    nc_ref[...] = tail[SUBLANES - (CONV_K - 1):]


def _const_spec(shape):
    zeros = (0,) * len(shape)
    return pl.BlockSpec(shape, lambda i: zeros, pipeline_mode=pl.Buffered(1))


def _layer_call(x, xs, tabs, ck, cv, st, w_in, meta, tabs0, tabs_s, nw, qnw, knw, cw, sinks,
                wq, wk, wza, wpa, wpb, wo, bd):
    tm = TM_PROMPT
    assert N_STEPS * DEC_PER_STEP == DEC_BATCH and TS % (2 * SUBLANES) == 0 and TS <= LANES
    assert TM_SUB % BLOCK == 0
    row_spec = lambda rows, w: pl.BlockSpec((rows, w), lambda i: (i, 0))
    cache_spec = pl.BlockSpec((DEC_PER_STEP, KV_WIDTH, WINDOW), lambda i: (i, 0, 0))
    st_spec = pl.BlockSpec((DEC_PER_STEP, CONV_K - 1, D_MODEL), lambda i: (i, 0, 0))
    consts = (meta, tabs0[0], tabs0[1], tabs_s[0], tabs_s[1], nw, qnw, knw, cw,
              wq, wk, wza, wpa, wpb, wo, bd)
    in_specs = [pl.BlockSpec(memory_space=pltpu.SMEM),
                row_spec(tm, D_MODEL), row_spec(TS, D_MODEL), row_spec(tm, LANES), row_spec(tm, LANES),
                cache_spec, cache_spec, st_spec, pl.BlockSpec(memory_space=pl.ANY)]
    in_specs += [_const_spec(a.shape) for a in consts]
    resident = lambda shape: pl.BlockSpec(shape, lambda i: (0, 0))
    out_specs = [row_spec(tm, D_MODEL), row_spec(TS, D_MODEL),
                 resident((KV_WIDTH, BLOCK)), resident((KV_WIDTH, BLOCK)),
                 resident((CONV_K - 1, D_MODEL)), cache_spec, cache_spec, st_spec]
    cache_sds = jax.ShapeDtypeStruct((DEC_BATCH, KV_WIDTH, WINDOW), F32)
    out_shape = [jax.ShapeDtypeStruct((SEQ, D_MODEL), F32),
                 jax.ShapeDtypeStruct((DEC_BATCH * DEC_SEQ, D_MODEL), F32),
                 jax.ShapeDtypeStruct((KV_WIDTH, BLOCK), F32),
                 jax.ShapeDtypeStruct((KV_WIDTH, BLOCK), F32),
                 jax.ShapeDtypeStruct((CONV_K - 1, D_MODEL), F32), cache_sds, cache_sds,
                 jax.ShapeDtypeStruct((DEC_BATCH, CONV_K - 1, D_MODEL), F32)]
    return pl.pallas_call(
        _layer_kernel,
        grid=(N_STEPS,),
        in_specs=in_specs,
        out_specs=out_specs,
        out_shape=out_shape,
        scratch_shapes=[pltpu.VMEM((BLOCK, KV_WIDTH), F32),
                        pltpu.VMEM((BLOCK, KV_WIDTH), F32),
                        pltpu.VMEM((SUBLANES, D_MODEL), F32),
                        pltpu.VMEM((D_MODEL, G_END - G_B), BF16),
                        pltpu.VMEM((GW_SLOTS, GW_CHUNK, G_END - G_B), F32),
                        pltpu.SemaphoreType.DMA((GW_SLOTS,))],
        compiler_params=pltpu.CompilerParams(
            dimension_semantics=("arbitrary",), vmem_limit_bytes=VMEM_LIMIT),
        name="hybrid_layer",
    )(sinks, x, xs, tabs[0], tabs[1], ck, cv, st, w_in, *consts)


def _rope_tables(positions):
    inv = np.power(ROPE_THETA, -np.arange(HALF, dtype=np.float64) * (2.0 / HEAD_DIM))
    ang = np.asarray(positions, dtype=np.float64)[:, None] * inv[None, :]
    cos = np.tile(np.cos(ang), (1, LANES // HALF))
    sin = np.tile(np.sin(ang), (1, LANES // HALF))
    return jnp.asarray(cos, F32), jnp.asarray(sin, F32)


def _head_mean_matrix():
    grp = np.arange(256) // HALF
    return jnp.asarray((grp[:, None] == grp[None, :]) / float(HEAD_DIM), BF16)


def kernel(x_prompt, x_sample, cache_k, cache_v, state_conv, meta_tokens, norm_w, w_in,
           q_norm_w, k_norm_w, sinks, conv_w, w_proj_a, w_proj_b, w_out):
    assert x_prompt.shape == (1, SEQ, D_MODEL) and x_sample.shape == (DEC_BATCH, DEC_SEQ, D_MODEL)
    assert w_in.shape[0] == 1, "single layer"
    wg = w_in[0, :, :G_B].astype(BF16)
    wq = (wg[:, :W_K].reshape(D_MODEL, N_KV_HEADS, GROUP, 2, HALF)
          .transpose(0, 2, 3, 1, 4).reshape(D_MODEL, D_MODEL))
    wk = (wg[:, W_K:W_V].reshape(D_MODEL, N_KV_HEADS, 2, HALF)
          .transpose(0, 2, 1, 3).reshape(D_MODEL, KV_WIDTH))
    wk = jnp.concatenate([wk, wg[:, W_V:W_ZA]], axis=1)
    wza = (wg[:, W_ZA:G_B].reshape(D_MODEL, N_KV_HEADS, GROUP, HEAD_DIM)
           .transpose(0, 2, 1, 3).reshape(D_MODEL, D_MODEL))
    wpa = (w_proj_a[0].astype(BF16).reshape(N_KV_HEADS, GROUP, HEAD_DIM, D_MODEL)
           .transpose(1, 0, 2, 3).reshape(D_MODEL, D_MODEL))
    wpb = w_proj_b[0].astype(BF16)
    wo = w_out[0].astype(BF16)

    nw = norm_w[0].reshape(1, D_MODEL)
    split = lambda v: jnp.broadcast_to(v.reshape(2, 1, HALF), (2, N_KV_HEADS, HALF)).reshape(1, 256)

    to_tiles = lambda c: c[0].transpose(0, 2, 3, 1).reshape(DEC_BATCH, KV_WIDTH, WINDOW)
    from_tiles = lambda c: (c.reshape(DEC_BATCH, N_KV_HEADS, HEAD_DIM, WINDOW)
                            .transpose(0, 3, 1, 2)[None])
    from_tile = lambda t: t.reshape(N_KV_HEADS, HEAD_DIM, WINDOW).transpose(2, 0, 1)

    tabs_p = _rope_tables(np.arange(SEQ) + N_META)
    tabs_0 = _rope_tables(np.arange(BLOCK) - LEAD)
    tabs_s = _rope_tables(PAST_LEN + (np.arange(TS) % DEC_SEQ))
    y_p, y_s, nkt_p, nvt_p, nc_p, nk_s, nv_s, nc_s = _layer_call(
        x_prompt[0], x_sample.reshape(DEC_BATCH * DEC_SEQ, D_MODEL), tabs_p,
        to_tiles(cache_k), to_tiles(cache_v), state_conv[0], w_in,
        meta_tokens.astype(x_prompt.dtype), tabs_0, tabs_s,
        nw, split(q_norm_w), split(k_norm_w), conv_w, sinks,
        wq, wk, wza, wpa, wpb, wo, _head_mean_matrix())

    return (y_p.reshape(1, SEQ, D_MODEL),
            y_s.reshape(DEC_BATCH, DEC_SEQ, D_MODEL),
            from_tile(nkt_p).reshape(1, 1, WINDOW, N_KV_HEADS, HEAD_DIM),
            from_tile(nvt_p).reshape(1, 1, WINDOW, N_KV_HEADS, HEAD_DIM),
            nc_p.reshape(1, 1, CONV_K - 1, D_MODEL),
            from_tiles(nk_s),
            from_tiles(nv_s),
            nc_s.reshape(1, DEC_BATCH, CONV_K - 1, D_MODEL))
```

```python
import functools

import numpy as np
import jax
import jax.numpy as jnp
from jax import lax
from jax.experimental import pallas as pl
from jax.experimental.pallas import tpu as pltpu

D_MODEL = 1024
SEQ = 16384
DEC_BATCH = 128
DEC_SEQ = 8
PAST_LEN = 16384
N_HEADS = 16
N_KV_HEADS = 4
GROUP = N_HEADS // N_KV_HEADS
HEAD_DIM = 64
HALF = HEAD_DIM // 2
KV_WIDTH = N_KV_HEADS * HEAD_DIM
WINDOW = 128
BLOCK = 128
ROPE_THETA = 10000.0
CONV_K = 3
N_META = 16
LEAD = BLOCK - N_META
EPS = 1e-6
NEG_INF = -1e30
Q_SCALE = HEAD_DIM ** -0.5

W_K = D_MODEL
W_V = W_K + KV_WIDTH
W_ZA = W_V + KV_WIDTH
G_B = W_ZA + D_MODEL
G_C = G_B + D_MODEL
G_H = G_C + D_MODEL
G_ZB = G_H + D_MODEL
G_GA = G_ZB + D_MODEL
G_GB = G_GA + D_MODEL
G_END = G_GB + D_MODEL

LANES = 128
SUBLANES = 8
VMEM_BYTES_V7X = 64 * 1024 * 1024

TM_PROMPT = 512
N_STEPS = SEQ // TM_PROMPT
DEC_PER_STEP = DEC_BATCH // N_STEPS
TS = DEC_PER_STEP * DEC_SEQ
GW_CHUNK = 64
GW_SLOTS = 4
SUB_TILES = 2
TM_SUB = TM_PROMPT // SUB_TILES
VMEM_LIMIT = VMEM_BYTES_V7X - 4 * 1024 * 1024

BF16 = jnp.bfloat16
F32 = jnp.float32


def _dot(a, b):
    return jnp.dot(a, b, preferred_element_type=F32)


def _dot_nt(a, b):
    return lax.dot_general(a, b, (((1,), (1,)), ((), ())), preferred_element_type=F32)


def _rms_rows(x, w):
    ms = jnp.mean(x * x, axis=-1, keepdims=True)
    return (x * lax.rsqrt(ms + EPS)) * w


def _head_rms_split(t, bd, w):
    n = t.shape[1] // 256
    r = t.shape[0]
    halves = [(t[:, 256 * c:256 * c + LANES], t[:, 256 * c + LANES:256 * (c + 1)]) for c in range(n)]
    pieces = [a * a + b * b for a, b in halves]
    if n == 1:
        pieces = [pieces[0], pieces[0]]
    pair_ms = [_dot(jnp.concatenate(pieces[2 * i:2 * i + 2], axis=1).astype(BF16), bd)
               for i in range(len(pieces) // 2)]
    ms = [pair_ms[i // 2][:, LANES * (i % 2):LANES * (i % 2 + 1)] for i in range(n)]
    out = []
    for (a, b), m in zip(halves, ms):
        scale = lax.rsqrt(m + EPS)
        out.append((a * scale) * w[:, :LANES])
        out.append((b * scale) * w[:, LANES:])
    return jnp.concatenate(out, axis=1)


def _rope_split(t, cos, sin):
    out = []
    for c in range(t.shape[1] // 256):
        a = t[:, 256 * c:256 * c + LANES]
        b = t[:, 256 * c + LANES:256 * (c + 1)]
        out.append(a * cos - b * sin)
        out.append(b * cos + a * sin)
    return jnp.concatenate(out, axis=1)


def _lane_group_masks(group_of_lane, dtype):
    return [(group_of_lane == g).astype(dtype) for g in range(N_KV_HEADS)]


def _softmax_parts(s, sk):
    m = jnp.maximum(jnp.max(s, axis=-1, keepdims=True), sk)
    p = jnp.exp(s - m)
    den = jnp.sum(p, axis=-1, keepdims=True) + jnp.exp(sk - m)
    return p, 1.0 / den


def _silu(z):
    return z * jax.nn.sigmoid(z)


def _rows_split_to_natural(t):
    return jnp.concatenate(
        [t[LANES * half + HALF * g:LANES * half + HALF * (g + 1)]
         for g in range(N_KV_HEADS) for half in range(2)], axis=0)


def _rows_natural_to_split(t):
    return jnp.concatenate(
        [t[HEAD_DIM * g + HALF * half:HEAD_DIM * g + HALF * (half + 1)]
         for half in range(2) for g in range(N_KV_HEADS)], axis=0)


def _prompt_attention(q, kb, vb, kprev, vprev, sinks_ref, first_step, fillers):
    tm = q.shape[0]
    lane = lax.broadcasted_iota(jnp.int32, (1, 2 * BLOCK), 1)
    kmasks = _lane_group_masks((lane & (LANES - 1)) >> 5, BF16)
    vmasks = _lane_group_masks(lane >> 6, BF16)
    vgroup = lane >> 6
    r_io = lax.broadcasted_iota(jnp.int32, (BLOCK, 2 * BLOCK), 0)
    c_io = lax.broadcasted_iota(jnp.int32, (BLOCK, 2 * BLOCK), 1)
    band = (c_io >= r_io) & (c_io <= r_io + WINDOW)
    first_band = band & (c_io >= jnp.where(first_step, LEAD, 0))

    o_blocks = []
    filled = []
    for b in range(tm // BLOCK):
        lo = b * BLOCK
        if b == 0:
            kpair = jnp.concatenate([kprev, kb[:BLOCK]], axis=0)
            vpair = jnp.concatenate([vprev, vb[:BLOCK]], axis=0)
            mask = first_band
        else:
            kpair = kb[lo - BLOCK:lo + BLOCK]
            vpair = vb[lo - BLOCK:lo + BLOCK]
            mask = band
        kbd = jnp.concatenate([kpair * km for km in kmasks], axis=0)
        vbd = jnp.concatenate([vpair * vm for vm in vmasks], axis=0)
        qs = jnp.concatenate([q[lo:lo + BLOCK, 256 * j:256 * (j + 1)] for j in range(GROUP)],
                             axis=0)
        s = _dot_nt(qs, kbd)
        filled.append(fillers[b]())
        p_rows = []
        mults = []
        for j in range(GROUP):
            p_cols = []
            invs = []
            for g in range(N_KV_HEADS):
                sl = s[BLOCK * j:BLOCK * (j + 1), 256 * g:256 * (g + 1)]
                sl = jnp.where(mask, sl, NEG_INF)
                p, inv = _softmax_parts(sl, sinks_ref[0, GROUP * g + j])
                p_cols.append(p.astype(BF16))
                invs.append(inv)
            p_rows.append(jnp.concatenate(p_cols, axis=1))
            mults.append(jnp.where(vgroup == 0, invs[0],
                                   jnp.where(vgroup == 1, invs[1],
                                             jnp.where(vgroup == 2, invs[2], invs[3]))))
        p_all = jnp.concatenate(p_rows, axis=0)
        half_k = 2 * 2 * BLOCK
        o = _dot(p_all[:, :half_k], vbd[:half_k]) + _dot(p_all[:, half_k:], vbd[half_k:])
        o_blocks.append(jnp.concatenate(
            [o[BLOCK * j:BLOCK * (j + 1)] * mults[j] for j in range(GROUP)], axis=1))
    return jnp.concatenate(o_blocks, axis=0), filled


def _decode_scores(qs, ks, vs, ck_ref, cv_ref, nck_ref, ncv_ref):
    t_len = DEC_SEQ
    n_exp = GROUP * N_KV_HEADS * t_len
    pad = jnp.zeros((LANES - qs.shape[0], KV_WIDTH), F32)
    kst = jnp.concatenate([ks, pad], axis=0).T
    vst = jnp.concatenate([vs, pad], axis=0).T
    kst_nat = _rows_split_to_natural(kst)
    kst_b = kst.astype(BF16)
    vst_b = vst.astype(BF16)

    lane = lax.broadcasted_iota(jnp.int32, (1, KV_WIDTH), 1)
    kgroup = (lane & (LANES - 1)) >> 5
    t_io = lax.broadcasted_iota(jnp.int32, (n_exp, WINDOW), 0) & (t_len - 1)
    c_io = lax.broadcasted_iota(jnp.int32, (n_exp, WINDOW), 1)
    old_ok = c_io >= t_io
    new_lane = lax.broadcasted_iota(jnp.int32, (KV_WIDTH, WINDOW), 1) >= WINDOW - t_len

    elems = []
    for b in range(qs.shape[0] // t_len):
        r0 = b * t_len
        qn = qs[r0:r0 + t_len]
        q_exp = jnp.concatenate(
            [jnp.where(kgroup == g, qn[:, 256 * j:256 * (j + 1)], 0.0)
             for j in range(GROUP) for g in range(N_KV_HEADS)], axis=0).astype(BF16)
        kt_old = ck_ref[b]
        vt_old = cv_ref[b]
        keys = jnp.concatenate([_rows_natural_to_split(kt_old).astype(BF16), kst_b], axis=1)
        vals = jnp.concatenate([vt_old.astype(BF16), vst_b], axis=1)
        s = _dot(q_exp, keys)
        new_ok = (c_io >= r0) & (c_io - t_io <= r0)
        s = jnp.where(jnp.concatenate([old_ok, new_ok], axis=1), s, NEG_INF)
        elems.append((s, vals))
        shift_new = WINDOW - t_len - r0
        nck_ref[b] = jnp.where(new_lane, pltpu.roll(kst_nat, shift_new, 1),
                               pltpu.roll(kt_old, WINDOW - t_len, 1))
        ncv_ref[b] = jnp.where(new_lane, pltpu.roll(vst, shift_new, 1),
                               pltpu.roll(vt_old, WINDOW - t_len, 1))
    return elems


def _decode_outputs(elems, sinks_ref):
    t_len = DEC_SEQ
    vgroup = lax.broadcasted_iota(jnp.int32, (1, KV_WIDTH), 1) >> 6
    skcol = jnp.concatenate(
        [jnp.full((t_len, 1), sinks_ref[0, GROUP * g + j], F32)
         for j in range(GROUP) for g in range(N_KV_HEADS)], axis=0)
    outs = []
    for s, vals in elems:
        p, inv = _softmax_parts(s, skcol)
        o = _dot_nt(p.astype(BF16), vals) * inv
        o_j = []
        for j in range(GROUP):
            acc = None
            for g in range(N_KV_HEADS):
                r = (j * N_KV_HEADS + g) * t_len
                piece = jnp.where(vgroup == g, o[r:r + t_len], 0.0)
                acc = piece if acc is None else acc + piece
            o_j.append(acc)
        outs.append(jnp.concatenate(o_j, axis=1))
    return jnp.concatenate(outs, axis=0)


def _gate_weight_copy(w_hbm, wstage, wsem, chunk):
    slot = chunk % GW_SLOTS
    return pltpu.make_async_copy(
        w_hbm.at[0, pl.ds(GW_CHUNK * chunk, GW_CHUNK), pl.ds(G_B, G_END - G_B)],
        wstage.at[slot], wsem.at[slot])


def _load_gate_weights(w_hbm, wg_s, wstage, wsem):
    n_chunks = D_MODEL // GW_CHUNK
    ahead = GW_SLOTS - 1
    for c in range(ahead):
        _gate_weight_copy(w_hbm, wstage, wsem, c).start()
    for c in range(n_chunks):
        if c + ahead < n_chunks:
            _gate_weight_copy(w_hbm, wstage, wsem, c + ahead).start()
        _gate_weight_copy(w_hbm, wstage, wsem, c).wait()
        wg_s[GW_CHUNK * c:GW_CHUNK * (c + 1), :] = wstage[c % GW_SLOTS].astype(BF16)


def _layer_kernel(sinks_ref, x_ref, xs_ref, cos_ref, sin_ref, ck_ref, cv_ref, st_ref, w_hbm,
                  meta_ref, cos0_ref, sin0_ref, coss_ref, sins_ref, nw_ref, qnw_ref, knw_ref, cw_ref,
                  wq_ref, wkv_ref, wza_ref, wpa_ref, wpb_ref, wo_ref, bd_ref,
                  y_ref, ys_ref, nkt_ref, nvt_ref, nc_ref, nck_ref, ncv_ref, ncs_ref,
                  kprev, vprev, utail, wg_s, wstage, wsem):
    i = pl.program_id(0)
    nw = nw_ref[...]
    bd = bd_ref[...]
    knw = knw_ref[...]

    @pl.when(i == 0)
    def _init():
        _load_gate_weights(w_hbm, wg_s, wstage, wsem)
        x0 = jnp.concatenate([jnp.zeros((LEAD, D_MODEL), F32), meta_ref[...]], axis=0)
        xn0 = _rms_rows(x0, nw).astype(BF16)
        kv0 = _dot(xn0, wkv_ref[...])
        k0 = _rope_split(_head_rms_split(kv0[:, :KV_WIDTH], bd, knw), cos0_ref[...], sin0_ref[...])
        v0 = kv0[:, KV_WIDTH:]
        c0 = _dot(xn0, wg_s[:, G_C - G_B:G_H - G_B])
        h0 = _dot(xn0, wg_s[:, G_H - G_B:G_ZB - G_B])
        row = lax.broadcasted_iota(jnp.int32, (BLOCK, D_MODEL), 0)
        u0 = jnp.where(row >= LEAD, c0 * h0, 0.0)
        kprev[...] = k0
        vprev[...] = v0
        utail[...] = u0[BLOCK - SUBLANES:BLOCK]

    cw = cw_ref[0]
    row8 = lax.broadcasted_iota(jnp.int32, (SUBLANES, D_MODEL), 0)
    t_io = lax.broadcasted_iota(jnp.int32, (DEC_PER_STEP, DEC_SEQ, D_MODEL), 1)
    k_prev = kprev[...].astype(BF16)
    v_prev = vprev[...].astype(BF16)
    tail = utail[...]
    def front(h):
        ps = slice(TM_SUB * h, TM_SUB * (h + 1))
        with_dec = h == SUB_TILES - 1
        if with_dec:
            x = jnp.concatenate([x_ref[ps, :], xs_ref[...]], axis=0)
            cos = jnp.concatenate([cos_ref[ps, :], coss_ref[...]], axis=0)
            sin = jnp.concatenate([sin_ref[ps, :], sins_ref[...]], axis=0)
        else:
            x = x_ref[ps, :]
            cos = cos_ref[ps, :]
            sin = sin_ref[ps, :]
        xn = _rms_rows(x, nw).astype(BF16)

        q = _head_rms_split(_dot(xn, wq_ref[...]), bd, qnw_ref[...])
        q = _rope_split(q, cos * Q_SCALE, sin * Q_SCALE)
        kv = _dot(xn, wkv_ref[...])
        k = _rope_split(_head_rms_split(kv[:, :KV_WIDTH], bd, knw), cos, sin)
        v = kv[:, KV_WIDTH:]
        kb = k[:TM_SUB].astype(BF16)
        vb = v[:TM_SUB].astype(BF16)
        return ps, with_dec, x, xn, q, k, v, kb, vb

    def mid(h, fr, k_prev, v_prev, tail):
        ps, with_dec, x, xn, q, k, v, kb, vb = fr
        first = (i == 0) if h == 0 else False
        wide = lambda lo: _dot(xn, wg_s[:, lo - G_B:lo - G_B + D_MODEL])
        o_a, (c, hc) = _prompt_attention(
            q[:TM_SUB].astype(BF16), kb, vb, k_prev, v_prev, sinks_ref, first,
            [lambda: wide(G_C), lambda: wide(G_H)])
        if with_dec:
            elems = _decode_scores(q[TM_SUB:], k[TM_SUB:], v[TM_SUB:], ck_ref, cv_ref, nck_ref,
                                   ncv_ref)
        b, z_b = wide(G_B), wide(G_ZB)
        if with_dec:
            o_a = jnp.concatenate([o_a, _decode_outputs(elems, sinks_ref)], axis=0)
        k_last = k[TM_SUB - BLOCK:TM_SUB]
        v_last = v[TM_SUB - BLOCK:TM_SUB]

        u_all = c * hc
        u = u_all[:TM_SUB]
        prev1 = tail[SUBLANES - 1:SUBLANES]
        prev2 = tail[SUBLANES - 2:SUBLANES - 1]
        r1 = pltpu.roll(u, 1, 0)
        r2 = pltpu.roll(u, 2, 0)
        um1 = jnp.concatenate(
            [jnp.where(row8 == 0, prev1, r1[:SUBLANES]), r1[SUBLANES:]], axis=0)
        um2 = jnp.concatenate(
            [jnp.where(row8 == 0, prev2, jnp.where(row8 == 1, prev1, r2[:SUBLANES])),
             r2[SUBLANES:]], axis=0)
        conv = cw[0:1] * um2 + cw[1:2] * um1 + cw[2:3] * u
        tail = u[TM_SUB - SUBLANES:]
        if with_dec:
            u3 = u_all[TM_SUB:].reshape(DEC_PER_STEP, DEC_SEQ, D_MODEL)
            st = st_ref[...]
            s_m2 = st[:, 0:1, :]
            s_m1 = st[:, 1:2, :]
            us1 = jnp.where(t_io == 0, s_m1, pltpu.roll(u3, 1, 1))
            us2 = jnp.where(t_io == 0, s_m2, jnp.where(t_io == 1, s_m1, pltpu.roll(u3, 2, 1)))
            conv_s = (cw[0:1] * us2 + cw[1:2] * us1 + cw[2:3] * u3).reshape(TS, D_MODEL)
            ncs_ref[...] = u3[:, DEC_SEQ - (CONV_K - 1):, :]
            conv = jnp.concatenate([conv, conv_s], axis=0)
        return o_a, conv, b, z_b, k_last, v_last, tail

    def back(fr, md):
        ps, with_dec, x, xn = fr[:4]
        o_a, conv, b, z_b = md[:4]
        z_a = _dot(xn, wza_ref[...])
        br_a = _dot((o_a * _silu(z_a)).astype(BF16), wpa_ref[...])
        br_b = _dot(((b * conv) * _silu(z_b)).astype(BF16), wpb_ref[...])
        g_a = _dot(xn, wg_s[:, G_GA - G_B:G_GB - G_B])
        g_b = _dot(xn, wg_s[:, G_GB - G_B:G_END - G_B])
        mixed = (jax.nn.sigmoid(g_a) * br_a + jax.nn.sigmoid(g_b) * br_b).astype(BF16)
        y = x + _dot(mixed, wo_ref[...])
        y_ref[ps, :] = y[:TM_SUB]
        if with_dec:
            ys_ref[...] = y[TM_SUB:]

    fronts = {0: front(0)}
    mids = {}
    for h in range(SUB_TILES):
        if h:
            k_prev = fronts[h - 1][7][TM_SUB - BLOCK:]
            v_prev = fronts[h - 1][8][TM_SUB - BLOCK:]
            tail = mids[h - 1][6]
        mids[h] = mid(h, fronts[h], k_prev, v_prev, tail)
        if h + 1 < SUB_TILES:
            fronts[h + 1] = front(h + 1)
        back(fronts[h], mids[h])
    k_last, v_last, tail = mids[SUB_TILES - 1][4:7]

    kprev[...] = k_last
    vprev[...] = v_last
    utail[...] = tail
    nkt_ref[...] = _rows_split_to_natural(k_last.T)
    nvt_ref[...] = v_last.T
    nc_ref[...] = tail[SUBLANES - (CONV_K - 1):]


def _const_spec(shape):
    zeros = (0,) * len(shape)
    return pl.BlockSpec(shape, lambda i: zeros, pipeline_mode=pl.Buffered(1))


def _layer_call(x, xs, tabs, ck, cv, st, w_in, meta, tabs0, tabs_s, nw, qnw, knw, cw, sinks,
                wq, wk, wza, wpa, wpb, wo, bd):
    tm = TM_PROMPT
    assert N_STEPS * DEC_PER_STEP == DEC_BATCH and TS % (2 * SUBLANES) == 0 and TS <= LANES
    assert TM_SUB % BLOCK == 0
    row_spec = lambda rows, w: pl.BlockSpec((rows, w), lambda i: (i, 0))
    cache_spec = pl.BlockSpec((DEC_PER_STEP, KV_WIDTH, WINDOW), lambda i: (i, 0, 0))
    st_spec = pl.BlockSpec((DEC_PER_STEP, CONV_K - 1, D_MODEL), lambda i: (i, 0, 0))
    consts = (meta, tabs0[0], tabs0[1], tabs_s[0], tabs_s[1], nw, qnw, knw, cw,
              wq, wk, wza, wpa, wpb, wo, bd)
    in_specs = [pl.BlockSpec(memory_space=pltpu.SMEM),
                row_spec(tm, D_MODEL), row_spec(TS, D_MODEL), row_spec(tm, LANES), row_spec(tm, LANES),
                cache_spec, cache_spec, st_spec, pl.BlockSpec(memory_space=pl.ANY)]
    in_specs += [_const_spec(a.shape) for a in consts]
    resident = lambda shape: pl.BlockSpec(shape, lambda i: (0, 0))
    out_specs = [row_spec(tm, D_MODEL), row_spec(TS, D_MODEL),
                 resident((KV_WIDTH, BLOCK)), resident((KV_WIDTH, BLOCK)),
                 resident((CONV_K - 1, D_MODEL)), cache_spec, cache_spec, st_spec]
    cache_sds = jax.ShapeDtypeStruct((DEC_BATCH, KV_WIDTH, WINDOW), F32)
    out_shape = [jax.ShapeDtypeStruct((SEQ, D_MODEL), F32),
                 jax.ShapeDtypeStruct((DEC_BATCH * DEC_SEQ, D_MODEL), F32),
                 jax.ShapeDtypeStruct((KV_WIDTH, BLOCK), F32),
                 jax.ShapeDtypeStruct((KV_WIDTH, BLOCK), F32),
                 jax.ShapeDtypeStruct((CONV_K - 1, D_MODEL), F32), cache_sds, cache_sds,
                 jax.ShapeDtypeStruct((DEC_BATCH, CONV_K - 1, D_MODEL), F32)]
    return pl.pallas_call(
        _layer_kernel,
        grid=(N_STEPS,),
        in_specs=in_specs,
        out_specs=out_specs,
        out_shape=out_shape,
        scratch_shapes=[pltpu.VMEM((BLOCK, KV_WIDTH), F32),
                        pltpu.VMEM((BLOCK, KV_WIDTH), F32),
                        pltpu.VMEM((SUBLANES, D_MODEL), F32),
                        pltpu.VMEM((D_MODEL, G_END - G_B), BF16),
                        pltpu.VMEM((GW_SLOTS, GW_CHUNK, G_END - G_B), F32),
                        pltpu.SemaphoreType.DMA((GW_SLOTS,))],
        compiler_params=pltpu.CompilerParams(
            dimension_semantics=("arbitrary",), vmem_limit_bytes=VMEM_LIMIT),
        name="hybrid_layer",
    )(sinks, x, xs, tabs[0], tabs[1], ck, cv, st, w_in, *consts)


def _rope_tables(positions):
    inv = np.power(ROPE_THETA, -np.arange(HALF, dtype=np.float64) * (2.0 / HEAD_DIM))
    ang = np.asarray(positions, dtype=np.float64)[:, None] * inv[None, :]
    cos = np.tile(np.cos(ang), (1, LANES // HALF))
    sin = np.tile(np.sin(ang), (1, LANES // HALF))
    return jnp.asarray(cos, F32), jnp.asarray(sin, F32)


def _head_mean_matrix():
    grp = np.arange(256) // HALF
    return jnp.asarray((grp[:, None] == grp[None, :]) / float(HEAD_DIM), BF16)


def kernel(x_prompt, x_sample, cache_k, cache_v, state_conv, meta_tokens, norm_w, w_in,
           q_norm_w, k_norm_w, sinks, conv_w, w_proj_a, w_proj_b, w_out):
    assert x_prompt.shape == (1, SEQ, D_MODEL) and x_sample.shape == (DEC_BATCH, DEC_SEQ, D_MODEL)
    assert w_in.shape[0] == 1, "single layer"
    wg = w_in[0, :, :G_B].astype(BF16)
    wq = (wg[:, :W_K].reshape(D_MODEL, N_KV_HEADS, GROUP, 2, HALF)
          .transpose(0, 2, 3, 1, 4).reshape(D_MODEL, D_MODEL))
    wk = (wg[:, W_K:W_V].reshape(D_MODEL, N_KV_HEADS, 2, HALF)
          .transpose(0, 2, 1, 3).reshape(D_MODEL, KV_WIDTH))
    wk = jnp.concatenate([wk, wg[:, W_V:W_ZA]], axis=1)
    wza = (wg[:, W_ZA:G_B].reshape(D_MODEL, N_KV_HEADS, GROUP, HEAD_DIM)
           .transpose(0, 2, 1, 3).reshape(D_MODEL, D_MODEL))
    wpa = (w_proj_a[0].astype(BF16).reshape(N_KV_HEADS, GROUP, HEAD_DIM, D_MODEL)
           .transpose(1, 0, 2, 3).reshape(D_MODEL, D_MODEL))
    wpb = w_proj_b[0].astype(BF16)
    wo = w_out[0].astype(BF16)

    nw = norm_w[0].reshape(1, D_MODEL)
    split = lambda v: jnp.broadcast_to(v.reshape(2, 1, HALF), (2, N_KV_HEADS, HALF)).reshape(1, 256)

    to_tiles = lambda c: c[0].transpose(0, 2, 3, 1).reshape(DEC_BATCH, KV_WIDTH, WINDOW)
    from_tiles = lambda c: (c.reshape(DEC_BATCH, N_KV_HEADS, HEAD_DIM, WINDOW)
                            .transpose(0, 3, 1, 2)[None])
    from_tile = lambda t: t.reshape(N_KV_HEADS, HEAD_DIM, WINDOW).transpose(2, 0, 1)

    tabs_p = _rope_tables(np.arange(SEQ) + N_META)
    tabs_0 = _rope_tables(np.arange(BLOCK) - LEAD)
    tabs_s = _rope_tables(PAST_LEN + (np.arange(TS) % DEC_SEQ))
    y_p, y_s, nkt_p, nvt_p, nc_p, nk_s, nv_s, nc_s = _layer_call(
        x_prompt[0], x_sample.reshape(DEC_BATCH * DEC_SEQ, D_MODEL), tabs_p,
        to_tiles(cache_k), to_tiles(cache_v), state_conv[0], w_in,
        meta_tokens.astype(x_prompt.dtype), tabs_0, tabs_s,
        nw, split(q_norm_w), split(k_norm_w), conv_w, sinks,
        wq, wk, wza, wpa, wpb, wo, _head_mean_matrix())

    return (y_p.reshape(1, SEQ, D_MODEL),
            y_s.reshape(DEC_BATCH, DEC_SEQ, D_MODEL),
            from_tile(nkt_p).reshape(1, 1, WINDOW, N_KV_HEADS, HEAD_DIM),
            from_tile(nvt_p).reshape(1, 1, WINDOW, N_KV_HEADS, HEAD_DIM),
            nc_p.reshape(1, 1, CONV_K - 1, D_MODEL),
            from_tiles(nk_s),
            from_tiles(nv_s),
            nc_s.reshape(1, DEC_BATCH, CONV_K - 1, D_MODEL))
```

```python
import functools

import numpy as np
import jax
import jax.numpy as jnp
from jax import lax
from jax.experimental import pallas as pl
from jax.experimental.pallas import tpu as pltpu

D_MODEL = 1024
SEQ = 16384
DEC_BATCH = 128
DEC_SEQ = 8
PAST_LEN = 16384
N_HEADS = 16
N_KV_HEADS = 4
GROUP = N_HEADS // N_KV_HEADS
HEAD_DIM = 64
HALF = HEAD_DIM // 2
KV_WIDTH = N_KV_HEADS * HEAD_DIM
WINDOW = 128
BLOCK = 128
ROPE_THETA = 10000.0
CONV_K = 3
N_META = 16
LEAD = BLOCK - N_META
EPS = 1e-6
NEG_INF = -1e30
Q_SCALE = HEAD_DIM ** -0.5

W_K = D_MODEL
W_V = W_K + KV_WIDTH
W_ZA = W_V + KV_WIDTH
G_B = W_ZA + D_MODEL
G_C = G_B + D_MODEL
G_H = G_C + D_MODEL
G_ZB = G_H + D_MODEL
G_GA = G_ZB + D_MODEL
G_GB = G_GA + D_MODEL
G_END = G_GB + D_MODEL

LANES = 128
SUBLANES = 8
VMEM_BYTES_V7X = 64 * 1024 * 1024

TM_PROMPT = 512
N_STEPS = SEQ // TM_PROMPT
DEC_PER_STEP = DEC_BATCH // N_STEPS
TS = DEC_PER_STEP * DEC_SEQ
GW_CHUNK = 64
GW_SLOTS = 4
SUB_TILES = 2
TM_SUB = TM_PROMPT // SUB_TILES
VMEM_LIMIT = VMEM_BYTES_V7X - 4 * 1024 * 1024

BF16 = jnp.bfloat16
F32 = jnp.float32


def _dot(a, b):
    return jnp.dot(a, b, preferred_element_type=F32)


def _dot_nt(a, b):
    return lax.dot_general(a, b, (((1,), (1,)), ((), ())), preferred_element_type=F32)


def _rms_rows(x, w):
    ms = jnp.mean(x * x, axis=-1, keepdims=True)
    return (x * lax.rsqrt(ms + EPS)) * w


def _head_rms_split(t, bd, w):
    n = t.shape[1] // 256
    r = t.shape[0]
    halves = [(t[:, 256 * c:256 * c + LANES], t[:, 256 * c + LANES:256 * (c + 1)]) for c in range(n)]
    pieces = [a * a + b * b for a, b in halves]
    if n == 1:
        pieces = [pieces[0], pieces[0]]
    pair_ms = [_dot(jnp.concatenate(pieces[2 * i:2 * i + 2], axis=1).astype(BF16), bd)
               for i in range(len(pieces) // 2)]
    ms = [pair_ms[i // 2][:, LANES * (i % 2):LANES * (i % 2 + 1)] for i in range(n)]
    out = []
    for (a, b), m in zip(halves, ms):
        scale = lax.rsqrt(m + EPS)
        out.append((a * scale) * w[:, :LANES])
        out.append((b * scale) * w[:, LANES:])
    return jnp.concatenate(out, axis=1)


def _rope_split(t, cos, sin):
    out = []
    for c in range(t.shape[1] // 256):
        a = t[:, 256 * c:256 * c + LANES]
        b = t[:, 256 * c + LANES:256 * (c + 1)]
        out.append(a * cos - b * sin)
        out.append(b * cos + a * sin)
    return jnp.concatenate(out, axis=1)


def _lane_group_masks(group_of_lane, dtype):
    return [(group_of_lane == g).astype(dtype) for g in range(N_KV_HEADS)]


def _softmax_parts(s, sk):
    m = jnp.maximum(jnp.max(s, axis=-1, keepdims=True), sk)
    p = jnp.exp(s - m)
    den = jnp.sum(p, axis=-1, keepdims=True) + jnp.exp(sk - m)
    return p, 1.0 / den


def _silu(z):
    return z * jax.nn.sigmoid(z)


def _rows_split_to_natural(t):
    return jnp.concatenate(
        [t[LANES * half + HALF * g:LANES * half + HALF * (g + 1)]
         for g in range(N_KV_HEADS) for half in range(2)], axis=0)


def _rows_natural_to_split(t):
    return jnp.concatenate(
        [t[HEAD_DIM * g + HALF * half:HEAD_DIM * g + HALF * (half + 1)]
         for half in range(2) for g in range(N_KV_HEADS)], axis=0)


def _prompt_attention(q, kb, vb, kprev, vprev, sinks_ref, first_step):
    tm = q.shape[0]
    lane = lax.broadcasted_iota(jnp.int32, (1, 2 * BLOCK), 1)
    kmasks = _lane_group_masks((lane & (LANES - 1)) >> 5, BF16)
    vmasks = _lane_group_masks(lane >> 6, BF16)
    vgroup = lane >> 6
    r_io = lax.broadcasted_iota(jnp.int32, (BLOCK, 2 * BLOCK), 0)
    c_io = lax.broadcasted_iota(jnp.int32, (BLOCK, 2 * BLOCK), 1)
    band = (c_io >= r_io) & (c_io <= r_io + WINDOW)
    first_band = band & (c_io >= jnp.where(first_step, LEAD, 0))

    o_blocks = []
    for b in range(tm // BLOCK):
        lo = b * BLOCK
        if b == 0:
            kpair = jnp.concatenate([kprev, kb[:BLOCK]], axis=0)
            vpair = jnp.concatenate([vprev, vb[:BLOCK]], axis=0)
            mask = first_band
        else:
            kpair = kb[lo - BLOCK:lo + BLOCK]
            vpair = vb[lo - BLOCK:lo + BLOCK]
            mask = band
        kbd = jnp.concatenate([kpair * km for km in kmasks], axis=0)
        vbd = jnp.concatenate([vpair * vm for vm in vmasks], axis=0)
        qs = jnp.concatenate([q[lo:lo + BLOCK, 256 * j:256 * (j + 1)] for j in range(GROUP)],
                             axis=0)
        s = _dot_nt(qs, kbd)
        p_rows = []
        mults = []
        for j in range(GROUP):
            p_cols = []
            invs = []
            for g in range(N_KV_HEADS):
                sl = s[BLOCK * j:BLOCK * (j + 1), 256 * g:256 * (g + 1)]
                sl = jnp.where(mask, sl, NEG_INF)
                p, inv = _softmax_parts(sl, sinks_ref[0, GROUP * g + j])
                p_cols.append(p.astype(BF16))
                invs.append(inv)
            p_rows.append(jnp.concatenate(p_cols, axis=1))
            mults.append(jnp.where(vgroup == 0, invs[0],
                                   jnp.where(vgroup == 1, invs[1],
                                             jnp.where(vgroup == 2, invs[2], invs[3]))))
        o = _dot(jnp.concatenate(p_rows, axis=0), vbd)
        o_blocks.append(jnp.concatenate(
            [o[BLOCK * j:BLOCK * (j + 1)] * mults[j] for j in range(GROUP)], axis=1))
    return jnp.concatenate(o_blocks, axis=0)


def _decode_scores(qs, ks, vs, ck_ref, cv_ref, nck_ref, ncv_ref):
    t_len = DEC_SEQ
    n_exp = GROUP * N_KV_HEADS * t_len
    pad = jnp.zeros((LANES - qs.shape[0], KV_WIDTH), F32)
    kst = jnp.concatenate([ks, pad], axis=0).T
    vst = jnp.concatenate([vs, pad], axis=0).T
    kst_nat = _rows_split_to_natural(kst)
    kst_b = kst.astype(BF16)
    vst_b = vst.astype(BF16)

    lane = lax.broadcasted_iota(jnp.int32, (1, KV_WIDTH), 1)
    kgroup = (lane & (LANES - 1)) >> 5
    t_io = lax.broadcasted_iota(jnp.int32, (n_exp, WINDOW), 0) & (t_len - 1)
    c_io = lax.broadcasted_iota(jnp.int32, (n_exp, WINDOW), 1)
    old_ok = c_io >= t_io
    new_lane = lax.broadcasted_iota(jnp.int32, (KV_WIDTH, WINDOW), 1) >= WINDOW - t_len

    elems = []
    for b in range(qs.shape[0] // t_len):
        r0 = b * t_len
        qn = qs[r0:r0 + t_len]
        q_exp = jnp.concatenate(
            [jnp.where(kgroup == g, qn[:, 256 * j:256 * (j + 1)], 0.0)
             for j in range(GROUP) for g in range(N_KV_HEADS)], axis=0).astype(BF16)
        kt_old = ck_ref[b]
        vt_old = cv_ref[b]
        keys = jnp.concatenate([_rows_natural_to_split(kt_old).astype(BF16), kst_b], axis=1)
        vals = jnp.concatenate([vt_old.astype(BF16), vst_b], axis=1)
        s = _dot(q_exp, keys)
        new_ok = (c_io >= r0) & (c_io - t_io <= r0)
        s = jnp.where(jnp.concatenate([old_ok, new_ok], axis=1), s, NEG_INF)
        elems.append((s, vals))
        shift_new = WINDOW - t_len - r0
        nck_ref[b] = jnp.where(new_lane, pltpu.roll(kst_nat, shift_new, 1),
                               pltpu.roll(kt_old, WINDOW - t_len, 1))
        ncv_ref[b] = jnp.where(new_lane, pltpu.roll(vst, shift_new, 1),
                               pltpu.roll(vt_old, WINDOW - t_len, 1))
    return elems


def _decode_outputs(elems, sinks_ref):
    t_len = DEC_SEQ
    vgroup = lax.broadcasted_iota(jnp.int32, (1, KV_WIDTH), 1) >> 6
    skcol = jnp.concatenate(
        [jnp.full((t_len, 1), sinks_ref[0, GROUP * g + j], F32)
         for j in range(GROUP) for g in range(N_KV_HEADS)], axis=0)
    outs = []
    for s, vals in elems:
        p, inv = _softmax_parts(s, skcol)
        o = _dot_nt(p.astype(BF16), vals) * inv
        o_j = []
        for j in range(GROUP):
            acc = None
            for g in range(N_KV_HEADS):
                r = (j * N_KV_HEADS + g) * t_len
                piece = jnp.where(vgroup == g, o[r:r + t_len], 0.0)
                acc = piece if acc is None else acc + piece
            o_j.append(acc)
        outs.append(jnp.concatenate(o_j, axis=1))
    return jnp.concatenate(outs, axis=0)


def _gate_weight_copy(w_hbm, wstage, wsem, chunk):
    slot = chunk % GW_SLOTS
    return pltpu.make_async_copy(
        w_hbm.at[0, pl.ds(GW_CHUNK * chunk, GW_CHUNK), pl.ds(G_B, G_END - G_B)],
        wstage.at[slot], wsem.at[slot])


def _load_gate_weights(w_hbm, wg_s, wstage, wsem):
    n_chunks = D_MODEL // GW_CHUNK
    ahead = GW_SLOTS - 1
    for c in range(ahead):
        _gate_weight_copy(w_hbm, wstage, wsem, c).start()
    for c in range(n_chunks):
        if c + ahead < n_chunks:
            _gate_weight_copy(w_hbm, wstage, wsem, c + ahead).start()
        _gate_weight_copy(w_hbm, wstage, wsem, c).wait()
        wg_s[GW_CHUNK * c:GW_CHUNK * (c + 1), :] = wstage[c % GW_SLOTS].astype(BF16)


def _layer_kernel(sinks_ref, x_ref, xs_ref, cos_ref, sin_ref, ck_ref, cv_ref, st_ref, w_hbm,
                  meta_ref, cos0_ref, sin0_ref, coss_ref, sins_ref, nw_ref, qnw_ref, knw_ref, cw_ref,
                  wq_ref, wkv_ref, wza_ref, wpa_ref, wpb_ref, wo_ref, bd_ref,
                  y_ref, ys_ref, nkt_ref, nvt_ref, nc_ref, nck_ref, ncv_ref, ncs_ref,
                  kprev, vprev, utail, wg_s, wstage, wsem):
    i = pl.program_id(0)
    nw = nw_ref[...]
    bd = bd_ref[...]
    knw = knw_ref[...]

    @pl.when(i == 0)
    def _init():
        _load_gate_weights(w_hbm, wg_s, wstage, wsem)
        x0 = jnp.concatenate([jnp.zeros((LEAD, D_MODEL), F32), meta_ref[...]], axis=0)
        xn0 = _rms_rows(x0, nw).astype(BF16)
        kv0 = _dot(xn0, wkv_ref[...])
        k0 = _rope_split(_head_rms_split(kv0[:, :KV_WIDTH], bd, knw), cos0_ref[...], sin0_ref[...])
        v0 = kv0[:, KV_WIDTH:]
        c0 = _dot(xn0, wg_s[:, G_C - G_B:G_H - G_B])
        h0 = _dot(xn0, wg_s[:, G_H - G_B:G_ZB - G_B])
        row = lax.broadcasted_iota(jnp.int32, (BLOCK, D_MODEL), 0)
        u0 = jnp.where(row >= LEAD, c0 * h0, 0.0)
        kprev[...] = k0
        vprev[...] = v0
        utail[...] = u0[BLOCK - SUBLANES:BLOCK]

    cw = cw_ref[0]
    row8 = lax.broadcasted_iota(jnp.int32, (SUBLANES, D_MODEL), 0)
    t_io = lax.broadcasted_iota(jnp.int32, (DEC_PER_STEP, DEC_SEQ, D_MODEL), 1)
    k_prev = kprev[...].astype(BF16)
    v_prev = vprev[...].astype(BF16)
    tail = utail[...]
    def front(h):
        ps = slice(TM_SUB * h, TM_SUB * (h + 1))
        with_dec = h == SUB_TILES - 1
        if with_dec:
            x = jnp.concatenate([x_ref[ps, :], xs_ref[...]], axis=0)
            cos = jnp.concatenate([cos_ref[ps, :], coss_ref[...]], axis=0)
            sin = jnp.concatenate([sin_ref[ps, :], sins_ref[...]], axis=0)
        else:
            x = x_ref[ps, :]
            cos = cos_ref[ps, :]
            sin = sin_ref[ps, :]
        xn = _rms_rows(x, nw).astype(BF16)

        q = _head_rms_split(_dot(xn, wq_ref[...]), bd, qnw_ref[...])
        q = _rope_split(q, cos * Q_SCALE, sin * Q_SCALE)
        kv = _dot(xn, wkv_ref[...])
        k = _rope_split(_head_rms_split(kv[:, :KV_WIDTH], bd, knw), cos, sin)
        v = kv[:, KV_WIDTH:]
        kb = k[:TM_SUB].astype(BF16)
        vb = v[:TM_SUB].astype(BF16)
        return ps, with_dec, x, xn, q, k, v, kb, vb

    def mid(h, fr, k_prev, v_prev, tail):
        ps, with_dec, x, xn, q, k, v, kb, vb = fr
        first = (i == 0) if h == 0 else False
        o_a = _prompt_attention(q[:TM_SUB].astype(BF16), kb, vb, k_prev, v_prev, sinks_ref, first)
        if with_dec:
            elems = _decode_scores(q[TM_SUB:], k[TM_SUB:], v[TM_SUB:], ck_ref, cv_ref, nck_ref,
                                   ncv_ref)
        c, hc, b, z_b = [_dot(xn, wg_s[:, lo - G_B:lo - G_B + D_MODEL])
                         for lo in (G_C, G_H, G_B, G_ZB)]
        if with_dec:
            o_a = jnp.concatenate([o_a, _decode_outputs(elems, sinks_ref)], axis=0)
        k_last = k[TM_SUB - BLOCK:TM_SUB]
        v_last = v[TM_SUB - BLOCK:TM_SUB]

        u_all = c * hc
        u = u_all[:TM_SUB]
        prev1 = tail[SUBLANES - 1:SUBLANES]
        prev2 = tail[SUBLANES - 2:SUBLANES - 1]
        r1 = pltpu.roll(u, 1, 0)
        r2 = pltpu.roll(u, 2, 0)
        um1 = jnp.concatenate(
            [jnp.where(row8 == 0, prev1, r1[:SUBLANES]), r1[SUBLANES:]], axis=0)
        um2 = jnp.concatenate(
            [jnp.where(row8 == 0, prev2, jnp.where(row8 == 1, prev1, r2[:SUBLANES])),
             r2[SUBLANES:]], axis=0)
        conv = cw[0:1] * um2 + cw[1:2] * um1 + cw[2:3] * u
        tail = u[TM_SUB - SUBLANES:]
        if with_dec:
            u3 = u_all[TM_SUB:].reshape(DEC_PER_STEP, DEC_SEQ, D_MODEL)
            st = st_ref[...]
            s_m2 = st[:, 0:1, :]
            s_m1 = st[:, 1:2, :]
            us1 = jnp.where(t_io == 0, s_m1, pltpu.roll(u3, 1, 1))
            us2 = jnp.where(t_io == 0, s_m2, jnp.where(t_io == 1, s_m1, pltpu.roll(u3, 2, 1)))
            conv_s = (cw[0:1] * us2 + cw[1:2] * us1 + cw[2:3] * u3).reshape(TS, D_MODEL)
            ncs_ref[...] = u3[:, DEC_SEQ - (CONV_K - 1):, :]
            conv = jnp.concatenate([conv, conv_s], axis=0)
        return o_a, conv, b, z_b, k_last, v_last, tail

    def back(fr, md):
        ps, with_dec, x, xn = fr[:4]
        o_a, conv, b, z_b = md[:4]
        z_a = _dot(xn, wza_ref[...])
        br_a = _dot((o_a * _silu(z_a)).astype(BF16), wpa_ref[...])
        br_b = _dot(((b * conv) * _silu(z_b)).astype(BF16), wpb_ref[...])
        g_a = _dot(xn, wg_s[:, G_GA - G_B:G_GB - G_B])
        g_b = _dot(xn, wg_s[:, G_GB - G_B:G_END - G_B])
        mixed = (jax.nn.sigmoid(g_a) * br_a + jax.nn.sigmoid(g_b) * br_b).astype(BF16)
        y = x + _dot(mixed, wo_ref[...])
        y_ref[ps, :] = y[:TM_SUB]
        if with_dec:
            ys_ref[...] = y[TM_SUB:]

    fronts = {0: front(0)}
    mids = {}
    for h in range(SUB_TILES):
        if h:
            k_prev = fronts[h - 1][7][TM_SUB - BLOCK:]
            v_prev = fronts[h - 1][8][TM_SUB - BLOCK:]
            tail = mids[h - 1][6]
        mids[h] = mid(h, fronts[h], k_prev, v_prev, tail)
        if h + 1 < SUB_TILES:
            fronts[h + 1] = front(h + 1)
        back(fronts[h], mids[h])
    k_last, v_last, tail = mids[SUB_TILES - 1][4:7]

    kprev[...] = k_last
    vprev[...] = v_last
    utail[...] = tail
    nkt_ref[...] = _rows_split_to_natural(k_last.T)
    nvt_ref[...] = v_last.T
    nc_ref[...] = tail[SUBLANES - (CONV_K - 1):]


def _const_spec(shape):
    zeros = (0,) * len(shape)
    return pl.BlockSpec(shape, lambda i: zeros, pipeline_mode=pl.Buffered(1))


def _layer_call(x, xs, tabs, ck, cv, st, w_in, meta, tabs0, tabs_s, nw, qnw, knw, cw, sinks,
                wq, wk, wza, wpa, wpb, wo, bd):
    tm = TM_PROMPT
    assert N_STEPS * DEC_PER_STEP == DEC_BATCH and TS % (2 * SUBLANES) == 0 and TS <= LANES
    assert TM_SUB % BLOCK == 0
    row_spec = lambda rows, w: pl.BlockSpec((rows, w), lambda i: (i, 0))
    cache_spec = pl.BlockSpec((DEC_PER_STEP, KV_WIDTH, WINDOW), lambda i: (i, 0, 0))
    st_spec = pl.BlockSpec((DEC_PER_STEP, CONV_K - 1, D_MODEL), lambda i: (i, 0, 0))
    consts = (meta, tabs0[0], tabs0[1], tabs_s[0], tabs_s[1], nw, qnw, knw, cw,
              wq, wk, wza, wpa, wpb, wo, bd)
    in_specs = [pl.BlockSpec(memory_space=pltpu.SMEM),
                row_spec(tm, D_MODEL), row_spec(TS, D_MODEL), row_spec(tm, LANES), row_spec(tm, LANES),
                cache_spec, cache_spec, st_spec, pl.BlockSpec(memory_space=pl.ANY)]
    in_specs += [_const_spec(a.shape) for a in consts]
    resident = lambda shape: pl.BlockSpec(shape, lambda i: (0, 0))
    out_specs = [row_spec(tm, D_MODEL), row_spec(TS, D_MODEL),
                 resident((KV_WIDTH, BLOCK)), resident((KV_WIDTH, BLOCK)),
                 resident((CONV_K - 1, D_MODEL)), cache_spec, cache_spec, st_spec]
    cache_sds = jax.ShapeDtypeStruct((DEC_BATCH, KV_WIDTH, WINDOW), F32)
    out_shape = [jax.ShapeDtypeStruct((SEQ, D_MODEL), F32),
                 jax.ShapeDtypeStruct((DEC_BATCH * DEC_SEQ, D_MODEL), F32),
                 jax.ShapeDtypeStruct((KV_WIDTH, BLOCK), F32),
                 jax.ShapeDtypeStruct((KV_WIDTH, BLOCK), F32),
                 jax.ShapeDtypeStruct((CONV_K - 1, D_MODEL), F32), cache_sds, cache_sds,
                 jax.ShapeDtypeStruct((DEC_BATCH, CONV_K - 1, D_MODEL), F32)]
    return pl.pallas_call(
        _layer_kernel,
        grid=(N_STEPS,),
        in_specs=in_specs,
        out_specs=out_specs,
        out_shape=out_shape,
        scratch_shapes=[pltpu.VMEM((BLOCK, KV_WIDTH), F32),
                        pltpu.VMEM((BLOCK, KV_WIDTH), F32),
                        pltpu.VMEM((SUBLANES, D_MODEL), F32),
                        pltpu.VMEM((D_MODEL, G_END - G_B), BF16),
                        pltpu.VMEM((GW_SLOTS, GW_CHUNK, G_END - G_B), F32),
                        pltpu.SemaphoreType.DMA((GW_SLOTS,))],
        compiler_params=pltpu.CompilerParams(
            dimension_semantics=("arbitrary",), vmem_limit_bytes=VMEM_LIMIT),
        name="hybrid_layer",
    )(sinks, x, xs, tabs[0], tabs[1], ck, cv, st, w_in, *consts)


def _rope_tables(positions):
    inv = np.power(ROPE_THETA, -np.arange(HALF, dtype=np.float64) * (2.0 / HEAD_DIM))
    ang = np.asarray(positions, dtype=np.float64)[:, None] * inv[None, :]
    cos = np.tile(np.cos(ang), (1, LANES // HALF))
    sin = np.tile(np.sin(ang), (1, LANES // HALF))
    return jnp.asarray(cos, F32), jnp.asarray(sin, F32)


def _head_mean_matrix():
    grp = np.arange(256) // HALF
    return jnp.asarray((grp[:, None] == grp[None, :]) / float(HEAD_DIM), BF16)


def kernel(x_prompt, x_sample, cache_k, cache_v, state_conv, meta_tokens, norm_w, w_in,
           q_norm_w, k_norm_w, sinks, conv_w, w_proj_a, w_proj_b, w_out):
    assert x_prompt.shape == (1, SEQ, D_MODEL) and x_sample.shape == (DEC_BATCH, DEC_SEQ, D_MODEL)
    assert w_in.shape[0] == 1, "single layer"
    wg = w_in[0, :, :G_B].astype(BF16)
    wq = (wg[:, :W_K].reshape(D_MODEL, N_KV_HEADS, GROUP, 2, HALF)
          .transpose(0, 2, 3, 1, 4).reshape(D_MODEL, D_MODEL))
    wk = (wg[:, W_K:W_V].reshape(D_MODEL, N_KV_HEADS, 2, HALF)
          .transpose(0, 2, 1, 3).reshape(D_MODEL, KV_WIDTH))
    wk = jnp.concatenate([wk, wg[:, W_V:W_ZA]], axis=1)
    wza = (wg[:, W_ZA:G_B].reshape(D_MODEL, N_KV_HEADS, GROUP, HEAD_DIM)
           .transpose(0, 2, 1, 3).reshape(D_MODEL, D_MODEL))
    wpa = (w_proj_a[0].astype(BF16).reshape(N_KV_HEADS, GROUP, HEAD_DIM, D_MODEL)
           .transpose(1, 0, 2, 3).reshape(D_MODEL, D_MODEL))
    wpb = w_proj_b[0].astype(BF16)
    wo = w_out[0].astype(BF16)

    nw = norm_w[0].reshape(1, D_MODEL)
    split = lambda v: jnp.broadcast_to(v.reshape(2, 1, HALF), (2, N_KV_HEADS, HALF)).reshape(1, 256)

    to_tiles = lambda c: c[0].transpose(0, 2, 3, 1).reshape(DEC_BATCH, KV_WIDTH, WINDOW)
    from_tiles = lambda c: (c.reshape(DEC_BATCH, N_KV_HEADS, HEAD_DIM, WINDOW)
                            .transpose(0, 3, 1, 2)[None])
    from_tile = lambda t: t.reshape(N_KV_HEADS, HEAD_DIM, WINDOW).transpose(2, 0, 1)

    tabs_p = _rope_tables(np.arange(SEQ) + N_META)
    tabs_0 = _rope_tables(np.arange(BLOCK) - LEAD)
    tabs_s = _rope_tables(PAST_LEN + (np.arange(TS) % DEC_SEQ))
    y_p, y_s, nkt_p, nvt_p, nc_p, nk_s, nv_s, nc_s = _layer_call(
        x_prompt[0], x_sample.reshape(DEC_BATCH * DEC_SEQ, D_MODEL), tabs_p,
        to_tiles(cache_k), to_tiles(cache_v), state_conv[0], w_in,
        meta_tokens.astype(x_prompt.dtype), tabs_0, tabs_s,
        nw, split(q_norm_w), split(k_norm_w), conv_w, sinks,
        wq, wk, wza, wpa, wpb, wo, _head_mean_matrix())

    return (y_p.reshape(1, SEQ, D_MODEL),
            y_s.reshape(DEC_BATCH, DEC_SEQ, D_MODEL),
            from_tile(nkt_p).reshape(1, 1, WINDOW, N_KV_HEADS, HEAD_DIM),
            from_tile(nvt_p).reshape(1, 1, WINDOW, N_KV_HEADS, HEAD_DIM),
            nc_p.reshape(1, 1, CONV_K - 1, D_MODEL),
            from_tiles(nk_s),
            from_tiles(nv_s),
            nc_s.reshape(1, DEC_BATCH, CONV_K - 1, D_MODEL))
```

```python
import functools

import numpy as np
import jax
import jax.numpy as jnp
from jax import lax
from jax.experimental import pallas as pl
from jax.experimental.pallas import tpu as pltpu

D_MODEL = 1024
SEQ = 16384
DEC_BATCH = 128
DEC_SEQ = 8
PAST_LEN = 16384
N_HEADS = 16
N_KV_HEADS = 4
GROUP = N_HEADS // N_KV_HEADS
HEAD_DIM = 64
HALF = HEAD_DIM // 2
KV_WIDTH = N_KV_HEADS * HEAD_DIM
WINDOW = 128
BLOCK = 128
ROPE_THETA = 10000.0
CONV_K = 3
N_META = 16
LEAD = BLOCK - N_META
EPS = 1e-6
NEG_INF = -1e30
Q_SCALE = HEAD_DIM ** -0.5

W_K = D_MODEL
W_V = W_K + KV_WIDTH
W_ZA = W_V + KV_WIDTH
G_B = W_ZA + D_MODEL
G_C = G_B + D_MODEL
G_H = G_C + D_MODEL
G_ZB = G_H + D_MODEL
G_GA = G_ZB + D_MODEL
G_GB = G_GA + D_MODEL
G_END = G_GB + D_MODEL

LANES = 128
SUBLANES = 8
VMEM_BYTES_V7X = 64 * 1024 * 1024

TM_PROMPT = 512
N_STEPS = SEQ // TM_PROMPT
DEC_PER_STEP = DEC_BATCH // N_STEPS
TS = DEC_PER_STEP * DEC_SEQ
GW_CHUNK = 64
GW_SLOTS = 4
SUB_TILES = 2
TM_SUB = TM_PROMPT // SUB_TILES
VMEM_LIMIT = VMEM_BYTES_V7X - 4 * 1024 * 1024

BF16 = jnp.bfloat16
F32 = jnp.float32


def _dot(a, b):
    return jnp.dot(a, b, preferred_element_type=F32)


def _dot_nt(a, b):
    return lax.dot_general(a, b, (((1,), (1,)), ((), ())), preferred_element_type=F32)


def _rms_rows(x, w):
    ms = jnp.mean(x * x, axis=-1, keepdims=True)
    return (x * lax.rsqrt(ms + EPS)) * w


def _head_rms_split(t, bd, w):
    n = t.shape[1] // 256
    r = t.shape[0]
    halves = [(t[:, 256 * c:256 * c + LANES], t[:, 256 * c + LANES:256 * (c + 1)]) for c in range(n)]
    pieces = [a * a + b * b for a, b in halves]
    if n == 1:
        pieces = [pieces[0], pieces[0]]
    pair_ms = [_dot(jnp.concatenate(pieces[2 * i:2 * i + 2], axis=1).astype(BF16), bd)
               for i in range(len(pieces) // 2)]
    ms = [pair_ms[i // 2][:, LANES * (i % 2):LANES * (i % 2 + 1)] for i in range(n)]
    out = []
    for (a, b), m in zip(halves, ms):
        scale = lax.rsqrt(m + EPS)
        out.append((a * scale) * w[:, :LANES])
        out.append((b * scale) * w[:, LANES:])
    return jnp.concatenate(out, axis=1)


def _rope_split(t, cos, sin):
    out = []
    for c in range(t.shape[1] // 256):
        a = t[:, 256 * c:256 * c + LANES]
        b = t[:, 256 * c + LANES:256 * (c + 1)]
        out.append(a * cos - b * sin)
        out.append(b * cos + a * sin)
    return jnp.concatenate(out, axis=1)


def _lane_group_masks(group_of_lane, dtype):
    return [(group_of_lane == g).astype(dtype) for g in range(N_KV_HEADS)]


def _softmax_parts(s, sk):
    m = jnp.maximum(jnp.max(s, axis=-1, keepdims=True), sk)
    p = jnp.exp(s - m)
    den = jnp.sum(p, axis=-1, keepdims=True) + jnp.exp(sk - m)
    return p, 1.0 / den


def _silu(z):
    return z * jax.nn.sigmoid(z)


def _rows_split_to_natural(t):
    return jnp.concatenate(
        [t[LANES * half + HALF * g:LANES * half + HALF * (g + 1)]
         for g in range(N_KV_HEADS) for half in range(2)], axis=0)


def _rows_natural_to_split(t):
    return jnp.concatenate(
        [t[HEAD_DIM * g + HALF * half:HEAD_DIM * g + HALF * (half + 1)]
         for half in range(2) for g in range(N_KV_HEADS)], axis=0)


def _prompt_attention(q, kb, vb, kprev, vprev, sinks_ref, first_step):
    tm = q.shape[0]
    lane = lax.broadcasted_iota(jnp.int32, (1, 2 * BLOCK), 1)
    kmasks = _lane_group_masks((lane & (LANES - 1)) >> 5, BF16)
    vmasks = _lane_group_masks(lane >> 6, BF16)
    vgroup = lane >> 6
    r_io = lax.broadcasted_iota(jnp.int32, (BLOCK, 2 * BLOCK), 0)
    c_io = lax.broadcasted_iota(jnp.int32, (BLOCK, 2 * BLOCK), 1)
    band = (c_io >= r_io) & (c_io <= r_io + WINDOW)
    first_band = band & (c_io >= jnp.where(first_step, LEAD, 0))

    o_blocks = []
    for b in range(tm // BLOCK):
        lo = b * BLOCK
        if b == 0:
            kpair = jnp.concatenate([kprev, kb[:BLOCK]], axis=0)
            vpair = jnp.concatenate([vprev, vb[:BLOCK]], axis=0)
            mask = first_band
        else:
            kpair = kb[lo - BLOCK:lo + BLOCK]
            vpair = vb[lo - BLOCK:lo + BLOCK]
            mask = band
        kbd = jnp.concatenate([kpair * km for km in kmasks], axis=0)
        vbd = jnp.concatenate([vpair * vm for vm in vmasks], axis=0)
        qs = jnp.concatenate([q[lo:lo + BLOCK, 256 * j:256 * (j + 1)] for j in range(GROUP)],
                             axis=0)
        s_g = [_dot_nt(qs, kbd[256 * g:256 * (g + 1)]) for g in range(N_KV_HEADS)]
        p_rows = []
        mults = []
        for j in range(GROUP):
            p_cols = []
            invs = []
            for g in range(N_KV_HEADS):
                sl = s_g[g][BLOCK * j:BLOCK * (j + 1)]
                sl = jnp.where(mask, sl, NEG_INF)
                p, inv = _softmax_parts(sl, sinks_ref[0, GROUP * g + j])
                p_cols.append(p.astype(BF16))
                invs.append(inv)
            p_rows.append(jnp.concatenate(p_cols, axis=1))
            mults.append(jnp.where(vgroup == 0, invs[0],
                                   jnp.where(vgroup == 1, invs[1],
                                             jnp.where(vgroup == 2, invs[2], invs[3]))))
        p_all = jnp.concatenate(p_rows, axis=0)
        kw = 2 * BLOCK
        parts = [_dot(p_all[:, kw * g:kw * (g + 1)], vbd[kw * g:kw * (g + 1)])
                 for g in range(N_KV_HEADS)]
        o = (parts[0] + parts[1]) + (parts[2] + parts[3])
        o_blocks.append(jnp.concatenate(
            [o[BLOCK * j:BLOCK * (j + 1)] * mults[j] for j in range(GROUP)], axis=1))
    return jnp.concatenate(o_blocks, axis=0)


def _decode_scores(qs, ks, vs, ck_ref, cv_ref, nck_ref, ncv_ref):
    t_len = DEC_SEQ
    n_exp = GROUP * N_KV_HEADS * t_len
    pad = jnp.zeros((LANES - qs.shape[0], KV_WIDTH), F32)
    kst = jnp.concatenate([ks, pad], axis=0).T
    vst = jnp.concatenate([vs, pad], axis=0).T
    kst_nat = _rows_split_to_natural(kst)
    kst_b = kst.astype(BF16)
    vst_b = vst.astype(BF16)

    lane = lax.broadcasted_iota(jnp.int32, (1, KV_WIDTH), 1)
    kgroup = (lane & (LANES - 1)) >> 5
    t_io = lax.broadcasted_iota(jnp.int32, (n_exp, WINDOW), 0) & (t_len - 1)
    c_io = lax.broadcasted_iota(jnp.int32, (n_exp, WINDOW), 1)
    old_ok = c_io >= t_io
    new_lane = lax.broadcasted_iota(jnp.int32, (KV_WIDTH, WINDOW), 1) >= WINDOW - t_len

    elems = []
    for b in range(qs.shape[0] // t_len):
        r0 = b * t_len
        qn = qs[r0:r0 + t_len]
        q_exp = jnp.concatenate(
            [jnp.where(kgroup == g, qn[:, 256 * j:256 * (j + 1)], 0.0)
             for j in range(GROUP) for g in range(N_KV_HEADS)], axis=0).astype(BF16)
        kt_old = ck_ref[b]
        vt_old = cv_ref[b]
        keys = jnp.concatenate([_rows_natural_to_split(kt_old).astype(BF16), kst_b], axis=1)
        vals = jnp.concatenate([vt_old.astype(BF16), vst_b], axis=1)
        s = _dot(q_exp, keys)
        new_ok = (c_io >= r0) & (c_io - t_io <= r0)
        s = jnp.where(jnp.concatenate([old_ok, new_ok], axis=1), s, NEG_INF)
        elems.append((s, vals))
        shift_new = WINDOW - t_len - r0
        nck_ref[b] = jnp.where(new_lane, pltpu.roll(kst_nat, shift_new, 1),
                               pltpu.roll(kt_old, WINDOW - t_len, 1))
        ncv_ref[b] = jnp.where(new_lane, pltpu.roll(vst, shift_new, 1),
                               pltpu.roll(vt_old, WINDOW - t_len, 1))
    return elems


def _decode_outputs(elems, sinks_ref):
    t_len = DEC_SEQ
    vgroup = lax.broadcasted_iota(jnp.int32, (1, KV_WIDTH), 1) >> 6
    skcol = jnp.concatenate(
        [jnp.full((t_len, 1), sinks_ref[0, GROUP * g + j], F32)
         for j in range(GROUP) for g in range(N_KV_HEADS)], axis=0)
    outs = []
    for s, vals in elems:
        p, inv = _softmax_parts(s, skcol)
        o = _dot_nt(p.astype(BF16), vals) * inv
        o_j = []
        for j in range(GROUP):
            acc = None
            for g in range(N_KV_HEADS):
                r = (j * N_KV_HEADS + g) * t_len
                piece = jnp.where(vgroup == g, o[r:r + t_len], 0.0)
                acc = piece if acc is None else acc + piece
            o_j.append(acc)
        outs.append(jnp.concatenate(o_j, axis=1))
    return jnp.concatenate(outs, axis=0)


def _gate_weight_copy(w_hbm, wstage, wsem, chunk):
    slot = chunk % GW_SLOTS
    return pltpu.make_async_copy(
        w_hbm.at[0, pl.ds(GW_CHUNK * chunk, GW_CHUNK), pl.ds(G_B, G_END - G_B)],
        wstage.at[slot], wsem.at[slot])


def _load_gate_weights(w_hbm, wg_s, wstage, wsem):
    n_chunks = D_MODEL // GW_CHUNK
    ahead = GW_SLOTS - 1
    for c in range(ahead):
        _gate_weight_copy(w_hbm, wstage, wsem, c).start()
    for c in range(n_chunks):
        if c + ahead < n_chunks:
            _gate_weight_copy(w_hbm, wstage, wsem, c + ahead).start()
        _gate_weight_copy(w_hbm, wstage, wsem, c).wait()
        wg_s[GW_CHUNK * c:GW_CHUNK * (c + 1), :] = wstage[c % GW_SLOTS].astype(BF16)


def _layer_kernel(sinks_ref, x_ref, xs_ref, cos_ref, sin_ref, ck_ref, cv_ref, st_ref, w_hbm,
                  meta_ref, cos0_ref, sin0_ref, coss_ref, sins_ref, nw_ref, qnw_ref, knw_ref, cw_ref,
                  wq_ref, wkv_ref, wza_ref, wpa_ref, wpb_ref, wo_ref, bd_ref,
                  y_ref, ys_ref, nkt_ref, nvt_ref, nc_ref, nck_ref, ncv_ref, ncs_ref,
                  kprev, vprev, utail, wg_s, wstage, wsem):
    i = pl.program_id(0)
    nw = nw_ref[...]
    bd = bd_ref[...]
    knw = knw_ref[...]

    @pl.when(i == 0)
    def _init():
        _load_gate_weights(w_hbm, wg_s, wstage, wsem)
        x0 = jnp.concatenate([jnp.zeros((LEAD, D_MODEL), F32), meta_ref[...]], axis=0)
        xn0 = _rms_rows(x0, nw).astype(BF16)
        kv0 = _dot(xn0, wkv_ref[...])
        k0 = _rope_split(_head_rms_split(kv0[:, :KV_WIDTH], bd, knw), cos0_ref[...], sin0_ref[...])
        v0 = kv0[:, KV_WIDTH:]
        c0 = _dot(xn0, wg_s[:, G_C - G_B:G_H - G_B])
        h0 = _dot(xn0, wg_s[:, G_H - G_B:G_ZB - G_B])
        row = lax.broadcasted_iota(jnp.int32, (BLOCK, D_MODEL), 0)
        u0 = jnp.where(row >= LEAD, c0 * h0, 0.0)
        kprev[...] = k0
        vprev[...] = v0
        utail[...] = u0[BLOCK - SUBLANES:BLOCK]

    cw = cw_ref[0]
    row8 = lax.broadcasted_iota(jnp.int32, (SUBLANES, D_MODEL), 0)
    t_io = lax.broadcasted_iota(jnp.int32, (DEC_PER_STEP, DEC_SEQ, D_MODEL), 1)
    k_prev = kprev[...].astype(BF16)
    v_prev = vprev[...].astype(BF16)
    tail = utail[...]
    def front(h):
        ps = slice(TM_SUB * h, TM_SUB * (h + 1))
        with_dec = h == SUB_TILES - 1
        if with_dec:
            x = jnp.concatenate([x_ref[ps, :], xs_ref[...]], axis=0)
            cos = jnp.concatenate([cos_ref[ps, :], coss_ref[...]], axis=0)
            sin = jnp.concatenate([sin_ref[ps, :], sins_ref[...]], axis=0)
        else:
            x = x_ref[ps, :]
            cos = cos_ref[ps, :]
            sin = sin_ref[ps, :]
        xn = _rms_rows(x, nw).astype(BF16)

        q = _head_rms_split(_dot(xn, wq_ref[...]), bd, qnw_ref[...])
        q = _rope_split(q, cos * Q_SCALE, sin * Q_SCALE)
        kv = _dot(xn, wkv_ref[...])
        k = _rope_split(_head_rms_split(kv[:, :KV_WIDTH], bd, knw), cos, sin)
        v = kv[:, KV_WIDTH:]
        kb = k[:TM_SUB].astype(BF16)
        vb = v[:TM_SUB].astype(BF16)
        return ps, with_dec, x, xn, q, k, v, kb, vb

    def mid(h, fr, k_prev, v_prev, tail):
        ps, with_dec, x, xn, q, k, v, kb, vb = fr
        first = (i == 0) if h == 0 else False
        o_a = _prompt_attention(q[:TM_SUB].astype(BF16), kb, vb, k_prev, v_prev, sinks_ref, first)
        if with_dec:
            elems = _decode_scores(q[TM_SUB:], k[TM_SUB:], v[TM_SUB:], ck_ref, cv_ref, nck_ref,
                                   ncv_ref)
        c, hc, b, z_b = [_dot(xn, wg_s[:, lo - G_B:lo - G_B + D_MODEL])
                         for lo in (G_C, G_H, G_B, G_ZB)]
        if with_dec:
            o_a = jnp.concatenate([o_a, _decode_outputs(elems, sinks_ref)], axis=0)
        k_last = k[TM_SUB - BLOCK:TM_SUB]
        v_last = v[TM_SUB - BLOCK:TM_SUB]

        u_all = c * hc
        u = u_all[:TM_SUB]
        prev1 = tail[SUBLANES - 1:SUBLANES]
        prev2 = tail[SUBLANES - 2:SUBLANES - 1]
        r1 = pltpu.roll(u, 1, 0)
        r2 = pltpu.roll(u, 2, 0)
        um1 = jnp.concatenate(
            [jnp.where(row8 == 0, prev1, r1[:SUBLANES]), r1[SUBLANES:]], axis=0)
        um2 = jnp.concatenate(
            [jnp.where(row8 == 0, prev2, jnp.where(row8 == 1, prev1, r2[:SUBLANES])),
             r2[SUBLANES:]], axis=0)
        conv = cw[0:1] * um2 + cw[1:2] * um1 + cw[2:3] * u
        tail = u[TM_SUB - SUBLANES:]
        if with_dec:
            u3 = u_all[TM_SUB:].reshape(DEC_PER_STEP, DEC_SEQ, D_MODEL)
            st = st_ref[...]
            s_m2 = st[:, 0:1, :]
            s_m1 = st[:, 1:2, :]
            us1 = jnp.where(t_io == 0, s_m1, pltpu.roll(u3, 1, 1))
            us2 = jnp.where(t_io == 0, s_m2, jnp.where(t_io == 1, s_m1, pltpu.roll(u3, 2, 1)))
            conv_s = (cw[0:1] * us2 + cw[1:2] * us1 + cw[2:3] * u3).reshape(TS, D_MODEL)
            ncs_ref[...] = u3[:, DEC_SEQ - (CONV_K - 1):, :]
            conv = jnp.concatenate([conv, conv_s], axis=0)
        return o_a, conv, b, z_b, k_last, v_last, tail

    def back(fr, md):
        ps, with_dec, x, xn = fr[:4]
        o_a, conv, b, z_b = md[:4]
        z_a = _dot(xn, wza_ref[...])
        br_a = _dot((o_a * _silu(z_a)).astype(BF16), wpa_ref[...])
        br_b = _dot(((b * conv) * _silu(z_b)).astype(BF16), wpb_ref[...])
        g_a = _dot(xn, wg_s[:, G_GA - G_B:G_GB - G_B])
        g_b = _dot(xn, wg_s[:, G_GB - G_B:G_END - G_B])
        mixed = (jax.nn.sigmoid(g_a) * br_a + jax.nn.sigmoid(g_b) * br_b).astype(BF16)
        y = x + _dot(mixed, wo_ref[...])
        y_ref[ps, :] = y[:TM_SUB]
        if with_dec:
            ys_ref[...] = y[TM_SUB:]

    fronts = {0: front(0)}
    mids = {}
    for h in range(SUB_TILES):
        if h:
            k_prev = fronts[h - 1][7][TM_SUB - BLOCK:]
            v_prev = fronts[h - 1][8][TM_SUB - BLOCK:]
            tail = mids[h - 1][6]
        mids[h] = mid(h, fronts[h], k_prev, v_prev, tail)
        if h + 1 < SUB_TILES:
            fronts[h + 1] = front(h + 1)
        back(fronts[h], mids[h])
    k_last, v_last, tail = mids[SUB_TILES - 1][4:7]

    kprev[...] = k_last
    vprev[...] = v_last
    utail[...] = tail
    nkt_ref[...] = _rows_split_to_natural(k_last.T)
    nvt_ref[...] = v_last.T
    nc_ref[...] = tail[SUBLANES - (CONV_K - 1):]


def _const_spec(shape):
    zeros = (0,) * len(shape)
    return pl.BlockSpec(shape, lambda i: zeros, pipeline_mode=pl.Buffered(1))


def _layer_call(x, xs, tabs, ck, cv, st, w_in, meta, tabs0, tabs_s, nw, qnw, knw, cw, sinks,
                wq, wk, wza, wpa, wpb, wo, bd):
    tm = TM_PROMPT
    assert N_STEPS * DEC_PER_STEP == DEC_BATCH and TS % (2 * SUBLANES) == 0 and TS <= LANES
    assert TM_SUB % BLOCK == 0
    row_spec = lambda rows, w: pl.BlockSpec((rows, w), lambda i: (i, 0))
    cache_spec = pl.BlockSpec((DEC_PER_STEP, KV_WIDTH, WINDOW), lambda i: (i, 0, 0))
    st_spec = pl.BlockSpec((DEC_PER_STEP, CONV_K - 1, D_MODEL), lambda i: (i, 0, 0))
    consts = (meta, tabs0[0], tabs0[1], tabs_s[0], tabs_s[1], nw, qnw, knw, cw,
              wq, wk, wza, wpa, wpb, wo, bd)
    in_specs = [pl.BlockSpec(memory_space=pltpu.SMEM),
                row_spec(tm, D_MODEL), row_spec(TS, D_MODEL), row_spec(tm, LANES), row_spec(tm, LANES),
                cache_spec, cache_spec, st_spec, pl.BlockSpec(memory_space=pl.ANY)]
    in_specs += [_const_spec(a.shape) for a in consts]
    resident = lambda shape: pl.BlockSpec(shape, lambda i: (0, 0))
    out_specs = [row_spec(tm, D_MODEL), row_spec(TS, D_MODEL),
                 resident((KV_WIDTH, BLOCK)), resident((KV_WIDTH, BLOCK)),
                 resident((CONV_K - 1, D_MODEL)), cache_spec, cache_spec, st_spec]
    cache_sds = jax.ShapeDtypeStruct((DEC_BATCH, KV_WIDTH, WINDOW), F32)
    out_shape = [jax.ShapeDtypeStruct((SEQ, D_MODEL), F32),
                 jax.ShapeDtypeStruct((DEC_BATCH * DEC_SEQ, D_MODEL), F32),
                 jax.ShapeDtypeStruct((KV_WIDTH, BLOCK), F32),
                 jax.ShapeDtypeStruct((KV_WIDTH, BLOCK), F32),
                 jax.ShapeDtypeStruct((CONV_K - 1, D_MODEL), F32), cache_sds, cache_sds,
                 jax.ShapeDtypeStruct((DEC_BATCH, CONV_K - 1, D_MODEL), F32)]
    return pl.pallas_call(
        _layer_kernel,
        grid=(N_STEPS,),
        in_specs=in_specs,
        out_specs=out_specs,
        out_shape=out_shape,
        scratch_shapes=[pltpu.VMEM((BLOCK, KV_WIDTH), F32),
                        pltpu.VMEM((BLOCK, KV_WIDTH), F32),
                        pltpu.VMEM((SUBLANES, D_MODEL), F32),
                        pltpu.VMEM((D_MODEL, G_END - G_B), BF16),
                        pltpu.VMEM((GW_SLOTS, GW_CHUNK, G_END - G_B), F32),
                        pltpu.SemaphoreType.DMA((GW_SLOTS,))],
        compiler_params=pltpu.CompilerParams(
            dimension_semantics=("arbitrary",), vmem_limit_bytes=VMEM_LIMIT),
        name="hybrid_layer",
    )(sinks, x, xs, tabs[0], tabs[1], ck, cv, st, w_in, *consts)


def _rope_tables(positions):
    inv = np.power(ROPE_THETA, -np.arange(HALF, dtype=np.float64) * (2.0 / HEAD_DIM))
    ang = np.asarray(positions, dtype=np.float64)[:, None] * inv[None, :]
    cos = np.tile(np.cos(ang), (1, LANES // HALF))
    sin = np.tile(np.sin(ang), (1, LANES // HALF))
    return jnp.asarray(cos, F32), jnp.asarray(sin, F32)


def _head_mean_matrix():
    grp = np.arange(256) // HALF
    return jnp.asarray((grp[:, None] == grp[None, :]) / float(HEAD_DIM), BF16)


def kernel(x_prompt, x_sample, cache_k, cache_v, state_conv, meta_tokens, norm_w, w_in,
           q_norm_w, k_norm_w, sinks, conv_w, w_proj_a, w_proj_b, w_out):
    assert x_prompt.shape == (1, SEQ, D_MODEL) and x_sample.shape == (DEC_BATCH, DEC_SEQ, D_MODEL)
    assert w_in.shape[0] == 1, "single layer"
    wg = w_in[0, :, :G_B].astype(BF16)
    wq = (wg[:, :W_K].reshape(D_MODEL, N_KV_HEADS, GROUP, 2, HALF)
          .transpose(0, 2, 3, 1, 4).reshape(D_MODEL, D_MODEL))
    wk = (wg[:, W_K:W_V].reshape(D_MODEL, N_KV_HEADS, 2, HALF)
          .transpose(0, 2, 1, 3).reshape(D_MODEL, KV_WIDTH))
    wk = jnp.concatenate([wk, wg[:, W_V:W_ZA]], axis=1)
    wza = (wg[:, W_ZA:G_B].reshape(D_MODEL, N_KV_HEADS, GROUP, HEAD_DIM)
           .transpose(0, 2, 1, 3).reshape(D_MODEL, D_MODEL))
    wpa = (w_proj_a[0].astype(BF16).reshape(N_KV_HEADS, GROUP, HEAD_DIM, D_MODEL)
           .transpose(1, 0, 2, 3).reshape(D_MODEL, D_MODEL))
    wpb = w_proj_b[0].astype(BF16)
    wo = w_out[0].astype(BF16)

    nw = norm_w[0].reshape(1, D_MODEL)
    split = lambda v: jnp.broadcast_to(v.reshape(2, 1, HALF), (2, N_KV_HEADS, HALF)).reshape(1, 256)

    to_tiles = lambda c: c[0].transpose(0, 2, 3, 1).reshape(DEC_BATCH, KV_WIDTH, WINDOW)
    from_tiles = lambda c: (c.reshape(DEC_BATCH, N_KV_HEADS, HEAD_DIM, WINDOW)
                            .transpose(0, 3, 1, 2)[None])
    from_tile = lambda t: t.reshape(N_KV_HEADS, HEAD_DIM, WINDOW).transpose(2, 0, 1)

    tabs_p = _rope_tables(np.arange(SEQ) + N_META)
    tabs_0 = _rope_tables(np.arange(BLOCK) - LEAD)
    tabs_s = _rope_tables(PAST_LEN + (np.arange(TS) % DEC_SEQ))
    y_p, y_s, nkt_p, nvt_p, nc_p, nk_s, nv_s, nc_s = _layer_call(
        x_prompt[0], x_sample.reshape(DEC_BATCH * DEC_SEQ, D_MODEL), tabs_p,
        to_tiles(cache_k), to_tiles(cache_v), state_conv[0], w_in,
        meta_tokens.astype(x_prompt.dtype), tabs_0, tabs_s,
        nw, split(q_norm_w), split(k_norm_w), conv_w, sinks,
        wq, wk, wza, wpa, wpb, wo, _head_mean_matrix())

    return (y_p.reshape(1, SEQ, D_MODEL),
            y_s.reshape(DEC_BATCH, DEC_SEQ, D_MODEL),
            from_tile(nkt_p).reshape(1, 1, WINDOW, N_KV_HEADS, HEAD_DIM),
            from_tile(nvt_p).reshape(1, 1, WINDOW, N_KV_HEADS, HEAD_DIM),
            nc_p.reshape(1, 1, CONV_K - 1, D_MODEL),
            from_tiles(nk_s),
            from_tiles(nv_s),
            nc_s.reshape(1, DEC_BATCH, CONV_K - 1, D_MODEL))
```

```python
import functools

import numpy as np
import jax
import jax.numpy as jnp
from jax import lax
from jax.experimental import pallas as pl
from jax.experimental.pallas import tpu as pltpu

D_MODEL = 1024
SEQ = 16384
DEC_BATCH = 128
DEC_SEQ = 8
PAST_LEN = 16384
N_HEADS = 16
N_KV_HEADS = 4
GROUP = N_HEADS // N_KV_HEADS
HEAD_DIM = 64
HALF = HEAD_DIM // 2
KV_WIDTH = N_KV_HEADS * HEAD_DIM
WINDOW = 128
BLOCK = 128
ROPE_THETA = 10000.0
CONV_K = 3
N_META = 16
LEAD = BLOCK - N_META
EPS = 1e-6
NEG_INF = -1e30
Q_SCALE = HEAD_DIM ** -0.5

W_K = D_MODEL
W_V = W_K + KV_WIDTH
W_ZA = W_V + KV_WIDTH
G_B = W_ZA + D_MODEL
G_C = G_B + D_MODEL
G_H = G_C + D_MODEL
G_ZB = G_H + D_MODEL
G_GA = G_ZB + D_MODEL
G_GB = G_GA + D_MODEL
G_END = G_GB + D_MODEL

LANES = 128
SUBLANES = 8
VMEM_BYTES_V7X = 64 * 1024 * 1024

TM_PROMPT = 512
N_STEPS = SEQ // TM_PROMPT
DEC_PER_STEP = DEC_BATCH // N_STEPS
TS = DEC_PER_STEP * DEC_SEQ
GW_CHUNK = 32
GW_SLOTS = 8
SUB_TILES = 2
TM_SUB = TM_PROMPT // SUB_TILES
VMEM_LIMIT = VMEM_BYTES_V7X - 4 * 1024 * 1024

BF16 = jnp.bfloat16
F32 = jnp.float32


def _dot(a, b):
    return jnp.dot(a, b, preferred_element_type=F32)


def _dot_nt(a, b):
    return lax.dot_general(a, b, (((1,), (1,)), ((), ())), preferred_element_type=F32)


def _rms_rows(x, w):
    ms = jnp.mean(x * x, axis=-1, keepdims=True)
    return (x * lax.rsqrt(ms + EPS)) * w


def _head_rms_split(t, bd, w):
    n = t.shape[1] // 256
    r = t.shape[0]
    halves = [(t[:, 256 * c:256 * c + LANES], t[:, 256 * c + LANES:256 * (c + 1)]) for c in range(n)]
    pieces = [a * a + b * b for a, b in halves]
    if n == 1:
        pieces = [pieces[0], pieces[0]]
    pair_ms = [_dot(jnp.concatenate(pieces[2 * i:2 * i + 2], axis=1).astype(BF16), bd)
               for i in range(len(pieces) // 2)]
    ms = [pair_ms[i // 2][:, LANES * (i % 2):LANES * (i % 2 + 1)] for i in range(n)]
    out = []
    for (a, b), m in zip(halves, ms):
        scale = lax.rsqrt(m + EPS)
        out.append((a * scale) * w[:, :LANES])
        out.append((b * scale) * w[:, LANES:])
    return jnp.concatenate(out, axis=1)


def _rope_split(t, cos, sin):
    out = []
    for c in range(t.shape[1] // 256):
        a = t[:, 256 * c:256 * c + LANES]
        b = t[:, 256 * c + LANES:256 * (c + 1)]
        out.append(a * cos - b * sin)
        out.append(b * cos + a * sin)
    return jnp.concatenate(out, axis=1)


def _lane_group_masks(group_of_lane, dtype):
    return [(group_of_lane == g).astype(dtype) for g in range(N_KV_HEADS)]


def _softmax_parts(s, sk):
    m = jnp.maximum(jnp.max(s, axis=-1, keepdims=True), sk)
    p = jnp.exp(s - m)
    den = jnp.sum(p, axis=-1, keepdims=True) + jnp.exp(sk - m)
    return p, 1.0 / den


def _silu(z):
    return z * jax.nn.sigmoid(z)


def _rows_split_to_natural(t):
    return jnp.concatenate(
        [t[LANES * half + HALF * g:LANES * half + HALF * (g + 1)]
         for g in range(N_KV_HEADS) for half in range(2)], axis=0)


def _rows_natural_to_split(t):
    return jnp.concatenate(
        [t[HEAD_DIM * g + HALF * half:HEAD_DIM * g + HALF * (half + 1)]
         for half in range(2) for g in range(N_KV_HEADS)], axis=0)


def _prompt_attention(q, kb, vb, kprev, vprev, sinks_ref, first_step):
    tm = q.shape[0]
    lane = lax.broadcasted_iota(jnp.int32, (1, 2 * BLOCK), 1)
    kmasks = _lane_group_masks((lane & (LANES - 1)) >> 5, BF16)
    vmasks = _lane_group_masks(lane >> 6, BF16)
    vgroup = lane >> 6
    r_io = lax.broadcasted_iota(jnp.int32, (BLOCK, 2 * BLOCK), 0)
    c_io = lax.broadcasted_iota(jnp.int32, (BLOCK, 2 * BLOCK), 1)
    band = (c_io >= r_io) & (c_io <= r_io + WINDOW)
    first_band = band & (c_io >= jnp.where(first_step, LEAD, 0))

    o_blocks = []
    for b in range(tm // BLOCK):
        lo = b * BLOCK
        if b == 0:
            kpair = jnp.concatenate([kprev, kb[:BLOCK]], axis=0)
            vpair = jnp.concatenate([vprev, vb[:BLOCK]], axis=0)
            mask = first_band
        else:
            kpair = kb[lo - BLOCK:lo + BLOCK]
            vpair = vb[lo - BLOCK:lo + BLOCK]
            mask = band
        kbd = jnp.concatenate([kpair * km for km in kmasks], axis=0)
        vbd = jnp.concatenate([vpair * vm for vm in vmasks], axis=0)
        qs = jnp.concatenate([q[lo:lo + BLOCK, 256 * j:256 * (j + 1)] for j in range(GROUP)],
                             axis=0)
        s = _dot_nt(qs, kbd)
        p_rows = []
        mults = []
        for j in range(GROUP):
            p_cols = []
            invs = []
            for g in range(N_KV_HEADS):
                sl = s[BLOCK * j:BLOCK * (j + 1), 256 * g:256 * (g + 1)]
                sl = jnp.where(mask, sl, NEG_INF)
                p, inv = _softmax_parts(sl, sinks_ref[0, GROUP * g + j])
                p_cols.append(p.astype(BF16))
                invs.append(inv)
            p_rows.append(jnp.concatenate(p_cols, axis=1))
            mults.append(jnp.where(vgroup == 0, invs[0],
                                   jnp.where(vgroup == 1, invs[1],
                                             jnp.where(vgroup == 2, invs[2], invs[3]))))
        p_all = jnp.concatenate(p_rows, axis=0)
        kw = 2 * BLOCK
        parts = [_dot(p_all[:, kw * g:kw * (g + 1)], vbd[kw * g:kw * (g + 1)])
                 for g in range(N_KV_HEADS)]
        o = (parts[0] + parts[1]) + (parts[2] + parts[3])
        o_blocks.append(jnp.concatenate(
            [o[BLOCK * j:BLOCK * (j + 1)] * mults[j] for j in range(GROUP)], axis=1))
    return jnp.concatenate(o_blocks, axis=0)


def _decode_scores(qs, ks, vs, ck_ref, cv_ref, nck_ref, ncv_ref):
    t_len = DEC_SEQ
    n_exp = GROUP * N_KV_HEADS * t_len
    pad = jnp.zeros((LANES - qs.shape[0], KV_WIDTH), F32)
    kst = jnp.concatenate([ks, pad], axis=0).T
    vst = jnp.concatenate([vs, pad], axis=0).T
    kst_nat = _rows_split_to_natural(kst)
    kst_b = kst.astype(BF16)
    vst_b = vst.astype(BF16)

    lane = lax.broadcasted_iota(jnp.int32, (1, KV_WIDTH), 1)
    kgroup = (lane & (LANES - 1)) >> 5
    t_io = lax.broadcasted_iota(jnp.int32, (n_exp, WINDOW), 0) & (t_len - 1)
    c_io = lax.broadcasted_iota(jnp.int32, (n_exp, WINDOW), 1)
    old_ok = c_io >= t_io
    new_lane = lax.broadcasted_iota(jnp.int32, (KV_WIDTH, WINDOW), 1) >= WINDOW - t_len

    elems = []
    for b in range(qs.shape[0] // t_len):
        r0 = b * t_len
        qn = qs[r0:r0 + t_len]
        q_exp = jnp.concatenate(
            [jnp.where(kgroup == g, qn[:, 256 * j:256 * (j + 1)], 0.0)
             for j in range(GROUP) for g in range(N_KV_HEADS)], axis=0).astype(BF16)
        kt_old = ck_ref[b]
        vt_old = cv_ref[b]
        keys = jnp.concatenate([_rows_natural_to_split(kt_old).astype(BF16), kst_b], axis=1)
        vals = jnp.concatenate([vt_old.astype(BF16), vst_b], axis=1)
        s = _dot(q_exp, keys)
        new_ok = (c_io >= r0) & (c_io - t_io <= r0)
        s = jnp.where(jnp.concatenate([old_ok, new_ok], axis=1), s, NEG_INF)
        elems.append((s, vals))
        shift_new = WINDOW - t_len - r0
        nck_ref[b] = jnp.where(new_lane, pltpu.roll(kst_nat, shift_new, 1),
                               pltpu.roll(kt_old, WINDOW - t_len, 1))
        ncv_ref[b] = jnp.where(new_lane, pltpu.roll(vst, shift_new, 1),
                               pltpu.roll(vt_old, WINDOW - t_len, 1))
    return elems


def _decode_outputs(elems, sinks_ref):
    t_len = DEC_SEQ
    vgroup = lax.broadcasted_iota(jnp.int32, (1, KV_WIDTH), 1) >> 6
    skcol = jnp.concatenate(
        [jnp.full((t_len, 1), sinks_ref[0, GROUP * g + j], F32)
         for j in range(GROUP) for g in range(N_KV_HEADS)], axis=0)
    outs = []
    for s, vals in elems:
        p, inv = _softmax_parts(s, skcol)
        o = _dot_nt(p.astype(BF16), vals) * inv
        o_j = []
        for j in range(GROUP):
            acc = None
            for g in range(N_KV_HEADS):
                r = (j * N_KV_HEADS + g) * t_len
                piece = jnp.where(vgroup == g, o[r:r + t_len], 0.0)
                acc = piece if acc is None else acc + piece
            o_j.append(acc)
        outs.append(jnp.concatenate(o_j, axis=1))
    return jnp.concatenate(outs, axis=0)


def _gate_weight_copy(w_hbm, wstage, wsem, chunk):
    slot = chunk % GW_SLOTS
    return pltpu.make_async_copy(
        w_hbm.at[0, pl.ds(GW_CHUNK * chunk, GW_CHUNK), pl.ds(G_B, G_END - G_B)],
        wstage.at[slot], wsem.at[slot])


def _load_gate_weights(w_hbm, wg_s, wstage, wsem):
    n_chunks = D_MODEL // GW_CHUNK
    ahead = GW_SLOTS - 1
    for c in range(ahead):
        _gate_weight_copy(w_hbm, wstage, wsem, c).start()
    for c in range(n_chunks):
        if c + ahead < n_chunks:
            _gate_weight_copy(w_hbm, wstage, wsem, c + ahead).start()
        _gate_weight_copy(w_hbm, wstage, wsem, c).wait()
        wg_s[GW_CHUNK * c:GW_CHUNK * (c + 1), :] = wstage[c % GW_SLOTS].astype(BF16)


def _layer_kernel(sinks_ref, x_ref, xs_ref, cos_ref, sin_ref, ck_ref, cv_ref, st_ref, w_hbm,
                  meta_ref, cos0_ref, sin0_ref, coss_ref, sins_ref, nw_ref, qnw_ref, knw_ref, cw_ref,
                  wq_ref, wkv_ref, wza_ref, wpa_ref, wpb_ref, wo_ref, bd_ref,
                  y_ref, ys_ref, nkt_ref, nvt_ref, nc_ref, nck_ref, ncv_ref, ncs_ref,
                  kprev, vprev, utail, wg_s, wstage, wsem):
    i = pl.program_id(0)
    nw = nw_ref[...]
    bd = bd_ref[...]
    knw = knw_ref[...]

    @pl.when(i == 0)
    def _init():
        _load_gate_weights(w_hbm, wg_s, wstage, wsem)
        x0 = jnp.concatenate([jnp.zeros((LEAD, D_MODEL), F32), meta_ref[...]], axis=0)
        xn0 = _rms_rows(x0, nw).astype(BF16)
        kv0 = _dot(xn0, wkv_ref[...])
        k0 = _rope_split(_head_rms_split(kv0[:, :KV_WIDTH], bd, knw), cos0_ref[...], sin0_ref[...])
        v0 = kv0[:, KV_WIDTH:]
        c0 = _dot(xn0, wg_s[:, G_C - G_B:G_H - G_B])
        h0 = _dot(xn0, wg_s[:, G_H - G_B:G_ZB - G_B])
        row = lax.broadcasted_iota(jnp.int32, (BLOCK, D_MODEL), 0)
        u0 = jnp.where(row >= LEAD, c0 * h0, 0.0)
        kprev[...] = k0
        vprev[...] = v0
        utail[...] = u0[BLOCK - SUBLANES:BLOCK]

    cw = cw_ref[0]
    row8 = lax.broadcasted_iota(jnp.int32, (SUBLANES, D_MODEL), 0)
    t_io = lax.broadcasted_iota(jnp.int32, (DEC_PER_STEP, DEC_SEQ, D_MODEL), 1)
    k_prev = kprev[...].astype(BF16)
    v_prev = vprev[...].astype(BF16)
    tail = utail[...]
    def front(h):
        ps = slice(TM_SUB * h, TM_SUB * (h + 1))
        with_dec = h == SUB_TILES - 1
        if with_dec:
            x = jnp.concatenate([x_ref[ps, :], xs_ref[...]], axis=0)
            cos = jnp.concatenate([cos_ref[ps, :], coss_ref[...]], axis=0)
            sin = jnp.concatenate([sin_ref[ps, :], sins_ref[...]], axis=0)
        else:
            x = x_ref[ps, :]
            cos = cos_ref[ps, :]
            sin = sin_ref[ps, :]
        xn = _rms_rows(x, nw).astype(BF16)

        q = _head_rms_split(_dot(xn, wq_ref[...]), bd, qnw_ref[...])
        q = _rope_split(q, cos * Q_SCALE, sin * Q_SCALE)
        kv = _dot(xn, wkv_ref[...])
        k = _rope_split(_head_rms_split(kv[:, :KV_WIDTH], bd, knw), cos, sin)
        v = kv[:, KV_WIDTH:]
        kb = k[:TM_SUB].astype(BF16)
        vb = v[:TM_SUB].astype(BF16)
        return ps, with_dec, x, xn, q, k, v, kb, vb

    def mid(h, fr, k_prev, v_prev, tail):
        ps, with_dec, x, xn, q, k, v, kb, vb = fr
        first = (i == 0) if h == 0 else False
        o_a = _prompt_attention(q[:TM_SUB].astype(BF16), kb, vb, k_prev, v_prev, sinks_ref, first)
        if with_dec:
            elems = _decode_scores(q[TM_SUB:], k[TM_SUB:], v[TM_SUB:], ck_ref, cv_ref, nck_ref,
                                   ncv_ref)
        c, hc, b, z_b = [_dot(xn, wg_s[:, lo - G_B:lo - G_B + D_MODEL])
                         for lo in (G_C, G_H, G_B, G_ZB)]
        if with_dec:
            o_a = jnp.concatenate([o_a, _decode_outputs(elems, sinks_ref)], axis=0)
        k_last = k[TM_SUB - BLOCK:TM_SUB]
        v_last = v[TM_SUB - BLOCK:TM_SUB]

        u_all = c * hc
        u = u_all[:TM_SUB]
        prev1 = tail[SUBLANES - 1:SUBLANES]
        prev2 = tail[SUBLANES - 2:SUBLANES - 1]
        r1 = pltpu.roll(u, 1, 0)
        r2 = pltpu.roll(u, 2, 0)
        um1 = jnp.concatenate(
            [jnp.where(row8 == 0, prev1, r1[:SUBLANES]), r1[SUBLANES:]], axis=0)
        um2 = jnp.concatenate(
            [jnp.where(row8 == 0, prev2, jnp.where(row8 == 1, prev1, r2[:SUBLANES])),
             r2[SUBLANES:]], axis=0)
        conv = cw[0:1] * um2 + cw[1:2] * um1 + cw[2:3] * u
        tail = u[TM_SUB - SUBLANES:]
        if with_dec:
            u3 = u_all[TM_SUB:].reshape(DEC_PER_STEP, DEC_SEQ, D_MODEL)
            st = st_ref[...]
            s_m2 = st[:, 0:1, :]
            s_m1 = st[:, 1:2, :]
            us1 = jnp.where(t_io == 0, s_m1, pltpu.roll(u3, 1, 1))
            us2 = jnp.where(t_io == 0, s_m2, jnp.where(t_io == 1, s_m1, pltpu.roll(u3, 2, 1)))
            conv_s = (cw[0:1] * us2 + cw[1:2] * us1 + cw[2:3] * u3).reshape(TS, D_MODEL)
            ncs_ref[...] = u3[:, DEC_SEQ - (CONV_K - 1):, :]
            conv = jnp.concatenate([conv, conv_s], axis=0)
        return o_a, conv, b, z_b, k_last, v_last, tail

    def back(fr, md):
        ps, with_dec, x, xn = fr[:4]
        o_a, conv, b, z_b = md[:4]
        z_a = _dot(xn, wza_ref[...])
        br_a = _dot((o_a * _silu(z_a)).astype(BF16), wpa_ref[...])
        br_b = _dot(((b * conv) * _silu(z_b)).astype(BF16), wpb_ref[...])
        g_a = _dot(xn, wg_s[:, G_GA - G_B:G_GB - G_B])
        g_b = _dot(xn, wg_s[:, G_GB - G_B:G_END - G_B])
        mixed = (jax.nn.sigmoid(g_a) * br_a + jax.nn.sigmoid(g_b) * br_b).astype(BF16)
        y = x + _dot(mixed, wo_ref[...])
        y_ref[ps, :] = y[:TM_SUB]
        if with_dec:
            ys_ref[...] = y[TM_SUB:]

    fronts = {0: front(0)}
    mids = {}
    for h in range(SUB_TILES):
        if h:
            k_prev = fronts[h - 1][7][TM_SUB - BLOCK:]
            v_prev = fronts[h - 1][8][TM_SUB - BLOCK:]
            tail = mids[h - 1][6]
        mids[h] = mid(h, fronts[h], k_prev, v_prev, tail)
        if h + 1 < SUB_TILES:
            fronts[h + 1] = front(h + 1)
        back(fronts[h], mids[h])
    k_last, v_last, tail = mids[SUB_TILES - 1][4:7]

    kprev[...] = k_last
    vprev[...] = v_last
    utail[...] = tail
    nkt_ref[...] = _rows_split_to_natural(k_last.T)
    nvt_ref[...] = v_last.T
    nc_ref[...] = tail[SUBLANES - (CONV_K - 1):]


def _const_spec(shape):
    zeros = (0,) * len(shape)
    return pl.BlockSpec(shape, lambda i: zeros, pipeline_mode=pl.Buffered(1))


def _layer_call(x, xs, tabs, ck, cv, st, w_in, meta, tabs0, tabs_s, nw, qnw, knw, cw, sinks,
                wq, wk, wza, wpa, wpb, wo, bd):
    tm = TM_PROMPT
    assert N_STEPS * DEC_PER_STEP == DEC_BATCH and TS % (2 * SUBLANES) == 0 and TS <= LANES
    assert TM_SUB % BLOCK == 0
    row_spec = lambda rows, w: pl.BlockSpec((rows, w), lambda i: (i, 0))
    cache_spec = pl.BlockSpec((DEC_PER_STEP, KV_WIDTH, WINDOW), lambda i: (i, 0, 0))
    st_spec = pl.BlockSpec((DEC_PER_STEP, CONV_K - 1, D_MODEL), lambda i: (i, 0, 0))
    consts = (meta, tabs0[0], tabs0[1], tabs_s[0], tabs_s[1], nw, qnw, knw, cw,
              wq, wk, wza, wpa, wpb, wo, bd)
    in_specs = [pl.BlockSpec(memory_space=pltpu.SMEM),
                row_spec(tm, D_MODEL), row_spec(TS, D_MODEL), row_spec(tm, LANES), row_spec(tm, LANES),
                cache_spec, cache_spec, st_spec, pl.BlockSpec(memory_space=pl.ANY)]
    in_specs += [_const_spec(a.shape) for a in consts]
    resident = lambda shape: pl.BlockSpec(shape, lambda i: (0, 0))
    out_specs = [row_spec(tm, D_MODEL), row_spec(TS, D_MODEL),
                 resident((KV_WIDTH, BLOCK)), resident((KV_WIDTH, BLOCK)),
                 resident((CONV_K - 1, D_MODEL)), cache_spec, cache_spec, st_spec]
    cache_sds = jax.ShapeDtypeStruct((DEC_BATCH, KV_WIDTH, WINDOW), F32)
    out_shape = [jax.ShapeDtypeStruct((SEQ, D_MODEL), F32),
                 jax.ShapeDtypeStruct((DEC_BATCH * DEC_SEQ, D_MODEL), F32),
                 jax.ShapeDtypeStruct((KV_WIDTH, BLOCK), F32),
                 jax.ShapeDtypeStruct((KV_WIDTH, BLOCK), F32),
                 jax.ShapeDtypeStruct((CONV_K - 1, D_MODEL), F32), cache_sds, cache_sds,
                 jax.ShapeDtypeStruct((DEC_BATCH, CONV_K - 1, D_MODEL), F32)]
    return pl.pallas_call(
        _layer_kernel,
        grid=(N_STEPS,),
        in_specs=in_specs,
        out_specs=out_specs,
        out_shape=out_shape,
        scratch_shapes=[pltpu.VMEM((BLOCK, KV_WIDTH), F32),
                        pltpu.VMEM((BLOCK, KV_WIDTH), F32),
                        pltpu.VMEM((SUBLANES, D_MODEL), F32),
                        pltpu.VMEM((D_MODEL, G_END - G_B), BF16),
                        pltpu.VMEM((GW_SLOTS, GW_CHUNK, G_END - G_B), F32),
                        pltpu.SemaphoreType.DMA((GW_SLOTS,))],
        compiler_params=pltpu.CompilerParams(
            dimension_semantics=("arbitrary",), vmem_limit_bytes=VMEM_LIMIT),
        name="hybrid_layer",
    )(sinks, x, xs, tabs[0], tabs[1], ck, cv, st, w_in, *consts)


def _rope_tables(positions):
    inv = np.power(ROPE_THETA, -np.arange(HALF, dtype=np.float64) * (2.0 / HEAD_DIM))
    ang = np.asarray(positions, dtype=np.float64)[:, None] * inv[None, :]
    cos = np.tile(np.cos(ang), (1, LANES // HALF))
    sin = np.tile(np.sin(ang), (1, LANES // HALF))
    return jnp.asarray(cos, F32), jnp.asarray(sin, F32)


def _head_mean_matrix():
    grp = np.arange(256) // HALF
    return jnp.asarray((grp[:, None] == grp[None, :]) / float(HEAD_DIM), BF16)


def kernel(x_prompt, x_sample, cache_k, cache_v, state_conv, meta_tokens, norm_w, w_in,
           q_norm_w, k_norm_w, sinks, conv_w, w_proj_a, w_proj_b, w_out):
    assert x_prompt.shape == (1, SEQ, D_MODEL) and x_sample.shape == (DEC_BATCH, DEC_SEQ, D_MODEL)
    assert w_in.shape[0] == 1, "single layer"
    wg = w_in[0, :, :G_B].astype(BF16)
    wq = (wg[:, :W_K].reshape(D_MODEL, N_KV_HEADS, GROUP, 2, HALF)
          .transpose(0, 2, 3, 1, 4).reshape(D_MODEL, D_MODEL))
    wk = (wg[:, W_K:W_V].reshape(D_MODEL, N_KV_HEADS, 2, HALF)
          .transpose(0, 2, 1, 3).reshape(D_MODEL, KV_WIDTH))
    wk = jnp.concatenate([wk, wg[:, W_V:W_ZA]], axis=1)
    wza = (wg[:, W_ZA:G_B].reshape(D_MODEL, N_KV_HEADS, GROUP, HEAD_DIM)
           .transpose(0, 2, 1, 3).reshape(D_MODEL, D_MODEL))
    wpa = (w_proj_a[0].astype(BF16).reshape(N_KV_HEADS, GROUP, HEAD_DIM, D_MODEL)
           .transpose(1, 0, 2, 3).reshape(D_MODEL, D_MODEL))
    wpb = w_proj_b[0].astype(BF16)
    wo = w_out[0].astype(BF16)

    nw = norm_w[0].reshape(1, D_MODEL)
    split = lambda v: jnp.broadcast_to(v.reshape(2, 1, HALF), (2, N_KV_HEADS, HALF)).reshape(1, 256)

    to_tiles = lambda c: c[0].transpose(0, 2, 3, 1).reshape(DEC_BATCH, KV_WIDTH, WINDOW)
    from_tiles = lambda c: (c.reshape(DEC_BATCH, N_KV_HEADS, HEAD_DIM, WINDOW)
                            .transpose(0, 3, 1, 2)[None])
    from_tile = lambda t: t.reshape(N_KV_HEADS, HEAD_DIM, WINDOW).transpose(2, 0, 1)

    tabs_p = _rope_tables(np.arange(SEQ) + N_META)
    tabs_0 = _rope_tables(np.arange(BLOCK) - LEAD)
    tabs_s = _rope_tables(PAST_LEN + (np.arange(TS) % DEC_SEQ))
    y_p, y_s, nkt_p, nvt_p, nc_p, nk_s, nv_s, nc_s = _layer_call(
        x_prompt[0], x_sample.reshape(DEC_BATCH * DEC_SEQ, D_MODEL), tabs_p,
        to_tiles(cache_k), to_tiles(cache_v), state_conv[0], w_in,
        meta_tokens.astype(x_prompt.dtype), tabs_0, tabs_s,
        nw, split(q_norm_w), split(k_norm_w), conv_w, sinks,
        wq, wk, wza, wpa, wpb, wo, _head_mean_matrix())

    return (y_p.reshape(1, SEQ, D_MODEL),
            y_s.reshape(DEC_BATCH, DEC_SEQ, D_MODEL),
            from_tile(nkt_p).reshape(1, 1, WINDOW, N_KV_HEADS, HEAD_DIM),
            from_tile(nvt_p).reshape(1, 1, WINDOW, N_KV_HEADS, HEAD_DIM),
            nc_p.reshape(1, 1, CONV_K - 1, D_MODEL),
            from_tiles(nk_s),
            from_tiles(nv_s),
            nc_s.reshape(1, DEC_BATCH, CONV_K - 1, D_MODEL))
```

```python
import functools

import numpy as np
import jax
import jax.numpy as jnp
from jax import lax
from jax.experimental import pallas as pl
from jax.experimental.pallas import tpu as pltpu

D_MODEL = 1024
SEQ = 16384
DEC_BATCH = 128
DEC_SEQ = 8
PAST_LEN = 16384
N_HEADS = 16
N_KV_HEADS = 4
GROUP = N_HEADS // N_KV_HEADS
HEAD_DIM = 64
HALF = HEAD_DIM // 2
KV_WIDTH = N_KV_HEADS * HEAD_DIM
WINDOW = 128
BLOCK = 128
ROPE_THETA = 10000.0
CONV_K = 3
N_META = 16
LEAD = BLOCK - N_META
EPS = 1e-6
NEG_INF = -1e30
Q_SCALE = HEAD_DIM ** -0.5

W_K = D_MODEL
W_V = W_K + KV_WIDTH
W_ZA = W_V + KV_WIDTH
G_B = W_ZA + D_MODEL
G_C = G_B + D_MODEL
G_H = G_C + D_MODEL
G_ZB = G_H + D_MODEL
G_GA = G_ZB + D_MODEL
G_GB = G_GA + D_MODEL
G_END = G_GB + D_MODEL

LANES = 128
SUBLANES = 8
VMEM_BYTES_V7X = 64 * 1024 * 1024

TM_PROMPT = 512
N_STEPS = SEQ // TM_PROMPT
DEC_PER_STEP = DEC_BATCH // N_STEPS
TS = DEC_PER_STEP * DEC_SEQ
GW_CHUNK = 64
GW_SLOTS = 4
SUB_TILES = 2
TM_SUB = TM_PROMPT // SUB_TILES
VMEM_LIMIT = VMEM_BYTES_V7X - 4 * 1024 * 1024

BF16 = jnp.bfloat16
F32 = jnp.float32


def _dot(a, b):
    return jnp.dot(a, b, preferred_element_type=F32)


def _dot_nt(a, b):
    return lax.dot_general(a, b, (((1,), (1,)), ((), ())), preferred_element_type=F32)


def _dot_quarters(a, w_ref, lo):
    return jnp.concatenate(
        [_dot(a, w_ref[:, lo + 256 * c:lo + 256 * (c + 1)]) for c in range(D_MODEL // 256)], axis=1)


def _rms_rows(x, w):
    ms = jnp.mean(x * x, axis=-1, keepdims=True)
    return (x * lax.rsqrt(ms + EPS)) * w


def _head_rms_split(t, bd, w):
    n = t.shape[1] // 256
    r = t.shape[0]
    halves = [(t[:, 256 * c:256 * c + LANES], t[:, 256 * c + LANES:256 * (c + 1)]) for c in range(n)]
    pieces = [a * a + b * b for a, b in halves]
    if n == 1:
        pieces = [pieces[0], pieces[0]]
    pair_ms = [_dot(jnp.concatenate(pieces[2 * i:2 * i + 2], axis=1).astype(BF16), bd)
               for i in range(len(pieces) // 2)]
    ms = [pair_ms[i // 2][:, LANES * (i % 2):LANES * (i % 2 + 1)] for i in range(n)]
    out = []
    for (a, b), m in zip(halves, ms):
        scale = lax.rsqrt(m + EPS)
        out.append((a * scale) * w[:, :LANES])
        out.append((b * scale) * w[:, LANES:])
    return jnp.concatenate(out, axis=1)


def _rope_split(t, cos, sin):
    out = []
    for c in range(t.shape[1] // 256):
        a = t[:, 256 * c:256 * c + LANES]
        b = t[:, 256 * c + LANES:256 * (c + 1)]
        out.append(a * cos - b * sin)
        out.append(b * cos + a * sin)
    return jnp.concatenate(out, axis=1)


def _lane_group_masks(group_of_lane, dtype):
    return [(group_of_lane == g).astype(dtype) for g in range(N_KV_HEADS)]


def _softmax_parts(s, sk):
    m = jnp.maximum(jnp.max(s, axis=-1, keepdims=True), sk)
    p = jnp.exp(s - m)
    den = jnp.sum(p, axis=-1, keepdims=True) + jnp.exp(sk - m)
    return p, 1.0 / den


def _silu(z):
    return z * jax.nn.sigmoid(z)


def _rows_split_to_natural(t):
    return jnp.concatenate(
        [t[LANES * half + HALF * g:LANES * half + HALF * (g + 1)]
         for g in range(N_KV_HEADS) for half in range(2)], axis=0)


def _rows_natural_to_split(t):
    return jnp.concatenate(
        [t[HEAD_DIM * g + HALF * half:HEAD_DIM * g + HALF * (half + 1)]
         for half in range(2) for g in range(N_KV_HEADS)], axis=0)


def _prompt_attention(q, kb, vb, kprev, vprev, sinks_ref, first_step):
    tm = q.shape[0]
    lane = lax.broadcasted_iota(jnp.int32, (1, 2 * BLOCK), 1)
    kmasks = _lane_group_masks((lane & (LANES - 1)) >> 5, BF16)
    vmasks = _lane_group_masks(lane >> 6, BF16)
    vgroup = lane >> 6
    r_io = lax.broadcasted_iota(jnp.int32, (BLOCK, 2 * BLOCK), 0)
    c_io = lax.broadcasted_iota(jnp.int32, (BLOCK, 2 * BLOCK), 1)
    band = (c_io >= r_io) & (c_io <= r_io + WINDOW)
    first_band = band & (c_io >= jnp.where(first_step, LEAD, 0))

    o_blocks = []
    for b in range(tm // BLOCK):
        lo = b * BLOCK
        if b == 0:
            kpair = jnp.concatenate([kprev, kb[:BLOCK]], axis=0)
            vpair = jnp.concatenate([vprev, vb[:BLOCK]], axis=0)
            mask = first_band
        else:
            kpair = kb[lo - BLOCK:lo + BLOCK]
            vpair = vb[lo - BLOCK:lo + BLOCK]
            mask = band
        kbd = jnp.concatenate([kpair * km for km in kmasks], axis=0)
        vbd = jnp.concatenate([vpair * vm for vm in vmasks], axis=0)
        qs = jnp.concatenate([q[lo:lo + BLOCK, 256 * j:256 * (j + 1)] for j in range(GROUP)],
                             axis=0)
        s = _dot_nt(qs, kbd)
        p_rows = []
        mults = []
        for j in range(GROUP):
            p_cols = []
            invs = []
            for g in range(N_KV_HEADS):
                sl = s[BLOCK * j:BLOCK * (j + 1), 256 * g:256 * (g + 1)]
                sl = jnp.where(mask, sl, NEG_INF)
                p, inv = _softmax_parts(sl, sinks_ref[0, GROUP * g + j])
                p_cols.append(p.astype(BF16))
                invs.append(inv)
            p_rows.append(jnp.concatenate(p_cols, axis=1))
            mults.append(jnp.where(vgroup == 0, invs[0],
                                   jnp.where(vgroup == 1, invs[1],
                                             jnp.where(vgroup == 2, invs[2], invs[3]))))
        p_all = jnp.concatenate(p_rows, axis=0)
        kw = 2 * BLOCK
        parts = [_dot(p_all[:, kw * g:kw * (g + 1)], vbd[kw * g:kw * (g + 1)])
                 for g in range(N_KV_HEADS)]
        o = (parts[0] + parts[1]) + (parts[2] + parts[3])
        o_blocks.append(jnp.concatenate(
            [o[BLOCK * j:BLOCK * (j + 1)] * mults[j] for j in range(GROUP)], axis=1))
    return jnp.concatenate(o_blocks, axis=0)


def _decode_scores(qs, ks, vs, ck_ref, cv_ref, nck_ref, ncv_ref):
    t_len = DEC_SEQ
    n_exp = GROUP * N_KV_HEADS * t_len
    pad = jnp.zeros((LANES - qs.shape[0], KV_WIDTH), F32)
    kst = jnp.concatenate([ks, pad], axis=0).T
    vst = jnp.concatenate([vs, pad], axis=0).T
    kst_nat = _rows_split_to_natural(kst)
    kst_b = kst.astype(BF16)
    vst_b = vst.astype(BF16)

    lane = lax.broadcasted_iota(jnp.int32, (1, KV_WIDTH), 1)
    kgroup = (lane & (LANES - 1)) >> 5
    t_io = lax.broadcasted_iota(jnp.int32, (n_exp, WINDOW), 0) & (t_len - 1)
    c_io = lax.broadcasted_iota(jnp.int32, (n_exp, WINDOW), 1)
    old_ok = c_io >= t_io
    new_lane = lax.broadcasted_iota(jnp.int32, (KV_WIDTH, WINDOW), 1) >= WINDOW - t_len

    elems = []
    for b in range(qs.shape[0] // t_len):
        r0 = b * t_len
        qn = qs[r0:r0 + t_len]
        q_exp = jnp.concatenate(
            [jnp.where(kgroup == g, qn[:, 256 * j:256 * (j + 1)], 0.0)
             for j in range(GROUP) for g in range(N_KV_HEADS)], axis=0).astype(BF16)
        kt_old = ck_ref[b]
        vt_old = cv_ref[b]
        keys = jnp.concatenate([_rows_natural_to_split(kt_old).astype(BF16), kst_b], axis=1)
        vals = jnp.concatenate([vt_old.astype(BF16), vst_b], axis=1)
        s = _dot(q_exp, keys)
        new_ok = (c_io >= r0) & (c_io - t_io <= r0)
        s = jnp.where(jnp.concatenate([old_ok, new_ok], axis=1), s, NEG_INF)
        elems.append((s, vals))
        shift_new = WINDOW - t_len - r0
        nck_ref[b] = jnp.where(new_lane, pltpu.roll(kst_nat, shift_new, 1),
                               pltpu.roll(kt_old, WINDOW - t_len, 1))
        ncv_ref[b] = jnp.where(new_lane, pltpu.roll(vst, shift_new, 1),
                               pltpu.roll(vt_old, WINDOW - t_len, 1))
    return elems


def _decode_outputs(elems, sinks_ref):
    t_len = DEC_SEQ
    vgroup = lax.broadcasted_iota(jnp.int32, (1, KV_WIDTH), 1) >> 6
    skcol = jnp.concatenate(
        [jnp.full((t_len, 1), sinks_ref[0, GROUP * g + j], F32)
         for j in range(GROUP) for g in range(N_KV_HEADS)], axis=0)
    outs = []
    for s, vals in elems:
        p, inv = _softmax_parts(s, skcol)
        o = _dot_nt(p.astype(BF16), vals) * inv
        o_j = []
        for j in range(GROUP):
            acc = None
            for g in range(N_KV_HEADS):
                r = (j * N_KV_HEADS + g) * t_len
                piece = jnp.where(vgroup == g, o[r:r + t_len], 0.0)
                acc = piece if acc is None else acc + piece
            o_j.append(acc)
        outs.append(jnp.concatenate(o_j, axis=1))
    return jnp.concatenate(outs, axis=0)


def _gate_weight_copy(w_hbm, wstage, wsem, chunk):
    slot = chunk % GW_SLOTS
    return pltpu.make_async_copy(
        w_hbm.at[0, pl.ds(GW_CHUNK * chunk, GW_CHUNK), pl.ds(G_B, G_END - G_B)],
        wstage.at[slot], wsem.at[slot])


def _load_gate_weights(w_hbm, wg_s, wstage, wsem):
    n_chunks = D_MODEL // GW_CHUNK
    ahead = GW_SLOTS - 1
    for c in range(ahead):
        _gate_weight_copy(w_hbm, wstage, wsem, c).start()
    for c in range(n_chunks):
        if c + ahead < n_chunks:
            _gate_weight_copy(w_hbm, wstage, wsem, c + ahead).start()
        _gate_weight_copy(w_hbm, wstage, wsem, c).wait()
        wg_s[GW_CHUNK * c:GW_CHUNK * (c + 1), :] = wstage[c % GW_SLOTS].astype(BF16)


def _layer_kernel(sinks_ref, x_ref, xs_ref, cos_ref, sin_ref, ck_ref, cv_ref, st_ref, w_hbm,
                  meta_ref, cos0_ref, sin0_ref, coss_ref, sins_ref, nw_ref, qnw_ref, knw_ref, cw_ref,
                  wq_ref, wkv_ref, wza_ref, wpa_ref, wpb_ref, wo_ref, bd_ref,
                  y_ref, ys_ref, nkt_ref, nvt_ref, nc_ref, nck_ref, ncv_ref, ncs_ref,
                  kprev, vprev, utail, wg_s, wstage, wsem):
    i = pl.program_id(0)
    nw = nw_ref[...]
    bd = bd_ref[...]
    knw = knw_ref[...]

    @pl.when(i == 0)
    def _init():
        _load_gate_weights(w_hbm, wg_s, wstage, wsem)
        x0 = jnp.concatenate([jnp.zeros((LEAD, D_MODEL), F32), meta_ref[...]], axis=0)
        xn0 = _rms_rows(x0, nw).astype(BF16)
        kv0 = _dot(xn0, wkv_ref[...])
        k0 = _rope_split(_head_rms_split(kv0[:, :KV_WIDTH], bd, knw), cos0_ref[...], sin0_ref[...])
        v0 = kv0[:, KV_WIDTH:]
        c0 = _dot(xn0, wg_s[:, G_C - G_B:G_H - G_B])
        h0 = _dot(xn0, wg_s[:, G_H - G_B:G_ZB - G_B])
        row = lax.broadcasted_iota(jnp.int32, (BLOCK, D_MODEL), 0)
        u0 = jnp.where(row >= LEAD, c0 * h0, 0.0)
        kprev[...] = k0
        vprev[...] = v0
        utail[...] = u0[BLOCK - SUBLANES:BLOCK]

    cw = cw_ref[0]
    row8 = lax.broadcasted_iota(jnp.int32, (SUBLANES, D_MODEL), 0)
    t_io = lax.broadcasted_iota(jnp.int32, (DEC_PER_STEP, DEC_SEQ, D_MODEL), 1)
    k_prev = kprev[...].astype(BF16)
    v_prev = vprev[...].astype(BF16)
    tail = utail[...]
    def front(h):
        ps = slice(TM_SUB * h, TM_SUB * (h + 1))
        with_dec = h == SUB_TILES - 1
        if with_dec:
            x = jnp.concatenate([x_ref[ps, :], xs_ref[...]], axis=0)
            cos = jnp.concatenate([cos_ref[ps, :], coss_ref[...]], axis=0)
            sin = jnp.concatenate([sin_ref[ps, :], sins_ref[...]], axis=0)
        else:
            x = x_ref[ps, :]
            cos = cos_ref[ps, :]
            sin = sin_ref[ps, :]
        xn = _rms_rows(x, nw).astype(BF16)

        q = _head_rms_split(_dot(xn, wq_ref[...]), bd, qnw_ref[...])
        q = _rope_split(q, cos * Q_SCALE, sin * Q_SCALE)
        kv = _dot(xn, wkv_ref[...])
        k = _rope_split(_head_rms_split(kv[:, :KV_WIDTH], bd, knw), cos, sin)
        v = kv[:, KV_WIDTH:]
        kb = k[:TM_SUB].astype(BF16)
        vb = v[:TM_SUB].astype(BF16)
        return ps, with_dec, x, xn, q, k, v, kb, vb

    def mid(h, fr, k_prev, v_prev, tail):
        ps, with_dec, x, xn, q, k, v, kb, vb = fr
        first = (i == 0) if h == 0 else False
        o_a = _prompt_attention(q[:TM_SUB].astype(BF16), kb, vb, k_prev, v_prev, sinks_ref, first)
        if with_dec:
            elems = _decode_scores(q[TM_SUB:], k[TM_SUB:], v[TM_SUB:], ck_ref, cv_ref, nck_ref,
                                   ncv_ref)
        c, hc, b, z_b = [_dot_quarters(xn, wg_s, lo - G_B)
                         for lo in (G_C, G_H, G_B, G_ZB)]
        if with_dec:
            o_a = jnp.concatenate([o_a, _decode_outputs(elems, sinks_ref)], axis=0)
        k_last = k[TM_SUB - BLOCK:TM_SUB]
        v_last = v[TM_SUB - BLOCK:TM_SUB]

        u_all = c * hc
        u = u_all[:TM_SUB]
        prev1 = tail[SUBLANES - 1:SUBLANES]
        prev2 = tail[SUBLANES - 2:SUBLANES - 1]
        r1 = pltpu.roll(u, 1, 0)
        r2 = pltpu.roll(u, 2, 0)
        um1 = jnp.concatenate(
            [jnp.where(row8 == 0, prev1, r1[:SUBLANES]), r1[SUBLANES:]], axis=0)
        um2 = jnp.concatenate(
            [jnp.where(row8 == 0, prev2, jnp.where(row8 == 1, prev1, r2[:SUBLANES])),
             r2[SUBLANES:]], axis=0)
        conv = cw[0:1] * um2 + cw[1:2] * um1 + cw[2:3] * u
        tail = u[TM_SUB - SUBLANES:]
        if with_dec:
            u3 = u_all[TM_SUB:].reshape(DEC_PER_STEP, DEC_SEQ, D_MODEL)
            st = st_ref[...]
            s_m2 = st[:, 0:1, :]
            s_m1 = st[:, 1:2, :]
            us1 = jnp.where(t_io == 0, s_m1, pltpu.roll(u3, 1, 1))
            us2 = jnp.where(t_io == 0, s_m2, jnp.where(t_io == 1, s_m1, pltpu.roll(u3, 2, 1)))
            conv_s = (cw[0:1] * us2 + cw[1:2] * us1 + cw[2:3] * u3).reshape(TS, D_MODEL)
            ncs_ref[...] = u3[:, DEC_SEQ - (CONV_K - 1):, :]
            conv = jnp.concatenate([conv, conv_s], axis=0)
        return o_a, conv, b, z_b, k_last, v_last, tail

    def back(fr, md):
        ps, with_dec, x, xn = fr[:4]
        o_a, conv, b, z_b = md[:4]
        z_a = _dot(xn, wza_ref[...])
        br_a = _dot((o_a * _silu(z_a)).astype(BF16), wpa_ref[...])
        br_b = _dot(((b * conv) * _silu(z_b)).astype(BF16), wpb_ref[...])
        g_a = _dot_quarters(xn, wg_s, G_GA - G_B)
        g_b = _dot_quarters(xn, wg_s, G_GB - G_B)
        mixed = (jax.nn.sigmoid(g_a) * br_a + jax.nn.sigmoid(g_b) * br_b).astype(BF16)
        y = x + _dot(mixed, wo_ref[...])
        y_ref[ps, :] = y[:TM_SUB]
        if with_dec:
            ys_ref[...] = y[TM_SUB:]

    fronts = {0: front(0)}
    mids = {}
    for h in range(SUB_TILES):
        if h:
            k_prev = fronts[h - 1][7][TM_SUB - BLOCK:]
            v_prev = fronts[h - 1][8][TM_SUB - BLOCK:]
            tail = mids[h - 1][6]
        mids[h] = mid(h, fronts[h], k_prev, v_prev, tail)
        if h + 1 < SUB_TILES:
            fronts[h + 1] = front(h + 1)
        back(fronts[h], mids[h])
    k_last, v_last, tail = mids[SUB_TILES - 1][4:7]

    kprev[...] = k_last
    vprev[...] = v_last
    utail[...] = tail
    nkt_ref[...] = _rows_split_to_natural(k_last.T)
    nvt_ref[...] = v_last.T
    nc_ref[...] = tail[SUBLANES - (CONV_K - 1):]


def _const_spec(shape):
    zeros = (0,) * len(shape)
    return pl.BlockSpec(shape, lambda i: zeros, pipeline_mode=pl.Buffered(1))


def _layer_call(x, xs, tabs, ck, cv, st, w_in, meta, tabs0, tabs_s, nw, qnw, knw, cw, sinks,
                wq, wk, wza, wpa, wpb, wo, bd):
    tm = TM_PROMPT
    assert N_STEPS * DEC_PER_STEP == DEC_BATCH and TS % (2 * SUBLANES) == 0 and TS <= LANES
    assert TM_SUB % BLOCK == 0
    row_spec = lambda rows, w: pl.BlockSpec((rows, w), lambda i: (i, 0))
    cache_spec = pl.BlockSpec((DEC_PER_STEP, KV_WIDTH, WINDOW), lambda i: (i, 0, 0))
    st_spec = pl.BlockSpec((DEC_PER_STEP, CONV_K - 1, D_MODEL), lambda i: (i, 0, 0))
    consts = (meta, tabs0[0], tabs0[1], tabs_s[0], tabs_s[1], nw, qnw, knw, cw,
              wq, wk, wza, wpa, wpb, wo, bd)
    in_specs = [pl.BlockSpec(memory_space=pltpu.SMEM),
                row_spec(tm, D_MODEL), row_spec(TS, D_MODEL), row_spec(tm, LANES), row_spec(tm, LANES),
                cache_spec, cache_spec, st_spec, pl.BlockSpec(memory_space=pl.ANY)]
    in_specs += [_const_spec(a.shape) for a in consts]
    resident = lambda shape: pl.BlockSpec(shape, lambda i: (0, 0))
    out_specs = [row_spec(tm, D_MODEL), row_spec(TS, D_MODEL),
                 resident((KV_WIDTH, BLOCK)), resident((KV_WIDTH, BLOCK)),
                 resident((CONV_K - 1, D_MODEL)), cache_spec, cache_spec, st_spec]
    cache_sds = jax.ShapeDtypeStruct((DEC_BATCH, KV_WIDTH, WINDOW), F32)
    out_shape = [jax.ShapeDtypeStruct((SEQ, D_MODEL), F32),
                 jax.ShapeDtypeStruct((DEC_BATCH * DEC_SEQ, D_MODEL), F32),
                 jax.ShapeDtypeStruct((KV_WIDTH, BLOCK), F32),
                 jax.ShapeDtypeStruct((KV_WIDTH, BLOCK), F32),
                 jax.ShapeDtypeStruct((CONV_K - 1, D_MODEL), F32), cache_sds, cache_sds,
                 jax.ShapeDtypeStruct((DEC_BATCH, CONV_K - 1, D_MODEL), F32)]
    return pl.pallas_call(
        _layer_kernel,
        grid=(N_STEPS,),
        in_specs=in_specs,
        out_specs=out_specs,
        out_shape=out_shape,
        scratch_shapes=[pltpu.VMEM((BLOCK, KV_WIDTH), F32),
                        pltpu.VMEM((BLOCK, KV_WIDTH), F32),
                        pltpu.VMEM((SUBLANES, D_MODEL), F32),
                        pltpu.VMEM((D_MODEL, G_END - G_B), BF16),
                        pltpu.VMEM((GW_SLOTS, GW_CHUNK, G_END - G_B), F32),
                        pltpu.SemaphoreType.DMA((GW_SLOTS,))],
        compiler_params=pltpu.CompilerParams(
            dimension_semantics=("arbitrary",), vmem_limit_bytes=VMEM_LIMIT),
        name="hybrid_layer",
    )(sinks, x, xs, tabs[0], tabs[1], ck, cv, st, w_in, *consts)


def _rope_tables(positions):
    inv = np.power(ROPE_THETA, -np.arange(HALF, dtype=np.float64) * (2.0 / HEAD_DIM))
    ang = np.asarray(positions, dtype=np.float64)[:, None] * inv[None, :]
    cos = np.tile(np.cos(ang), (1, LANES // HALF))
    sin = np.tile(np.sin(ang), (1, LANES // HALF))
    return jnp.asarray(cos, F32), jnp.asarray(sin, F32)


def _head_mean_matrix():
    grp = np.arange(256) // HALF
    return jnp.asarray((grp[:, None] == grp[None, :]) / float(HEAD_DIM), BF16)


def kernel(x_prompt, x_sample, cache_k, cache_v, state_conv, meta_tokens, norm_w, w_in,
           q_norm_w, k_norm_w, sinks, conv_w, w_proj_a, w_proj_b, w_out):
    assert x_prompt.shape == (1, SEQ, D_MODEL) and x_sample.shape == (DEC_BATCH, DEC_SEQ, D_MODEL)
    assert w_in.shape[0] == 1, "single layer"
    wg = w_in[0, :, :G_B].astype(BF16)
    wq = (wg[:, :W_K].reshape(D_MODEL, N_KV_HEADS, GROUP, 2, HALF)
          .transpose(0, 2, 3, 1, 4).reshape(D_MODEL, D_MODEL))
    wk = (wg[:, W_K:W_V].reshape(D_MODEL, N_KV_HEADS, 2, HALF)
          .transpose(0, 2, 1, 3).reshape(D_MODEL, KV_WIDTH))
    wk = jnp.concatenate([wk, wg[:, W_V:W_ZA]], axis=1)
    wza = (wg[:, W_ZA:G_B].reshape(D_MODEL, N_KV_HEADS, GROUP, HEAD_DIM)
           .transpose(0, 2, 1, 3).reshape(D_MODEL, D_MODEL))
    wpa = (w_proj_a[0].astype(BF16).reshape(N_KV_HEADS, GROUP, HEAD_DIM, D_MODEL)
           .transpose(1, 0, 2, 3).reshape(D_MODEL, D_MODEL))
    wpb = w_proj_b[0].astype(BF16)
    wo = w_out[0].astype(BF16)

    nw = norm_w[0].reshape(1, D_MODEL)
    split = lambda v: jnp.broadcast_to(v.reshape(2, 1, HALF), (2, N_KV_HEADS, HALF)).reshape(1, 256)

    to_tiles = lambda c: c[0].transpose(0, 2, 3, 1).reshape(DEC_BATCH, KV_WIDTH, WINDOW)
    from_tiles = lambda c: (c.reshape(DEC_BATCH, N_KV_HEADS, HEAD_DIM, WINDOW)
                            .transpose(0, 3, 1, 2)[None])
    from_tile = lambda t: t.reshape(N_KV_HEADS, HEAD_DIM, WINDOW).transpose(2, 0, 1)

    tabs_p = _rope_tables(np.arange(SEQ) + N_META)
    tabs_0 = _rope_tables(np.arange(BLOCK) - LEAD)
    tabs_s = _rope_tables(PAST_LEN + (np.arange(TS) % DEC_SEQ))
    y_p, y_s, nkt_p, nvt_p, nc_p, nk_s, nv_s, nc_s = _layer_call(
        x_prompt[0], x_sample.reshape(DEC_BATCH * DEC_SEQ, D_MODEL), tabs_p,
        to_tiles(cache_k), to_tiles(cache_v), state_conv[0], w_in,
        meta_tokens.astype(x_prompt.dtype), tabs_0, tabs_s,
        nw, split(q_norm_w), split(k_norm_w), conv_w, sinks,
        wq, wk, wza, wpa, wpb, wo, _head_mean_matrix())

    return (y_p.reshape(1, SEQ, D_MODEL),
            y_s.reshape(DEC_BATCH, DEC_SEQ, D_MODEL),
            from_tile(nkt_p).reshape(1, 1, WINDOW, N_KV_HEADS, HEAD_DIM),
            from_tile(nvt_p).reshape(1, 1, WINDOW, N_KV_HEADS, HEAD_DIM),
            nc_p.reshape(1, 1, CONV_K - 1, D_MODEL),
            from_tiles(nk_s),
            from_tiles(nv_s),
            nc_s.reshape(1, DEC_BATCH, CONV_K - 1, D_MODEL))
```

```python
import functools

import numpy as np
import jax
import jax.numpy as jnp
from jax import lax
from jax.experimental import pallas as pl
from jax.experimental.pallas import tpu as pltpu

D_MODEL = 1024
SEQ = 16384
DEC_BATCH = 128
DEC_SEQ = 8
PAST_LEN = 16384
N_HEADS = 16
N_KV_HEADS = 4
GROUP = N_HEADS // N_KV_HEADS
HEAD_DIM = 64
HALF = HEAD_DIM // 2
KV_WIDTH = N_KV_HEADS * HEAD_DIM
WINDOW = 128
BLOCK = 128
ROPE_THETA = 10000.0
CONV_K = 3
N_META = 16
LEAD = BLOCK - N_META
EPS = 1e-6
NEG_INF = -1e30
Q_SCALE = HEAD_DIM ** -0.5

W_K = D_MODEL
W_V = W_K + KV_WIDTH
W_ZA = W_V + KV_WIDTH
G_B = W_ZA + D_MODEL
G_C = G_B + D_MODEL
G_H = G_C + D_MODEL
G_ZB = G_H + D_MODEL
G_GA = G_ZB + D_MODEL
G_GB = G_GA + D_MODEL
G_END = G_GB + D_MODEL

LANES = 128
SUBLANES = 8
VMEM_BYTES_V7X = 64 * 1024 * 1024

TM_PROMPT = 512
N_STEPS = SEQ // TM_PROMPT
DEC_PER_STEP = DEC_BATCH // N_STEPS
TS = DEC_PER_STEP * DEC_SEQ
GW_CHUNK = 64
GW_SLOTS = 4
SUB_TILES = 2
TM_SUB = TM_PROMPT // SUB_TILES
VMEM_LIMIT = VMEM_BYTES_V7X - 4 * 1024 * 1024

BF16 = jnp.bfloat16
F32 = jnp.float32


def _dot(a, b):
    return jnp.dot(a, b, preferred_element_type=F32)


def _dot_nt(a, b):
    return lax.dot_general(a, b, (((1,), (1,)), ((), ())), preferred_element_type=F32)


def _rms_rows(x, w):
    ms = jnp.mean(x * x, axis=-1, keepdims=True)
    return (x * lax.rsqrt(ms + EPS)) * w


def _head_rms_split(t, bd, w):
    n = t.shape[1] // 256
    r = t.shape[0]
    halves = [(t[:, 256 * c:256 * c + LANES], t[:, 256 * c + LANES:256 * (c + 1)]) for c in range(n)]
    pieces = [a * a + b * b for a, b in halves]
    if n == 1:
        pieces = [pieces[0], pieces[0]]
    pair_ms = [_dot(jnp.concatenate(pieces[2 * i:2 * i + 2], axis=1).astype(BF16), bd)
               for i in range(len(pieces) // 2)]
    ms = [pair_ms[i // 2][:, LANES * (i % 2):LANES * (i % 2 + 1)] for i in range(n)]
    out = []
    for (a, b), m in zip(halves, ms):
        scale = lax.rsqrt(m + EPS)
        out.append((a * scale) * w[:, :LANES])
        out.append((b * scale) * w[:, LANES:])
    return jnp.concatenate(out, axis=1)


def _rope_split(t, cos, sin):
    out = []
    for c in range(t.shape[1] // 256):
        a = t[:, 256 * c:256 * c + LANES]
        b = t[:, 256 * c + LANES:256 * (c + 1)]
        out.append(a * cos - b * sin)
        out.append(b * cos + a * sin)
    return jnp.concatenate(out, axis=1)


def _lane_group_masks(group_of_lane, dtype):
    return [(group_of_lane == g).astype(dtype) for g in range(N_KV_HEADS)]


def _softmax_parts(s, sk):
    m = jnp.maximum(jnp.max(s, axis=-1, keepdims=True), sk)
    p = jnp.exp(s - m)
    den = jnp.sum(p, axis=-1, keepdims=True) + jnp.exp(sk - m)
    return p, 1.0 / den


def _silu(z):
    return z * jax.nn.sigmoid(z)


def _rows_split_to_natural(t):
    return jnp.concatenate(
        [t[LANES * half + HALF * g:LANES * half + HALF * (g + 1)]
         for g in range(N_KV_HEADS) for half in range(2)], axis=0)


def _rows_natural_to_split(t):
    return jnp.concatenate(
        [t[HEAD_DIM * g + HALF * half:HEAD_DIM * g + HALF * (half + 1)]
         for half in range(2) for g in range(N_KV_HEADS)], axis=0)


def _prompt_attention(q, kb, vb, kprev, vprev, sinks_ref, first_step):
    tm = q.shape[0]
    lane = lax.broadcasted_iota(jnp.int32, (1, 2 * BLOCK), 1)
    kmasks = _lane_group_masks((lane & (LANES - 1)) >> 5, BF16)
    vmasks = _lane_group_masks(lane >> 6, BF16)
    vgroup = lane >> 6
    r_io = lax.broadcasted_iota(jnp.int32, (BLOCK, 2 * BLOCK), 0)
    c_io = lax.broadcasted_iota(jnp.int32, (BLOCK, 2 * BLOCK), 1)
    band = (c_io >= r_io) & (c_io <= r_io + WINDOW)
    first_band = band & (c_io >= jnp.where(first_step, LEAD, 0))

    o_blocks = []
    for b in range(tm // BLOCK):
        lo = b * BLOCK
        if b == 0:
            kpair = jnp.concatenate([kprev, kb[:BLOCK]], axis=0)
            vpair = jnp.concatenate([vprev, vb[:BLOCK]], axis=0)
            mask = first_band
        else:
            kpair = kb[lo - BLOCK:lo + BLOCK]
            vpair = vb[lo - BLOCK:lo + BLOCK]
            mask = band
        kbd = jnp.concatenate([kpair * km for km in kmasks], axis=0)
        vbd = jnp.concatenate([vpair * vm for vm in vmasks], axis=0)
        qs = jnp.concatenate([q[lo:lo + BLOCK, 256 * j:256 * (j + 1)] for j in range(GROUP)],
                             axis=0)
        s = _dot_nt(qs, kbd)
        p_rows = []
        mults = []
        for j in range(GROUP):
            p_cols = []
            invs = []
            for g in range(N_KV_HEADS):
                sl = s[BLOCK * j:BLOCK * (j + 1), 256 * g:256 * (g + 1)]
                sl = jnp.where(mask, sl, NEG_INF)
                p, inv = _softmax_parts(sl, sinks_ref[0, GROUP * g + j])
                p_cols.append(p.astype(BF16))
                invs.append(inv)
            p_rows.append(jnp.concatenate(p_cols, axis=1))
            mults.append(jnp.where(vgroup == 0, invs[0],
                                   jnp.where(vgroup == 1, invs[1],
                                             jnp.where(vgroup == 2, invs[2], invs[3]))))
        p_all = jnp.concatenate(p_rows, axis=0)
        kw = 2 * BLOCK
        parts = [_dot(p_all[:, kw * g:kw * (g + 1)], vbd[kw * g:kw * (g + 1)])
                 for g in range(N_KV_HEADS)]
        o = (parts[0] + parts[1]) + (parts[2] + parts[3])
        o_blocks.append(jnp.concatenate(
            [o[BLOCK * j:BLOCK * (j + 1)] * mults[j] for j in range(GROUP)], axis=1))
    return jnp.concatenate(o_blocks, axis=0)


def _decode_scores(qs, ks, vs, ck_ref, cv_ref, nck_ref, ncv_ref):
    t_len = DEC_SEQ
    n_exp = GROUP * N_KV_HEADS * t_len
    pad = jnp.zeros((LANES - qs.shape[0], KV_WIDTH), F32)
    kst = jnp.concatenate([ks, pad], axis=0).T
    vst = jnp.concatenate([vs, pad], axis=0).T
    kst_nat = _rows_split_to_natural(kst)
    kst_b = kst.astype(BF16)
    vst_b = vst.astype(BF16)

    lane = lax.broadcasted_iota(jnp.int32, (1, KV_WIDTH), 1)
    kgroup = (lane & (LANES - 1)) >> 5
    t_io = lax.broadcasted_iota(jnp.int32, (n_exp, WINDOW), 0) & (t_len - 1)
    c_io = lax.broadcasted_iota(jnp.int32, (n_exp, WINDOW), 1)
    old_ok = c_io >= t_io
    new_lane = lax.broadcasted_iota(jnp.int32, (KV_WIDTH, WINDOW), 1) >= WINDOW - t_len

    elems = []
    for b in range(qs.shape[0] // t_len):
        r0 = b * t_len
        qn = qs[r0:r0 + t_len]
        q_exp = jnp.concatenate(
            [jnp.where(kgroup == g, qn[:, 256 * j:256 * (j + 1)], 0.0)
             for j in range(GROUP) for g in range(N_KV_HEADS)], axis=0).astype(BF16)
        kt_old = ck_ref[b]
        vt_old = cv_ref[b]
        keys = jnp.concatenate([_rows_natural_to_split(kt_old).astype(BF16), kst_b], axis=1)
        vals = jnp.concatenate([vt_old.astype(BF16), vst_b], axis=1)
        s = _dot(q_exp, keys)
        new_ok = (c_io >= r0) & (c_io - t_io <= r0)
        s = jnp.where(jnp.concatenate([old_ok, new_ok], axis=1), s, NEG_INF)
        elems.append((s, vals))
        shift_new = WINDOW - t_len - r0
        nck_ref[b] = jnp.where(new_lane, pltpu.roll(kst_nat, shift_new, 1),
                               pltpu.roll(kt_old, WINDOW - t_len, 1))
        ncv_ref[b] = jnp.where(new_lane, pltpu.roll(vst, shift_new, 1),
                               pltpu.roll(vt_old, WINDOW - t_len, 1))
    return elems


def _decode_outputs(elems, sinks_ref):
    t_len = DEC_SEQ
    vgroup = lax.broadcasted_iota(jnp.int32, (1, KV_WIDTH), 1) >> 6
    skcol = jnp.concatenate(
        [jnp.full((t_len, 1), sinks_ref[0, GROUP * g + j], F32)
         for j in range(GROUP) for g in range(N_KV_HEADS)], axis=0)
    outs = []
    for s, vals in elems:
        p, inv = _softmax_parts(s, skcol)
        o = _dot_nt(p.astype(BF16), vals) * inv
        o_j = []
        for j in range(GROUP):
            acc = None
            for g in range(N_KV_HEADS):
                r = (j * N_KV_HEADS + g) * t_len
                piece = jnp.where(vgroup == g, o[r:r + t_len], 0.0)
                acc = piece if acc is None else acc + piece
            o_j.append(acc)
        outs.append(jnp.concatenate(o_j, axis=1))
    return jnp.concatenate(outs, axis=0)


def _gate_weight_copy(w_hbm, wstage, wsem, chunk):
    slot = chunk % GW_SLOTS
    return pltpu.make_async_copy(
        w_hbm.at[0, pl.ds(GW_CHUNK * chunk, GW_CHUNK), pl.ds(G_B, G_END - G_B)],
        wstage.at[slot], wsem.at[slot])


def _load_gate_weights(w_hbm, wg_s, wstage, wsem):
    n_chunks = D_MODEL // GW_CHUNK
    ahead = GW_SLOTS - 1
    for c in range(ahead):
        _gate_weight_copy(w_hbm, wstage, wsem, c).start()
    for c in range(n_chunks):
        if c + ahead < n_chunks:
            _gate_weight_copy(w_hbm, wstage, wsem, c + ahead).start()
        _gate_weight_copy(w_hbm, wstage, wsem, c).wait()
        wg_s[GW_CHUNK * c:GW_CHUNK * (c + 1), :] = wstage[c % GW_SLOTS].astype(BF16)


def _layer_kernel(sinks_ref, x_ref, xs_ref, cos_ref, sin_ref, ck_ref, cv_ref, st_ref, w_hbm,
                  meta_ref, cos0_ref, sin0_ref, coss_ref, sins_ref, nw_ref, qnw_ref, knw_ref, cw_ref,
                  wq_ref, wkv_ref, wza_ref, wpa_ref, wpb_ref, wo_ref, bd_ref,
                  y_ref, ys_ref, nkt_ref, nvt_ref, nc_ref, nck_ref, ncv_ref, ncs_ref,
                  kprev, vprev, utail, wg_s, wstage, wsem):
    i = pl.program_id(0)
    nw = nw_ref[...]
    bd = bd_ref[...]
    knw = knw_ref[...]

    @pl.when(i == 0)
    def _init():
        _load_gate_weights(w_hbm, wg_s, wstage, wsem)
        x0 = jnp.concatenate([jnp.zeros((LEAD, D_MODEL), F32), meta_ref[...]], axis=0)
        xn0 = _rms_rows(x0, nw).astype(BF16)
        kv0 = _dot(xn0, wkv_ref[...])
        k0 = _rope_split(_head_rms_split(kv0[:, :KV_WIDTH], bd, knw), cos0_ref[...], sin0_ref[...])
        v0 = kv0[:, KV_WIDTH:]
        c0 = _dot(xn0, wg_s[:, G_C - G_B:G_H - G_B])
        h0 = _dot(xn0, wg_s[:, G_H - G_B:G_ZB - G_B])
        row = lax.broadcasted_iota(jnp.int32, (BLOCK, D_MODEL), 0)
        u0 = jnp.where(row >= LEAD, c0 * h0, 0.0)
        kprev[...] = k0
        vprev[...] = v0
        utail[...] = u0[BLOCK - SUBLANES:BLOCK]

    cw = cw_ref[0]
    row8 = lax.broadcasted_iota(jnp.int32, (SUBLANES, D_MODEL), 0)
    t_io = lax.broadcasted_iota(jnp.int32, (DEC_PER_STEP, DEC_SEQ, D_MODEL), 1)
    k_prev = kprev[...].astype(BF16)
    v_prev = vprev[...].astype(BF16)
    tail = utail[...]
    def front(h):
        ps = slice(TM_SUB * h, TM_SUB * (h + 1))
        with_dec = h == SUB_TILES - 1
        if with_dec:
            x = jnp.concatenate([x_ref[ps, :], xs_ref[...]], axis=0)
            cos = jnp.concatenate([cos_ref[ps, :], coss_ref[...]], axis=0)
            sin = jnp.concatenate([sin_ref[ps, :], sins_ref[...]], axis=0)
        else:
            x = x_ref[ps, :]
            cos = cos_ref[ps, :]
            sin = sin_ref[ps, :]
        xn = _rms_rows(x, nw).astype(BF16)

        q = _head_rms_split(_dot(xn, wq_ref[...]), bd, qnw_ref[...])
        q = _rope_split(q, cos * Q_SCALE, sin * Q_SCALE)
        k = _dot(xn, wkv_ref[:, :KV_WIDTH])
        v = _dot(xn, wkv_ref[:, KV_WIDTH:])
        k = _rope_split(_head_rms_split(k, bd, knw), cos, sin)
        kb = k[:TM_SUB].astype(BF16)
        vb = v[:TM_SUB].astype(BF16)
        return ps, with_dec, x, xn, q, k, v, kb, vb

    def mid(h, fr, k_prev, v_prev, tail):
        ps, with_dec, x, xn, q, k, v, kb, vb = fr
        first = (i == 0) if h == 0 else False
        o_a = _prompt_attention(q[:TM_SUB].astype(BF16), kb, vb, k_prev, v_prev, sinks_ref, first)
        if with_dec:
            elems = _decode_scores(q[TM_SUB:], k[TM_SUB:], v[TM_SUB:], ck_ref, cv_ref, nck_ref,
                                   ncv_ref)
        c, hc, b, z_b = [_dot(xn, wg_s[:, lo - G_B:lo - G_B + D_MODEL])
                         for lo in (G_C, G_H, G_B, G_ZB)]
        if with_dec:
            o_a = jnp.concatenate([o_a, _decode_outputs(elems, sinks_ref)], axis=0)
        k_last = k[TM_SUB - BLOCK:TM_SUB]
        v_last = v[TM_SUB - BLOCK:TM_SUB]

        u_all = c * hc
        u = u_all[:TM_SUB]
        prev1 = tail[SUBLANES - 1:SUBLANES]
        prev2 = tail[SUBLANES - 2:SUBLANES - 1]
        r1 = pltpu.roll(u, 1, 0)
        r2 = pltpu.roll(u, 2, 0)
        um1 = jnp.concatenate(
            [jnp.where(row8 == 0, prev1, r1[:SUBLANES]), r1[SUBLANES:]], axis=0)
        um2 = jnp.concatenate(
            [jnp.where(row8 == 0, prev2, jnp.where(row8 == 1, prev1, r2[:SUBLANES])),
             r2[SUBLANES:]], axis=0)
        conv = cw[0:1] * um2 + cw[1:2] * um1 + cw[2:3] * u
        tail = u[TM_SUB - SUBLANES:]
        if with_dec:
            u3 = u_all[TM_SUB:].reshape(DEC_PER_STEP, DEC_SEQ, D_MODEL)
            st = st_ref[...]
            s_m2 = st[:, 0:1, :]
            s_m1 = st[:, 1:2, :]
            us1 = jnp.where(t_io == 0, s_m1, pltpu.roll(u3, 1, 1))
            us2 = jnp.where(t_io == 0, s_m2, jnp.where(t_io == 1, s_m1, pltpu.roll(u3, 2, 1)))
            conv_s = (cw[0:1] * us2 + cw[1:2] * us1 + cw[2:3] * u3).reshape(TS, D_MODEL)
            ncs_ref[...] = u3[:, DEC_SEQ - (CONV_K - 1):, :]
            conv = jnp.concatenate([conv, conv_s], axis=0)
        return o_a, conv, b, z_b, k_last, v_last, tail

    def back(fr, md):
        ps, with_dec, x, xn = fr[:4]
        o_a, conv, b, z_b = md[:4]
        z_a = _dot(xn, wza_ref[...])
        br_a = _dot((o_a * _silu(z_a)).astype(BF16), wpa_ref[...])
        br_b = _dot(((b * conv) * _silu(z_b)).astype(BF16), wpb_ref[...])
        g_a = _dot(xn, wg_s[:, G_GA - G_B:G_GB - G_B])
        g_b = _dot(xn, wg_s[:, G_GB - G_B:G_END - G_B])
        mixed = (jax.nn.sigmoid(g_a) * br_a + jax.nn.sigmoid(g_b) * br_b).astype(BF16)
        y = x + _dot(mixed, wo_ref[...])
        y_ref[ps, :] = y[:TM_SUB]
        if with_dec:
            ys_ref[...] = y[TM_SUB:]

    fronts = {0: front(0)}
    mids = {}
    for h in range(SUB_TILES):
        if h:
            k_prev = fronts[h - 1][7][TM_SUB - BLOCK:]
            v_prev = fronts[h - 1][8][TM_SUB - BLOCK:]
            tail = mids[h - 1][6]
        mids[h] = mid(h, fronts[h], k_prev, v_prev, tail)
        if h + 1 < SUB_TILES:
            fronts[h + 1] = front(h + 1)
        back(fronts[h], mids[h])
    k_last, v_last, tail = mids[SUB_TILES - 1][4:7]

    kprev[...] = k_last
    vprev[...] = v_last
    utail[...] = tail
    nkt_ref[...] = _rows_split_to_natural(k_last.T)
    nvt_ref[...] = v_last.T
    nc_ref[...] = tail[SUBLANES - (CONV_K - 1):]


def _const_spec(shape):
    zeros = (0,) * len(shape)
    return pl.BlockSpec(shape, lambda i: zeros, pipeline_mode=pl.Buffered(1))


def _layer_call(x, xs, tabs, ck, cv, st, w_in, meta, tabs0, tabs_s, nw, qnw, knw, cw, sinks,
                wq, wk, wza, wpa, wpb, wo, bd):
    tm = TM_PROMPT
    assert N_STEPS * DEC_PER_STEP == DEC_BATCH and TS % (2 * SUBLANES) == 0 and TS <= LANES
    assert TM_SUB % BLOCK == 0
    row_spec = lambda rows, w: pl.BlockSpec((rows, w), lambda i: (i, 0))
    cache_spec = pl.BlockSpec((DEC_PER_STEP, KV_WIDTH, WINDOW), lambda i: (i, 0, 0))
    st_spec = pl.BlockSpec((DEC_PER_STEP, CONV_K - 1, D_MODEL), lambda i: (i, 0, 0))
    consts = (meta, tabs0[0], tabs0[1], tabs_s[0], tabs_s[1], nw, qnw, knw, cw,
              wq, wk, wza, wpa, wpb, wo, bd)
    in_specs = [pl.BlockSpec(memory_space=pltpu.SMEM),
                row_spec(tm, D_MODEL), row_spec(TS, D_MODEL), row_spec(tm, LANES), row_spec(tm, LANES),
                cache_spec, cache_spec, st_spec, pl.BlockSpec(memory_space=pl.ANY)]
    in_specs += [_const_spec(a.shape) for a in consts]
    resident = lambda shape: pl.BlockSpec(shape, lambda i: (0, 0))
    out_specs = [row_spec(tm, D_MODEL), row_spec(TS, D_MODEL),
                 resident((KV_WIDTH, BLOCK)), resident((KV_WIDTH, BLOCK)),
                 resident((CONV_K - 1, D_MODEL)), cache_spec, cache_spec, st_spec]
    cache_sds = jax.ShapeDtypeStruct((DEC_BATCH, KV_WIDTH, WINDOW), F32)
    out_shape = [jax.ShapeDtypeStruct((SEQ, D_MODEL), F32),
                 jax.ShapeDtypeStruct((DEC_BATCH * DEC_SEQ, D_MODEL), F32),
                 jax.ShapeDtypeStruct((KV_WIDTH, BLOCK), F32),
                 jax.ShapeDtypeStruct((KV_WIDTH, BLOCK), F32),
                 jax.ShapeDtypeStruct((CONV_K - 1, D_MODEL), F32), cache_sds, cache_sds,
                 jax.ShapeDtypeStruct((DEC_BATCH, CONV_K - 1, D_MODEL), F32)]
    return pl.pallas_call(
        _layer_kernel,
        grid=(N_STEPS,),
        in_specs=in_specs,
        out_specs=out_specs,
        out_shape=out_shape,
        scratch_shapes=[pltpu.VMEM((BLOCK, KV_WIDTH), F32),
                        pltpu.VMEM((BLOCK, KV_WIDTH), F32),
                        pltpu.VMEM((SUBLANES, D_MODEL), F32),
                        pltpu.VMEM((D_MODEL, G_END - G_B), BF16),
                        pltpu.VMEM((GW_SLOTS, GW_CHUNK, G_END - G_B), F32),
                        pltpu.SemaphoreType.DMA((GW_SLOTS,))],
        compiler_params=pltpu.CompilerParams(
            dimension_semantics=("arbitrary",), vmem_limit_bytes=VMEM_LIMIT),
        name="hybrid_layer",
    )(sinks, x, xs, tabs[0], tabs[1], ck, cv, st, w_in, *consts)


def _rope_tables(positions):
    inv = np.power(ROPE_THETA, -np.arange(HALF, dtype=np.float64) * (2.0 / HEAD_DIM))
    ang = np.asarray(positions, dtype=np.float64)[:, None] * inv[None, :]
    cos = np.tile(np.cos(ang), (1, LANES // HALF))
    sin = np.tile(np.sin(ang), (1, LANES // HALF))
    return jnp.asarray(cos, F32), jnp.asarray(sin, F32)


def _head_mean_matrix():
    grp = np.arange(256) // HALF
    return jnp.asarray((grp[:, None] == grp[None, :]) / float(HEAD_DIM), BF16)


def kernel(x_prompt, x_sample, cache_k, cache_v, state_conv, meta_tokens, norm_w, w_in,
           q_norm_w, k_norm_w, sinks, conv_w, w_proj_a, w_proj_b, w_out):
    assert x_prompt.shape == (1, SEQ, D_MODEL) and x_sample.shape == (DEC_BATCH, DEC_SEQ, D_MODEL)
    assert w_in.shape[0] == 1, "single layer"
    wg = w_in[0, :, :G_B].astype(BF16)
    wq = (wg[:, :W_K].reshape(D_MODEL, N_KV_HEADS, GROUP, 2, HALF)
          .transpose(0, 2, 3, 1, 4).reshape(D_MODEL, D_MODEL))
    wk = (wg[:, W_K:W_V].reshape(D_MODEL, N_KV_HEADS, 2, HALF)
          .transpose(0, 2, 1, 3).reshape(D_MODEL, KV_WIDTH))
    wk = jnp.concatenate([wk, wg[:, W_V:W_ZA]], axis=1)
    wza = (wg[:, W_ZA:G_B].reshape(D_MODEL, N_KV_HEADS, GROUP, HEAD_DIM)
           .transpose(0, 2, 1, 3).reshape(D_MODEL, D_MODEL))
    wpa = (w_proj_a[0].astype(BF16).reshape(N_KV_HEADS, GROUP, HEAD_DIM, D_MODEL)
           .transpose(1, 0, 2, 3).reshape(D_MODEL, D_MODEL))
    wpb = w_proj_b[0].astype(BF16)
    wo = w_out[0].astype(BF16)

    nw = norm_w[0].reshape(1, D_MODEL)
    split = lambda v: jnp.broadcast_to(v.reshape(2, 1, HALF), (2, N_KV_HEADS, HALF)).reshape(1, 256)

    to_tiles = lambda c: c[0].transpose(0, 2, 3, 1).reshape(DEC_BATCH, KV_WIDTH, WINDOW)
    from_tiles = lambda c: (c.reshape(DEC_BATCH, N_KV_HEADS, HEAD_DIM, WINDOW)
                            .transpose(0, 3, 1, 2)[None])
    from_tile = lambda t: t.reshape(N_KV_HEADS, HEAD_DIM, WINDOW).transpose(2, 0, 1)

    tabs_p = _rope_tables(np.arange(SEQ) + N_META)
    tabs_0 = _rope_tables(np.arange(BLOCK) - LEAD)
    tabs_s = _rope_tables(PAST_LEN + (np.arange(TS) % DEC_SEQ))
    y_p, y_s, nkt_p, nvt_p, nc_p, nk_s, nv_s, nc_s = _layer_call(
        x_prompt[0], x_sample.reshape(DEC_BATCH * DEC_SEQ, D_MODEL), tabs_p,
        to_tiles(cache_k), to_tiles(cache_v), state_conv[0], w_in,
        meta_tokens.astype(x_prompt.dtype), tabs_0, tabs_s,
        nw, split(q_norm_w), split(k_norm_w), conv_w, sinks,
        wq, wk, wza, wpa, wpb, wo, _head_mean_matrix())

    return (y_p.reshape(1, SEQ, D_MODEL),
            y_s.reshape(DEC_BATCH, DEC_SEQ, D_MODEL),
            from_tile(nkt_p).reshape(1, 1, WINDOW, N_KV_HEADS, HEAD_DIM),
            from_tile(nvt_p).reshape(1, 1, WINDOW, N_KV_HEADS, HEAD_DIM),
            nc_p.reshape(1, 1, CONV_K - 1, D_MODEL),
            from_tiles(nk_s),
            from_tiles(nv_s),
            nc_s.reshape(1, DEC_BATCH, CONV_K - 1, D_MODEL))
```
